```python
import jax, jax.numpy as jnp
from jax import lax
import numpy as np

D_MODEL = 1024
BATCH = 8
SEQ = 4096
DEPTH = 2

GRID_W = 64
CTX_LEN = 256
N_MIXERS = 2
HEAD_DIM = 64
N_HEADS = D_MODEL // HEAD_DIM
DECAY_LORA = 64
AAA_LORA = 64
GATE_LORA = 128
CONV_W = 3
N_EXPERTS = 16
N_GROUPS = 4
EXPERTS_PER_GROUP = N_EXPERTS // N_GROUPS
TOP_K = 2
EXPERT_DIM = 512
N_RWKV = (DEPTH + N_MIXERS - 1) // N_MIXERS
N_SCONV = DEPTH // N_MIXERS
RMS_EPS = 1e-6
GN_EPS = 64e-5
L2_EPS = 1e-12

kernel_name = 'hybrid_rwkv7_shortconv_grouped_moe_prefix_dit'


def _rmsnorm(x, g):
    xf = x.astype(jnp.float32)
    xf = xf * lax.rsqrt(jnp.mean(xf * xf, axis=-1, keepdims=True) + RMS_EPS)
    return (xf * g.astype(jnp.float32)).astype(x.dtype)


def _modulate(x, shift, scale):
    return x * (1 + scale) + shift


def _row_neighbours(x, n_rows, row_len):
    b, t, d = x.shape
    xr = x.reshape(b, n_rows, row_len, d)
    pad = jnp.zeros_like(xr[:, :, :1])
    prev = jnp.concatenate([pad, xr[:, :, :-1]], axis=2).reshape(b, t, d)
    nxt = jnp.concatenate([xr[:, :, 1:], pad], axis=2).reshape(b, t, d)
    return prev, nxt


def _heads(z):
    b, t = z.shape[0], z.shape[1]
    return z.astype(jnp.float32).reshape(b, t, N_HEADS, HEAD_DIM)


def _rwkv_shared(xn, n_rows, row_len, mu, w_rkv, g1, g2, k_k):
    prev, nxt = _row_neighbours(xn, n_rows, row_len)
    xx = 0.5 * (prev + nxt) - xn
    x_rkv = jnp.stack([xn + xx * mu[0], xn + xx * mu[1], xn + xx * mu[2]])
    r, k, v = jnp.einsum('pbtd,pde->pbte', x_rkv, w_rkv)
    g = jax.nn.sigmoid((xn + xx * mu[5]) @ g1) @ g2
    r, k, v = _heads(r), _heads(k), _heads(v)
    kk = k * k_k.astype(jnp.float32).reshape(N_HEADS, HEAD_DIM)
    kk = kk * lax.rsqrt(jnp.maximum(jnp.sum(kk * kk, axis=-1, keepdims=True), L2_EPS * L2_EPS))
    return xx, r, k, v, kk, g


def _rwkv_direction(xn, xx, mu, k, w0, w1, w2, a0, a1, a2, k_a):
    xw = xn + xx * mu[3]
    xa = xn + xx * mu[4]
    w_log = -jax.nn.softplus(-(w0 + jnp.tanh(xw @ w1) @ w2).astype(jnp.float32)) - 0.5
    decay = _heads(jnp.exp(-jnp.exp(w_log)))
    a = _heads(jax.nn.sigmoid((a0 + (xa @ a1) @ a2).astype(jnp.float32)))
    k_dir = k * (1 + (a - 1) * k_a.astype(jnp.float32).reshape(N_HEADS, HEAD_DIM))
    return decay, a, k_dir


def _wkv_scan(decay, k, v, kk, a, r, s0, reverse):
    with_output = r is not None
    xs = jax.tree_util.tree_map(lambda z: jnp.moveaxis(z, 1, 0), (decay, k, v, kk, a, r))

    def step(s, inp):
        w_t, k_t, v_t, kk_t, a_t, r_t = inp
        sa = jnp.einsum('bhvk,bhk->bhv', s, -kk_t)
        s = (s * w_t[:, :, None, :] + sa[..., None] * (kk_t * a_t)[:, :, None, :]
             + v_t[..., None] * k_t[:, :, None, :])
        y = jnp.einsum('bhvk,bhk->bhv', s, r_t) if with_output else None
        return s, y

    s_final, ys = lax.scan(step, s0, xs, reverse=reverse)
    return s_final, (jnp.moveaxis(ys, 0, 1) if with_output else None)


def _bonus(r, k_dir, v, r_k):
    return jnp.sum(r * k_dir * r_k.astype(jnp.float32), axis=-1, keepdims=True) * v


def _rwkv_readout(y, g, gn_w, gn_b, w_o, dtype):
    b, t = y.shape[0], y.shape[1]
    mean = jnp.mean(y, axis=-1, keepdims=True)
    var = jnp.mean(jnp.square(y - mean), axis=-1, keepdims=True)
    y = ((y - mean) * lax.rsqrt(var + GN_EPS)).reshape(b, t, D_MODEL)
    y = (y * gn_w.astype(jnp.float32) + gn_b.astype(jnp.float32)).astype(dtype)
    return (y * g) @ w_o


def _rwkv_mixer(xn, n_rows, xnc, ctx_out, mu, w_rkv, w0, w1, w2, a0, a1, a2, g1, g2,
                k_k, k_a, r_k, gn_w, gn_b, w_o):
    ctx_len = xnc.shape[1]
    xx_l, r_l, k_l, v_l, kk_l, g_l = _rwkv_shared(xn, n_rows, GRID_W, mu, w_rkv, g1, g2, k_k)
    xx_c, r_c, k_c, v_c, kk_c, g_c = _rwkv_shared(xnc, 1, ctx_len, mu, w_rkv, g1, g2, k_k)
    s0 = jnp.zeros((xn.shape[0], N_HEADS, HEAD_DIM, HEAD_DIM), jnp.float32)
    y_lat, y_ctx = [], []
    for p, reverse in enumerate((False, True)):
        dec_l, a_l, kd_l = _rwkv_direction(xn, xx_l, mu, k_l, w0[p], w1[p], w2[p], a0[p], a1[p], a2[p], k_a)
        dec_c, a_c, kd_c = _rwkv_direction(xnc, xx_c, mu, k_c, w0[p], w1[p], w2[p], a0[p], a1[p], a2[p], k_a)
        s_ctx, yc = _wkv_scan(dec_c, kd_c, v_c, kk_c, a_c, r_c if ctx_out else None, s0, reverse)
        _, yl = _wkv_scan(dec_l, kd_l, v_l, kk_l, a_l, r_l, s_ctx, reverse)
        y_lat.append(yl + _bonus(r_l, kd_l, v_l, r_k))
        if ctx_out:
            y_ctx.append(yc + _bonus(r_c, kd_c, v_c, r_k))
    out_lat = _rwkv_readout(y_lat[0] + y_lat[1], g_l, gn_w, gn_b, w_o, xn.dtype)
    out_ctx = _rwkv_readout(y_ctx[0] + y_ctx[1], g_c, gn_w, gn_b, w_o, xnc.dtype) if ctx_out else None
    return out_lat, out_ctx


def _short_conv(xn, n_rows, row_len, w_in, conv_w, w_out):
    bg, cg, xin = jnp.split(xn @ w_in, 3, axis=-1)
    u = cg * xin
    prev, nxt = _row_neighbours(u, n_rows, row_len)
    conv = conv_w[0] * prev + conv_w[1] * u + conv_w[2] * nxt
    return (bg * conv) @ w_out


def _moe(h, router_w, router_b, w_gate, w_up, w_down):
    b, t, d = h.shape
    hf = h.reshape(b * t, d)
    s = jax.nn.sigmoid(jnp.einsum('nd,de->ne', hf.astype(jnp.float32), router_w.astype(jnp.float32)))
    sel = (s + router_b.astype(jnp.float32)).reshape(-1, N_GROUPS, EXPERTS_PER_GROUP)
    group_score = jnp.sum(lax.top_k(sel, TOP_K)[0], axis=-1)
    g_idx = jnp.argmax(group_score, axis=-1)
    in_group = jnp.take_along_axis(sel, g_idx[:, None, None], axis=1)[:, 0]
    _, local = lax.top_k(in_group, TOP_K)
    experts = g_idx[:, None] * EXPERTS_PER_GROUP + local
    gate = jnp.take_along_axis(s, experts, axis=1)
    gate = gate / jnp.sum(gate, axis=-1, keepdims=True)
    combine = jnp.einsum('nk,nke->ne', gate, jax.nn.one_hot(experts, N_EXPERTS, dtype=jnp.float32)).astype(h.dtype)
    out = jnp.zeros_like(hf)
    for e in range(N_EXPERTS):
        he = jax.nn.silu(hf @ w_gate[e]) * (hf @ w_up[e])
        out = out + combine[:, e:e + 1] * (he @ w_down[e])
    return out.reshape(b, t, d)


def setup_inputs(seed: int = 0) -> dict:
    key = jax.random.key(seed)
    ks = iter(jax.random.split(key, 48))

    def nrm(shape, scale):
        return scale * jax.random.normal(next(ks), shape, jnp.float32)

    def uni(shape, lo, hi):
        return jax.random.uniform(next(ks), shape, jnp.float32, lo, hi)

    d, na, nb = D_MODEL, N_RWKV, N_SCONV
    inv = d ** -0.5
    return {
        'x': nrm((BATCH, SEQ, d), 1.0),
        'c': nrm((BATCH, d), 1.0),
        'ctx': nrm((BATCH, CTX_LEN, d), 1.0),
        'c_ctx': nrm((d,), 1.0),
        'ada_w': nrm((DEPTH, d, 6 * d), 0.5 * inv),
        'ada_b': nrm((DEPTH, 6 * d), 0.01),
        'norm_g': 1.0 + nrm((DEPTH, 2, d), 0.02),
        'rw_mu': uni((na, 6, d), 0.0, 1.0),
        'rw_w_rkv': nrm((na, 3, d, d), inv),
        'rw_w0': uni((na, 2, d), -6.0, -1.0),
        'rw_w1': nrm((na, 2, d, DECAY_LORA), inv),
        'rw_w2': nrm((na, 2, DECAY_LORA, d), 0.1 * DECAY_LORA ** -0.5),
        'rw_a0': nrm((na, 2, d), 0.1),
        'rw_a1': nrm((na, 2, d, AAA_LORA), inv),
        'rw_a2': nrm((na, 2, AAA_LORA, d), 0.5 * AAA_LORA ** -0.5),
        'rw_g1': nrm((na, d, GATE_LORA), inv),
        'rw_g2': nrm((na, GATE_LORA, d), GATE_LORA ** -0.5),
        'rw_k_k': 0.85 + nrm((na, d), 0.05),
        'rw_k_a': 1.0 + nrm((na, d), 0.05),
        'rw_r_k': nrm((na, N_HEADS, HEAD_DIM), 0.1),
        'rw_gn_w': 1.0 + nrm((na, d), 0.02),
        'rw_gn_b': nrm((na, d), 0.01),
        'rw_w_o': nrm((na, d, d), inv),
        'sc_w_in': nrm((nb, d, 3 * d), inv),
        'sc_conv': nrm((nb, CONV_W, d), CONV_W ** -0.5),
        'sc_w_out': nrm((nb, d, d), inv),
        'router_w': nrm((d, N_EXPERTS), inv),
        'router_b': nrm((N_EXPERTS,), 0.01),
        'moe_w_gate': nrm((DEPTH, N_EXPERTS, d, EXPERT_DIM), inv),
        'moe_w_up': nrm((DEPTH, N_EXPERTS, d, EXPERT_DIM), inv),
        'moe_w_down': nrm((DEPTH, N_EXPERTS, EXPERT_DIM, d), EXPERT_DIM ** -0.5),
        'final_g': 1.0 + nrm((d,), 0.02),
    }


def reference(x, c, ctx, c_ctx, ada_w, ada_b, norm_g, rw_mu, rw_w_rkv, rw_w0, rw_w1, rw_w2,
              rw_a0, rw_a1, rw_a2, rw_g1, rw_g2, rw_k_k, rw_k_a, rw_r_k, rw_gn_w, rw_gn_b, rw_w_o,
              sc_w_in, sc_conv, sc_w_out, router_w, router_b, moe_w_gate, moe_w_up, moe_w_down,
              final_g):
    rows = x.shape[1] // GRID_W
    ctx_len = ctx.shape[1]
    mod_lat = jnp.einsum('bd,lde->lbe', jax.nn.silu(c), ada_w) + ada_b[:, None, :]
    mod_ctx = jnp.einsum('d,lde->le', jax.nn.silu(c_ctx), ada_w) + ada_b
    h, hc = x, ctx
    for i in range(DEPTH):
        kind = i % N_MIXERS
        idx = i // N_MIXERS
        ctx_live = any(j % N_MIXERS == 0 for j in range(i + 1, DEPTH))
        sh1, sc1, gt1, sh2, sc2, gt2 = jnp.split(mod_lat[i][:, None, :], 6, axis=-1)
        csh1, csc1, cgt1, csh2, csc2, cgt2 = jnp.split(mod_ctx[i][None, None, :], 6, axis=-1)
        xn = _modulate(_rmsnorm(h, norm_g[i, 0]), sh1, sc1)
        xnc = _modulate(_rmsnorm(hc, norm_g[i, 0]), csh1, csc1) if (kind == 0 or ctx_live) else None
        if kind == 0:
            y, yc = _rwkv_mixer(xn, rows, xnc, ctx_live, rw_mu[idx], rw_w_rkv[idx], rw_w0[idx],
                                rw_w1[idx], rw_w2[idx], rw_a0[idx], rw_a1[idx], rw_a2[idx],
                                rw_g1[idx], rw_g2[idx], rw_k_k[idx], rw_k_a[idx], rw_r_k[idx],
                                rw_gn_w[idx], rw_gn_b[idx], rw_w_o[idx])
        else:
            y = _short_conv(xn, rows, GRID_W, sc_w_in[idx], sc_conv[idx], sc_w_out[idx])
            yc = _short_conv(xnc, 1, ctx_len, sc_w_in[idx], sc_conv[idx], sc_w_out[idx]) if ctx_live else None
        h = h + gt1 * y
        h = h + gt2 * _moe(_modulate(_rmsnorm(h, norm_g[i, 1]), sh2, sc2), router_w, router_b,
                           moe_w_gate[i], moe_w_up[i], moe_w_down[i])
        if ctx_live:
            hc = hc + cgt1 * yc
            hc = hc + cgt2 * _moe(_modulate(_rmsnorm(hc, norm_g[i, 1]), csh2, csc2), router_w, router_b,
                                  moe_w_gate[i], moe_w_up[i], moe_w_down[i])
    return _rmsnorm(h, final_g)
```

```python
import functools

import jax
import jax.numpy as jnp
from jax import lax
from jax.experimental import pallas as pl
from jax.experimental.pallas import tpu as pltpu

F32 = jnp.float32
BF16 = jnp.bfloat16

D_MODEL = 1024
HEAD_DIM = 64
N_HEADS = D_MODEL // HEAD_DIM
LANES = 128
N_PAIRS = D_MODEL // LANES
CHUNK = 64
N_EXPERTS = 16
EXPERTS_PER_GROUP = 4
N_GROUPS = N_EXPERTS // EXPERTS_PER_GROUP
RMS_EPS = 1e-6
GN_EPS = 64e-5
L2_EPS = 1e-12

TOKEN_TILE = 256
MOE_TOKEN_TILE = 512
PAIRS_PER_STEP = 2
ADA_COL_TILE = 1536
VMEM_LIMIT_BYTES = 56 * 1024 * 1024


def _sigmoid(x):
    return 1.0 / (1.0 + jnp.exp(-x))


def _mm(a, b):
    return jnp.dot(a.astype(BF16), b.astype(BF16), preferred_element_type=F32)


def _mm_nt(a, b):
    return lax.dot_general(a.astype(BF16), b.astype(BF16), (((1,), (1,)), ((), ())),
                           preferred_element_type=F32)


def _split3(x):
    hi = x.astype(BF16)
    r1 = x - hi.astype(F32)
    mid = r1.astype(BF16)
    lo = (r1 - mid.astype(F32)).astype(BF16)
    return hi, mid, lo


def _head_sum(x):
    rows = x.shape[0]
    left = lax.broadcasted_iota(jnp.int32, (rows, LANES), 1) < HEAD_DIM
    outs = []
    for j in range(N_PAIRS):
        xb = x[:, LANES * j:LANES * (j + 1)]
        sa = jnp.sum(jnp.where(left, xb, 0.0), axis=-1, keepdims=True)
        sb = jnp.sum(jnp.where(left, 0.0, xb), axis=-1, keepdims=True)
        outs.append(jnp.where(left, sa, sb))
    return jnp.concatenate(outs, axis=-1)


def _rms_modulate(x, g, shift, scale):
    ms = jnp.mean(x * x, axis=-1, keepdims=True)
    xn = x * lax.rsqrt(ms + RMS_EPS) * g
    return xn * (1.0 + scale) + shift


def _row_neighbours(x, row_len):
    rows = x.shape[0]
    pos = lax.broadcasted_iota(jnp.int32, x.shape, 0) & (row_len - 1)
    prev = jnp.where(pos == 0, 0.0, pltpu.roll(x, 1, 0))
    nxt = jnp.where(pos == row_len - 1, 0.0, pltpu.roll(x, rows - 1, 0))
    return prev, nxt


def _ada_kernel(c_ref, w_ref, b_ref, o_ref):
    c = c_ref[...]
    s = c * _sigmoid(c)
    o_ref[0] = _mm(s, w_ref[0]) + b_ref[0]


def _ada(cc, ada_w, ada_b):
    depth, d, n = ada_w.shape
    rows = cc.shape[0]
    return pl.pallas_call(
        _ada_kernel,
        grid=(depth, n // ADA_COL_TILE),
        in_specs=[
            pl.BlockSpec((rows, d), lambda l, j: (0, 0)),
            pl.BlockSpec((1, d, ADA_COL_TILE), lambda l, j: (l, 0, j)),
            pl.BlockSpec((1, 1, ADA_COL_TILE), lambda l, j: (l, 0, j)),
        ],
        out_specs=pl.BlockSpec((1, rows, ADA_COL_TILE), lambda l, j: (l, 0, j)),
        out_shape=jax.ShapeDtypeStruct((depth, rows, n), F32),
        compiler_params=pltpu.CompilerParams(
            dimension_semantics=("arbitrary", "arbitrary"), vmem_limit_bytes=VMEM_LIMIT_BYTES),
        name="ada",
    )(cc, ada_w, ada_b.reshape(depth, 1, n))


def _rwkv_pre_kernel(ctx_len, x_ref, sh_ref, sc_ref, ng_ref, mu_ref, wrkv_ref, w1_ref, w2_ref, w0_ref,
                     a1_ref, a2_ref, a0_ref, g1_ref, g2_ref, kk_ref, ka_ref, rk_ref,
                     r_out, v_out, al_out, g_out, bon_out, lw0_out, lw1_out, kd0_out, kd1_out,
                     b0_out, b1_out):
    j = pl.program_id(1)
    row_len = jnp.where(j == 0, ctx_len, CHUNK)
    xn = _rms_modulate(x_ref[0], ng_ref[...], sh_ref[0], sc_ref[0])
    prev, nxt = _row_neighbours(xn, row_len)
    xx = 0.5 * (prev + nxt) - xn

    def mix(i):
        return xn + xx * mu_ref[i:i + 1, :]

    r = _mm(mix(0), wrkv_ref[0])
    k = _mm(mix(1), wrkv_ref[1])
    v = _mm(mix(2), wrkv_ref[2])
    r_out[0] = r.astype(r_out.dtype)
    v_out[0] = v.astype(v_out.dtype)
    g_out[0] = _mm(_sigmoid(_mm(mix(5), g1_ref[...])), g2_ref[...]).astype(g_out.dtype)

    kk = k * kk_ref[...]
    kk = kk * lax.rsqrt(jnp.maximum(_head_sum(kk * kk), L2_EPS * L2_EPS))
    al_out[0] = (-kk).astype(al_out.dtype)

    wl = jnp.tanh(_mm(mix(3), w1_ref[...]))
    al = _mm(mix(4), a1_ref[...])
    bonus_dot = None
    for p, (lw_out, kd_out, b_out) in enumerate(((lw0_out, kd0_out, b0_out), (lw1_out, kd1_out, b1_out))):
        z = w0_ref[p:p + 1, :] + _mm(wl, w2_ref[p])
        lw_out[0] = -jnp.exp(-0.5) * _sigmoid(z)
        a = _sigmoid(a0_ref[p:p + 1, :] + _mm(al, a2_ref[p]))
        kd = k * (1.0 + (a - 1.0) * ka_ref[...])
        kd_out[0] = kd.astype(kd_out.dtype)
        b_out[0] = (kk * a).astype(b_out.dtype)
        t = r * kd * rk_ref[...]
        bonus_dot = t if bonus_dot is None else bonus_dot + t
    bon_out[0] = (_head_sum(bonus_dot) * v).astype(bon_out.dtype)


def _rwkv_pre(xall, mod, n_batch, ctx_len, norm_g, mu, w_rkv, w1, w2, w0, a1, a2, a0, g1, g2, k_k, k_a, r_k):
    b, tt, d = xall.shape
    assert ctx_len == TOKEN_TILE and tt % TOKEN_TILE == 0

    def mod_spec(part):
        return pl.BlockSpec((1, 1, d), lambda i, j: (jnp.where(j == 0, n_batch, i), 0, part))

    def full(a):
        nd = a.ndim
        return pl.BlockSpec(a.shape, lambda i, j: (0,) * nd)

    tile = pl.BlockSpec((1, TOKEN_TILE, d), lambda i, j: (i, j, 0))
    params = (norm_g, mu, w_rkv, w1, w2, w0, a1, a2, a0, g1, g2, k_k, k_a, r_k)
    out_dtypes = (BF16,) * 5 + (F32, F32) + (BF16,) * 4
    return pl.pallas_call(
        functools.partial(_rwkv_pre_kernel, ctx_len),
        grid=(b, tt // TOKEN_TILE),
        in_specs=[tile, mod_spec(0), mod_spec(1)] + [full(a) for a in params],
        out_specs=[tile] * len(out_dtypes),
        out_shape=[jax.ShapeDtypeStruct((b, tt, d), dt) for dt in out_dtypes],
        compiler_params=pltpu.CompilerParams(
            dimension_semantics=("arbitrary", "arbitrary"), vmem_limit_bytes=VMEM_LIMIT_BYTES),
        name="rwkv_pre",
    )(xall, mod, mod, *params)


def _expand(x, left):
    return jnp.concatenate([jnp.where(left, x, 0.0), jnp.where(left, 0.0, x)], axis=0)


def _chunk_step(lw, r, kd, v, al, be, h_bd, reverse):
    c = CHUNK
    left = lax.broadcasted_iota(jnp.int32, (c, LANES), 1) < HEAD_DIM
    ti = lax.broadcasted_iota(jnp.int32, (c, c), 0)
    tj = lax.broadcasted_iota(jnp.int32, (c, c), 1)
    tri = jnp.where((tj >= ti) if reverse else (tj <= ti), 1.0, 0.0).astype(BF16)
    hi, mid, lo = _split3(lw)
    cs = (jnp.dot(tri, hi, preferred_element_type=F32) + jnp.dot(tri, mid, preferred_element_type=F32)
          + jnp.dot(tri, lo, preferred_element_type=F32))
    tot = cs[0:1, :] if reverse else cs[c - 1:c, :]
    e_neg = jnp.exp(-cs)
    e_rem = jnp.exp(tot - cs)
    a_e = _expand(al * jnp.exp(cs - lw), left)
    r_e = _expand(r * jnp.exp(cs), left)
    b_e = _expand(be * e_neg, left)
    k_e = _expand(kd * e_neg, left)
    v_e = _expand(v, left)
    bh_t = _expand(be * e_rem, left).T
    kh_t = _expand(kd * e_rem, left).T

    n2 = 2 * c
    sc = _mm_nt(jnp.concatenate([a_e, r_e], axis=0), jnp.concatenate([b_e, k_e], axis=0))
    ri = lax.broadcasted_iota(jnp.int32, (n2, n2), 0)
    ci = lax.broadcasted_iota(jnp.int32, (n2, n2), 1)
    rt, ct = ri & (c - 1), ci & (c - 1)
    strict = (ct > rt) if reverse else (ct < rt)
    incl = (ct >= rt) if reverse else (ct <= rt)
    n_ab = jnp.where(strict, sc[:n2, :n2], 0.0)
    a_ak = jnp.where(strict, sc[:n2, n2:], 0.0)
    a_rb = jnp.where(incl, sc[n2:, :n2], 0.0)
    a_rk = jnp.where(incl, sc[n2:, n2:], 0.0)

    eye = jnp.where(ri == ci, 1.0, 0.0)
    t_m = eye + n_ab
    p = _mm(n_ab, n_ab)
    order = 4
    while order < c:
        x = _mm(jnp.concatenate([p, t_m], axis=0), p)
        t_m = t_m + x[n2:]
        p = x[:n2]
        order *= 2
    t_m = t_m + _mm(t_m, p)

    xu = _mm(t_m, jnp.concatenate([a_e, _mm(a_ak, v_e)], axis=1))
    mg = _mm(bh_t, xu)
    m_bd = mg[:, :n2] + jnp.where(ri == ci, jnp.exp(tot), 0.0)
    g_bd = mg[:, n2:] + _mm(kh_t, v_e)
    ry = _mm(a_rb, xu)
    r_hat = r_e + ry[:, :n2]
    y_e = _mm(r_hat, h_bd) + ry[:, n2:] + _mm(a_rk, v_e)
    h_new = _mm(m_bd, h_bd) + g_bd
    return y_e[:c] + y_e[c:], h_new


def _wkv_kernel(rf_ref, vf_ref, af_ref, lwf_ref, kdf_ref, bf_ref,
                rr_ref, vr_ref, ar_ref, lwr_ref, kdr_ref, br_ref,
                yf_ref, yr_ref, hf_ref, hr_ref):
    @pl.when(pl.program_id(2) == 0)
    def _():
        hf_ref[...] = jnp.zeros_like(hf_ref)
        hr_ref[...] = jnp.zeros_like(hr_ref)

    for p in range(PAIRS_PER_STEP):
        sl = slice(p * LANES, (p + 1) * LANES)
        for reverse, (r_ref, v_ref, a_ref, lw_ref, kd_ref, b_ref, y_ref, h_ref) in enumerate((
                (rf_ref, vf_ref, af_ref, lwf_ref, kdf_ref, bf_ref, yf_ref, hf_ref),
                (rr_ref, vr_ref, ar_ref, lwr_ref, kdr_ref, br_ref, yr_ref, hr_ref))):
            y, h_new = _chunk_step(
                lw_ref[0, :, sl], r_ref[0, :, sl].astype(F32), kd_ref[0, :, sl].astype(F32),
                v_ref[0, :, sl].astype(F32), a_ref[0, :, sl].astype(F32), b_ref[0, :, sl].astype(F32),
                h_ref[p], bool(reverse))
            y_ref[0, :, sl] = y
            h_ref[p] = h_new


def _wkv(r, v, al, lw0, kd0, b0, lw1, kd1, b1, ctx_len):
    b, tt, d = r.shape
    n_chunks = tt // CHUNK
    n_ctx = ctx_len // CHUNK
    width = PAIRS_PER_STEP * LANES

    def fwd_map(i, j, s):
        return (i, s, j)

    def rev_map(i, j, s):
        return (i, jnp.where(s < n_ctx, n_ctx - 1 - s, n_chunks - 1 - (s - n_ctx)), j)

    fwd = pl.BlockSpec((1, CHUNK, width), fwd_map)
    rev = pl.BlockSpec((1, CHUNK, width), rev_map)
    return pl.pallas_call(
        _wkv_kernel,
        grid=(b, d // width, n_chunks),
        in_specs=[fwd] * 6 + [rev] * 6,
        out_specs=[fwd, rev],
        out_shape=[jax.ShapeDtypeStruct((b, tt, d), F32)] * 2,
        scratch_shapes=[pltpu.VMEM((PAIRS_PER_STEP, LANES, LANES), F32)] * 2,
        compiler_params=pltpu.CompilerParams(
            dimension_semantics=("arbitrary", "arbitrary", "arbitrary"), vmem_limit_bytes=VMEM_LIMIT_BYTES),
        name="wkv",
    )(r, v, al, lw0, kd0, b0, r, v, al, lw1, kd1, b1)


def _route(h1, ng, shift, scale, rwt_ref, rb_ref, hm_out, comb_out):
    hm = _rms_modulate(h1, ng, shift, scale)
    hm_out[0] = hm.astype(hm_out.dtype)
    w_hi, w_mid, _ = _split3(rwt_ref[...])
    h_hi, h_mid, _ = _split3(hm)
    nt = (((1,), (1,)), ((), ()))
    logits = (lax.dot_general(w_hi, h_hi, nt, preferred_element_type=F32)
              + lax.dot_general(w_hi, h_mid, nt, preferred_element_type=F32)
              + lax.dot_general(w_mid, h_hi, nt, preferred_element_type=F32))
    s = _sigmoid(logits)
    sel = s + rb_ref[...]
    a = [sel[e:e + 1, :] for e in range(N_EXPERTS)]
    best = best_idx = None
    for g in range(N_GROUPS):
        m = a[g * EXPERTS_PER_GROUP:(g + 1) * EXPERTS_PER_GROUP]
        score = None
        for i in range(EXPERTS_PER_GROUP):
            for k in range(i + 1, EXPERTS_PER_GROUP):
                pair = m[i] + m[k]
                score = pair if score is None else jnp.maximum(score, pair)
        if best is None:
            best, best_idx = score, jnp.zeros_like(score, dtype=jnp.int32)
        else:
            upd = score > best
            best_idx = jnp.where(upd, g, best_idx)
            best = jnp.where(upd, score, best)
    gates = []
    for e in range(N_EXPERTS):
        g = e // EXPERTS_PER_GROUP
        ahead = jnp.zeros_like(best_idx)
        for k in range(g * EXPERTS_PER_GROUP, (g + 1) * EXPERTS_PER_GROUP):
            if k == e:
                continue
            beats = (a[k] >= a[e]) if k < e else (a[k] > a[e])
            ahead = ahead + beats.astype(jnp.int32)
        chosen = (best_idx == g) & (ahead < 2)
        gates.append(jnp.where(chosen, s[e:e + 1, :], 0.0))
    denom = gates[0]
    for e in range(1, N_EXPERTS):
        denom = denom + gates[e]
    comb_out[...] = jnp.concatenate(gates, axis=0) / denom


def _readout_kernel(yf_ref, yr_ref, bon_ref, g_ref, h_ref, gt_ref, gnw_ref, gnb_ref, wo_ref,
                    ng_ref, sh_ref, sc_ref, rwt_ref, rb_ref, h1_out, hm_out, comb_out):
    y = yf_ref[0] + yr_ref[0] + bon_ref[0].astype(F32)
    mean = _head_sum(y) * (1.0 / HEAD_DIM)
    yc = y - mean
    var = _head_sum(yc * yc) * (1.0 / HEAD_DIM)
    yn = yc * lax.rsqrt(var + GN_EPS) * gnw_ref[...] + gnb_ref[...]
    out = _mm(yn * g_ref[0].astype(F32), wo_ref[...])
    h1 = h_ref[0] + gt_ref[0] * out
    h1_out[0] = h1
    _route(h1, ng_ref[...], sh_ref[0], sc_ref[0], rwt_ref, rb_ref, hm_out, comb_out)


def _mod_spec3(part, d):
    return pl.BlockSpec((1, 1, d), lambda i, j: (i, 0, part))


def _full2(a):
    nd = a.ndim
    return pl.BlockSpec(a.shape, lambda i, j: (0,) * nd)


def _readout(yf, yr, bonus, g, h, mod, ctx_len, gn_w, gn_b, w_o, norm_g, router_wt, router_b):
    b, t, d = h.shape
    off = ctx_len // TOKEN_TILE
    nt = t // TOKEN_TILE
    shifted = pl.BlockSpec((1, TOKEN_TILE, d), lambda i, j: (i, j + off, 0))
    tile = pl.BlockSpec((1, TOKEN_TILE, d), lambda i, j: (i, j, 0))
    params_a = (gn_w, gn_b, w_o, norm_g)
    params_b = (router_wt, router_b)
    return pl.pallas_call(
        _readout_kernel,
        grid=(b, nt),
        in_specs=[shifted] * 4 + [tile, _mod_spec3(2, d)] + [_full2(a) for a in params_a]
        + [_mod_spec3(3, d), _mod_spec3(4, d)] + [_full2(a) for a in params_b],
        out_specs=[tile, tile, pl.BlockSpec((N_EXPERTS, TOKEN_TILE), lambda i, j: (0, i * nt + j))],
        out_shape=[jax.ShapeDtypeStruct((b, t, d), F32), jax.ShapeDtypeStruct((b, t, d), BF16),
                   jax.ShapeDtypeStruct((N_EXPERTS, b * t), F32)],
        compiler_params=pltpu.CompilerParams(
            dimension_semantics=("arbitrary", "arbitrary"), vmem_limit_bytes=VMEM_LIMIT_BYTES),
        name="readout",
    )(yf, yr, bonus, g, h, mod, *params_a, mod, mod, *params_b)


def _sconv_kernel(h_ref, sh1_ref, sc1_ref, gt_ref, ng1_ref, win_ref, cw_ref, wout_ref,
                  ng2_ref, sh2_ref, sc2_ref, rwt_ref, rb_ref, h1_out, hm_out, comb_out):
    h = h_ref[0]
    d = h.shape[1]
    xn = _rms_modulate(h, ng1_ref[...], sh1_ref[0], sc1_ref[0]).astype(BF16)
    bg = jnp.dot(xn, win_ref[:, 0:d], preferred_element_type=F32)
    u = (jnp.dot(xn, win_ref[:, d:2 * d], preferred_element_type=F32)
         * jnp.dot(xn, win_ref[:, 2 * d:3 * d], preferred_element_type=F32))
    prev, nxt = _row_neighbours(u, CHUNK)
    conv = cw_ref[0:1, :] * prev + cw_ref[1:2, :] * u + cw_ref[2:3, :] * nxt
    h1 = h + gt_ref[0] * _mm(bg * conv, wout_ref[...])
    h1_out[0] = h1
    _route(h1, ng2_ref[...], sh2_ref[0], sc2_ref[0], rwt_ref, rb_ref, hm_out, comb_out)


def _sconv(h, mod, norm_g1, w_in, conv_w, w_out, norm_g2, router_wt, router_b):
    b, t, d = h.shape
    nt = t // TOKEN_TILE
    tile = pl.BlockSpec((1, TOKEN_TILE, d), lambda i, j: (i, j, 0))
    params_a = (norm_g1, w_in, conv_w, w_out, norm_g2)
    params_b = (router_wt, router_b)
    return pl.pallas_call(
        _sconv_kernel,
        grid=(b, nt),
        in_specs=[tile, _mod_spec3(0, d), _mod_spec3(1, d), _mod_spec3(2, d)] + [_full2(a) for a in params_a]
        + [_mod_spec3(3, d), _mod_spec3(4, d)] + [_full2(a) for a in params_b],
        out_specs=[tile, tile, pl.BlockSpec((N_EXPERTS, TOKEN_TILE), lambda i, j: (0, i * nt + j))],
        out_shape=[jax.ShapeDtypeStruct((b, t, d), F32), jax.ShapeDtypeStruct((b, t, d), BF16),
                   jax.ShapeDtypeStruct((N_EXPERTS, b * t), F32)],
        compiler_params=pltpu.CompilerParams(
            dimension_semantics=("arbitrary", "arbitrary"), vmem_limit_bytes=VMEM_LIMIT_BYTES),
        name="sconv",
    )(h, mod, mod, mod, *params_a, mod, mod, *params_b)


def _moe_kernel(final_norm, hm_ref, comb_ref, wg_ref, wu_ref, wd_ref, h1_ref, gt_ref, fg_ref, o_ref, acc_ref):
    e = pl.program_id(2)

    @pl.when(e == 0)
    def _():
        acc_ref[...] = jnp.zeros_like(acc_ref)

    hm = hm_ref[0]
    gate = jnp.dot(hm, wg_ref[0], preferred_element_type=F32)
    up = jnp.dot(hm, wu_ref[0], preferred_element_type=F32)
    comb = comb_ref[...]
    lane = lax.broadcasted_iota(jnp.int32, comb.shape, 1)
    ce = jnp.sum(jnp.where(lane == e, comb, 0.0), axis=-1, keepdims=True)
    he = gate * _sigmoid(gate) * up * ce
    acc_ref[...] += jnp.dot(he.astype(BF16), wd_ref[0], preferred_element_type=F32)

    @pl.when(e == pl.num_programs(2) - 1)
    def _():
        h2 = h1_ref[0] + gt_ref[0] * acc_ref[...]
        if final_norm:
            ms = jnp.mean(h2 * h2, axis=-1, keepdims=True)
            h2 = h2 * lax.rsqrt(ms + RMS_EPS) * fg_ref[...]
        o_ref[0] = h2


def _moe(hm, comb, w_gate, w_up, w_down, h1, mod, final_g, final_norm):
    b, t, d = h1.shape
    n_e, _, de = w_gate.shape
    nt = t // MOE_TOKEN_TILE
    tile = pl.BlockSpec((1, MOE_TOKEN_TILE, d), lambda i, j, e: (i, j, 0))
    return pl.pallas_call(
        functools.partial(_moe_kernel, final_norm),
        grid=(b, nt, n_e),
        in_specs=[
            tile,
            pl.BlockSpec((MOE_TOKEN_TILE, n_e), lambda i, j, e: (i * nt + j, 0)),
            pl.BlockSpec((1, d, de), lambda i, j, e: (e, 0, 0)),
            pl.BlockSpec((1, d, de), lambda i, j, e: (e, 0, 0)),
            pl.BlockSpec((1, de, d), lambda i, j, e: (e, 0, 0)),
            tile,
            pl.BlockSpec((1, 1, d), lambda i, j, e: (i, 0, 5)),
            pl.BlockSpec((1, d), lambda i, j, e: (0, 0)),
        ],
        out_specs=tile,
        out_shape=jax.ShapeDtypeStruct((b, t, d), F32),
        scratch_shapes=[pltpu.VMEM((MOE_TOKEN_TILE, d), F32)],
        compiler_params=pltpu.CompilerParams(
            dimension_semantics=("arbitrary", "arbitrary", "arbitrary"), vmem_limit_bytes=VMEM_LIMIT_BYTES),
        name="moe",
    )(hm, comb, w_gate, w_up, w_down, h1, mod, final_g)


def kernel(x, c, ctx, c_ctx, ada_w, ada_b, norm_g, rw_mu, rw_w_rkv, rw_w0, rw_w1, rw_w2, rw_a0, rw_a1, rw_a2, rw_g1, rw_g2, rw_k_k, rw_k_a, rw_r_k, rw_gn_w, rw_gn_b, rw_w_o, sc_w_in, sc_conv, sc_w_out, router_w, router_b, moe_w_gate, moe_w_up, moe_w_down, final_g):
    b, t, d = x.shape
    ctx_len = ctx.shape[1]
    depth = ada_w.shape[0]
    assert d == D_MODEL and depth == 2 and t % MOE_TOKEN_TILE == 0

    mod_rows = 16
    cc = jnp.concatenate([c, c_ctx[None, :], jnp.zeros((mod_rows - b - 1, d), F32)], axis=0)
    mod = _ada(cc, ada_w, ada_b).reshape(depth, mod_rows, 1, 6 * d)

    row = lambda a: a.reshape(1, d)
    router_wt = router_w.T
    router_bc = router_b.reshape(N_EXPERTS, 1)

    xall = jnp.concatenate([ctx, x], axis=1)
    w1 = jnp.concatenate([rw_w1[0, 0], rw_w1[0, 1]], axis=1).astype(BF16)
    a1 = jnp.concatenate([rw_a1[0, 0], rw_a1[0, 1]], axis=1).astype(BF16)

    def pad_dirs(w):
        z = jnp.zeros_like(w[0])
        return jnp.stack([jnp.concatenate([w[0], z], axis=0), jnp.concatenate([z, w[1]], axis=0)]).astype(BF16)

    (r, v, al, g, bonus, lw0, lw1, kd0, kd1, b0, b1) = _rwkv_pre(
        xall, mod[0], b, ctx_len, row(norm_g[0, 0]), rw_mu[0], rw_w_rkv[0].astype(BF16), w1,
        pad_dirs(rw_w2[0]), rw_w0[0], a1, pad_dirs(rw_a2[0]), rw_a0[0],
        rw_g1[0].astype(BF16), rw_g2[0].astype(BF16), row(rw_k_k[0]), row(rw_k_a[0]), row(rw_r_k[0]))
    yf, yr = _wkv(r, v, al, lw0, kd0, b0, lw1, kd1, b1, ctx_len)
    h1, hm, comb_t = _readout(yf, yr, bonus, g, x, mod[0], ctx_len, row(rw_gn_w[0]), row(rw_gn_b[0]),
                              rw_w_o[0].astype(BF16), row(norm_g[0, 1]), router_wt, router_bc)
    h2 = _moe(hm, comb_t.T, moe_w_gate[0].astype(BF16), moe_w_up[0].astype(BF16),
              moe_w_down[0].astype(BF16), h1, mod[0], row(final_g), False)

    h1, hm, comb_t = _sconv(h2, mod[1], row(norm_g[1, 0]), sc_w_in[0].astype(BF16), sc_conv[0],
                            sc_w_out[0].astype(BF16), row(norm_g[1, 1]), router_wt, router_bc)
    return _moe(hm, comb_t.T, moe_w_gate[1].astype(BF16), moe_w_up[1].astype(BF16),
                moe_w_down[1].astype(BF16), h1, mod[1], row(final_g), True)
```

```python
import functools

import jax
import jax.numpy as jnp
from jax import lax
from jax.experimental import pallas as pl
from jax.experimental.pallas import tpu as pltpu

F32 = jnp.float32
BF16 = jnp.bfloat16

D_MODEL = 1024
HEAD_DIM = 64
N_HEADS = D_MODEL // HEAD_DIM
LANES = 128
N_PAIRS = D_MODEL // LANES
CHUNK = 64
N_EXPERTS = 16
EXPERTS_PER_GROUP = 4
N_GROUPS = N_EXPERTS // EXPERTS_PER_GROUP
RMS_EPS = 1e-6
GN_EPS = 64e-5
L2_EPS = 1e-12

TOKEN_TILE = 256
MOE_TOKEN_TILE = 512
PAIRS_PER_STEP = 8
ADA_COL_TILE = 1536
VMEM_LIMIT_BYTES = 56 * 1024 * 1024


def _sigmoid(x):
    return 1.0 / (1.0 + jnp.exp(-x))


def _mm(a, b):
    return jnp.dot(a.astype(BF16), b.astype(BF16), preferred_element_type=F32)


def _mm_nt(a, b):
    return lax.dot_general(a.astype(BF16), b.astype(BF16), (((1,), (1,)), ((), ())),
                           preferred_element_type=F32)


def _split3(x):
    hi = x.astype(BF16)
    r1 = x - hi.astype(F32)
    mid = r1.astype(BF16)
    lo = (r1 - mid.astype(F32)).astype(BF16)
    return hi, mid, lo


def _head_sum(x):
    rows = x.shape[0]
    left = lax.broadcasted_iota(jnp.int32, (rows, LANES), 1) < HEAD_DIM
    outs = []
    for j in range(N_PAIRS):
        xb = x[:, LANES * j:LANES * (j + 1)]
        sa = jnp.sum(jnp.where(left, xb, 0.0), axis=-1, keepdims=True)
        sb = jnp.sum(jnp.where(left, 0.0, xb), axis=-1, keepdims=True)
        outs.append(jnp.where(left, sa, sb))
    return jnp.concatenate(outs, axis=-1)


def _rms_modulate(x, g, shift, scale):
    ms = jnp.mean(x * x, axis=-1, keepdims=True)
    xn = x * lax.rsqrt(ms + RMS_EPS) * g
    return xn * (1.0 + scale) + shift


def _row_neighbours(x, row_len):
    rows = x.shape[0]
    pos = lax.broadcasted_iota(jnp.int32, x.shape, 0) & (row_len - 1)
    prev = jnp.where(pos == 0, 0.0, pltpu.roll(x, 1, 0))
    nxt = jnp.where(pos == row_len - 1, 0.0, pltpu.roll(x, rows - 1, 0))
    return prev, nxt


def _ada_kernel(c_ref, w_ref, b_ref, o_ref):
    c = c_ref[...]
    s = c * _sigmoid(c)
    o_ref[0] = _mm(s, w_ref[0]) + b_ref[0]


def _ada(cc, ada_w, ada_b):
    depth, d, n = ada_w.shape
    rows = cc.shape[0]
    return pl.pallas_call(
        _ada_kernel,
        grid=(depth, n // ADA_COL_TILE),
        in_specs=[
            pl.BlockSpec((rows, d), lambda l, j: (0, 0)),
            pl.BlockSpec((1, d, ADA_COL_TILE), lambda l, j: (l, 0, j)),
            pl.BlockSpec((1, 1, ADA_COL_TILE), lambda l, j: (l, 0, j)),
        ],
        out_specs=pl.BlockSpec((1, rows, ADA_COL_TILE), lambda l, j: (l, 0, j)),
        out_shape=jax.ShapeDtypeStruct((depth, rows, n), F32),
        compiler_params=pltpu.CompilerParams(
            dimension_semantics=("arbitrary", "arbitrary"), vmem_limit_bytes=VMEM_LIMIT_BYTES),
        name="ada",
    )(cc, ada_w, ada_b.reshape(depth, 1, n))


def _rwkv_pre_kernel(ctx_len, x_ref, sh_ref, sc_ref, ng_ref, mu_ref, wrkv_ref, w1_ref, w2_ref, w0_ref,
                     a1_ref, a2_ref, a0_ref, g1_ref, g2_ref, kk_ref, ka_ref, rk_ref,
                     r_out, v_out, g_out, bon_out, cs0_out, cs1_out, al0_out, al1_out, kd0_out, kd1_out,
                     b0_out, b1_out):
    j = pl.program_id(1)
    row_len = jnp.where(j == 0, ctx_len, CHUNK)
    xn = _rms_modulate(x_ref[0], ng_ref[...], sh_ref[0], sc_ref[0])
    prev, nxt = _row_neighbours(xn, row_len)
    xx = 0.5 * (prev + nxt) - xn

    def mix(i):
        return xn + xx * mu_ref[i:i + 1, :]

    r = _mm(mix(0), wrkv_ref[0])
    k = _mm(mix(1), wrkv_ref[1])
    v = _mm(mix(2), wrkv_ref[2])
    r_out[0] = r.astype(r_out.dtype)
    v_out[0] = v.astype(v_out.dtype)
    g_out[0] = _mm(_sigmoid(_mm(mix(5), g1_ref[...])), g2_ref[...]).astype(g_out.dtype)

    kk = k * kk_ref[...]
    kk = kk * lax.rsqrt(jnp.maximum(_head_sum(kk * kk), L2_EPS * L2_EPS))

    wl = jnp.tanh(_mm(mix(3), w1_ref[...]))
    al = _mm(mix(4), a1_ref[...])
    rows = xn.shape[0]
    ti = lax.broadcasted_iota(jnp.int32, (rows, rows), 0)
    tj = lax.broadcasted_iota(jnp.int32, (rows, rows), 1)
    same_chunk = (ti ^ tj) < CHUNK
    bonus_dot = None
    for p, (cs_out, al_out, kd_out, b_out) in enumerate(((cs0_out, al0_out, kd0_out, b0_out),
                                                         (cs1_out, al1_out, kd1_out, b1_out))):
        z = w0_ref[p:p + 1, :] + _mm(wl, w2_ref[p])
        lw = -jnp.exp(-0.5) * _sigmoid(z)
        tri = jnp.where(same_chunk & ((tj >= ti) if p else (tj <= ti)), 1.0, 0.0).astype(BF16)
        hi, mid, lo = _split3(lw)
        cs_out[0] = (jnp.dot(tri, hi, preferred_element_type=F32) + jnp.dot(tri, mid, preferred_element_type=F32)
                     + jnp.dot(tri, lo, preferred_element_type=F32))
        al_out[0] = (-kk * jnp.exp(-lw)).astype(al_out.dtype)
        a = _sigmoid(a0_ref[p:p + 1, :] + _mm(al, a2_ref[p]))
        kd = k * (1.0 + (a - 1.0) * ka_ref[...])
        kd_out[0] = kd.astype(kd_out.dtype)
        b_out[0] = (kk * a).astype(b_out.dtype)
        t = r * kd * rk_ref[...]
        bonus_dot = t if bonus_dot is None else bonus_dot + t
    bon_out[0] = (_head_sum(bonus_dot) * v).astype(bon_out.dtype)


def _rwkv_pre(xall, mod, n_batch, ctx_len, norm_g, mu, w_rkv, w1, w2, w0, a1, a2, a0, g1, g2, k_k, k_a, r_k):
    b, tt, d = xall.shape
    assert ctx_len == TOKEN_TILE and tt % TOKEN_TILE == 0

    def mod_spec(part):
        return pl.BlockSpec((1, 1, d), lambda i, j: (jnp.where(j == 0, n_batch, i), 0, part))

    def full(a):
        nd = a.ndim
        return pl.BlockSpec(a.shape, lambda i, j: (0,) * nd)

    tile = pl.BlockSpec((1, TOKEN_TILE, d), lambda i, j: (i, j, 0))
    params = (norm_g, mu, w_rkv, w1, w2, w0, a1, a2, a0, g1, g2, k_k, k_a, r_k)
    out_dtypes = (BF16,) * 4 + (F32, F32) + (BF16,) * 6
    return pl.pallas_call(
        functools.partial(_rwkv_pre_kernel, ctx_len),
        grid=(b, tt // TOKEN_TILE),
        in_specs=[tile, mod_spec(0), mod_spec(1)] + [full(a) for a in params],
        out_specs=[tile] * len(out_dtypes),
        out_shape=[jax.ShapeDtypeStruct((b, tt, d), dt) for dt in out_dtypes],
        compiler_params=pltpu.CompilerParams(
            dimension_semantics=("arbitrary", "arbitrary"), vmem_limit_bytes=VMEM_LIMIT_BYTES),
        name="rwkv_pre",
    )(xall, mod, mod, *params)


def _expand(x, left):
    return jnp.concatenate([jnp.where(left, x, 0.0), jnp.where(left, 0.0, x)], axis=0)


def _dot(a, b):
    return jnp.dot(a, b, preferred_element_type=F32)


def _chunk_steps(chains):
    assert CHUNK == HEAD_DIM
    c = CHUNK
    lane = lax.broadcasted_iota(jnp.int32, (c, LANES), 1)
    left = lane < HEAD_DIM
    tt = lax.broadcasted_iota(jnp.int32, (c, LANES), 0)
    jj = lane & (c - 1)
    diag = jj == tt
    tri = {False: (jj < tt, jj <= tt), True: (jj > tt, jj >= tt)}

    def bd(x):
        return _expand(x, left).astype(BF16)

    def fold_t(x):
        xt = _expand(x, left).T
        return xt[:c] + xt[c:]

    def rows(*xs):
        return jnp.concatenate(xs, axis=0).astype(BF16)

    prep = []
    for cs, alw, r, kd, v, be, h, reverse in chains:
        tot = cs[0:1, :] if reverse else cs[c - 1:c, :]
        e_pos, e_neg, e_rem = jnp.exp(cs), jnp.exp(-cs), jnp.exp(tot - cs)
        at, rt = alw * e_pos, r * e_pos
        prep.append(dict(
            lhs=rows(at, rt), rhs=jnp.concatenate([bd(be * e_neg), bd(kd * e_neg)], axis=0),
            a_e=bd(at), rt=rt, v_e=bd(v), bh_t=fold_t(be * e_rem), kh_t=fold_t(kd * e_rem),
            g_tot=jnp.exp(tot), h_e=bd(h), strict=tri[reverse][0], incl=tri[reverse][1]))

    for q in prep:
        sc = lax.dot_general(q["lhs"], q["rhs"], (((1,), (1,)), ((), ())), preferred_element_type=F32)
        q["n"] = jnp.where(q["strict"], sc[:c, :LANES], 0.0)
        a_ak = jnp.where(q["strict"], sc[:c, LANES:], 0.0)
        q["a_rb"] = jnp.where(q["incl"], sc[c:, :LANES], 0.0)
        a_rk = jnp.where(q["incl"], sc[c:, LANES:], 0.0)
        q["vv"] = _dot(rows(a_ak, q["kh_t"], a_rk), q["v_e"])

    for q in prep:
        q["t"] = jnp.where(diag, 1.0, 0.0) + q["n"]
        q["p"] = _dot(q["n"].astype(BF16), bd(q["n"]))
    order = 4
    while order < c:
        for q in prep:
            x = _dot(rows(q["p"], q["t"]), bd(q["p"]))
            q["p"] = x[:c]
            q["t"] = q["t"] + x[c:]
        order *= 2
    for q in prep:
        q["t"] = q["t"] + _dot(q["t"].astype(BF16), bd(q["p"]))

    for q in prep:
        xu = _dot(q["t"].astype(BF16), jnp.concatenate([q["a_e"], bd(q["vv"][:c])], axis=1))
        q["au_e"] = jnp.concatenate([bd(xu[:, :LANES]), bd(xu[:, LANES:])], axis=1)
    for q in prep:
        z = _dot(rows(q["bh_t"], q["a_rb"]), q["au_e"])
        q["m"] = z[:c, :LANES] + jnp.where(diag, q["g_tot"], 0.0)
        q["g"] = z[:c, LANES:] + q["vv"][c:2 * c]
        q["r_hat"] = q["rt"] + z[c:, :LANES]
        q["yi"] = z[c:, LANES:] + q["vv"][2 * c:]
    outs = []
    for q in prep:
        o = _dot(rows(q["r_hat"], q["m"]), q["h_e"])
        outs.append((o[:c] + q["yi"], o[c:] + q["g"]))
    return outs


def _wkv_kernel(rf_ref, vf_ref, csf_ref, af_ref, kdf_ref, bf_ref,
                rr_ref, vr_ref, csr_ref, ar_ref, kdr_ref, br_ref,
                yf_ref, yr_ref, hf_ref, hr_ref):
    @pl.when(pl.program_id(2) == 0)
    def _():
        hf_ref[...] = jnp.zeros_like(hf_ref)
        hr_ref[...] = jnp.zeros_like(hr_ref)

    dirs = ((rf_ref, vf_ref, csf_ref, af_ref, kdf_ref, bf_ref, yf_ref, hf_ref),
            (rr_ref, vr_ref, csr_ref, ar_ref, kdr_ref, br_ref, yr_ref, hr_ref))
    chains, dests = [], []
    for p in range(PAIRS_PER_STEP):
        sl = slice(p * LANES, (p + 1) * LANES)
        for reverse, (r_ref, v_ref, cs_ref, a_ref, kd_ref, b_ref, y_ref, h_ref) in enumerate(dirs):
            chains.append((cs_ref[0, :, sl], a_ref[0, :, sl].astype(F32), r_ref[0, :, sl].astype(F32),
                           kd_ref[0, :, sl].astype(F32), v_ref[0, :, sl].astype(F32),
                           b_ref[0, :, sl].astype(F32), h_ref[p], bool(reverse)))
            dests.append((y_ref, h_ref, p, sl))
    for (y, h_new), (y_ref, h_ref, p, sl) in zip(_chunk_steps(chains), dests):
        y_ref[0, :, sl] = y
        h_ref[p] = h_new


def _wkv(r, v, cs0, al0, kd0, b0, cs1, al1, kd1, b1, ctx_len):
    b, tt, d = r.shape
    n_chunks = tt // CHUNK
    n_ctx = ctx_len // CHUNK
    width = PAIRS_PER_STEP * LANES

    def fwd_map(i, j, s):
        return (i, s, j)

    def rev_map(i, j, s):
        return (i, jnp.where(s < n_ctx, n_ctx - 1 - s, n_chunks - 1 - (s - n_ctx)), j)

    fwd = pl.BlockSpec((1, CHUNK, width), fwd_map)
    rev = pl.BlockSpec((1, CHUNK, width), rev_map)
    return pl.pallas_call(
        _wkv_kernel,
        grid=(b, d // width, n_chunks),
        in_specs=[fwd] * 6 + [rev] * 6,
        out_specs=[fwd, rev],
        out_shape=[jax.ShapeDtypeStruct((b, tt, d), F32)] * 2,
        scratch_shapes=[pltpu.VMEM((PAIRS_PER_STEP, HEAD_DIM, LANES), F32)] * 2,
        compiler_params=pltpu.CompilerParams(
            dimension_semantics=("arbitrary", "arbitrary", "arbitrary"), vmem_limit_bytes=VMEM_LIMIT_BYTES),
        name="wkv",
    )(r, v, cs0, al0, kd0, b0, r, v, cs1, al1, kd1, b1)


def _route(h1, ng, shift, scale, rwt_ref, rb_ref, hm_out, comb_out):
    hm = _rms_modulate(h1, ng, shift, scale)
    hm_out[0] = hm.astype(hm_out.dtype)
    w_hi, w_mid, _ = _split3(rwt_ref[...])
    h_hi, h_mid, _ = _split3(hm)
    nt = (((1,), (1,)), ((), ()))
    logits = (lax.dot_general(w_hi, h_hi, nt, preferred_element_type=F32)
              + lax.dot_general(w_hi, h_mid, nt, preferred_element_type=F32)
              + lax.dot_general(w_mid, h_hi, nt, preferred_element_type=F32))
    s = _sigmoid(logits)
    sel = s + rb_ref[...]
    a = [sel[e:e + 1, :] for e in range(N_EXPERTS)]
    best = best_idx = None
    for g in range(N_GROUPS):
        m = a[g * EXPERTS_PER_GROUP:(g + 1) * EXPERTS_PER_GROUP]
        score = None
        for i in range(EXPERTS_PER_GROUP):
            for k in range(i + 1, EXPERTS_PER_GROUP):
                pair = m[i] + m[k]
                score = pair if score is None else jnp.maximum(score, pair)
        if best is None:
            best, best_idx = score, jnp.zeros_like(score, dtype=jnp.int32)
        else:
            upd = score > best
            best_idx = jnp.where(upd, g, best_idx)
            best = jnp.where(upd, score, best)
    gates = []
    for e in range(N_EXPERTS):
        g = e // EXPERTS_PER_GROUP
        ahead = jnp.zeros_like(best_idx)
        for k in range(g * EXPERTS_PER_GROUP, (g + 1) * EXPERTS_PER_GROUP):
            if k == e:
                continue
            beats = (a[k] >= a[e]) if k < e else (a[k] > a[e])
            ahead = ahead + beats.astype(jnp.int32)
        chosen = (best_idx == g) & (ahead < 2)
        gates.append(jnp.where(chosen, s[e:e + 1, :], 0.0))
    denom = gates[0]
    for e in range(1, N_EXPERTS):
        denom = denom + gates[e]
    comb_out[...] = jnp.concatenate(gates, axis=0) / denom


def _readout_kernel(yf_ref, yr_ref, bon_ref, g_ref, h_ref, gt_ref, gnw_ref, gnb_ref, wo_ref,
                    ng_ref, sh_ref, sc_ref, rwt_ref, rb_ref, h1_out, hm_out, comb_out):
    y = yf_ref[0] + yr_ref[0] + bon_ref[0].astype(F32)
    mean = _head_sum(y) * (1.0 / HEAD_DIM)
    yc = y - mean
    var = _head_sum(yc * yc) * (1.0 / HEAD_DIM)
    yn = yc * lax.rsqrt(var + GN_EPS) * gnw_ref[...] + gnb_ref[...]
    out = _mm(yn * g_ref[0].astype(F32), wo_ref[...])
    h1 = h_ref[0] + gt_ref[0] * out
    h1_out[0] = h1
    _route(h1, ng_ref[...], sh_ref[0], sc_ref[0], rwt_ref, rb_ref, hm_out, comb_out)


def _mod_spec3(part, d):
    return pl.BlockSpec((1, 1, d), lambda i, j: (i, 0, part))


def _full2(a):
    nd = a.ndim
    return pl.BlockSpec(a.shape, lambda i, j: (0,) * nd)


def _readout(yf, yr, bonus, g, h, mod, ctx_len, gn_w, gn_b, w_o, norm_g, router_wt, router_b):
    b, t, d = h.shape
    off = ctx_len // TOKEN_TILE
    nt = t // TOKEN_TILE
    shifted = pl.BlockSpec((1, TOKEN_TILE, d), lambda i, j: (i, j + off, 0))
    tile = pl.BlockSpec((1, TOKEN_TILE, d), lambda i, j: (i, j, 0))
    params_a = (gn_w, gn_b, w_o, norm_g)
    params_b = (router_wt, router_b)
    return pl.pallas_call(
        _readout_kernel,
        grid=(b, nt),
        in_specs=[shifted] * 4 + [tile, _mod_spec3(2, d)] + [_full2(a) for a in params_a]
        + [_mod_spec3(3, d), _mod_spec3(4, d)] + [_full2(a) for a in params_b],
        out_specs=[tile, tile, pl.BlockSpec((N_EXPERTS, TOKEN_TILE), lambda i, j: (0, i * nt + j))],
        out_shape=[jax.ShapeDtypeStruct((b, t, d), F32), jax.ShapeDtypeStruct((b, t, d), BF16),
                   jax.ShapeDtypeStruct((N_EXPERTS, b * t), F32)],
        compiler_params=pltpu.CompilerParams(
            dimension_semantics=("arbitrary", "arbitrary"), vmem_limit_bytes=VMEM_LIMIT_BYTES),
        name="readout",
    )(yf, yr, bonus, g, h, mod, *params_a, mod, mod, *params_b)


def _sconv_kernel(h_ref, sh1_ref, sc1_ref, gt_ref, ng1_ref, win_ref, cw_ref, wout_ref,
                  ng2_ref, sh2_ref, sc2_ref, rwt_ref, rb_ref, h1_out, hm_out, comb_out):
    h = h_ref[0]
    d = h.shape[1]
    xn = _rms_modulate(h, ng1_ref[...], sh1_ref[0], sc1_ref[0]).astype(BF16)
    bg = jnp.dot(xn, win_ref[:, 0:d], preferred_element_type=F32)
    u = (jnp.dot(xn, win_ref[:, d:2 * d], preferred_element_type=F32)
         * jnp.dot(xn, win_ref[:, 2 * d:3 * d], preferred_element_type=F32))
    prev, nxt = _row_neighbours(u, CHUNK)
    conv = cw_ref[0:1, :] * prev + cw_ref[1:2, :] * u + cw_ref[2:3, :] * nxt
    h1 = h + gt_ref[0] * _mm(bg * conv, wout_ref[...])
    h1_out[0] = h1
    _route(h1, ng2_ref[...], sh2_ref[0], sc2_ref[0], rwt_ref, rb_ref, hm_out, comb_out)


def _sconv(h, mod, norm_g1, w_in, conv_w, w_out, norm_g2, router_wt, router_b):
    b, t, d = h.shape
    nt = t // TOKEN_TILE
    tile = pl.BlockSpec((1, TOKEN_TILE, d), lambda i, j: (i, j, 0))
    params_a = (norm_g1, w_in, conv_w, w_out, norm_g2)
    params_b = (router_wt, router_b)
    return pl.pallas_call(
        _sconv_kernel,
        grid=(b, nt),
        in_specs=[tile, _mod_spec3(0, d), _mod_spec3(1, d), _mod_spec3(2, d)] + [_full2(a) for a in params_a]
        + [_mod_spec3(3, d), _mod_spec3(4, d)] + [_full2(a) for a in params_b],
        out_specs=[tile, tile, pl.BlockSpec((N_EXPERTS, TOKEN_TILE), lambda i, j: (0, i * nt + j))],
        out_shape=[jax.ShapeDtypeStruct((b, t, d), F32), jax.ShapeDtypeStruct((b, t, d), BF16),
                   jax.ShapeDtypeStruct((N_EXPERTS, b * t), F32)],
        compiler_params=pltpu.CompilerParams(
            dimension_semantics=("arbitrary", "arbitrary"), vmem_limit_bytes=VMEM_LIMIT_BYTES),
        name="sconv",
    )(h, mod, mod, mod, *params_a, mod, mod, *params_b)


def _moe_kernel(final_norm, hm_ref, comb_ref, wg_ref, wu_ref, wd_ref, h1_ref, gt_ref, fg_ref, o_ref, acc_ref):
    e = pl.program_id(2)

    @pl.when(e == 0)
    def _():
        acc_ref[...] = jnp.zeros_like(acc_ref)

    hm = hm_ref[0]
    gate = jnp.dot(hm, wg_ref[0], preferred_element_type=F32)
    up = jnp.dot(hm, wu_ref[0], preferred_element_type=F32)
    comb = comb_ref[...]
    lane = lax.broadcasted_iota(jnp.int32, comb.shape, 1)
    ce = jnp.sum(jnp.where(lane == e, comb, 0.0), axis=-1, keepdims=True)
    he = gate * _sigmoid(gate) * up * ce
    acc_ref[...] += jnp.dot(he.astype(BF16), wd_ref[0], preferred_element_type=F32)

    @pl.when(e == pl.num_programs(2) - 1)
    def _():
        h2 = h1_ref[0] + gt_ref[0] * acc_ref[...]
        if final_norm:
            ms = jnp.mean(h2 * h2, axis=-1, keepdims=True)
            h2 = h2 * lax.rsqrt(ms + RMS_EPS) * fg_ref[...]
        o_ref[0] = h2


def _moe(hm, comb, w_gate, w_up, w_down, h1, mod, final_g, final_norm):
    b, t, d = h1.shape
    n_e, _, de = w_gate.shape
    nt = t // MOE_TOKEN_TILE
    tile = pl.BlockSpec((1, MOE_TOKEN_TILE, d), lambda i, j, e: (i, j, 0))
    return pl.pallas_call(
        functools.partial(_moe_kernel, final_norm),
        grid=(b, nt, n_e),
        in_specs=[
            tile,
            pl.BlockSpec((MOE_TOKEN_TILE, n_e), lambda i, j, e: (i * nt + j, 0)),
            pl.BlockSpec((1, d, de), lambda i, j, e: (e, 0, 0)),
            pl.BlockSpec((1, d, de), lambda i, j, e: (e, 0, 0)),
            pl.BlockSpec((1, de, d), lambda i, j, e: (e, 0, 0)),
            tile,
            pl.BlockSpec((1, 1, d), lambda i, j, e: (i, 0, 5)),
            pl.BlockSpec((1, d), lambda i, j, e: (0, 0)),
        ],
        out_specs=tile,
        out_shape=jax.ShapeDtypeStruct((b, t, d), F32),
        scratch_shapes=[pltpu.VMEM((MOE_TOKEN_TILE, d), F32)],
        compiler_params=pltpu.CompilerParams(
            dimension_semantics=("arbitrary", "arbitrary", "arbitrary"), vmem_limit_bytes=VMEM_LIMIT_BYTES),
        name="moe",
    )(hm, comb, w_gate, w_up, w_down, h1, mod, final_g)


def kernel(x, c, ctx, c_ctx, ada_w, ada_b, norm_g, rw_mu, rw_w_rkv, rw_w0, rw_w1, rw_w2, rw_a0, rw_a1, rw_a2, rw_g1, rw_g2, rw_k_k, rw_k_a, rw_r_k, rw_gn_w, rw_gn_b, rw_w_o, sc_w_in, sc_conv, sc_w_out, router_w, router_b, moe_w_gate, moe_w_up, moe_w_down, final_g):
    b, t, d = x.shape
    ctx_len = ctx.shape[1]
    depth = ada_w.shape[0]
    assert d == D_MODEL and depth == 2 and t % MOE_TOKEN_TILE == 0

    mod_rows = 16
    cc = jnp.concatenate([c, c_ctx[None, :], jnp.zeros((mod_rows - b - 1, d), F32)], axis=0)
    mod = _ada(cc, ada_w, ada_b).reshape(depth, mod_rows, 1, 6 * d)

    row = lambda a: a.reshape(1, d)
    router_wt = router_w.T
    router_bc = router_b.reshape(N_EXPERTS, 1)

    xall = jnp.concatenate([ctx, x], axis=1)
    w1 = jnp.concatenate([rw_w1[0, 0], rw_w1[0, 1]], axis=1).astype(BF16)
    a1 = jnp.concatenate([rw_a1[0, 0], rw_a1[0, 1]], axis=1).astype(BF16)

    def pad_dirs(w):
        z = jnp.zeros_like(w[0])
        return jnp.stack([jnp.concatenate([w[0], z], axis=0), jnp.concatenate([z, w[1]], axis=0)]).astype(BF16)

    (r, v, g, bonus, cs0, cs1, al0, al1, kd0, kd1, b0, b1) = _rwkv_pre(
        xall, mod[0], b, ctx_len, row(norm_g[0, 0]), rw_mu[0], rw_w_rkv[0].astype(BF16), w1,
        pad_dirs(rw_w2[0]), rw_w0[0], a1, pad_dirs(rw_a2[0]), rw_a0[0],
        rw_g1[0].astype(BF16), rw_g2[0].astype(BF16), row(rw_k_k[0]), row(rw_k_a[0]), row(rw_r_k[0]))
    yf, yr = _wkv(r, v, cs0, al0, kd0, b0, cs1, al1, kd1, b1, ctx_len)
    h1, hm, comb_t = _readout(yf, yr, bonus, g, x, mod[0], ctx_len, row(rw_gn_w[0]), row(rw_gn_b[0]),
                              rw_w_o[0].astype(BF16), row(norm_g[0, 1]), router_wt, router_bc)
    h2 = _moe(hm, comb_t.T, moe_w_gate[0].astype(BF16), moe_w_up[0].astype(BF16),
              moe_w_down[0].astype(BF16), h1, mod[0], row(final_g), False)

    h1, hm, comb_t = _sconv(h2, mod[1], row(norm_g[1, 0]), sc_w_in[0].astype(BF16), sc_conv[0],
                            sc_w_out[0].astype(BF16), row(norm_g[1, 1]), router_wt, router_bc)
    return _moe(hm, comb_t.T, moe_w_gate[1].astype(BF16), moe_w_up[1].astype(BF16),
                moe_w_down[1].astype(BF16), h1, mod[1], row(final_g), True)
```

```python
import functools

import jax
import jax.numpy as jnp
from jax import lax
from jax.experimental import pallas as pl
from jax.experimental.pallas import tpu as pltpu
from jax.experimental.pallas import tpu_sc as plsc

F32 = jnp.float32
BF16 = jnp.bfloat16

D_MODEL = 1024
HEAD_DIM = 64
N_HEADS = D_MODEL // HEAD_DIM
LANES = 128
N_PAIRS = D_MODEL // LANES
CHUNK = 64
N_EXPERTS = 16
EXPERTS_PER_GROUP = 4
N_GROUPS = N_EXPERTS // EXPERTS_PER_GROUP
RMS_EPS = 1e-6
GN_EPS = 64e-5
L2_EPS = 1e-12

TOKEN_TILE = 256
EXPERT_ROW_TILE = 512
POSITION_TILE = 512
SC_CORES, SC_SUBCORES, SC_LANES = 2, 16, 16
SC_WORKERS = SC_CORES * SC_SUBCORES
SC_GATHER_ROWS = 64
PAIRS_PER_STEP = 8
ADA_COL_TILE = 1536
VMEM_LIMIT_BYTES = 56 * 1024 * 1024


def _sigmoid(x):
    return 1.0 / (1.0 + jnp.exp(-x))


def _mm(a, b):
    return jnp.dot(a.astype(BF16), b.astype(BF16), preferred_element_type=F32)


def _mm_nt(a, b):
    return lax.dot_general(a.astype(BF16), b.astype(BF16), (((1,), (1,)), ((), ())),
                           preferred_element_type=F32)


def _split3(x):
    hi = x.astype(BF16)
    r1 = x - hi.astype(F32)
    mid = r1.astype(BF16)
    lo = (r1 - mid.astype(F32)).astype(BF16)
    return hi, mid, lo


def _head_sum(x):
    rows = x.shape[0]
    left = lax.broadcasted_iota(jnp.int32, (rows, LANES), 1) < HEAD_DIM
    outs = []
    for j in range(N_PAIRS):
        xb = x[:, LANES * j:LANES * (j + 1)]
        sa = jnp.sum(jnp.where(left, xb, 0.0), axis=-1, keepdims=True)
        sb = jnp.sum(jnp.where(left, 0.0, xb), axis=-1, keepdims=True)
        outs.append(jnp.where(left, sa, sb))
    return jnp.concatenate(outs, axis=-1)


def _rms_modulate(x, g, shift, scale):
    ms = jnp.mean(x * x, axis=-1, keepdims=True)
    xn = x * lax.rsqrt(ms + RMS_EPS) * g
    return xn * (1.0 + scale) + shift


def _row_neighbours(x, row_len):
    rows = x.shape[0]
    pos = lax.broadcasted_iota(jnp.int32, x.shape, 0) & (row_len - 1)
    prev = jnp.where(pos == 0, 0.0, pltpu.roll(x, 1, 0))
    nxt = jnp.where(pos == row_len - 1, 0.0, pltpu.roll(x, rows - 1, 0))
    return prev, nxt


def _ada_kernel(c_ref, w_ref, b_ref, o_ref):
    c = c_ref[...]
    s = c * _sigmoid(c)
    o_ref[0] = _mm(s, w_ref[0]) + b_ref[0]


def _ada(cc, ada_w, ada_b):
    depth, d, n = ada_w.shape
    rows = cc.shape[0]
    return pl.pallas_call(
        _ada_kernel,
        grid=(depth, n // ADA_COL_TILE),
        in_specs=[
            pl.BlockSpec((rows, d), lambda l, j: (0, 0)),
            pl.BlockSpec((1, d, ADA_COL_TILE), lambda l, j: (l, 0, j)),
            pl.BlockSpec((1, 1, ADA_COL_TILE), lambda l, j: (l, 0, j)),
        ],
        out_specs=pl.BlockSpec((1, rows, ADA_COL_TILE), lambda l, j: (l, 0, j)),
        out_shape=jax.ShapeDtypeStruct((depth, rows, n), F32),
        compiler_params=pltpu.CompilerParams(
            dimension_semantics=("arbitrary", "arbitrary"), vmem_limit_bytes=VMEM_LIMIT_BYTES),
        name="ada",
    )(cc, ada_w, ada_b.reshape(depth, 1, n))


def _rwkv_pre_kernel(ctx_len, x_ref, sh_ref, sc_ref, ng_ref, mu_ref, wrkv_ref, w1_ref, w2_ref, w0_ref,
                     a1_ref, a2_ref, a0_ref, g1_ref, g2_ref, kk_ref, ka_ref, rk_ref,
                     r_out, v_out, g_out, bon_out, cs0_out, cs1_out, al0_out, al1_out, kd0_out, kd1_out,
                     b0_out, b1_out):
    j = pl.program_id(1)
    row_len = jnp.where(j == 0, ctx_len, CHUNK)
    xn = _rms_modulate(x_ref[0], ng_ref[...], sh_ref[0], sc_ref[0])
    prev, nxt = _row_neighbours(xn, row_len)
    xx = 0.5 * (prev + nxt) - xn

    def mix(i):
        return xn + xx * mu_ref[i:i + 1, :]

    r = _mm(mix(0), wrkv_ref[0])
    k = _mm(mix(1), wrkv_ref[1])
    v = _mm(mix(2), wrkv_ref[2])
    r_out[0] = r.astype(r_out.dtype)
    v_out[0] = v.astype(v_out.dtype)
    g_out[0] = _mm(_sigmoid(_mm(mix(5), g1_ref[...])), g2_ref[...]).astype(g_out.dtype)

    kk = k * kk_ref[...]
    kk = kk * lax.rsqrt(jnp.maximum(_head_sum(kk * kk), L2_EPS * L2_EPS))

    wl = jnp.tanh(_mm(mix(3), w1_ref[...]))
    al = _mm(mix(4), a1_ref[...])
    rows = xn.shape[0]
    ti = lax.broadcasted_iota(jnp.int32, (rows, rows), 0)
    tj = lax.broadcasted_iota(jnp.int32, (rows, rows), 1)
    same_chunk = (ti ^ tj) < CHUNK
    bonus_dot = None
    for p, (cs_out, al_out, kd_out, b_out) in enumerate(((cs0_out, al0_out, kd0_out, b0_out),
                                                         (cs1_out, al1_out, kd1_out, b1_out))):
        z = w0_ref[p:p + 1, :] + _mm(wl, w2_ref[p])
        lw = -jnp.exp(-0.5) * _sigmoid(z)
        tri = jnp.where(same_chunk & ((tj >= ti) if p else (tj <= ti)), 1.0, 0.0).astype(BF16)
        hi, mid, lo = _split3(lw)
        cs_out[0] = (jnp.dot(tri, hi, preferred_element_type=F32) + jnp.dot(tri, mid, preferred_element_type=F32)
                     + jnp.dot(tri, lo, preferred_element_type=F32))
        al_out[0] = (-kk * jnp.exp(-lw)).astype(al_out.dtype)
        a = _sigmoid(a0_ref[p:p + 1, :] + _mm(al, a2_ref[p]))
        kd = k * (1.0 + (a - 1.0) * ka_ref[...])
        kd_out[0] = kd.astype(kd_out.dtype)
        b_out[0] = (kk * a).astype(b_out.dtype)
        t = r * kd * rk_ref[...]
        bonus_dot = t if bonus_dot is None else bonus_dot + t
    bon_out[0] = (_head_sum(bonus_dot) * v).astype(bon_out.dtype)


def _rwkv_pre(xall, mod, n_batch, ctx_len, norm_g, mu, w_rkv, w1, w2, w0, a1, a2, a0, g1, g2, k_k, k_a, r_k):
    b, tt, d = xall.shape
    assert ctx_len == TOKEN_TILE and tt % TOKEN_TILE == 0

    def mod_spec(part):
        return pl.BlockSpec((1, 1, d), lambda i, j: (jnp.where(j == 0, n_batch, i), 0, part))

    def full(a):
        nd = a.ndim
        return pl.BlockSpec(a.shape, lambda i, j: (0,) * nd)

    tile = pl.BlockSpec((1, TOKEN_TILE, d), lambda i, j: (i, j, 0))
    params = (norm_g, mu, w_rkv, w1, w2, w0, a1, a2, a0, g1, g2, k_k, k_a, r_k)
    out_dtypes = (BF16,) * 4 + (F32, F32) + (BF16,) * 6
    return pl.pallas_call(
        functools.partial(_rwkv_pre_kernel, ctx_len),
        grid=(b, tt // TOKEN_TILE),
        in_specs=[tile, mod_spec(0), mod_spec(1)] + [full(a) for a in params],
        out_specs=[tile] * len(out_dtypes),
        out_shape=[jax.ShapeDtypeStruct((b, tt, d), dt) for dt in out_dtypes],
        compiler_params=pltpu.CompilerParams(
            dimension_semantics=("arbitrary", "arbitrary"), vmem_limit_bytes=VMEM_LIMIT_BYTES),
        name="rwkv_pre",
    )(xall, mod, mod, *params)


def _expand(x, left):
    return jnp.concatenate([jnp.where(left, x, 0.0), jnp.where(left, 0.0, x)], axis=0)


def _dot(a, b):
    return jnp.dot(a, b, preferred_element_type=F32)


def _chunk_steps(chains):
    assert CHUNK == HEAD_DIM
    c = CHUNK
    lane = lax.broadcasted_iota(jnp.int32, (c, LANES), 1)
    left = lane < HEAD_DIM
    tt = lax.broadcasted_iota(jnp.int32, (c, LANES), 0)
    jj = lane & (c - 1)
    diag = jj == tt
    tri = {False: (jj < tt, jj <= tt), True: (jj > tt, jj >= tt)}

    def bd(x):
        return _expand(x, left).astype(BF16)

    def fold_t(x):
        xt = _expand(x, left).T
        return xt[:c] + xt[c:]

    def rows(*xs):
        return jnp.concatenate(xs, axis=0).astype(BF16)

    prep = []
    for cs, alw, r, kd, v, be, h, reverse in chains:
        tot = cs[0:1, :] if reverse else cs[c - 1:c, :]
        e_pos, e_neg, e_rem = jnp.exp(cs), jnp.exp(-cs), jnp.exp(tot - cs)
        at, rt = alw * e_pos, r * e_pos
        prep.append(dict(
            lhs=rows(at, rt), rhs=jnp.concatenate([bd(be * e_neg), bd(kd * e_neg)], axis=0),
            a_e=bd(at), rt=rt, v_e=bd(v), bh_t=fold_t(be * e_rem), kh_t=fold_t(kd * e_rem),
            g_tot=jnp.exp(tot), h_e=bd(h), strict=tri[reverse][0], incl=tri[reverse][1]))

    for q in prep:
        sc = lax.dot_general(q["lhs"], q["rhs"], (((1,), (1,)), ((), ())), preferred_element_type=F32)
        q["n"] = jnp.where(q["strict"], sc[:c, :LANES], 0.0)
        a_ak = jnp.where(q["strict"], sc[:c, LANES:], 0.0)
        q["a_rb"] = jnp.where(q["incl"], sc[c:, :LANES], 0.0)
        a_rk = jnp.where(q["incl"], sc[c:, LANES:], 0.0)
        q["vv"] = _dot(rows(a_ak, q["kh_t"], a_rk), q["v_e"])

    for q in prep:
        q["t"] = jnp.where(diag, 1.0, 0.0) + q["n"]
        q["p"] = _dot(q["n"].astype(BF16), bd(q["n"]))
    order = 4
    while order < c:
        for q in prep:
            x = _dot(rows(q["p"], q["t"]), bd(q["p"]))
            q["p"] = x[:c]
            q["t"] = q["t"] + x[c:]
        order *= 2
    for q in prep:
        q["t"] = q["t"] + _dot(q["t"].astype(BF16), bd(q["p"]))

    for q in prep:
        xu = _dot(q["t"].astype(BF16), jnp.concatenate([q["a_e"], bd(q["vv"][:c])], axis=1))
        q["au_e"] = jnp.concatenate([bd(xu[:, :LANES]), bd(xu[:, LANES:])], axis=1)
    for q in prep:
        z = _dot(rows(q["bh_t"], q["a_rb"]), q["au_e"])
        q["m"] = z[:c, :LANES] + jnp.where(diag, q["g_tot"], 0.0)
        q["g"] = z[:c, LANES:] + q["vv"][c:2 * c]
        q["r_hat"] = q["rt"] + z[c:, :LANES]
        q["yi"] = z[c:, LANES:] + q["vv"][2 * c:]
    outs = []
    for q in prep:
        o = _dot(rows(q["r_hat"], q["m"]), q["h_e"])
        outs.append((o[:c] + q["yi"], o[c:] + q["g"]))
    return outs


def _wkv_kernel(rf_ref, vf_ref, csf_ref, af_ref, kdf_ref, bf_ref,
                rr_ref, vr_ref, csr_ref, ar_ref, kdr_ref, br_ref,
                yf_ref, yr_ref, hf_ref, hr_ref):
    @pl.when(pl.program_id(2) == 0)
    def _():
        hf_ref[...] = jnp.zeros_like(hf_ref)
        hr_ref[...] = jnp.zeros_like(hr_ref)

    dirs = ((rf_ref, vf_ref, csf_ref, af_ref, kdf_ref, bf_ref, yf_ref, hf_ref),
            (rr_ref, vr_ref, csr_ref, ar_ref, kdr_ref, br_ref, yr_ref, hr_ref))
    chains, dests = [], []
    for p in range(PAIRS_PER_STEP):
        sl = slice(p * LANES, (p + 1) * LANES)
        for reverse, (r_ref, v_ref, cs_ref, a_ref, kd_ref, b_ref, y_ref, h_ref) in enumerate(dirs):
            chains.append((cs_ref[0, :, sl], a_ref[0, :, sl].astype(F32), r_ref[0, :, sl].astype(F32),
                           kd_ref[0, :, sl].astype(F32), v_ref[0, :, sl].astype(F32),
                           b_ref[0, :, sl].astype(F32), h_ref[p], bool(reverse)))
            dests.append((y_ref, h_ref, p, sl))
    for (y, h_new), (y_ref, h_ref, p, sl) in zip(_chunk_steps(chains), dests):
        y_ref[0, :, sl] = y
        h_ref[p] = h_new


def _wkv(r, v, cs0, al0, kd0, b0, cs1, al1, kd1, b1, ctx_len):
    b, tt, d = r.shape
    n_chunks = tt // CHUNK
    n_ctx = ctx_len // CHUNK
    width = PAIRS_PER_STEP * LANES

    def fwd_map(i, j, s):
        return (i, s, j)

    def rev_map(i, j, s):
        return (i, jnp.where(s < n_ctx, n_ctx - 1 - s, n_chunks - 1 - (s - n_ctx)), j)

    fwd = pl.BlockSpec((1, CHUNK, width), fwd_map)
    rev = pl.BlockSpec((1, CHUNK, width), rev_map)
    return pl.pallas_call(
        _wkv_kernel,
        grid=(b, d // width, n_chunks),
        in_specs=[fwd] * 6 + [rev] * 6,
        out_specs=[fwd, rev],
        out_shape=[jax.ShapeDtypeStruct((b, tt, d), F32)] * 2,
        scratch_shapes=[pltpu.VMEM((PAIRS_PER_STEP, HEAD_DIM, LANES), F32)] * 2,
        compiler_params=pltpu.CompilerParams(
            dimension_semantics=("arbitrary", "arbitrary", "arbitrary"), vmem_limit_bytes=VMEM_LIMIT_BYTES),
        name="wkv",
    )(r, v, cs0, al0, kd0, b0, r, v, cs1, al1, kd1, b1)


def _pack_bf16_pairs(x):
    half = x.shape[1] // 2
    lo = pltpu.bitcast(x[:, :half].astype(BF16).astype(F32), jnp.int32)
    hi = pltpu.bitcast(x[:, half:].astype(BF16).astype(F32), jnp.int32)
    return lax.shift_right_logical(lo, jnp.int32(16)) | (hi & jnp.int32(-65536))


def _unpack_bf16_pairs(w):
    lo = pltpu.bitcast(lax.shift_left(w, jnp.int32(16)), F32)
    hi = pltpu.bitcast(w & jnp.int32(-65536), F32)
    return jnp.concatenate([lo, hi], axis=1)


def _route(h1, ng, shift, scale, rwt_ref, rb_ref, hm_out, eidx_out, gate_out):
    hm = _rms_modulate(h1, ng, shift, scale)
    hm_out[...] = _pack_bf16_pairs(hm)
    w_hi, w_mid, _ = _split3(rwt_ref[...])
    h_hi, h_mid, _ = _split3(hm)
    nt = (((1,), (1,)), ((), ()))
    logits = (lax.dot_general(w_hi, h_hi, nt, preferred_element_type=F32)
              + lax.dot_general(w_hi, h_mid, nt, preferred_element_type=F32)
              + lax.dot_general(w_mid, h_hi, nt, preferred_element_type=F32))
    s = _sigmoid(logits)
    sel = s + rb_ref[...]
    a = [sel[e:e + 1, :] for e in range(N_EXPERTS)]
    best = best_idx = None
    for g in range(N_GROUPS):
        m = a[g * EXPERTS_PER_GROUP:(g + 1) * EXPERTS_PER_GROUP]
        score = None
        for i in range(EXPERTS_PER_GROUP):
            for k in range(i + 1, EXPERTS_PER_GROUP):
                pair = m[i] + m[k]
                score = pair if score is None else jnp.maximum(score, pair)
        if best is None:
            best, best_idx = score, jnp.zeros_like(score, dtype=jnp.int32)
        else:
            upd = score > best
            best_idx = jnp.where(upd, g, best_idx)
            best = jnp.where(upd, score, best)
    n_seen = jnp.zeros_like(best_idx)
    e_slot = [jnp.zeros_like(best_idx), jnp.zeros_like(best_idx)]
    g_slot = [jnp.zeros_like(best), jnp.zeros_like(best)]
    for e in range(N_EXPERTS):
        g = e // EXPERTS_PER_GROUP
        ahead = jnp.zeros_like(best_idx)
        for k in range(g * EXPERTS_PER_GROUP, (g + 1) * EXPERTS_PER_GROUP):
            if k == e:
                continue
            beats = (a[k] >= a[e]) if k < e else (a[k] > a[e])
            ahead = ahead + beats.astype(jnp.int32)
        chosen = (best_idx == g) & (ahead < 2)
        for slot in range(2):
            hit = chosen & (n_seen == slot)
            e_slot[slot] = jnp.where(hit, e, e_slot[slot])
            g_slot[slot] = jnp.where(hit, s[e:e + 1, :], g_slot[slot])
        n_seen = n_seen + chosen.astype(jnp.int32)
    eidx_out[...] = jnp.concatenate(e_slot, axis=0)
    gate_out[...] = jnp.concatenate(g_slot, axis=0) / (g_slot[0] + g_slot[1])


def _readout_kernel(yf_ref, yr_ref, bon_ref, g_ref, h_ref, gt_ref, gnw_ref, gnb_ref, wo_ref,
                    ng_ref, sh_ref, sc_ref, rwt_ref, rb_ref, h1_out, hm_out, eidx_out, gate_out):
    y = yf_ref[0] + yr_ref[0] + bon_ref[0].astype(F32)
    mean = _head_sum(y) * (1.0 / HEAD_DIM)
    yc = y - mean
    var = _head_sum(yc * yc) * (1.0 / HEAD_DIM)
    yn = yc * lax.rsqrt(var + GN_EPS) * gnw_ref[...] + gnb_ref[...]
    out = _mm(yn * g_ref[0].astype(F32), wo_ref[...])
    h1 = h_ref[0] + gt_ref[0] * out
    h1_out[0] = h1
    _route(h1, ng_ref[...], sh_ref[0], sc_ref[0], rwt_ref, rb_ref, hm_out, eidx_out, gate_out)


def _mod_spec3(part, d):
    return pl.BlockSpec((1, 1, d), lambda i, j: (i, 0, part))


def _route_out_specs(b, t, d):
    nt = t // TOKEN_TILE
    specs = [pl.BlockSpec((1, TOKEN_TILE, d), lambda i, j: (i, j, 0)),
             pl.BlockSpec((TOKEN_TILE, d // 2), lambda i, j: (i * nt + j, 0)),
             pl.BlockSpec((2, TOKEN_TILE), lambda i, j: (0, i * nt + j)),
             pl.BlockSpec((2, TOKEN_TILE), lambda i, j: (0, i * nt + j))]
    shapes = [jax.ShapeDtypeStruct((b, t, d), F32), jax.ShapeDtypeStruct((b * t, d // 2), jnp.int32),
              jax.ShapeDtypeStruct((2, b * t), jnp.int32), jax.ShapeDtypeStruct((2, b * t), F32)]
    return specs, shapes


def _full2(a):
    nd = a.ndim
    return pl.BlockSpec(a.shape, lambda i, j: (0,) * nd)


def _readout(yf, yr, bonus, g, h, mod, ctx_len, gn_w, gn_b, w_o, norm_g, router_wt, router_b):
    b, t, d = h.shape
    off = ctx_len // TOKEN_TILE
    nt = t // TOKEN_TILE
    shifted = pl.BlockSpec((1, TOKEN_TILE, d), lambda i, j: (i, j + off, 0))
    tile = pl.BlockSpec((1, TOKEN_TILE, d), lambda i, j: (i, j, 0))
    params_a = (gn_w, gn_b, w_o, norm_g)
    params_b = (router_wt, router_b)
    out_specs, out_shape = _route_out_specs(b, t, d)
    return pl.pallas_call(
        _readout_kernel,
        grid=(b, nt),
        in_specs=[shifted] * 4 + [tile, _mod_spec3(2, d)] + [_full2(a) for a in params_a]
        + [_mod_spec3(3, d), _mod_spec3(4, d)] + [_full2(a) for a in params_b],
        out_specs=out_specs,
        out_shape=out_shape,
        compiler_params=pltpu.CompilerParams(
            dimension_semantics=("arbitrary", "arbitrary"), vmem_limit_bytes=VMEM_LIMIT_BYTES),
        name="readout",
    )(yf, yr, bonus, g, h, mod, *params_a, mod, mod, *params_b)


def _sconv_kernel(h_ref, sh1_ref, sc1_ref, gt_ref, ng1_ref, win_ref, cw_ref, wout_ref,
                  ng2_ref, sh2_ref, sc2_ref, rwt_ref, rb_ref, h1_out, hm_out, eidx_out, gate_out):
    h = h_ref[0]
    d = h.shape[1]
    xn = _rms_modulate(h, ng1_ref[...], sh1_ref[0], sc1_ref[0]).astype(BF16)
    bg = jnp.dot(xn, win_ref[:, 0:d], preferred_element_type=F32)
    u = (jnp.dot(xn, win_ref[:, d:2 * d], preferred_element_type=F32)
         * jnp.dot(xn, win_ref[:, 2 * d:3 * d], preferred_element_type=F32))
    prev, nxt = _row_neighbours(u, CHUNK)
    conv = cw_ref[0:1, :] * prev + cw_ref[1:2, :] * u + cw_ref[2:3, :] * nxt
    h1 = h + gt_ref[0] * _mm(bg * conv, wout_ref[...])
    h1_out[0] = h1
    _route(h1, ng2_ref[...], sh2_ref[0], sc2_ref[0], rwt_ref, rb_ref, hm_out, eidx_out, gate_out)


def _sconv(h, mod, norm_g1, w_in, conv_w, w_out, norm_g2, router_wt, router_b):
    b, t, d = h.shape
    nt = t // TOKEN_TILE
    tile = pl.BlockSpec((1, TOKEN_TILE, d), lambda i, j: (i, j, 0))
    params_a = (norm_g1, w_in, conv_w, w_out, norm_g2)
    params_b = (router_wt, router_b)
    out_specs, out_shape = _route_out_specs(b, t, d)
    return pl.pallas_call(
        _sconv_kernel,
        grid=(b, nt),
        in_specs=[tile, _mod_spec3(0, d), _mod_spec3(1, d), _mod_spec3(2, d)] + [_full2(a) for a in params_a]
        + [_mod_spec3(3, d), _mod_spec3(4, d)] + [_full2(a) for a in params_b],
        out_specs=out_specs,
        out_shape=out_shape,
        compiler_params=pltpu.CompilerParams(
            dimension_semantics=("arbitrary", "arbitrary"), vmem_limit_bytes=VMEM_LIMIT_BYTES),
        name="sconv",
    )(h, mod, mod, mod, *params_a, mod, mod, *params_b)


def _positions_kernel(eidx_ref, pos_ref, ends_ref, run_ref):
    phase, i = pl.program_id(0), pl.program_id(1)
    tp = eidx_ref.shape[1]
    expert = lax.broadcasted_iota(jnp.int32, (N_EXPERTS, tp), 0)
    onehot = [jnp.where(eidx_ref[k:k + 1, :] == expert, 1.0, 0.0) for k in range(2)]

    @pl.when((phase == 0) & (i == 0))
    def _():
        run_ref[...] = jnp.zeros_like(run_ref)

    @pl.when(phase == 0)
    def _():
        run_ref[...] += jnp.sum(onehot[0] + onehot[1], axis=1, keepdims=True)

    @pl.when((phase == 1) & (i == 0))
    def _():
        padded = jnp.floor((run_ref[...] + (EXPERT_ROW_TILE - 1)) * (1.0 / EXPERT_ROW_TILE)) * EXPERT_ROW_TILE
        ei = lax.broadcasted_iota(jnp.int32, (N_EXPERTS, N_EXPERTS), 0)
        ej = lax.broadcasted_iota(jnp.int32, (N_EXPERTS, N_EXPERTS), 1)
        below = jnp.where(ej < ei, 1.0, 0.0)
        starts = jnp.dot(below, jnp.broadcast_to(padded, (N_EXPERTS, LANES)), precision=lax.Precision.HIGHEST,
                         preferred_element_type=F32)
        ends_ref[...] = (starts + padded).astype(jnp.int32)
        run_ref[...] = starts[:, 0:1]

    @pl.when(phase == 1)
    def _():
        ti = lax.broadcasted_iota(jnp.int32, (tp, tp), 0)
        tj = lax.broadcasted_iota(jnp.int32, (tp, tp), 1)
        before = jnp.where(ti < tj, 1.0, 0.0).astype(BF16)
        run = run_ref[...]
        for k in range(2):
            prefix = jnp.dot(onehot[k].astype(BF16), before, preferred_element_type=F32)
            pos_ref[k:k + 1, :] = jnp.sum(onehot[k] * (prefix + run), axis=0, keepdims=True).astype(jnp.int32)
            run = run + jnp.sum(onehot[k], axis=1, keepdims=True)
        run_ref[...] = run


def _positions(eidx):
    n = eidx.shape[1]
    nt = n // POSITION_TILE
    return pl.pallas_call(
        _positions_kernel,
        grid=(2, nt),
        in_specs=[pl.BlockSpec((2, POSITION_TILE), lambda p, i: (0, i))],
        out_specs=[pl.BlockSpec((2, POSITION_TILE), lambda p, i: (0, i * p)),
                   pl.BlockSpec((N_EXPERTS, LANES), lambda p, i: (0, 0))],
        out_shape=[jax.ShapeDtypeStruct((2, n), jnp.int32), jax.ShapeDtypeStruct((N_EXPERTS, LANES), jnp.int32)],
        scratch_shapes=[pltpu.VMEM((N_EXPERTS, 1), F32)],
        compiler_params=pltpu.CompilerParams(
            dimension_semantics=("arbitrary", "arbitrary"), vmem_limit_bytes=VMEM_LIMIT_BYTES),
        name="positions",
    )(eidx)


def _sc_mesh():
    return plsc.VectorSubcoreMesh(core_axis_name="c", subcore_axis_name="s")


def _sc_worker(n_workers_per_core=SC_SUBCORES):
    return lax.axis_index("c") * n_workers_per_core + lax.axis_index("s")


def _sc_invert(pos, n_rows):
    n_slots = pos.shape[0]
    n_tokens = n_slots // 2
    assert n_tokens & (n_tokens - 1) == 0
    per_worker = n_rows // SC_WORKERS
    assert per_worker * SC_WORKERS == n_rows and per_worker % SC_LANES == 0 and n_slots % SC_LANES == 0

    def body(pos_hbm, inv_hbm, pos_v, inv_v):
        base = _sc_worker() * per_worker
        pltpu.sync_copy(pos_hbm, pos_v)

        @pl.loop(0, per_worker, step=SC_LANES)
        def _(j):
            inv_v[pl.ds(j, SC_LANES)] = jnp.zeros((SC_LANES,), jnp.int32)

        @pl.loop(0, n_slots, step=SC_LANES)
        def _(s):
            local = pos_v[pl.ds(s, SC_LANES)] - base
            mine = (local >= 0) & (local < per_worker)
            token = (s + lax.iota(jnp.int32, SC_LANES)) & (n_tokens - 1)
            plsc.store_scatter(inv_v, [jnp.where(mine, local, 0)], token, mask=mine)

        pltpu.sync_copy(inv_v, inv_hbm.at[pl.ds(base, per_worker)])

    return pl.kernel(
        body, out_type=jax.ShapeDtypeStruct((n_rows,), jnp.int32), mesh=_sc_mesh(),
        scratch_types=[pltpu.VMEM((n_slots,), jnp.int32), pltpu.VMEM((per_worker,), jnp.int32)],
        compiler_params=pltpu.CompilerParams(needs_layout_passes=False),
        name="sc_invert",
    )(pos)


def _sc_gather(table, idx):
    n_rows, width = idx.shape[0], table.shape[1]
    per_worker = n_rows // SC_WORKERS
    assert per_worker * SC_WORKERS == n_rows and per_worker % SC_GATHER_ROWS == 0

    def body(table_hbm, idx_hbm, out_hbm, idx_v, rows_v, sem):
        base = _sc_worker() * per_worker

        @pl.loop(0, per_worker, step=SC_GATHER_ROWS)
        def _(j):
            pltpu.sync_copy(idx_hbm.at[pl.ds(base + j, SC_GATHER_ROWS)], idx_v)
            pltpu.async_copy(table_hbm.at[idx_v], rows_v, sem).wait()
            pltpu.sync_copy(rows_v, out_hbm.at[pl.ds(base + j, SC_GATHER_ROWS)])

    return pl.kernel(
        body, out_type=jax.ShapeDtypeStruct((n_rows, width), table.dtype), mesh=_sc_mesh(),
        scratch_types=[pltpu.VMEM((SC_GATHER_ROWS,), jnp.int32), pltpu.VMEM((SC_GATHER_ROWS, width), table.dtype),
                       pltpu.SemaphoreType.DMA],
        name="sc_gather",
    )(table, idx)


def _experts_kernel(te_ref, xs_ref, wg_ref, wu_ref, wd_ref, ys_ref):
    @pl.when(te_ref[pl.program_id(0)] < N_EXPERTS)
    def _():
        x = _unpack_bf16_pairs(xs_ref[...]).astype(BF16)
        gate = jnp.dot(x, wg_ref[0], preferred_element_type=F32)
        up = jnp.dot(x, wu_ref[0], preferred_element_type=F32)
        he = (gate * _sigmoid(gate) * up).astype(BF16)
        ys_ref[...] = _pack_bf16_pairs(jnp.dot(he, wd_ref[0], preferred_element_type=F32))


def _experts(tile_expert, xs, w_gate, w_up, w_down):
    n_rows, half = xs.shape
    n_e, d, de = w_gate.shape

    def w_map(i, te):
        return (jnp.minimum(te[i], n_e - 1), 0, 0)

    rows = pl.BlockSpec((EXPERT_ROW_TILE, half), lambda i, te: (i, 0))
    return pl.pallas_call(
        _experts_kernel,
        grid_spec=pltpu.PrefetchScalarGridSpec(
            num_scalar_prefetch=1,
            grid=(n_rows // EXPERT_ROW_TILE,),
            in_specs=[rows, pl.BlockSpec((1, d, de), w_map), pl.BlockSpec((1, d, de), w_map),
                      pl.BlockSpec((1, de, d), w_map)],
            out_specs=rows),
        out_shape=jax.ShapeDtypeStruct((n_rows, half), jnp.int32),
        compiler_params=pltpu.CompilerParams(
            dimension_semantics=("arbitrary",), vmem_limit_bytes=VMEM_LIMIT_BYTES),
        name="experts",
    )(tile_expert, xs, w_gate, w_up, w_down)


def _combine_kernel(final_norm, y0_ref, y1_ref, gate_ref, h1_ref, gt_ref, fg_ref, o_ref):
    gates = gate_ref[...]
    y = gates[:, 0:1] * _unpack_bf16_pairs(y0_ref[...]) + gates[:, 1:2] * _unpack_bf16_pairs(y1_ref[...])
    h2 = h1_ref[0] + gt_ref[0] * y
    if final_norm:
        ms = jnp.mean(h2 * h2, axis=-1, keepdims=True)
        h2 = h2 * lax.rsqrt(ms + RMS_EPS) * fg_ref[...]
    o_ref[0] = h2


def _combine(yg, gates, h1, mod, final_g, final_norm):
    b, t, d = h1.shape
    nt = t // TOKEN_TILE
    n_tiles = b * nt
    tile = pl.BlockSpec((1, TOKEN_TILE, d), lambda i, j: (i, j, 0))
    return pl.pallas_call(
        functools.partial(_combine_kernel, final_norm),
        grid=(b, nt),
        in_specs=[pl.BlockSpec((TOKEN_TILE, d // 2), lambda i, j: (i * nt + j, 0)),
                  pl.BlockSpec((TOKEN_TILE, d // 2), lambda i, j: (n_tiles + i * nt + j, 0)),
                  pl.BlockSpec((TOKEN_TILE, 2), lambda i, j: (i * nt + j, 0)),
                  tile, _mod_spec3(5, d), pl.BlockSpec((1, d), lambda i, j: (0, 0))],
        out_specs=tile,
        out_shape=jax.ShapeDtypeStruct((b, t, d), F32),
        compiler_params=pltpu.CompilerParams(
            dimension_semantics=("arbitrary", "arbitrary"), vmem_limit_bytes=VMEM_LIMIT_BYTES),
        name="combine",
    )(yg, yg, gates, h1, mod, final_g)


def _moe(hm, eidx, gates, w_gate, w_up, w_down, h1, mod, final_g, final_norm):
    n_slots = 2 * hm.shape[0]
    n_rows = n_slots + N_EXPERTS * EXPERT_ROW_TILE
    pos, ends = _positions(eidx)
    pos = pos.reshape(n_slots)
    tile_start = jnp.arange(n_rows // EXPERT_ROW_TILE, dtype=jnp.int32) * EXPERT_ROW_TILE
    tile_expert = jnp.sum(tile_start[:, None] >= ends[None, :, 0], axis=1).astype(jnp.int32)
    xs = _sc_gather(hm, _sc_invert(pos, n_rows))
    ys = _experts(tile_expert, xs, w_gate, w_up, w_down)
    return _combine(_sc_gather(ys, pos), gates.T, h1, mod, final_g, final_norm)


def kernel(x, c, ctx, c_ctx, ada_w, ada_b, norm_g, rw_mu, rw_w_rkv, rw_w0, rw_w1, rw_w2, rw_a0, rw_a1, rw_a2, rw_g1, rw_g2, rw_k_k, rw_k_a, rw_r_k, rw_gn_w, rw_gn_b, rw_w_o, sc_w_in, sc_conv, sc_w_out, router_w, router_b, moe_w_gate, moe_w_up, moe_w_down, final_g):
    b, t, d = x.shape
    ctx_len = ctx.shape[1]
    depth = ada_w.shape[0]
    assert d == D_MODEL and depth == 2 and t % POSITION_TILE == 0

    mod_rows = 16
    cc = jnp.concatenate([c, c_ctx[None, :], jnp.zeros((mod_rows - b - 1, d), F32)], axis=0)
    mod = _ada(cc, ada_w, ada_b).reshape(depth, mod_rows, 1, 6 * d)

    row = lambda a: a.reshape(1, d)
    router_wt = router_w.T
    router_bc = router_b.reshape(N_EXPERTS, 1)

    xall = jnp.concatenate([ctx, x], axis=1)
    w1 = jnp.concatenate([rw_w1[0, 0], rw_w1[0, 1]], axis=1).astype(BF16)
    a1 = jnp.concatenate([rw_a1[0, 0], rw_a1[0, 1]], axis=1).astype(BF16)

    def pad_dirs(w):
        z = jnp.zeros_like(w[0])
        return jnp.stack([jnp.concatenate([w[0], z], axis=0), jnp.concatenate([z, w[1]], axis=0)]).astype(BF16)

    (r, v, g, bonus, cs0, cs1, al0, al1, kd0, kd1, b0, b1) = _rwkv_pre(
        xall, mod[0], b, ctx_len, row(norm_g[0, 0]), rw_mu[0], rw_w_rkv[0].astype(BF16), w1,
        pad_dirs(rw_w2[0]), rw_w0[0], a1, pad_dirs(rw_a2[0]), rw_a0[0],
        rw_g1[0].astype(BF16), rw_g2[0].astype(BF16), row(rw_k_k[0]), row(rw_k_a[0]), row(rw_r_k[0]))
    yf, yr = _wkv(r, v, cs0, al0, kd0, b0, cs1, al1, kd1, b1, ctx_len)
    h1, hm, eidx, gates = _readout(yf, yr, bonus, g, x, mod[0], ctx_len, row(rw_gn_w[0]), row(rw_gn_b[0]),
                                   rw_w_o[0].astype(BF16), row(norm_g[0, 1]), router_wt, router_bc)
    h2 = _moe(hm, eidx, gates, moe_w_gate[0].astype(BF16), moe_w_up[0].astype(BF16),
              moe_w_down[0].astype(BF16), h1, mod[0], row(final_g), False)

    h1, hm, eidx, gates = _sconv(h2, mod[1], row(norm_g[1, 0]), sc_w_in[0].astype(BF16), sc_conv[0],
                                 sc_w_out[0].astype(BF16), row(norm_g[1, 1]), router_wt, router_bc)
    return _moe(hm, eidx, gates, moe_w_gate[1].astype(BF16), moe_w_up[1].astype(BF16),
                moe_w_down[1].astype(BF16), h1, mod[1], row(final_g), True)
```

```python
import functools

import jax
import jax.numpy as jnp
from jax import lax
from jax.experimental import pallas as pl
from jax.experimental.pallas import tpu as pltpu
from jax.experimental.pallas import tpu_sc as plsc

F32 = jnp.float32
BF16 = jnp.bfloat16

D_MODEL = 1024
HEAD_DIM = 64
N_HEADS = D_MODEL // HEAD_DIM
LANES = 128
N_PAIRS = D_MODEL // LANES
CHUNK = 64
N_EXPERTS = 16
EXPERTS_PER_GROUP = 4
N_GROUPS = N_EXPERTS // EXPERTS_PER_GROUP
RMS_EPS = 1e-6
GN_EPS = 64e-5
L2_EPS = 1e-12

TOKEN_TILE = 256
EXPERT_ROW_TILE = 512
POSITION_TILE = 512
SC_CORES, SC_SUBCORES, SC_LANES = 2, 16, 16
SC_WORKERS = SC_CORES * SC_SUBCORES
SC_GATHER_ROWS = 128
PAIRS_PER_STEP = 8
ADA_COL_TILE = 1536
VMEM_LIMIT_BYTES = 56 * 1024 * 1024


def _sigmoid(x):
    return 1.0 / (1.0 + jnp.exp(-x))


def _mm(a, b):
    return jnp.dot(a.astype(BF16), b.astype(BF16), preferred_element_type=F32)


def _mm_nt(a, b):
    return lax.dot_general(a.astype(BF16), b.astype(BF16), (((1,), (1,)), ((), ())),
                           preferred_element_type=F32)


def _split2(x):
    hi = x.astype(BF16)
    return hi, (x - hi.astype(F32)).astype(BF16)


def _head_sum(x):
    rows = x.shape[0]
    left = lax.broadcasted_iota(jnp.int32, (rows, LANES), 1) < HEAD_DIM
    outs = []
    for j in range(N_PAIRS):
        xb = x[:, LANES * j:LANES * (j + 1)]
        sa = jnp.sum(jnp.where(left, xb, 0.0), axis=-1, keepdims=True)
        sb = jnp.sum(jnp.where(left, 0.0, xb), axis=-1, keepdims=True)
        outs.append(jnp.where(left, sa, sb))
    return jnp.concatenate(outs, axis=-1)


def _rms_modulate(x, g, shift, scale):
    ms = jnp.mean(x * x, axis=-1, keepdims=True)
    xn = x * lax.rsqrt(ms + RMS_EPS) * g
    return xn * (1.0 + scale) + shift


def _row_neighbours(x, row_len):
    rows = x.shape[0]
    pos = lax.broadcasted_iota(jnp.int32, x.shape, 0) & (row_len - 1)
    prev = jnp.where(pos == 0, 0.0, pltpu.roll(x, 1, 0))
    nxt = jnp.where(pos == row_len - 1, 0.0, pltpu.roll(x, rows - 1, 0))
    return prev, nxt


def _ada_kernel(c_ref, w_ref, b_ref, o_ref):
    c = c_ref[...]
    s = c * _sigmoid(c)
    o_ref[0] = _mm(s, w_ref[0]) + b_ref[0]


def _ada(cc, ada_w, ada_b):
    depth, d, n = ada_w.shape
    rows = cc.shape[0]
    return pl.pallas_call(
        _ada_kernel,
        grid=(depth, n // ADA_COL_TILE),
        in_specs=[
            pl.BlockSpec((rows, d), lambda l, j: (0, 0)),
            pl.BlockSpec((1, d, ADA_COL_TILE), lambda l, j: (l, 0, j)),
            pl.BlockSpec((1, 1, ADA_COL_TILE), lambda l, j: (l, 0, j)),
        ],
        out_specs=pl.BlockSpec((1, rows, ADA_COL_TILE), lambda l, j: (l, 0, j)),
        out_shape=jax.ShapeDtypeStruct((depth, rows, n), F32),
        compiler_params=pltpu.CompilerParams(
            dimension_semantics=("arbitrary", "arbitrary"), vmem_limit_bytes=VMEM_LIMIT_BYTES),
        name="ada",
    )(cc, ada_w, ada_b.reshape(depth, 1, n))


def _rwkv_pre_kernel(ctx_len, ctx_ref, x_ref, sh_ref, sc_ref, ng_ref, mu_ref, wrkv_ref, w1_ref, w2_ref, w0_ref,
                     a1_ref, a2_ref, a0_ref, g1_ref, g2_ref, kk_ref, ka_ref, rk_ref,
                     r_out, v_out, g_out, bon_out, cs0_out, cs1_out, al0_out, al1_out, kd0_out, kd1_out,
                     b0_out, b1_out):
    j = pl.program_id(1)
    row_len = jnp.where(j == 0, ctx_len, CHUNK)
    xin = jnp.where(j == 0, ctx_ref[0], x_ref[0])
    xn = _rms_modulate(xin, ng_ref[...], sh_ref[0], sc_ref[0])
    prev, nxt = _row_neighbours(xn, row_len)
    xx = 0.5 * (prev + nxt) - xn

    def mix(i):
        return xn + xx * mu_ref[i:i + 1, :]

    r = _mm(mix(0), wrkv_ref[0])
    k = _mm(mix(1), wrkv_ref[1])
    v = _mm(mix(2), wrkv_ref[2])
    r_out[0] = r.astype(r_out.dtype)
    v_out[0] = v.astype(v_out.dtype)
    g_out[0] = _mm(_sigmoid(_mm(mix(5), g1_ref[...])), g2_ref[...]).astype(g_out.dtype)

    kk = k * kk_ref[...]
    kk = kk * lax.rsqrt(jnp.maximum(_head_sum(kk * kk), L2_EPS * L2_EPS))

    wl = jnp.tanh(_mm(mix(3), w1_ref[...]))
    al = _mm(mix(4), a1_ref[...])
    rows = xn.shape[0]
    ti = lax.broadcasted_iota(jnp.int32, (rows, rows), 0)
    tj = lax.broadcasted_iota(jnp.int32, (rows, rows), 1)
    same_chunk = (ti ^ tj) < CHUNK
    bonus_dot = None
    for p, (cs_out, al_out, kd_out, b_out) in enumerate(((cs0_out, al0_out, kd0_out, b0_out),
                                                         (cs1_out, al1_out, kd1_out, b1_out))):
        z = w0_ref[p:p + 1, :] + _mm(wl, w2_ref[p])
        lw = -jnp.exp(-0.5) * _sigmoid(z)
        tri = jnp.where(same_chunk & ((tj >= ti) if p else (tj <= ti)), 1.0, 0.0).astype(BF16)
        hi, lo = _split2(lw)
        cs_out[0] = jnp.dot(tri, hi, preferred_element_type=F32) + jnp.dot(tri, lo, preferred_element_type=F32)
        al_out[0] = (-kk * jnp.exp(-lw)).astype(al_out.dtype)
        a = _sigmoid(a0_ref[p:p + 1, :] + _mm(al, a2_ref[p]))
        kd = k * (1.0 + (a - 1.0) * ka_ref[...])
        kd_out[0] = kd.astype(kd_out.dtype)
        b_out[0] = (kk * a).astype(b_out.dtype)
        t = r * kd * rk_ref[...]
        bonus_dot = t if bonus_dot is None else bonus_dot + t
    bon_out[0] = (_head_sum(bonus_dot) * v).astype(bon_out.dtype)


def _rwkv_pre(ctx, x, mod, norm_g, mu, w_rkv, w1, w2, w0, a1, a2, a0, g1, g2, k_k, k_a, r_k):
    n_batch, t, d = x.shape
    ctx_len = ctx.shape[1]
    b, tt = n_batch, ctx_len + t
    assert ctx_len == TOKEN_TILE and t % TOKEN_TILE == 0

    def mod_spec(part):
        return pl.BlockSpec((1, 1, d), lambda i, j: (jnp.where(j == 0, n_batch, i), 0, part))

    def full(a):
        nd = a.ndim
        return pl.BlockSpec(a.shape, lambda i, j: (0,) * nd)

    tile = pl.BlockSpec((1, TOKEN_TILE, d), lambda i, j: (i, j, 0))
    params = (norm_g, mu, w_rkv, w1, w2, w0, a1, a2, a0, g1, g2, k_k, k_a, r_k)
    out_dtypes = (BF16,) * 4 + (F32, F32) + (BF16,) * 6
    return pl.pallas_call(
        functools.partial(_rwkv_pre_kernel, ctx_len),
        grid=(b, tt // TOKEN_TILE),
        in_specs=[pl.BlockSpec((1, TOKEN_TILE, d), lambda i, j: (i, 0, 0)),
                  pl.BlockSpec((1, TOKEN_TILE, d), lambda i, j: (i, jnp.maximum(j - 1, 0), 0)),
                  mod_spec(0), mod_spec(1)] + [full(a) for a in params],
        out_specs=[tile] * len(out_dtypes),
        out_shape=[jax.ShapeDtypeStruct((b, tt, d), dt) for dt in out_dtypes],
        compiler_params=pltpu.CompilerParams(
            dimension_semantics=("arbitrary", "arbitrary"), vmem_limit_bytes=VMEM_LIMIT_BYTES),
        name="rwkv_pre",
    )(ctx, x, mod, mod, *params)


def _expand(x, left):
    return jnp.concatenate([jnp.where(left, x, 0.0), jnp.where(left, 0.0, x)], axis=0)


def _dot(a, b):
    return jnp.dot(a, b, preferred_element_type=F32)


def _chunk_steps(chains):
    assert CHUNK == HEAD_DIM
    c = CHUNK
    lane = lax.broadcasted_iota(jnp.int32, (c, LANES), 1)
    left = lane < HEAD_DIM
    tt = lax.broadcasted_iota(jnp.int32, (c, LANES), 0)
    jj = lane & (c - 1)
    diag = jj == tt
    tri = {False: (jj < tt, jj <= tt), True: (jj > tt, jj >= tt)}

    def bd(x):
        return _expand(x, left).astype(BF16)

    def rows(*xs):
        return jnp.concatenate(xs, axis=0).astype(BF16)

    nt_dims = (((1,), (1,)), ((), ()))
    eye = jnp.where(diag, 1.0, 0.0).astype(BF16)

    prep = []
    for cs, alw, r, kd, v, be, h, reverse in chains:
        tot = cs[0:1, :] if reverse else cs[c - 1:c, :]
        e_pos, e_neg, e_rem = jnp.exp(cs), jnp.exp(-cs), jnp.exp(tot - cs)
        at, rt = alw * e_pos, r * e_pos
        q = dict(a_e=bd(at), rt=rt, v_e=bd(v), g_tot=jnp.exp(tot), h_e=bd(h), tri=tri[reverse])
        q["hat_t"] = lax.dot_general(eye, jnp.concatenate([bd(be * e_rem), bd(kd * e_rem)], axis=0), nt_dims,
                                     preferred_element_type=F32)
        q["sc"] = lax.dot_general(rows(at, rt), jnp.concatenate([bd(be * e_neg), bd(kd * e_neg)], axis=0),
                                  nt_dims, preferred_element_type=F32)
        prep.append(q)

    for q in prep:
        (strict, incl), sc = q["tri"], q["sc"]
        q["n"] = jnp.where(strict, sc[:c, :LANES], 0.0)
        a_ak = jnp.where(strict, sc[:c, LANES:], 0.0)
        q["a_rb"] = jnp.where(incl, sc[c:, :LANES], 0.0)
        a_rk = jnp.where(incl, sc[c:, LANES:], 0.0)
        q["bh_t"] = q["hat_t"][:, :LANES]
        q["vv"] = _dot(rows(a_ak, q["hat_t"][:, LANES:], a_rk), q["v_e"])

    for q in prep:
        q["t"] = jnp.where(diag, 1.0, 0.0) + q["n"]
        q["p"] = _dot(q["n"].astype(BF16), bd(q["n"]))
    order = 4
    while order < c:
        for q in prep:
            x = _dot(rows(q["p"], q["t"]), bd(q["p"]))
            q["p"] = x[:c]
            q["t"] = q["t"] + x[c:]
        order *= 2
    for q in prep:
        q["t"] = q["t"] + _dot(q["t"].astype(BF16), bd(q["p"]))

    for q in prep:
        xu = _dot(q["t"].astype(BF16), jnp.concatenate([q["a_e"], bd(q["vv"][:c])], axis=1))
        q["au_e"] = jnp.concatenate([bd(xu[:, :LANES]), bd(xu[:, LANES:])], axis=1)
    for q in prep:
        z = _dot(rows(q["bh_t"], q["a_rb"]), q["au_e"])
        q["m"] = z[:c, :LANES] + jnp.where(diag, q["g_tot"], 0.0)
        q["g"] = z[:c, LANES:] + q["vv"][c:2 * c]
        q["r_hat"] = q["rt"] + z[c:, :LANES]
        q["yi"] = z[c:, LANES:] + q["vv"][2 * c:]
    outs = []
    for q in prep:
        o = _dot(rows(q["r_hat"], q["m"]), q["h_e"])
        outs.append((o[:c] + q["yi"], o[c:] + q["g"]))
    return outs


def _wkv_kernel(rf_ref, vf_ref, csf_ref, af_ref, kdf_ref, bf_ref,
                rr_ref, vr_ref, csr_ref, ar_ref, kdr_ref, br_ref,
                yf_ref, yr_ref, hf_ref, hr_ref):
    @pl.when(pl.program_id(2) == 0)
    def _():
        hf_ref[...] = jnp.zeros_like(hf_ref)
        hr_ref[...] = jnp.zeros_like(hr_ref)

    dirs = ((rf_ref, vf_ref, csf_ref, af_ref, kdf_ref, bf_ref, yf_ref, hf_ref),
            (rr_ref, vr_ref, csr_ref, ar_ref, kdr_ref, br_ref, yr_ref, hr_ref))
    chains, dests = [], []
    for p in range(PAIRS_PER_STEP):
        sl = slice(p * LANES, (p + 1) * LANES)
        for reverse, (r_ref, v_ref, cs_ref, a_ref, kd_ref, b_ref, y_ref, h_ref) in enumerate(dirs):
            chains.append((cs_ref[0, :, sl], a_ref[0, :, sl].astype(F32), r_ref[0, :, sl].astype(F32),
                           kd_ref[0, :, sl].astype(F32), v_ref[0, :, sl].astype(F32),
                           b_ref[0, :, sl].astype(F32), h_ref[p], bool(reverse)))
            dests.append((y_ref, h_ref, p, sl))
    for (y, h_new), (y_ref, h_ref, p, sl) in zip(_chunk_steps(chains), dests):
        y_ref[0, :, sl] = y
        h_ref[p] = h_new


def _wkv(r, v, cs0, al0, kd0, b0, cs1, al1, kd1, b1, ctx_len):
    b, tt, d = r.shape
    n_chunks = tt // CHUNK
    n_ctx = ctx_len // CHUNK
    width = PAIRS_PER_STEP * LANES

    def fwd_map(i, j, s):
        return (i, s, j)

    def rev_map(i, j, s):
        return (i, jnp.where(s < n_ctx, n_ctx - 1 - s, n_chunks - 1 - (s - n_ctx)), j)

    fwd = pl.BlockSpec((1, CHUNK, width), fwd_map)
    rev = pl.BlockSpec((1, CHUNK, width), rev_map)
    return pl.pallas_call(
        _wkv_kernel,
        grid=(b, d // width, n_chunks),
        in_specs=[fwd] * 6 + [rev] * 6,
        out_specs=[fwd, rev],
        out_shape=[jax.ShapeDtypeStruct((b, tt, d), F32)] * 2,
        scratch_shapes=[pltpu.VMEM((PAIRS_PER_STEP, HEAD_DIM, LANES), F32)] * 2,
        compiler_params=pltpu.CompilerParams(
            dimension_semantics=("arbitrary", "arbitrary", "arbitrary"), vmem_limit_bytes=VMEM_LIMIT_BYTES),
        name="wkv",
    )(r, v, cs0, al0, kd0, b0, r, v, cs1, al1, kd1, b1)


def _pack_bf16_pairs(x):
    half = x.shape[1] // 2
    lo = pltpu.bitcast(x[:, :half].astype(BF16).astype(F32), jnp.int32)
    hi = pltpu.bitcast(x[:, half:].astype(BF16).astype(F32), jnp.int32)
    return lax.shift_right_logical(lo, jnp.int32(16)) | (hi & jnp.int32(-65536))


def _unpack_bf16_pairs(w):
    lo = pltpu.bitcast(lax.shift_left(w, jnp.int32(16)), F32)
    hi = pltpu.bitcast(w & jnp.int32(-65536), F32)
    return jnp.concatenate([lo, hi], axis=1)


def _route(h1, ng, shift, scale, rwt_ref, rb_ref, hm_out, eidx_out, gate_out):
    hm = _rms_modulate(h1, ng, shift, scale)
    hm_out[...] = _pack_bf16_pairs(hm)
    w_hi, w_mid = _split2(rwt_ref[...])
    h_hi, h_mid = _split2(hm)
    nt = (((1,), (1,)), ((), ()))
    logits = (lax.dot_general(w_hi, h_hi, nt, preferred_element_type=F32)
              + lax.dot_general(w_hi, h_mid, nt, preferred_element_type=F32)
              + lax.dot_general(w_mid, h_hi, nt, preferred_element_type=F32))
    s = _sigmoid(logits)
    sel = s + rb_ref[...]
    a = [sel[e:e + 1, :] for e in range(N_EXPERTS)]
    best = best_idx = None
    for g in range(N_GROUPS):
        m = a[g * EXPERTS_PER_GROUP:(g + 1) * EXPERTS_PER_GROUP]
        score = None
        for i in range(EXPERTS_PER_GROUP):
            for k in range(i + 1, EXPERTS_PER_GROUP):
                pair = m[i] + m[k]
                score = pair if score is None else jnp.maximum(score, pair)
        if best is None:
            best, best_idx = score, jnp.zeros_like(score, dtype=jnp.int32)
        else:
            upd = score > best
            best_idx = jnp.where(upd, g, best_idx)
            best = jnp.where(upd, score, best)
    n_seen = jnp.zeros_like(best_idx)
    e_slot = [jnp.zeros_like(best_idx), jnp.zeros_like(best_idx)]
    g_slot = [jnp.zeros_like(best), jnp.zeros_like(best)]
    for e in range(N_EXPERTS):
        g = e // EXPERTS_PER_GROUP
        ahead = jnp.zeros_like(best_idx)
        for k in range(g * EXPERTS_PER_GROUP, (g + 1) * EXPERTS_PER_GROUP):
            if k == e:
                continue
            beats = (a[k] >= a[e]) if k < e else (a[k] > a[e])
            ahead = ahead + beats.astype(jnp.int32)
        chosen = (best_idx == g) & (ahead < 2)
        for slot in range(2):
            hit = chosen & (n_seen == slot)
            e_slot[slot] = jnp.where(hit, e, e_slot[slot])
            g_slot[slot] = jnp.where(hit, s[e:e + 1, :], g_slot[slot])
        n_seen = n_seen + chosen.astype(jnp.int32)
    eidx_out[...] = jnp.concatenate(e_slot, axis=0)
    gate_out[...] = jnp.concatenate(g_slot, axis=0) / (g_slot[0] + g_slot[1])


def _readout_kernel(yf_ref, yr_ref, bon_ref, g_ref, h_ref, gt_ref, gnw_ref, gnb_ref, wo_ref,
                    ng_ref, sh_ref, sc_ref, rwt_ref, rb_ref, h1_out, hm_out, eidx_out, gate_out):
    y = yf_ref[0] + yr_ref[0] + bon_ref[0].astype(F32)
    mean = _head_sum(y) * (1.0 / HEAD_DIM)
    yc = y - mean
    var = _head_sum(yc * yc) * (1.0 / HEAD_DIM)
    yn = yc * lax.rsqrt(var + GN_EPS) * gnw_ref[...] + gnb_ref[...]
    out = _mm(yn * g_ref[0].astype(F32), wo_ref[...])
    h1 = h_ref[0] + gt_ref[0] * out
    h1_out[0] = h1
    _route(h1, ng_ref[...], sh_ref[0], sc_ref[0], rwt_ref, rb_ref, hm_out, eidx_out, gate_out)


def _mod_spec3(part, d):
    return pl.BlockSpec((1, 1, d), lambda i, j: (i, 0, part))


def _route_out_specs(b, t, d):
    nt = t // TOKEN_TILE
    specs = [pl.BlockSpec((1, TOKEN_TILE, d), lambda i, j: (i, j, 0)),
             pl.BlockSpec((TOKEN_TILE, d // 2), lambda i, j: (i * nt + j, 0)),
             pl.BlockSpec((2, TOKEN_TILE), lambda i, j: (0, i * nt + j)),
             pl.BlockSpec((2, TOKEN_TILE), lambda i, j: (0, i * nt + j))]
    shapes = [jax.ShapeDtypeStruct((b, t, d), F32), jax.ShapeDtypeStruct((b * t, d // 2), jnp.int32),
              jax.ShapeDtypeStruct((2, b * t), jnp.int32), jax.ShapeDtypeStruct((2, b * t), F32)]
    return specs, shapes


def _full2(a):
    nd = a.ndim
    return pl.BlockSpec(a.shape, lambda i, j: (0,) * nd)


def _readout(yf, yr, bonus, g, h, mod, ctx_len, gn_w, gn_b, w_o, norm_g, router_wt, router_b):
    b, t, d = h.shape
    off = ctx_len // TOKEN_TILE
    nt = t // TOKEN_TILE
    shifted = pl.BlockSpec((1, TOKEN_TILE, d), lambda i, j: (i, j + off, 0))
    tile = pl.BlockSpec((1, TOKEN_TILE, d), lambda i, j: (i, j, 0))
    params_a = (gn_w, gn_b, w_o, norm_g)
    params_b = (router_wt, router_b)
    out_specs, out_shape = _route_out_specs(b, t, d)
    return pl.pallas_call(
        _readout_kernel,
        grid=(b, nt),
        in_specs=[shifted] * 4 + [tile, _mod_spec3(2, d)] + [_full2(a) for a in params_a]
        + [_mod_spec3(3, d), _mod_spec3(4, d)] + [_full2(a) for a in params_b],
        out_specs=out_specs,
        out_shape=out_shape,
        compiler_params=pltpu.CompilerParams(
            dimension_semantics=("arbitrary", "arbitrary"), vmem_limit_bytes=VMEM_LIMIT_BYTES),
        name="readout",
    )(yf, yr, bonus, g, h, mod, *params_a, mod, mod, *params_b)


def _moe_residual(y0_ref, y1_ref, gate_ref, h1, gt):
    gates = gate_ref[...]
    y = gates[:, 0:1] * _unpack_bf16_pairs(y0_ref[...]) + gates[:, 1:2] * _unpack_bf16_pairs(y1_ref[...])
    return h1 + gt * y


def _sconv_kernel(y0_ref, y1_ref, gate_ref, hp_ref, gtp_ref, sh1_ref, sc1_ref, gt_ref, ng1_ref, win_ref, cw_ref,
                  wout_ref, ng2_ref, sh2_ref, sc2_ref, rwt_ref, rb_ref, h1_out, hm_out, eidx_out, gate_out):
    h = _moe_residual(y0_ref, y1_ref, gate_ref, hp_ref[0], gtp_ref[0])
    d = h.shape[1]
    xn = _rms_modulate(h, ng1_ref[...], sh1_ref[0], sc1_ref[0]).astype(BF16)
    bg = jnp.dot(xn, win_ref[:, 0:d], preferred_element_type=F32)
    u = (jnp.dot(xn, win_ref[:, d:2 * d], preferred_element_type=F32)
         * jnp.dot(xn, win_ref[:, 2 * d:3 * d], preferred_element_type=F32))
    prev, nxt = _row_neighbours(u, CHUNK)
    conv = cw_ref[0:1, :] * prev + cw_ref[1:2, :] * u + cw_ref[2:3, :] * nxt
    h1 = h + gt_ref[0] * _mm(bg * conv, wout_ref[...])
    h1_out[0] = h1
    _route(h1, ng2_ref[...], sh2_ref[0], sc2_ref[0], rwt_ref, rb_ref, hm_out, eidx_out, gate_out)


def _sconv(yg, gates, h_prev, mod_prev, mod, norm_g1, w_in, conv_w, w_out, norm_g2, router_wt, router_b):
    b, t, d = h_prev.shape
    nt = t // TOKEN_TILE
    tile = pl.BlockSpec((1, TOKEN_TILE, d), lambda i, j: (i, j, 0))
    params_a = (norm_g1, w_in, conv_w, w_out, norm_g2)
    params_b = (router_wt, router_b)
    out_specs, out_shape = _route_out_specs(b, t, d)
    return pl.pallas_call(
        _sconv_kernel,
        grid=(b, nt),
        in_specs=_moe_out_specs(b, t, d) + [tile, _mod_spec3(5, d)]
        + [_mod_spec3(0, d), _mod_spec3(1, d), _mod_spec3(2, d)] + [_full2(a) for a in params_a]
        + [_mod_spec3(3, d), _mod_spec3(4, d)] + [_full2(a) for a in params_b],
        out_specs=out_specs,
        out_shape=out_shape,
        compiler_params=pltpu.CompilerParams(
            dimension_semantics=("arbitrary", "arbitrary"), vmem_limit_bytes=VMEM_LIMIT_BYTES),
        name="sconv",
    )(yg, yg, gates, h_prev, mod_prev, mod, mod, mod, *params_a, mod, mod, *params_b)


def _positions_kernel(eidx_ref, pos_ref, ends_ref, run_ref):
    phase, i = pl.program_id(0), pl.program_id(1)
    tp = eidx_ref.shape[1]
    expert = lax.broadcasted_iota(jnp.int32, (N_EXPERTS, tp), 0)
    onehot = [jnp.where(eidx_ref[k:k + 1, :] == expert, 1.0, 0.0) for k in range(2)]

    @pl.when((phase == 0) & (i == 0))
    def _():
        run_ref[...] = jnp.zeros_like(run_ref)

    @pl.when(phase == 0)
    def _():
        run_ref[...] += jnp.sum(onehot[0] + onehot[1], axis=1, keepdims=True)

    @pl.when((phase == 1) & (i == 0))
    def _():
        padded = jnp.floor((run_ref[...] + (EXPERT_ROW_TILE - 1)) * (1.0 / EXPERT_ROW_TILE)) * EXPERT_ROW_TILE
        ei = lax.broadcasted_iota(jnp.int32, (N_EXPERTS, N_EXPERTS), 0)
        ej = lax.broadcasted_iota(jnp.int32, (N_EXPERTS, N_EXPERTS), 1)
        below = jnp.where(ej < ei, 1.0, 0.0)
        starts = jnp.dot(below, jnp.broadcast_to(padded, (N_EXPERTS, LANES)), precision=lax.Precision.HIGHEST,
                         preferred_element_type=F32)
        ends_ref[...] = (starts + padded).astype(jnp.int32)
        run_ref[...] = starts[:, 0:1]

    @pl.when(phase == 1)
    def _():
        ti = lax.broadcasted_iota(jnp.int32, (tp, tp), 0)
        tj = lax.broadcasted_iota(jnp.int32, (tp, tp), 1)
        before = jnp.where(ti < tj, 1.0, 0.0).astype(BF16)
        run = run_ref[...]
        for k in range(2):
            prefix = jnp.dot(onehot[k].astype(BF16), before, preferred_element_type=F32)
            pos_ref[k:k + 1, :] = jnp.sum(onehot[k] * (prefix + run), axis=0, keepdims=True).astype(jnp.int32)
            run = run + jnp.sum(onehot[k], axis=1, keepdims=True)
        run_ref[...] = run


def _positions(eidx):
    n = eidx.shape[1]
    nt = n // POSITION_TILE
    return pl.pallas_call(
        _positions_kernel,
        grid=(2, nt),
        in_specs=[pl.BlockSpec((2, POSITION_TILE), lambda p, i: (0, i))],
        out_specs=[pl.BlockSpec((2, POSITION_TILE), lambda p, i: (0, i * p)),
                   pl.BlockSpec((N_EXPERTS, LANES), lambda p, i: (0, 0))],
        out_shape=[jax.ShapeDtypeStruct((2, n), jnp.int32), jax.ShapeDtypeStruct((N_EXPERTS, LANES), jnp.int32)],
        scratch_shapes=[pltpu.VMEM((N_EXPERTS, 1), F32)],
        compiler_params=pltpu.CompilerParams(
            dimension_semantics=("arbitrary", "arbitrary"), vmem_limit_bytes=VMEM_LIMIT_BYTES),
        name="positions",
    )(eidx)


def _sc_mesh():
    return plsc.VectorSubcoreMesh(core_axis_name="c", subcore_axis_name="s")


def _sc_worker(n_workers_per_core=SC_SUBCORES):
    return lax.axis_index("c") * n_workers_per_core + lax.axis_index("s")


def _sc_invert(pos, n_rows):
    n_slots = pos.shape[0]
    n_tokens = n_slots // 2
    assert n_tokens & (n_tokens - 1) == 0
    per_worker = n_rows // SC_WORKERS
    assert per_worker * SC_WORKERS == n_rows and per_worker % SC_LANES == 0 and n_slots % SC_LANES == 0

    def body(pos_hbm, inv_hbm, pos_v, inv_v):
        base = _sc_worker() * per_worker
        pltpu.sync_copy(pos_hbm, pos_v)

        @pl.loop(0, per_worker, step=SC_LANES)
        def _(j):
            inv_v[pl.ds(j, SC_LANES)] = (base + j + lax.iota(jnp.int32, SC_LANES)) & (n_tokens - 1)

        @pl.loop(0, n_slots, step=SC_LANES)
        def _(s):
            local = pos_v[pl.ds(s, SC_LANES)] - base
            mine = (local >= 0) & (local < per_worker)
            token = (s + lax.iota(jnp.int32, SC_LANES)) & (n_tokens - 1)
            plsc.store_scatter(inv_v, [jnp.where(mine, local, 0)], token, mask=mine)

        pltpu.sync_copy(inv_v, inv_hbm.at[pl.ds(base, per_worker)])

    return pl.kernel(
        body, out_type=jax.ShapeDtypeStruct((n_rows,), jnp.int32), mesh=_sc_mesh(),
        scratch_types=[pltpu.VMEM((n_slots,), jnp.int32), pltpu.VMEM((per_worker,), jnp.int32)],
        compiler_params=pltpu.CompilerParams(needs_layout_passes=False),
        name="sc_invert",
    )(pos)


def _sc_gather(table, idx):
    n_rows, width = idx.shape[0], table.shape[1]
    per_worker = n_rows // SC_WORKERS
    assert per_worker * SC_WORKERS == n_rows and per_worker % SC_GATHER_ROWS == 0

    def body(table_hbm, idx_hbm, out_hbm, idx_v, rows_v, sem):
        base = _sc_worker() * per_worker

        @pl.loop(0, per_worker, step=SC_GATHER_ROWS)
        def _(j):
            pltpu.sync_copy(idx_hbm.at[pl.ds(base + j, SC_GATHER_ROWS)], idx_v)
            pltpu.async_copy(table_hbm.at[idx_v], rows_v, sem).wait()
            pltpu.sync_copy(rows_v, out_hbm.at[pl.ds(base + j, SC_GATHER_ROWS)])

    return pl.kernel(
        body, out_type=jax.ShapeDtypeStruct((n_rows, width), table.dtype), mesh=_sc_mesh(),
        scratch_types=[pltpu.VMEM((SC_GATHER_ROWS,), jnp.int32), pltpu.VMEM((SC_GATHER_ROWS, width), table.dtype),
                       pltpu.SemaphoreType.DMA],
        name="sc_gather",
    )(table, idx)


def _experts_kernel(te_ref, xs_ref, wg_ref, wu_ref, wd_ref, ys_ref):
    @pl.when(te_ref[pl.program_id(0)] < N_EXPERTS)
    def _():
        x = _unpack_bf16_pairs(xs_ref[...]).astype(BF16)
        gate = jnp.dot(x, wg_ref[0], preferred_element_type=F32)
        up = jnp.dot(x, wu_ref[0], preferred_element_type=F32)
        he = (gate * _sigmoid(gate) * up).astype(BF16)
        ys_ref[...] = _pack_bf16_pairs(jnp.dot(he, wd_ref[0], preferred_element_type=F32))


def _experts(tile_expert, xs, w_gate, w_up, w_down):
    n_rows, half = xs.shape
    n_e, d, de = w_gate.shape

    def w_map(i, te):
        return (jnp.minimum(te[i], n_e - 1), 0, 0)

    rows = pl.BlockSpec((EXPERT_ROW_TILE, half), lambda i, te: (i, 0))
    return pl.pallas_call(
        _experts_kernel,
        grid_spec=pltpu.PrefetchScalarGridSpec(
            num_scalar_prefetch=1,
            grid=(n_rows // EXPERT_ROW_TILE,),
            in_specs=[rows, pl.BlockSpec((1, d, de), w_map), pl.BlockSpec((1, d, de), w_map),
                      pl.BlockSpec((1, de, d), w_map)],
            out_specs=rows),
        out_shape=jax.ShapeDtypeStruct((n_rows, half), jnp.int32),
        compiler_params=pltpu.CompilerParams(
            dimension_semantics=("arbitrary",), vmem_limit_bytes=VMEM_LIMIT_BYTES),
        name="experts",
    )(tile_expert, xs, w_gate, w_up, w_down)


def _final_kernel(y0_ref, y1_ref, gate_ref, h1_ref, gt_ref, fg_ref, o_ref):
    h2 = _moe_residual(y0_ref, y1_ref, gate_ref, h1_ref[0], gt_ref[0])
    ms = jnp.mean(h2 * h2, axis=-1, keepdims=True)
    o_ref[0] = h2 * lax.rsqrt(ms + RMS_EPS) * fg_ref[...]


def _moe_out_specs(b, t, d):
    nt = t // TOKEN_TILE
    return [pl.BlockSpec((TOKEN_TILE, d // 2), lambda i, j: (i * nt + j, 0)),
            pl.BlockSpec((TOKEN_TILE, d // 2), lambda i, j: (b * nt + i * nt + j, 0)),
            pl.BlockSpec((TOKEN_TILE, 2), lambda i, j: (i * nt + j, 0))]


def _final(yg, gates, h1, mod, final_g):
    b, t, d = h1.shape
    tile = pl.BlockSpec((1, TOKEN_TILE, d), lambda i, j: (i, j, 0))
    return pl.pallas_call(
        _final_kernel,
        grid=(b, t // TOKEN_TILE),
        in_specs=_moe_out_specs(b, t, d) + [tile, _mod_spec3(5, d), pl.BlockSpec((1, d), lambda i, j: (0, 0))],
        out_specs=tile,
        out_shape=jax.ShapeDtypeStruct((b, t, d), F32),
        compiler_params=pltpu.CompilerParams(
            dimension_semantics=("arbitrary", "arbitrary"), vmem_limit_bytes=VMEM_LIMIT_BYTES),
        name="final",
    )(yg, yg, gates, h1, mod, final_g)


def _moe(hm, eidx, gates, w_gate, w_up, w_down):
    n_slots = 2 * hm.shape[0]
    n_rows = n_slots + N_EXPERTS * EXPERT_ROW_TILE
    pos, ends = _positions(eidx)
    pos = pos.reshape(n_slots)
    tile_start = jnp.arange(n_rows // EXPERT_ROW_TILE, dtype=jnp.int32) * EXPERT_ROW_TILE
    tile_expert = jnp.sum(tile_start[:, None] >= ends[None, :, 0], axis=1).astype(jnp.int32)
    xs = _sc_gather(hm, _sc_invert(pos, n_rows))
    ys = _experts(tile_expert, xs, w_gate, w_up, w_down)
    return _sc_gather(ys, pos), gates.T


def kernel(x, c, ctx, c_ctx, ada_w, ada_b, norm_g, rw_mu, rw_w_rkv, rw_w0, rw_w1, rw_w2, rw_a0, rw_a1, rw_a2, rw_g1, rw_g2, rw_k_k, rw_k_a, rw_r_k, rw_gn_w, rw_gn_b, rw_w_o, sc_w_in, sc_conv, sc_w_out, router_w, router_b, moe_w_gate, moe_w_up, moe_w_down, final_g):
    b, t, d = x.shape
    ctx_len = ctx.shape[1]
    depth = ada_w.shape[0]
    assert d == D_MODEL and depth == 2 and t % POSITION_TILE == 0

    mod_rows = 16
    cc = jnp.concatenate([c, c_ctx[None, :], jnp.zeros((mod_rows - b - 1, d), F32)], axis=0)
    mod = _ada(cc, ada_w, ada_b).reshape(depth, mod_rows, 1, 6 * d)

    row = lambda a: a.reshape(1, d)
    router_wt = router_w.T
    router_bc = router_b.reshape(N_EXPERTS, 1)

    w1 = jnp.concatenate([rw_w1[0, 0], rw_w1[0, 1]], axis=1).astype(BF16)
    a1 = jnp.concatenate([rw_a1[0, 0], rw_a1[0, 1]], axis=1).astype(BF16)

    def pad_dirs(w):
        z = jnp.zeros_like(w[0])
        return jnp.stack([jnp.concatenate([w[0], z], axis=0), jnp.concatenate([z, w[1]], axis=0)]).astype(BF16)

    (r, v, g, bonus, cs0, cs1, al0, al1, kd0, kd1, b0, b1) = _rwkv_pre(
        ctx, x, mod[0], row(norm_g[0, 0]), rw_mu[0], rw_w_rkv[0].astype(BF16), w1,
        pad_dirs(rw_w2[0]), rw_w0[0], a1, pad_dirs(rw_a2[0]), rw_a0[0],
        rw_g1[0].astype(BF16), rw_g2[0].astype(BF16), row(rw_k_k[0]), row(rw_k_a[0]), row(rw_r_k[0]))
    yf, yr = _wkv(r, v, cs0, al0, kd0, b0, cs1, al1, kd1, b1, ctx_len)
    h1, hm, eidx, gates = _readout(yf, yr, bonus, g, x, mod[0], ctx_len, row(rw_gn_w[0]), row(rw_gn_b[0]),
                                   rw_w_o[0].astype(BF16), row(norm_g[0, 1]), router_wt, router_bc)
    yg, gates = _moe(hm, eidx, gates, moe_w_gate[0].astype(BF16), moe_w_up[0].astype(BF16),
                     moe_w_down[0].astype(BF16))

    h1, hm, eidx, gates = _sconv(yg, gates, h1, mod[0], mod[1], row(norm_g[1, 0]), sc_w_in[0].astype(BF16),
                                 sc_conv[0], sc_w_out[0].astype(BF16), row(norm_g[1, 1]), router_wt, router_bc)
    yg, gates = _moe(hm, eidx, gates, moe_w_gate[1].astype(BF16), moe_w_up[1].astype(BF16),
                     moe_w_down[1].astype(BF16))
    return _final(yg, gates, h1, mod[1], row(final_g))
```

```python
import functools

import jax
import jax.numpy as jnp
from jax import lax
from jax.experimental import pallas as pl
from jax.experimental.pallas import tpu as pltpu
from jax.experimental.pallas import tpu_sc as plsc

F32 = jnp.float32
BF16 = jnp.bfloat16

D_MODEL = 1024
HEAD_DIM = 64
N_HEADS = D_MODEL // HEAD_DIM
LANES = 128
N_PAIRS = D_MODEL // LANES
CHUNK = 64
N_EXPERTS = 16
EXPERTS_PER_GROUP = 4
N_GROUPS = N_EXPERTS // EXPERTS_PER_GROUP
RMS_EPS = 1e-6
GN_EPS = 64e-5
L2_EPS = 1e-12

TOKEN_TILE = 256
EXPERT_ROW_TILE = 512
POSITION_TILE = 512
SC_CORES, SC_SUBCORES, SC_LANES = 2, 16, 16
SC_WORKERS = SC_CORES * SC_SUBCORES
N_STREAMS = 2
SC_GATHER_ROWS = 128
PAIRS_PER_STEP = 8
ADA_COL_TILE = 1536
VMEM_LIMIT_BYTES = 56 * 1024 * 1024


def _sigmoid(x):
    return 1.0 / (1.0 + jnp.exp(-x))


def _mm(a, b):
    return jnp.dot(a.astype(BF16), b.astype(BF16), preferred_element_type=F32)


def _mm_nt(a, b):
    return lax.dot_general(a.astype(BF16), b.astype(BF16), (((1,), (1,)), ((), ())),
                           preferred_element_type=F32)


def _split2(x):
    hi = x.astype(BF16)
    return hi, (x - hi.astype(F32)).astype(BF16)


def _head_sum(x):
    rows = x.shape[0]
    left = lax.broadcasted_iota(jnp.int32, (rows, LANES), 1) < HEAD_DIM
    outs = []
    for j in range(N_PAIRS):
        xb = x[:, LANES * j:LANES * (j + 1)]
        sa = jnp.sum(jnp.where(left, xb, 0.0), axis=-1, keepdims=True)
        sb = jnp.sum(jnp.where(left, 0.0, xb), axis=-1, keepdims=True)
        outs.append(jnp.where(left, sa, sb))
    return jnp.concatenate(outs, axis=-1)


def _rms_modulate(x, g, shift, scale):
    ms = jnp.mean(x * x, axis=-1, keepdims=True)
    xn = x * lax.rsqrt(ms + RMS_EPS) * g
    return xn * (1.0 + scale) + shift


def _row_neighbours(x, row_len):
    rows = x.shape[0]
    pos = lax.broadcasted_iota(jnp.int32, x.shape, 0) & (row_len - 1)
    prev = jnp.where(pos == 0, 0.0, pltpu.roll(x, 1, 0))
    nxt = jnp.where(pos == row_len - 1, 0.0, pltpu.roll(x, rows - 1, 0))
    return prev, nxt


def _ada_kernel(c_ref, w_ref, b_ref, o_ref):
    c = c_ref[...]
    s = c * _sigmoid(c)
    o_ref[0] = _mm(s, w_ref[0]) + b_ref[0]


def _ada(cc, ada_w, ada_b):
    depth, d, n = ada_w.shape
    rows = cc.shape[0]
    return pl.pallas_call(
        _ada_kernel,
        grid=(depth, n // ADA_COL_TILE),
        in_specs=[
            pl.BlockSpec((rows, d), lambda l, j: (0, 0)),
            pl.BlockSpec((1, d, ADA_COL_TILE), lambda l, j: (l, 0, j)),
            pl.BlockSpec((1, 1, ADA_COL_TILE), lambda l, j: (l, 0, j)),
        ],
        out_specs=pl.BlockSpec((1, rows, ADA_COL_TILE), lambda l, j: (l, 0, j)),
        out_shape=jax.ShapeDtypeStruct((depth, rows, n), F32),
        compiler_params=pltpu.CompilerParams(
            dimension_semantics=("arbitrary", "arbitrary"), vmem_limit_bytes=VMEM_LIMIT_BYTES),
        name="ada",
    )(cc, ada_w, ada_b.reshape(depth, 1, n))


def _rwkv_pre_kernel(ctx_len, ctx_ref, x_ref, sh_ref, sc_ref, ng_ref, mu_ref, wrkv_ref, w1_ref, w2_ref, w0_ref,
                     a1_ref, a2_ref, a0_ref, g1_ref, g2_ref, kk_ref, ka_ref, rk_ref,
                     r_out, v_out, g_out, bon_out, cs0_out, cs1_out, al0_out, al1_out, kd0_out, kd1_out,
                     b0_out, b1_out):
    j = pl.program_id(1)
    row_len = jnp.where(j == 0, ctx_len, CHUNK)
    xin = jnp.where(j == 0, ctx_ref[0], x_ref[0])
    xn = _rms_modulate(xin, ng_ref[...], sh_ref[0], sc_ref[0])
    prev, nxt = _row_neighbours(xn, row_len)
    xx = 0.5 * (prev + nxt) - xn

    def mix(i):
        return xn + xx * mu_ref[i:i + 1, :]

    r = _mm(mix(0), wrkv_ref[0])
    k = _mm(mix(1), wrkv_ref[1])
    v = _mm(mix(2), wrkv_ref[2])
    r_out[0] = r.astype(r_out.dtype)
    v_out[0] = v.astype(v_out.dtype)
    g_out[0] = _mm(_sigmoid(_mm(mix(5), g1_ref[...])), g2_ref[...]).astype(g_out.dtype)

    kk = k * kk_ref[...]
    kk = kk * lax.rsqrt(jnp.maximum(_head_sum(kk * kk), L2_EPS * L2_EPS))

    wl = jnp.tanh(_mm(mix(3), w1_ref[...]))
    al = _mm(mix(4), a1_ref[...])
    rows = xn.shape[0]
    ti = lax.broadcasted_iota(jnp.int32, (rows, rows), 0)
    tj = lax.broadcasted_iota(jnp.int32, (rows, rows), 1)
    same_chunk = (ti ^ tj) < CHUNK
    bonus_dot = None
    for p, (cs_out, al_out, kd_out, b_out) in enumerate(((cs0_out, al0_out, kd0_out, b0_out),
                                                         (cs1_out, al1_out, kd1_out, b1_out))):
        z = w0_ref[p:p + 1, :] + _mm(wl, w2_ref[p])
        lw = -jnp.exp(-0.5) * _sigmoid(z)
        tri = jnp.where(same_chunk & ((tj >= ti) if p else (tj <= ti)), 1.0, 0.0).astype(BF16)
        hi, lo = _split2(lw)
        cs_out[0] = jnp.dot(tri, hi, preferred_element_type=F32) + jnp.dot(tri, lo, preferred_element_type=F32)
        al_out[0] = (-kk * jnp.exp(-lw)).astype(al_out.dtype)
        a = _sigmoid(a0_ref[p:p + 1, :] + _mm(al, a2_ref[p]))
        kd = k * (1.0 + (a - 1.0) * ka_ref[...])
        kd_out[0] = kd.astype(kd_out.dtype)
        b_out[0] = (kk * a).astype(b_out.dtype)
        t = r * kd * rk_ref[...]
        bonus_dot = t if bonus_dot is None else bonus_dot + t
    bon_out[0] = (_head_sum(bonus_dot) * v).astype(bon_out.dtype)


def _rwkv_pre(ctx, x, mod, norm_g, mu, w_rkv, w1, w2, w0, a1, a2, a0, g1, g2, k_k, k_a, r_k):
    n_batch, t, d = x.shape
    ctx_len = ctx.shape[1]
    b, tt = n_batch, ctx_len + t
    assert ctx_len == TOKEN_TILE and t % TOKEN_TILE == 0

    def mod_spec(part):
        return pl.BlockSpec((1, 1, d), lambda i, j: (jnp.where(j == 0, n_batch, i), 0, part))

    def full(a):
        nd = a.ndim
        return pl.BlockSpec(a.shape, lambda i, j: (0,) * nd)

    tile = pl.BlockSpec((1, TOKEN_TILE, d), lambda i, j: (i, j, 0))
    params = (norm_g, mu, w_rkv, w1, w2, w0, a1, a2, a0, g1, g2, k_k, k_a, r_k)
    out_dtypes = (BF16,) * 4 + (F32, F32) + (BF16,) * 6
    return pl.pallas_call(
        functools.partial(_rwkv_pre_kernel, ctx_len),
        grid=(b, tt // TOKEN_TILE),
        in_specs=[pl.BlockSpec((1, TOKEN_TILE, d), lambda i, j: (i, 0, 0)),
                  pl.BlockSpec((1, TOKEN_TILE, d), lambda i, j: (i, jnp.maximum(j - 1, 0), 0)),
                  mod_spec(0), mod_spec(1)] + [full(a) for a in params],
        out_specs=[tile] * len(out_dtypes),
        out_shape=[jax.ShapeDtypeStruct((b, tt, d), dt) for dt in out_dtypes],
        compiler_params=pltpu.CompilerParams(
            dimension_semantics=("arbitrary", "arbitrary"), vmem_limit_bytes=VMEM_LIMIT_BYTES),
        name="rwkv_pre",
    )(ctx, x, mod, mod, *params)


def _expand(x, left):
    return jnp.concatenate([jnp.where(left, x, 0.0), jnp.where(left, 0.0, x)], axis=0)


def _dot(a, b):
    return jnp.dot(a, b, preferred_element_type=F32)


def _chunk_steps(chains):
    assert CHUNK == HEAD_DIM
    c = CHUNK
    lane = lax.broadcasted_iota(jnp.int32, (c, LANES), 1)
    left = lane < HEAD_DIM
    tt = lax.broadcasted_iota(jnp.int32, (c, LANES), 0)
    jj = lane & (c - 1)
    diag = jj == tt
    tri = {False: (jj < tt, jj <= tt), True: (jj > tt, jj >= tt)}

    def bd(x):
        return _expand(x, left).astype(BF16)

    def rows(*xs):
        return jnp.concatenate(xs, axis=0).astype(BF16)

    nt_dims = (((1,), (1,)), ((), ()))
    eye = jnp.where(diag, 1.0, 0.0).astype(BF16)

    prep = []
    for cs, alw, r, kd, v, be, h, reverse in chains:
        tot = cs[0:1, :] if reverse else cs[c - 1:c, :]
        e_pos, e_neg, e_rem = jnp.exp(cs), jnp.exp(-cs), jnp.exp(tot - cs)
        at, rt = alw * e_pos, r * e_pos
        q = dict(a_e=bd(at), rt=rt, v_e=bd(v), g_tot=jnp.exp(tot), h_e=bd(h), tri=tri[reverse])
        q["hat_t"] = lax.dot_general(eye, jnp.concatenate([bd(be * e_rem), bd(kd * e_rem)], axis=0), nt_dims,
                                     preferred_element_type=F32)
        q["sc"] = lax.dot_general(rows(at, rt), jnp.concatenate([bd(be * e_neg), bd(kd * e_neg)], axis=0),
                                  nt_dims, preferred_element_type=F32)
        prep.append(q)

    for q in prep:
        (strict, incl), sc = q["tri"], q["sc"]
        q["n"] = jnp.where(strict, sc[:c, :LANES], 0.0)
        a_ak = jnp.where(strict, sc[:c, LANES:], 0.0)
        q["a_rb"] = jnp.where(incl, sc[c:, :LANES], 0.0)
        a_rk = jnp.where(incl, sc[c:, LANES:], 0.0)
        q["bh_t"] = q["hat_t"][:, :LANES]
        q["vv"] = _dot(rows(a_ak, q["hat_t"][:, LANES:], a_rk), q["v_e"])

    for q in prep:
        q["t"] = jnp.where(diag, 1.0, 0.0) + q["n"]
        q["p"] = _dot(q["n"].astype(BF16), bd(q["n"]))
    order = 4
    while order < c:
        for q in prep:
            x = _dot(rows(q["p"], q["t"]), bd(q["p"]))
            q["p"] = x[:c]
            q["t"] = q["t"] + x[c:]
        order *= 2
    for q in prep:
        q["t"] = q["t"] + _dot(q["t"].astype(BF16), bd(q["p"]))

    for q in prep:
        xu = _dot(q["t"].astype(BF16), jnp.concatenate([q["a_e"], bd(q["vv"][:c])], axis=1))
        q["au_e"] = jnp.concatenate([bd(xu[:, :LANES]), bd(xu[:, LANES:])], axis=1)
    for q in prep:
        z = _dot(rows(q["bh_t"], q["a_rb"]), q["au_e"])
        q["m"] = z[:c, :LANES] + jnp.where(diag, q["g_tot"], 0.0)
        q["g"] = z[:c, LANES:] + q["vv"][c:2 * c]
        q["r_hat"] = q["rt"] + z[c:, :LANES]
        q["yi"] = z[c:, LANES:] + q["vv"][2 * c:]
    outs = []
    for q in prep:
        o = _dot(rows(q["r_hat"], q["m"]), q["h_e"])
        outs.append((o[:c] + q["yi"], o[c:] + q["g"]))
    return outs


def _wkv_kernel(rf_ref, vf_ref, csf_ref, af_ref, kdf_ref, bf_ref,
                rr_ref, vr_ref, csr_ref, ar_ref, kdr_ref, br_ref,
                yf_ref, yr_ref, hf_ref, hr_ref):
    @pl.when(pl.program_id(2) == 0)
    def _():
        hf_ref[...] = jnp.zeros_like(hf_ref)
        hr_ref[...] = jnp.zeros_like(hr_ref)

    dirs = ((rf_ref, vf_ref, csf_ref, af_ref, kdf_ref, bf_ref, yf_ref, hf_ref),
            (rr_ref, vr_ref, csr_ref, ar_ref, kdr_ref, br_ref, yr_ref, hr_ref))
    chains, dests = [], []
    for p in range(PAIRS_PER_STEP):
        sl = slice(p * LANES, (p + 1) * LANES)
        for reverse, (r_ref, v_ref, cs_ref, a_ref, kd_ref, b_ref, y_ref, h_ref) in enumerate(dirs):
            chains.append((cs_ref[0, :, sl], a_ref[0, :, sl].astype(F32), r_ref[0, :, sl].astype(F32),
                           kd_ref[0, :, sl].astype(F32), v_ref[0, :, sl].astype(F32),
                           b_ref[0, :, sl].astype(F32), h_ref[p], bool(reverse)))
            dests.append((y_ref, h_ref, p, sl))
    for (y, h_new), (y_ref, h_ref, p, sl) in zip(_chunk_steps(chains), dests):
        y_ref[0, :, sl] = y
        h_ref[p] = h_new


def _wkv(r, v, cs0, al0, kd0, b0, cs1, al1, kd1, b1, ctx_len):
    b, tt, d = r.shape
    n_chunks = tt // CHUNK
    n_ctx = ctx_len // CHUNK
    width = PAIRS_PER_STEP * LANES

    def fwd_map(i, j, s):
        return (i, s, j)

    def rev_map(i, j, s):
        return (i, jnp.where(s < n_ctx, n_ctx - 1 - s, n_chunks - 1 - (s - n_ctx)), j)

    fwd = pl.BlockSpec((1, CHUNK, width), fwd_map)
    rev = pl.BlockSpec((1, CHUNK, width), rev_map)
    return pl.pallas_call(
        _wkv_kernel,
        grid=(b, d // width, n_chunks),
        in_specs=[fwd] * 6 + [rev] * 6,
        out_specs=[fwd, rev],
        out_shape=[jax.ShapeDtypeStruct((b, tt, d), F32)] * 2,
        scratch_shapes=[pltpu.VMEM((PAIRS_PER_STEP, HEAD_DIM, LANES), F32)] * 2,
        compiler_params=pltpu.CompilerParams(
            dimension_semantics=("arbitrary", "arbitrary", "arbitrary"), vmem_limit_bytes=VMEM_LIMIT_BYTES),
        name="wkv",
    )(r, v, cs0, al0, kd0, b0, r, v, cs1, al1, kd1, b1)


def _pack_bf16_pairs(x):
    half = x.shape[1] // 2
    lo = pltpu.bitcast(x[:, :half].astype(BF16).astype(F32), jnp.int32)
    hi = pltpu.bitcast(x[:, half:].astype(BF16).astype(F32), jnp.int32)
    return lax.shift_right_logical(lo, jnp.int32(16)) | (hi & jnp.int32(-65536))


def _unpack_bf16_pairs(w):
    lo = pltpu.bitcast(lax.shift_left(w, jnp.int32(16)), F32)
    hi = pltpu.bitcast(w & jnp.int32(-65536), F32)
    return jnp.concatenate([lo, hi], axis=1)


def _route(h1, ng, shift, scale, rwt_ref, rb_ref, hm_out, eidx_out, gate_out):
    hm = _rms_modulate(h1, ng, shift, scale)
    hm_out[...] = _pack_bf16_pairs(hm)
    w_hi, w_mid = _split2(rwt_ref[...])
    h_hi, h_mid = _split2(hm)
    nt = (((1,), (1,)), ((), ()))
    logits = (lax.dot_general(w_hi, h_hi, nt, preferred_element_type=F32)
              + lax.dot_general(w_hi, h_mid, nt, preferred_element_type=F32)
              + lax.dot_general(w_mid, h_hi, nt, preferred_element_type=F32))
    s = _sigmoid(logits)
    sel = s + rb_ref[...]
    a = [sel[e:e + 1, :] for e in range(N_EXPERTS)]
    best = best_idx = None
    for g in range(N_GROUPS):
        m = a[g * EXPERTS_PER_GROUP:(g + 1) * EXPERTS_PER_GROUP]
        score = None
        for i in range(EXPERTS_PER_GROUP):
            for k in range(i + 1, EXPERTS_PER_GROUP):
                pair = m[i] + m[k]
                score = pair if score is None else jnp.maximum(score, pair)
        if best is None:
            best, best_idx = score, jnp.zeros_like(score, dtype=jnp.int32)
        else:
            upd = score > best
            best_idx = jnp.where(upd, g, best_idx)
            best = jnp.where(upd, score, best)
    n_seen = jnp.zeros_like(best_idx)
    e_slot = [jnp.zeros_like(best_idx), jnp.zeros_like(best_idx)]
    g_slot = [jnp.zeros_like(best), jnp.zeros_like(best)]
    for e in range(N_EXPERTS):
        g = e // EXPERTS_PER_GROUP
        ahead = jnp.zeros_like(best_idx)
        for k in range(g * EXPERTS_PER_GROUP, (g + 1) * EXPERTS_PER_GROUP):
            if k == e:
                continue
            beats = (a[k] >= a[e]) if k < e else (a[k] > a[e])
            ahead = ahead + beats.astype(jnp.int32)
        chosen = (best_idx == g) & (ahead < 2)
        for slot in range(2):
            hit = chosen & (n_seen == slot)
            e_slot[slot] = jnp.where(hit, e, e_slot[slot])
            g_slot[slot] = jnp.where(hit, s[e:e + 1, :], g_slot[slot])
        n_seen = n_seen + chosen.astype(jnp.int32)
    eidx_out[...] = jnp.concatenate(e_slot, axis=0)
    gate_out[...] = jnp.concatenate(g_slot, axis=0) / (g_slot[0] + g_slot[1])


def _readout_kernel(yf_ref, yr_ref, bon_ref, g_ref, h_ref, gt_ref, gnw_ref, gnb_ref, wo_ref,
                    ng_ref, sh_ref, sc_ref, rwt_ref, rb_ref, h1_out, hm_out, eidx_out, gate_out):
    y = yf_ref[0] + yr_ref[0] + bon_ref[0].astype(F32)
    mean = _head_sum(y) * (1.0 / HEAD_DIM)
    yc = y - mean
    var = _head_sum(yc * yc) * (1.0 / HEAD_DIM)
    yn = yc * lax.rsqrt(var + GN_EPS) * gnw_ref[...] + gnb_ref[...]
    out = _mm(yn * g_ref[0].astype(F32), wo_ref[...])
    h1 = h_ref[0] + gt_ref[0] * out
    h1_out[0] = h1
    _route(h1, ng_ref[...], sh_ref[0], sc_ref[0], rwt_ref, rb_ref, hm_out, eidx_out, gate_out)


def _mod_spec3(part, d, batch0=0):
    return pl.BlockSpec((1, 1, d), lambda i, j: (i + batch0, 0, part))


def _route_out_specs(b, t, d):
    nt = t // TOKEN_TILE
    specs = [pl.BlockSpec((1, TOKEN_TILE, d), lambda i, j: (i, j, 0)),
             pl.BlockSpec((TOKEN_TILE, d // 2), lambda i, j: (i * nt + j, 0)),
             pl.BlockSpec((2, TOKEN_TILE), lambda i, j: (0, i * nt + j)),
             pl.BlockSpec((2, TOKEN_TILE), lambda i, j: (0, i * nt + j))]
    shapes = [jax.ShapeDtypeStruct((b, t, d), F32), jax.ShapeDtypeStruct((b * t, d // 2), jnp.int32),
              jax.ShapeDtypeStruct((2, b * t), jnp.int32), jax.ShapeDtypeStruct((2, b * t), F32)]
    return specs, shapes


def _full2(a):
    nd = a.ndim
    return pl.BlockSpec(a.shape, lambda i, j: (0,) * nd)


def _readout(yf, yr, bonus, g, h, mod, ctx_len, gn_w, gn_b, w_o, norm_g, router_wt, router_b, batch0, b):
    _, t, d = h.shape
    off = ctx_len // TOKEN_TILE
    nt = t // TOKEN_TILE
    shifted = pl.BlockSpec((1, TOKEN_TILE, d), lambda i, j: (i + batch0, j + off, 0))
    tile = pl.BlockSpec((1, TOKEN_TILE, d), lambda i, j: (i + batch0, j, 0))
    params_a = (gn_w, gn_b, w_o, norm_g)
    params_b = (router_wt, router_b)
    out_specs, out_shape = _route_out_specs(b, t, d)
    return pl.pallas_call(
        _readout_kernel,
        grid=(b, nt),
        in_specs=[shifted] * 4 + [tile, _mod_spec3(2, d, batch0)] + [_full2(a) for a in params_a]
        + [_mod_spec3(3, d, batch0), _mod_spec3(4, d, batch0)] + [_full2(a) for a in params_b],
        out_specs=out_specs,
        out_shape=out_shape,
        compiler_params=pltpu.CompilerParams(
            dimension_semantics=("arbitrary", "arbitrary"), vmem_limit_bytes=VMEM_LIMIT_BYTES),
        name="readout",
    )(yf, yr, bonus, g, h, mod, *params_a, mod, mod, *params_b)


def _moe_residual(y0_ref, y1_ref, gate_ref, h1, gt):
    gates = gate_ref[...]
    y = gates[:, 0:1] * _unpack_bf16_pairs(y0_ref[...]) + gates[:, 1:2] * _unpack_bf16_pairs(y1_ref[...])
    return h1 + gt * y


def _sconv_kernel(y0_ref, y1_ref, gate_ref, hp_ref, gtp_ref, sh1_ref, sc1_ref, gt_ref, ng1_ref, win_ref, cw_ref,
                  wout_ref, ng2_ref, sh2_ref, sc2_ref, rwt_ref, rb_ref, h1_out, hm_out, eidx_out, gate_out):
    h = _moe_residual(y0_ref, y1_ref, gate_ref, hp_ref[0], gtp_ref[0])
    d = h.shape[1]
    xn = _rms_modulate(h, ng1_ref[...], sh1_ref[0], sc1_ref[0]).astype(BF16)
    bg = jnp.dot(xn, win_ref[:, 0:d], preferred_element_type=F32)
    u = (jnp.dot(xn, win_ref[:, d:2 * d], preferred_element_type=F32)
         * jnp.dot(xn, win_ref[:, 2 * d:3 * d], preferred_element_type=F32))
    prev, nxt = _row_neighbours(u, CHUNK)
    conv = cw_ref[0:1, :] * prev + cw_ref[1:2, :] * u + cw_ref[2:3, :] * nxt
    h1 = h + gt_ref[0] * _mm(bg * conv, wout_ref[...])
    h1_out[0] = h1
    _route(h1, ng2_ref[...], sh2_ref[0], sc2_ref[0], rwt_ref, rb_ref, hm_out, eidx_out, gate_out)


def _sconv(yg, gates, h_prev, mod_prev, mod, norm_g1, w_in, conv_w, w_out, norm_g2, router_wt, router_b, batch0):
    b, t, d = h_prev.shape
    nt = t // TOKEN_TILE
    tile = pl.BlockSpec((1, TOKEN_TILE, d), lambda i, j: (i, j, 0))
    params_a = (norm_g1, w_in, conv_w, w_out, norm_g2)
    params_b = (router_wt, router_b)
    out_specs, out_shape = _route_out_specs(b, t, d)
    return pl.pallas_call(
        _sconv_kernel,
        grid=(b, nt),
        in_specs=_moe_out_specs(b, t, d) + [tile, _mod_spec3(5, d, batch0)]
        + [_mod_spec3(0, d, batch0), _mod_spec3(1, d, batch0), _mod_spec3(2, d, batch0)]
        + [_full2(a) for a in params_a]
        + [_mod_spec3(3, d, batch0), _mod_spec3(4, d, batch0)] + [_full2(a) for a in params_b],
        out_specs=out_specs,
        out_shape=out_shape,
        compiler_params=pltpu.CompilerParams(
            dimension_semantics=("arbitrary", "arbitrary"), vmem_limit_bytes=VMEM_LIMIT_BYTES),
        name="sconv",
    )(yg, yg, gates, h_prev, mod_prev, mod, mod, mod, *params_a, mod, mod, *params_b)


def _positions_kernel(eidx_ref, pos_ref, ends_ref, run_ref):
    phase, i = pl.program_id(0), pl.program_id(1)
    tp = eidx_ref.shape[1]
    expert = lax.broadcasted_iota(jnp.int32, (N_EXPERTS, tp), 0)
    onehot = [jnp.where(eidx_ref[k:k + 1, :] == expert, 1.0, 0.0) for k in range(2)]

    @pl.when((phase == 0) & (i == 0))
    def _():
        run_ref[...] = jnp.zeros_like(run_ref)

    @pl.when(phase == 0)
    def _():
        run_ref[...] += jnp.sum(onehot[0] + onehot[1], axis=1, keepdims=True)

    @pl.when((phase == 1) & (i == 0))
    def _():
        padded = jnp.floor((run_ref[...] + (EXPERT_ROW_TILE - 1)) * (1.0 / EXPERT_ROW_TILE)) * EXPERT_ROW_TILE
        ei = lax.broadcasted_iota(jnp.int32, (N_EXPERTS, N_EXPERTS), 0)
        ej = lax.broadcasted_iota(jnp.int32, (N_EXPERTS, N_EXPERTS), 1)
        below = jnp.where(ej < ei, 1.0, 0.0)
        starts = jnp.dot(below, jnp.broadcast_to(padded, (N_EXPERTS, LANES)), precision=lax.Precision.HIGHEST,
                         preferred_element_type=F32)
        ends_ref[...] = (starts + padded).astype(jnp.int32)
        run_ref[...] = starts[:, 0:1]

    @pl.when(phase == 1)
    def _():
        ti = lax.broadcasted_iota(jnp.int32, (tp, tp), 0)
        tj = lax.broadcasted_iota(jnp.int32, (tp, tp), 1)
        before = jnp.where(ti < tj, 1.0, 0.0).astype(BF16)
        run = run_ref[...]
        for k in range(2):
            prefix = jnp.dot(onehot[k].astype(BF16), before, preferred_element_type=F32)
            pos_ref[k:k + 1, :] = jnp.sum(onehot[k] * (prefix + run), axis=0, keepdims=True).astype(jnp.int32)
            run = run + jnp.sum(onehot[k], axis=1, keepdims=True)
        run_ref[...] = run


def _positions(eidx):
    n = eidx.shape[1]
    nt = n // POSITION_TILE
    return pl.pallas_call(
        _positions_kernel,
        grid=(2, nt),
        in_specs=[pl.BlockSpec((2, POSITION_TILE), lambda p, i: (0, i))],
        out_specs=[pl.BlockSpec((2, POSITION_TILE), lambda p, i: (0, i * p)),
                   pl.BlockSpec((N_EXPERTS, LANES), lambda p, i: (0, 0))],
        out_shape=[jax.ShapeDtypeStruct((2, n), jnp.int32), jax.ShapeDtypeStruct((N_EXPERTS, LANES), jnp.int32)],
        scratch_shapes=[pltpu.VMEM((N_EXPERTS, 1), F32)],
        compiler_params=pltpu.CompilerParams(
            dimension_semantics=("arbitrary", "arbitrary"), vmem_limit_bytes=VMEM_LIMIT_BYTES),
        name="positions",
    )(eidx)


def _sc_mesh():
    return plsc.VectorSubcoreMesh(core_axis_name="c", subcore_axis_name="s")


def _sc_worker(n_workers_per_core=SC_SUBCORES):
    return lax.axis_index("c") * n_workers_per_core + lax.axis_index("s")


def _sc_invert(pos, n_rows):
    n_slots = pos.shape[0]
    n_tokens = n_slots // 2
    assert n_tokens & (n_tokens - 1) == 0
    per_worker = n_rows // SC_WORKERS
    assert per_worker * SC_WORKERS == n_rows and per_worker % SC_LANES == 0 and n_slots % SC_LANES == 0

    def body(pos_hbm, inv_hbm, pos_v, inv_v):
        base = _sc_worker() * per_worker
        pltpu.sync_copy(pos_hbm, pos_v)

        @pl.loop(0, per_worker, step=SC_LANES)
        def _(j):
            inv_v[pl.ds(j, SC_LANES)] = (base + j + lax.iota(jnp.int32, SC_LANES)) & (n_tokens - 1)

        @pl.loop(0, n_slots, step=SC_LANES)
        def _(s):
            local = pos_v[pl.ds(s, SC_LANES)] - base
            mine = (local >= 0) & (local < per_worker)
            token = (s + lax.iota(jnp.int32, SC_LANES)) & (n_tokens - 1)
            plsc.store_scatter(inv_v, [jnp.where(mine, local, 0)], token, mask=mine)

        pltpu.sync_copy(inv_v, inv_hbm.at[pl.ds(base, per_worker)])

    return pl.kernel(
        body, out_type=jax.ShapeDtypeStruct((n_rows,), jnp.int32), mesh=_sc_mesh(),
        scratch_types=[pltpu.VMEM((n_slots,), jnp.int32), pltpu.VMEM((per_worker,), jnp.int32)],
        compiler_params=pltpu.CompilerParams(needs_layout_passes=False),
        name="sc_invert",
    )(pos)


def _sc_gather(table, idx):
    n_rows, width = idx.shape[0], table.shape[1]
    per_worker = n_rows // SC_WORKERS
    assert per_worker * SC_WORKERS == n_rows and per_worker % SC_GATHER_ROWS == 0

    def body(table_hbm, idx_hbm, out_hbm, idx_v, rows_v, sem):
        base = _sc_worker() * per_worker

        @pl.loop(0, per_worker, step=SC_GATHER_ROWS)
        def _(j):
            pltpu.sync_copy(idx_hbm.at[pl.ds(base + j, SC_GATHER_ROWS)], idx_v)
            pltpu.async_copy(table_hbm.at[idx_v], rows_v, sem).wait()
            pltpu.sync_copy(rows_v, out_hbm.at[pl.ds(base + j, SC_GATHER_ROWS)])

    return pl.kernel(
        body, out_type=jax.ShapeDtypeStruct((n_rows, width), table.dtype), mesh=_sc_mesh(),
        scratch_types=[pltpu.VMEM((SC_GATHER_ROWS,), jnp.int32), pltpu.VMEM((SC_GATHER_ROWS, width), table.dtype),
                       pltpu.SemaphoreType.DMA],
        name="sc_gather",
    )(table, idx)


def _experts_kernel(te_ref, xs_ref, wg_ref, wu_ref, wd_ref, ys_ref, wg_bf, wu_bf, wd_bf):
    i = pl.program_id(0)
    expert = te_ref[i]

    @pl.when((i == 0) | (expert != te_ref[jnp.maximum(i - 1, 0)]))
    def _():
        wg_bf[...] = wg_ref[0, 0].astype(BF16)
        wu_bf[...] = wu_ref[0, 0].astype(BF16)
        wd_bf[...] = wd_ref[0, 0].astype(BF16)

    @pl.when(expert < N_EXPERTS)
    def _():
        x = _unpack_bf16_pairs(xs_ref[...]).astype(BF16)
        gate = jnp.dot(x, wg_bf[...], preferred_element_type=F32)
        up = jnp.dot(x, wu_bf[...], preferred_element_type=F32)
        he = (gate * _sigmoid(gate) * up).astype(BF16)
        ys_ref[...] = _pack_bf16_pairs(jnp.dot(he, wd_bf[...], preferred_element_type=F32))


def _experts(tile_expert, xs, layer, w_gate, w_up, w_down):
    n_rows, half = xs.shape
    _, n_e, d, de = w_gate.shape

    def w_map(i, te):
        return (layer, jnp.minimum(te[i], n_e - 1), 0, 0)

    rows = pl.BlockSpec((EXPERT_ROW_TILE, half), lambda i, te: (i, 0))
    return pl.pallas_call(
        _experts_kernel,
        grid_spec=pltpu.PrefetchScalarGridSpec(
            num_scalar_prefetch=1,
            grid=(n_rows // EXPERT_ROW_TILE,),
            in_specs=[rows, pl.BlockSpec((1, 1, d, de), w_map), pl.BlockSpec((1, 1, d, de), w_map),
                      pl.BlockSpec((1, 1, de, d), w_map)],
            out_specs=rows,
            scratch_shapes=[pltpu.VMEM((d, de), BF16), pltpu.VMEM((d, de), BF16), pltpu.VMEM((de, d), BF16)]),
        out_shape=jax.ShapeDtypeStruct((n_rows, half), jnp.int32),
        compiler_params=pltpu.CompilerParams(
            dimension_semantics=("arbitrary",), vmem_limit_bytes=VMEM_LIMIT_BYTES),
        name="experts",
    )(tile_expert, xs, w_gate, w_up, w_down)


def _final_kernel(y0_ref, y1_ref, gate_ref, h1_ref, gt_ref, fg_ref, *out_refs):
    o_ref = out_refs[-1]
    h2 = _moe_residual(y0_ref, y1_ref, gate_ref, h1_ref[0], gt_ref[0])
    ms = jnp.mean(h2 * h2, axis=-1, keepdims=True)
    o_ref[0] = h2 * lax.rsqrt(ms + RMS_EPS) * fg_ref[...]


def _moe_out_specs(b, t, d):
    nt = t // TOKEN_TILE
    return [pl.BlockSpec((TOKEN_TILE, d // 2), lambda i, j: (i * nt + j, 0)),
            pl.BlockSpec((TOKEN_TILE, d // 2), lambda i, j: (b * nt + i * nt + j, 0)),
            pl.BlockSpec((TOKEN_TILE, 2), lambda i, j: (i * nt + j, 0))]


def _final(yg, gates, h1, mod, final_g, batch0, n_batch, out_prev):
    b, t, d = h1.shape
    tile = pl.BlockSpec((1, TOKEN_TILE, d), lambda i, j: (i, j, 0))
    in_specs = _moe_out_specs(b, t, d) + [tile, _mod_spec3(5, d, batch0), pl.BlockSpec((1, d), lambda i, j: (0, 0))]
    args = [yg, yg, gates, h1, mod, final_g]
    aliases = {}
    if out_prev is not None:
        in_specs.append(pl.BlockSpec(memory_space=pl.ANY))
        aliases = {len(args): 0}
        args.append(out_prev)
    return pl.pallas_call(
        _final_kernel,
        grid=(b, t // TOKEN_TILE),
        in_specs=in_specs,
        out_specs=pl.BlockSpec((1, TOKEN_TILE, d), lambda i, j: (i + batch0, j, 0)),
        out_shape=jax.ShapeDtypeStruct((n_batch, t, d), F32),
        input_output_aliases=aliases,
        compiler_params=pltpu.CompilerParams(
            dimension_semantics=("arbitrary", "arbitrary"), vmem_limit_bytes=VMEM_LIMIT_BYTES),
        name="final",
    )(*args)


def _moe(hm, eidx, gates, layer, w_gate, w_up, w_down):
    n_slots = 2 * hm.shape[0]
    n_rows = n_slots + N_EXPERTS * EXPERT_ROW_TILE
    pos, ends = _positions(eidx)
    pos = pos.reshape(n_slots)
    tile_start = jnp.arange(n_rows // EXPERT_ROW_TILE, dtype=jnp.int32) * EXPERT_ROW_TILE
    tile_expert = jnp.sum(tile_start[:, None] >= ends[None, :, 0], axis=1).astype(jnp.int32)
    xs = _sc_gather(hm, _sc_invert(pos, n_rows))
    ys = _experts(tile_expert, xs, layer, w_gate, w_up, w_down)
    return _sc_gather(ys, pos), gates.T


def kernel(x, c, ctx, c_ctx, ada_w, ada_b, norm_g, rw_mu, rw_w_rkv, rw_w0, rw_w1, rw_w2, rw_a0, rw_a1, rw_a2, rw_g1, rw_g2, rw_k_k, rw_k_a, rw_r_k, rw_gn_w, rw_gn_b, rw_w_o, sc_w_in, sc_conv, sc_w_out, router_w, router_b, moe_w_gate, moe_w_up, moe_w_down, final_g):
    b, t, d = x.shape
    ctx_len = ctx.shape[1]
    depth = ada_w.shape[0]
    assert d == D_MODEL and depth == 2 and t % POSITION_TILE == 0

    mod_rows = 16
    cc = jnp.concatenate([c, c_ctx[None, :], jnp.zeros((mod_rows - b - 1, d), F32)], axis=0)
    mod = _ada(cc, ada_w, ada_b).reshape(depth, mod_rows, 1, 6 * d)

    row = lambda a: a.reshape(1, d)
    router_wt = router_w.T
    router_bc = router_b.reshape(N_EXPERTS, 1)

    w1 = jnp.concatenate([rw_w1[0, 0], rw_w1[0, 1]], axis=1).astype(BF16)
    a1 = jnp.concatenate([rw_a1[0, 0], rw_a1[0, 1]], axis=1).astype(BF16)

    def pad_dirs(w):
        z = jnp.zeros_like(w[0])
        return jnp.stack([jnp.concatenate([w[0], z], axis=0), jnp.concatenate([z, w[1]], axis=0)]).astype(BF16)

    (r, v, g, bonus, cs0, cs1, al0, al1, kd0, kd1, b0, b1) = _rwkv_pre(
        ctx, x, mod[0], row(norm_g[0, 0]), rw_mu[0], rw_w_rkv[0].astype(BF16), w1,
        pad_dirs(rw_w2[0]), rw_w0[0], a1, pad_dirs(rw_a2[0]), rw_a0[0],
        rw_g1[0].astype(BF16), rw_g2[0].astype(BF16), row(rw_k_k[0]), row(rw_k_a[0]), row(rw_r_k[0]))
    yf, yr = _wkv(r, v, cs0, al0, kd0, b0, cs1, al1, kd1, b1, ctx_len)
    assert b % N_STREAMS == 0
    nb = b // N_STREAMS
    w_o, w_in, w_out = rw_w_o[0].astype(BF16), sc_w_in[0].astype(BF16), sc_w_out[0].astype(BF16)
    out = None
    for batch0 in range(0, b, nb):
        h1, hm, eidx, gates = _readout(yf, yr, bonus, g, x, mod[0], ctx_len, row(rw_gn_w[0]), row(rw_gn_b[0]),
                                       w_o, row(norm_g[0, 1]), router_wt, router_bc, batch0, nb)
        yg, gates = _moe(hm, eidx, gates, 0, moe_w_gate, moe_w_up, moe_w_down)
        h1, hm, eidx, gates = _sconv(yg, gates, h1, mod[0], mod[1], row(norm_g[1, 0]), w_in, sc_conv[0], w_out,
                                     row(norm_g[1, 1]), router_wt, router_bc, batch0)
        yg, gates = _moe(hm, eidx, gates, 1, moe_w_gate, moe_w_up, moe_w_down)
        out = _final(yg, gates, h1, mod[1], row(final_g), batch0, b, out)
    return out
```

```python
import functools

import jax
import jax.numpy as jnp
from jax import lax
from jax.experimental import pallas as pl
from jax.experimental.pallas import tpu as pltpu
from jax.experimental.pallas import tpu_sc as plsc

F32 = jnp.float32
BF16 = jnp.bfloat16

D_MODEL = 1024
HEAD_DIM = 64
N_HEADS = D_MODEL // HEAD_DIM
LANES = 128
N_PAIRS = D_MODEL // LANES
CHUNK = 64
N_EXPERTS = 16
EXPERTS_PER_GROUP = 4
N_GROUPS = N_EXPERTS // EXPERTS_PER_GROUP
RMS_EPS = 1e-6
GN_EPS = 64e-5
L2_EPS = 1e-12

TOKEN_TILE = 256
EXPERT_ROW_TILE = 512
POSITION_TILE = 1024
SC_CORES, SC_SUBCORES, SC_LANES = 2, 16, 16
SC_WORKERS = SC_CORES * SC_SUBCORES
N_STREAMS = 2
SC_GATHER_ROWS = 128
PAIRS_PER_STEP = 8
ADA_COL_TILE = 1536
VMEM_LIMIT_BYTES = 56 * 1024 * 1024


def _sigmoid(x):
    return 1.0 / (1.0 + jnp.exp(-x))


def _mm(a, b):
    return jnp.dot(a.astype(BF16), b.astype(BF16), preferred_element_type=F32)


def _mm_nt(a, b):
    return lax.dot_general(a.astype(BF16), b.astype(BF16), (((1,), (1,)), ((), ())),
                           preferred_element_type=F32)


def _split2(x):
    hi = x.astype(BF16)
    return hi, (x - hi.astype(F32)).astype(BF16)


def _head_sum(x):
    rows = x.shape[0]
    left = lax.broadcasted_iota(jnp.int32, (rows, LANES), 1) < HEAD_DIM
    outs = []
    for j in range(N_PAIRS):
        xb = x[:, LANES * j:LANES * (j + 1)]
        sa = jnp.sum(jnp.where(left, xb, 0.0), axis=-1, keepdims=True)
        sb = jnp.sum(jnp.where(left, 0.0, xb), axis=-1, keepdims=True)
        outs.append(jnp.where(left, sa, sb))
    return jnp.concatenate(outs, axis=-1)


def _rms_modulate(x, g, shift, scale):
    ms = jnp.mean(x * x, axis=-1, keepdims=True)
    xn = x * lax.rsqrt(ms + RMS_EPS) * g
    return xn * (1.0 + scale) + shift


def _row_neighbours(x, row_len):
    rows = x.shape[0]
    pos = lax.broadcasted_iota(jnp.int32, x.shape, 0) & (row_len - 1)
    prev = jnp.where(pos == 0, 0.0, pltpu.roll(x, 1, 0))
    nxt = jnp.where(pos == row_len - 1, 0.0, pltpu.roll(x, rows - 1, 0))
    return prev, nxt


def _ada_kernel(c_ref, w_ref, b_ref, o_ref):
    c = c_ref[...]
    s = c * _sigmoid(c)
    o_ref[0] = _mm(s, w_ref[0]) + b_ref[0]


def _ada(cc, ada_w, ada_b):
    depth, d, n = ada_w.shape
    rows = cc.shape[0]
    return pl.pallas_call(
        _ada_kernel,
        grid=(depth, n // ADA_COL_TILE),
        in_specs=[
            pl.BlockSpec((rows, d), lambda l, j: (0, 0)),
            pl.BlockSpec((1, d, ADA_COL_TILE), lambda l, j: (l, 0, j)),
            pl.BlockSpec((1, 1, ADA_COL_TILE), lambda l, j: (l, 0, j)),
        ],
        out_specs=pl.BlockSpec((1, rows, ADA_COL_TILE), lambda l, j: (l, 0, j)),
        out_shape=jax.ShapeDtypeStruct((depth, rows, n), F32),
        compiler_params=pltpu.CompilerParams(
            dimension_semantics=("arbitrary", "arbitrary"), vmem_limit_bytes=VMEM_LIMIT_BYTES),
        name="ada",
    )(cc, ada_w, ada_b.reshape(depth, 1, n))


def _rwkv_pre_kernel(ctx_len, ctx_ref, x_ref, sh_ref, sc_ref, ng_ref, mu_ref, wrkv_ref, w1_ref, w2_ref, w0_ref,
                     a1_ref, a2_ref, a0_ref, g1_ref, g2_ref, kk_ref, ka_ref, rk_ref,
                     r_out, v_out, g_out, bon_out, cs0_out, cs1_out, al0_out, al1_out, kd0_out, kd1_out,
                     b0_out, b1_out):
    j = pl.program_id(1)
    row_len = jnp.where(j == 0, ctx_len, CHUNK)
    xin = jnp.where(j == 0, ctx_ref[0], x_ref[0])
    xn = _rms_modulate(xin, ng_ref[...], sh_ref[0], sc_ref[0])
    prev, nxt = _row_neighbours(xn, row_len)
    xx = 0.5 * (prev + nxt) - xn

    def mix(i):
        return xn + xx * mu_ref[i:i + 1, :]

    r = _mm(mix(0), wrkv_ref[0])
    k = _mm(mix(1), wrkv_ref[1])
    v = _mm(mix(2), wrkv_ref[2])
    r_out[0] = r.astype(r_out.dtype)
    v_out[0] = v.astype(v_out.dtype)
    g_out[0] = _mm(_sigmoid(_mm(mix(5), g1_ref[...])), g2_ref[...]).astype(g_out.dtype)

    kk = k * kk_ref[...]
    kk = kk * lax.rsqrt(jnp.maximum(_head_sum(kk * kk), L2_EPS * L2_EPS))

    wl = jnp.tanh(_mm(mix(3), w1_ref[...]))
    al = _mm(mix(4), a1_ref[...])
    rows = xn.shape[0]
    ti = lax.broadcasted_iota(jnp.int32, (rows, rows), 0)
    tj = lax.broadcasted_iota(jnp.int32, (rows, rows), 1)
    same_chunk = (ti ^ tj) < CHUNK
    bonus_dot = None
    for p, (cs_out, al_out, kd_out, b_out) in enumerate(((cs0_out, al0_out, kd0_out, b0_out),
                                                         (cs1_out, al1_out, kd1_out, b1_out))):
        z = w0_ref[p:p + 1, :] + _mm(wl, w2_ref[p])
        lw = -jnp.exp(-0.5) * _sigmoid(z)
        tri = jnp.where(same_chunk & ((tj >= ti) if p else (tj <= ti)), 1.0, 0.0).astype(BF16)
        hi, lo = _split2(lw)
        cs_out[0] = jnp.dot(tri, hi, preferred_element_type=F32) + jnp.dot(tri, lo, preferred_element_type=F32)
        al_out[0] = (-kk * jnp.exp(-lw)).astype(al_out.dtype)
        a = _sigmoid(a0_ref[p:p + 1, :] + _mm(al, a2_ref[p]))
        kd = k * (1.0 + (a - 1.0) * ka_ref[...])
        kd_out[0] = kd.astype(kd_out.dtype)
        b_out[0] = (kk * a).astype(b_out.dtype)
        t = r * kd * rk_ref[...]
        bonus_dot = t if bonus_dot is None else bonus_dot + t
    bon_out[0] = (_head_sum(bonus_dot) * v).astype(bon_out.dtype)


def _rwkv_pre(ctx, x, mod, norm_g, mu, w_rkv, w1, w2, w0, a1, a2, a0, g1, g2, k_k, k_a, r_k):
    n_batch, t, d = x.shape
    ctx_len = ctx.shape[1]
    b, tt = n_batch, ctx_len + t
    assert ctx_len == TOKEN_TILE and t % TOKEN_TILE == 0

    def mod_spec(part):
        return pl.BlockSpec((1, 1, d), lambda i, j: (jnp.where(j == 0, n_batch, i), 0, part))

    def full(a):
        nd = a.ndim
        return pl.BlockSpec(a.shape, lambda i, j: (0,) * nd)

    tile = pl.BlockSpec((1, TOKEN_TILE, d), lambda i, j: (i, j, 0))
    params = (norm_g, mu, w_rkv, w1, w2, w0, a1, a2, a0, g1, g2, k_k, k_a, r_k)
    out_dtypes = (BF16,) * 4 + (F32, F32) + (BF16,) * 6
    return pl.pallas_call(
        functools.partial(_rwkv_pre_kernel, ctx_len),
        grid=(b, tt // TOKEN_TILE),
        in_specs=[pl.BlockSpec((1, TOKEN_TILE, d), lambda i, j: (i, 0, 0)),
                  pl.BlockSpec((1, TOKEN_TILE, d), lambda i, j: (i, jnp.maximum(j - 1, 0), 0)),
                  mod_spec(0), mod_spec(1)] + [full(a) for a in params],
        out_specs=[tile] * len(out_dtypes),
        out_shape=[jax.ShapeDtypeStruct((b, tt, d), dt) for dt in out_dtypes],
        compiler_params=pltpu.CompilerParams(
            dimension_semantics=("arbitrary", "arbitrary"), vmem_limit_bytes=VMEM_LIMIT_BYTES),
        name="rwkv_pre",
    )(ctx, x, mod, mod, *params)


def _expand(x, left):
    return jnp.concatenate([jnp.where(left, x, 0.0), jnp.where(left, 0.0, x)], axis=0)


def _dot(a, b):
    return jnp.dot(a, b, preferred_element_type=F32)


def _chunk_steps(chains):
    assert CHUNK == HEAD_DIM
    c = CHUNK
    lane = lax.broadcasted_iota(jnp.int32, (c, LANES), 1)
    left = lane < HEAD_DIM
    tt = lax.broadcasted_iota(jnp.int32, (c, LANES), 0)
    jj = lane & (c - 1)
    diag = jj == tt
    tri = {False: (jj < tt, jj <= tt), True: (jj > tt, jj >= tt)}

    def bd(x):
        return _expand(x, left).astype(BF16)

    def rows(*xs):
        return jnp.concatenate(xs, axis=0).astype(BF16)

    nt_dims = (((1,), (1,)), ((), ()))
    eye = jnp.where(diag, 1.0, 0.0).astype(BF16)

    prep = []
    for cs, alw, r, kd, v, be, h, reverse in chains:
        tot = cs[0:1, :] if reverse else cs[c - 1:c, :]
        e_pos, e_neg, e_rem = jnp.exp(cs), jnp.exp(-cs), jnp.exp(tot - cs)
        at, rt = alw * e_pos, r * e_pos
        q = dict(a_e=bd(at), rt=rt, v_e=bd(v), g_tot=jnp.exp(tot), h_e=bd(h), tri=tri[reverse])
        q["hat_t"] = lax.dot_general(eye, jnp.concatenate([bd(be * e_rem), bd(kd * e_rem)], axis=0), nt_dims,
                                     preferred_element_type=F32)
        q["sc"] = lax.dot_general(rows(at, rt), jnp.concatenate([bd(be * e_neg), bd(kd * e_neg)], axis=0),
                                  nt_dims, preferred_element_type=F32)
        prep.append(q)

    for q in prep:
        (strict, incl), sc = q["tri"], q["sc"]
        q["n"] = jnp.where(strict, sc[:c, :LANES], 0.0)
        a_ak = jnp.where(strict, sc[:c, LANES:], 0.0)
        q["a_rb"] = jnp.where(incl, sc[c:, :LANES], 0.0)
        a_rk = jnp.where(incl, sc[c:, LANES:], 0.0)
        q["bh_t"] = q["hat_t"][:, :LANES]
        q["vv"] = _dot(rows(a_ak, q["hat_t"][:, LANES:], a_rk), q["v_e"])

    for q in prep:
        q["t"] = jnp.where(diag, 1.0, 0.0) + q["n"]
        q["p"] = _dot(q["n"].astype(BF16), bd(q["n"]))
    order = 4
    while order < c:
        for q in prep:
            x = _dot(rows(q["p"], q["t"]), bd(q["p"]))
            q["p"] = x[:c]
            q["t"] = q["t"] + x[c:]
        order *= 2
    for q in prep:
        q["t"] = q["t"] + _dot(q["t"].astype(BF16), bd(q["p"]))

    for q in prep:
        xu = _dot(q["t"].astype(BF16), jnp.concatenate([q["a_e"], bd(q["vv"][:c])], axis=1))
        q["au_e"] = jnp.concatenate([bd(xu[:, :LANES]), bd(xu[:, LANES:])], axis=1)
    for q in prep:
        z = _dot(rows(q["bh_t"], q["a_rb"]), q["au_e"])
        q["m"] = z[:c, :LANES] + jnp.where(diag, q["g_tot"], 0.0)
        q["g"] = z[:c, LANES:] + q["vv"][c:2 * c]
        q["r_hat"] = q["rt"] + z[c:, :LANES]
        q["yi"] = z[c:, LANES:] + q["vv"][2 * c:]
    outs = []
    for q in prep:
        o = _dot(rows(q["r_hat"], q["m"]), q["h_e"])
        outs.append((o[:c] + q["yi"], o[c:] + q["g"]))
    return outs


def _wkv_kernel(rf_ref, vf_ref, csf_ref, af_ref, kdf_ref, bf_ref,
                rr_ref, vr_ref, csr_ref, ar_ref, kdr_ref, br_ref,
                yf_ref, yr_ref, hf_ref, hr_ref):
    @pl.when(pl.program_id(2) == 0)
    def _():
        hf_ref[...] = jnp.zeros_like(hf_ref)
        hr_ref[...] = jnp.zeros_like(hr_ref)

    dirs = ((rf_ref, vf_ref, csf_ref, af_ref, kdf_ref, bf_ref, yf_ref, hf_ref),
            (rr_ref, vr_ref, csr_ref, ar_ref, kdr_ref, br_ref, yr_ref, hr_ref))
    chains, dests = [], []
    for p in range(PAIRS_PER_STEP):
        sl = slice(p * LANES, (p + 1) * LANES)
        for reverse, (r_ref, v_ref, cs_ref, a_ref, kd_ref, b_ref, y_ref, h_ref) in enumerate(dirs):
            chains.append((cs_ref[0, :, sl], a_ref[0, :, sl].astype(F32), r_ref[0, :, sl].astype(F32),
                           kd_ref[0, :, sl].astype(F32), v_ref[0, :, sl].astype(F32),
                           b_ref[0, :, sl].astype(F32), h_ref[p], bool(reverse)))
            dests.append((y_ref, h_ref, p, sl))
    for (y, h_new), (y_ref, h_ref, p, sl) in zip(_chunk_steps(chains), dests):
        y_ref[0, :, sl] = y
        h_ref[p] = h_new


def _wkv(r, v, cs0, al0, kd0, b0, cs1, al1, kd1, b1, ctx_len):
    b, tt, d = r.shape
    n_chunks = tt // CHUNK
    n_ctx = ctx_len // CHUNK
    width = PAIRS_PER_STEP * LANES

    def fwd_map(i, j, s):
        return (i, s, j)

    def rev_map(i, j, s):
        return (i, jnp.where(s < n_ctx, n_ctx - 1 - s, n_chunks - 1 - (s - n_ctx)), j)

    fwd = pl.BlockSpec((1, CHUNK, width), fwd_map)
    rev = pl.BlockSpec((1, CHUNK, width), rev_map)
    return pl.pallas_call(
        _wkv_kernel,
        grid=(b, d // width, n_chunks),
        in_specs=[fwd] * 6 + [rev] * 6,
        out_specs=[fwd, rev],
        out_shape=[jax.ShapeDtypeStruct((b, tt, d), F32)] * 2,
        scratch_shapes=[pltpu.VMEM((PAIRS_PER_STEP, HEAD_DIM, LANES), F32)] * 2,
        compiler_params=pltpu.CompilerParams(
            dimension_semantics=("arbitrary", "arbitrary", "arbitrary"), vmem_limit_bytes=VMEM_LIMIT_BYTES),
        name="wkv",
    )(r, v, cs0, al0, kd0, b0, r, v, cs1, al1, kd1, b1)


def _pack_bf16_pairs(x):
    half = x.shape[1] // 2
    lo = pltpu.bitcast(x[:, :half].astype(BF16).astype(F32), jnp.int32)
    hi = pltpu.bitcast(x[:, half:].astype(BF16).astype(F32), jnp.int32)
    return lax.shift_right_logical(lo, jnp.int32(16)) | (hi & jnp.int32(-65536))


def _unpack_bf16_pairs(w):
    lo = pltpu.bitcast(lax.shift_left(w, jnp.int32(16)), F32)
    hi = pltpu.bitcast(w & jnp.int32(-65536), F32)
    return jnp.concatenate([lo, hi], axis=1)


def _route(h1, ng, shift, scale, rwt_ref, rb_ref, hm_out, eidx_out, gate_out):
    hm = _rms_modulate(h1, ng, shift, scale)
    hm_out[...] = _pack_bf16_pairs(hm)
    w_hi, w_mid = _split2(rwt_ref[...])
    h_hi, h_mid = _split2(hm)
    logits_t = _dot(h_hi, w_hi) + _dot(h_mid, w_hi) + _dot(h_hi, w_mid)
    logits = logits_t.T[:N_EXPERTS]
    s = _sigmoid(logits)
    sel = s + rb_ref[...]
    assert EXPERTS_PER_GROUP == 4 and N_GROUPS == 4
    cands = []
    for g in range(N_GROUPS):
        m = [sel[e:e + 1, :] for e in range(g * EXPERTS_PER_GROUP, (g + 1) * EXPERTS_PER_GROUP)]
        sg = [s[e:e + 1, :] for e in range(g * EXPERTS_PER_GROUP, (g + 1) * EXPERTS_PER_GROUP)]
        pairs = [m[i] + m[k] for i in range(4) for k in range(i + 1, 4)]
        score = jnp.maximum(jnp.maximum(jnp.maximum(pairs[0], pairs[1]), jnp.maximum(pairs[2], pairs[3])),
                            jnp.maximum(pairs[4], pairs[5]))
        chosen = []
        for i in range(4):
            ahead = [((m[k] >= m[i]) if k < i else (m[k] > m[i])).astype(jnp.int32) for k in range(4) if k != i]
            chosen.append(ahead[0] + ahead[1] + ahead[2] < 2)
        base = g * EXPERTS_PER_GROUP
        lo_idx = jnp.where(chosen[0], base, jnp.where(chosen[1], base + 1, base + 2))
        lo_gate = jnp.where(chosen[0], sg[0], jnp.where(chosen[1], sg[1], sg[2]))
        hi_idx = jnp.where(chosen[3], base + 3, jnp.where(chosen[2], base + 2, base + 1))
        hi_gate = jnp.where(chosen[3], sg[3], jnp.where(chosen[2], sg[2], sg[1]))
        cands.append((score, lo_idx, hi_idx, lo_gate, hi_gate))

    def better(x, y):
        win = y[0] > x[0]
        return tuple(jnp.where(win, yv, xv) for xv, yv in zip(x, y))

    _, e_lo, e_hi, g_lo, g_hi = better(better(cands[0], cands[1]), better(cands[2], cands[3]))
    eidx_out[...] = jnp.concatenate([e_lo, e_hi], axis=0)
    gate_out[...] = jnp.concatenate([g_lo, g_hi], axis=0) / (g_lo + g_hi)


def _readout_kernel(yf_ref, yr_ref, bon_ref, g_ref, h_ref, gt_ref, gnw_ref, gnb_ref, wo_ref,
                    ng_ref, sh_ref, sc_ref, rwt_ref, rb_ref, h1_out, hm_out, eidx_out, gate_out):
    y = yf_ref[0] + yr_ref[0] + bon_ref[0].astype(F32)
    mean = _head_sum(y) * (1.0 / HEAD_DIM)
    yc = y - mean
    var = _head_sum(yc * yc) * (1.0 / HEAD_DIM)
    yn = yc * lax.rsqrt(var + GN_EPS) * gnw_ref[...] + gnb_ref[...]
    out = _mm(yn * g_ref[0].astype(F32), wo_ref[...])
    h1 = h_ref[0] + gt_ref[0] * out
    h1_out[0] = h1
    _route(h1, ng_ref[...], sh_ref[0], sc_ref[0], rwt_ref, rb_ref, hm_out, eidx_out, gate_out)


def _mod_spec3(part, d, batch0=0):
    return pl.BlockSpec((1, 1, d), lambda i, j: (i + batch0, 0, part))


def _route_out_specs(b, t, d):
    nt = t // TOKEN_TILE
    specs = [pl.BlockSpec((1, TOKEN_TILE, d), lambda i, j: (i, j, 0)),
             pl.BlockSpec((TOKEN_TILE, d // 2), lambda i, j: (i * nt + j, 0)),
             pl.BlockSpec((2, TOKEN_TILE), lambda i, j: (0, i * nt + j)),
             pl.BlockSpec((2, TOKEN_TILE), lambda i, j: (0, i * nt + j))]
    shapes = [jax.ShapeDtypeStruct((b, t, d), F32), jax.ShapeDtypeStruct((b * t, d // 2), jnp.int32),
              jax.ShapeDtypeStruct((2, b * t), jnp.int32), jax.ShapeDtypeStruct((2, b * t), F32)]
    return specs, shapes


def _full2(a):
    nd = a.ndim
    return pl.BlockSpec(a.shape, lambda i, j: (0,) * nd)


def _readout(yf, yr, bonus, g, h, mod, ctx_len, gn_w, gn_b, w_o, norm_g, router_wt, router_b, batch0, b):
    _, t, d = h.shape
    off = ctx_len // TOKEN_TILE
    nt = t // TOKEN_TILE
    shifted = pl.BlockSpec((1, TOKEN_TILE, d), lambda i, j: (i + batch0, j + off, 0))
    tile = pl.BlockSpec((1, TOKEN_TILE, d), lambda i, j: (i + batch0, j, 0))
    params_a = (gn_w, gn_b, w_o, norm_g)
    params_b = (router_wt, router_b)
    out_specs, out_shape = _route_out_specs(b, t, d)
    return pl.pallas_call(
        _readout_kernel,
        grid=(b, nt),
        in_specs=[shifted] * 4 + [tile, _mod_spec3(2, d, batch0)] + [_full2(a) for a in params_a]
        + [_mod_spec3(3, d, batch0), _mod_spec3(4, d, batch0)] + [_full2(a) for a in params_b],
        out_specs=out_specs,
        out_shape=out_shape,
        compiler_params=pltpu.CompilerParams(
            dimension_semantics=("arbitrary", "arbitrary"), vmem_limit_bytes=VMEM_LIMIT_BYTES),
        name="readout",
    )(yf, yr, bonus, g, h, mod, *params_a, mod, mod, *params_b)


def _moe_residual(y0_ref, y1_ref, gate_ref, h1, gt):
    gates = gate_ref[...]
    y = gates[:, 0:1] * _unpack_bf16_pairs(y0_ref[...]) + gates[:, 1:2] * _unpack_bf16_pairs(y1_ref[...])
    return h1 + gt * y


def _sconv_kernel(y0_ref, y1_ref, gate_ref, hp_ref, gtp_ref, sh1_ref, sc1_ref, gt_ref, ng1_ref, win_ref, cw_ref,
                  wout_ref, ng2_ref, sh2_ref, sc2_ref, rwt_ref, rb_ref, h1_out, hm_out, eidx_out, gate_out):
    h = _moe_residual(y0_ref, y1_ref, gate_ref, hp_ref[0], gtp_ref[0])
    d = h.shape[1]
    xn = _rms_modulate(h, ng1_ref[...], sh1_ref[0], sc1_ref[0]).astype(BF16)
    bg = jnp.dot(xn, win_ref[:, 0:d], preferred_element_type=F32)
    u = (jnp.dot(xn, win_ref[:, d:2 * d], preferred_element_type=F32)
         * jnp.dot(xn, win_ref[:, 2 * d:3 * d], preferred_element_type=F32))
    prev, nxt = _row_neighbours(u, CHUNK)
    conv = cw_ref[0:1, :] * prev + cw_ref[1:2, :] * u + cw_ref[2:3, :] * nxt
    h1 = h + gt_ref[0] * _mm(bg * conv, wout_ref[...])
    h1_out[0] = h1
    _route(h1, ng2_ref[...], sh2_ref[0], sc2_ref[0], rwt_ref, rb_ref, hm_out, eidx_out, gate_out)


def _sconv(yg, gates, h_prev, mod_prev, mod, norm_g1, w_in, conv_w, w_out, norm_g2, router_wt, router_b, batch0):
    b, t, d = h_prev.shape
    nt = t // TOKEN_TILE
    tile = pl.BlockSpec((1, TOKEN_TILE, d), lambda i, j: (i, j, 0))
    params_a = (norm_g1, w_in, conv_w, w_out, norm_g2)
    params_b = (router_wt, router_b)
    out_specs, out_shape = _route_out_specs(b, t, d)
    return pl.pallas_call(
        _sconv_kernel,
        grid=(b, nt),
        in_specs=_moe_out_specs(b, t, d) + [tile, _mod_spec3(5, d, batch0)]
        + [_mod_spec3(0, d, batch0), _mod_spec3(1, d, batch0), _mod_spec3(2, d, batch0)]
        + [_full2(a) for a in params_a]
        + [_mod_spec3(3, d, batch0), _mod_spec3(4, d, batch0)] + [_full2(a) for a in params_b],
        out_specs=out_specs,
        out_shape=out_shape,
        compiler_params=pltpu.CompilerParams(
            dimension_semantics=("arbitrary", "arbitrary"), vmem_limit_bytes=VMEM_LIMIT_BYTES),
        name="sconv",
    )(yg, yg, gates, h_prev, mod_prev, mod, mod, mod, *params_a, mod, mod, *params_b)


def _positions_kernel(eidx_ref, pos_ref, ends_ref, run_ref):
    phase, i = pl.program_id(0), pl.program_id(1)
    tp = eidx_ref.shape[1]
    expert = lax.broadcasted_iota(jnp.int32, (N_EXPERTS, tp), 0)
    onehot = [jnp.where(eidx_ref[k:k + 1, :] == expert, 1.0, 0.0) for k in range(2)]

    @pl.when((phase == 0) & (i == 0))
    def _():
        run_ref[...] = jnp.zeros_like(run_ref)

    @pl.when(phase == 0)
    def _():
        run_ref[...] += jnp.sum(onehot[0] + onehot[1], axis=1, keepdims=True)

    @pl.when((phase == 1) & (i == 0))
    def _():
        padded = jnp.floor((run_ref[...] + (EXPERT_ROW_TILE - 1)) * (1.0 / EXPERT_ROW_TILE)) * EXPERT_ROW_TILE
        ei = lax.broadcasted_iota(jnp.int32, (N_EXPERTS, N_EXPERTS), 0)
        ej = lax.broadcasted_iota(jnp.int32, (N_EXPERTS, N_EXPERTS), 1)
        below = jnp.where(ej < ei, 1.0, 0.0)
        starts = jnp.dot(below, jnp.broadcast_to(padded, (N_EXPERTS, LANES)), precision=lax.Precision.HIGHEST,
                         preferred_element_type=F32)
        ends_ref[...] = (starts + padded).astype(jnp.int32)
        run_ref[...] = starts[:, 0:1]

    @pl.when(phase == 1)
    def _():
        ti = lax.broadcasted_iota(jnp.int32, (tp, tp), 0)
        tj = lax.broadcasted_iota(jnp.int32, (tp, tp), 1)
        before = jnp.where(ti < tj, 1.0, 0.0).astype(BF16)
        run = run_ref[...]
        for k in range(2):
            prefix = jnp.dot(onehot[k].astype(BF16), before, preferred_element_type=F32)
            pos_ref[k:k + 1, :] = jnp.sum(onehot[k] * (prefix + run), axis=0, keepdims=True).astype(jnp.int32)
            run = run + jnp.sum(onehot[k], axis=1, keepdims=True)
        run_ref[...] = run


def _positions(eidx):
    n = eidx.shape[1]
    nt = n // POSITION_TILE
    return pl.pallas_call(
        _positions_kernel,
        grid=(2, nt),
        in_specs=[pl.BlockSpec((2, POSITION_TILE), lambda p, i: (0, i))],
        out_specs=[pl.BlockSpec((2, POSITION_TILE), lambda p, i: (0, i * p)),
                   pl.BlockSpec((N_EXPERTS, LANES), lambda p, i: (0, 0))],
        out_shape=[jax.ShapeDtypeStruct((2, n), jnp.int32), jax.ShapeDtypeStruct((N_EXPERTS, LANES), jnp.int32)],
        scratch_shapes=[pltpu.VMEM((N_EXPERTS, 1), F32)],
        compiler_params=pltpu.CompilerParams(
            dimension_semantics=("arbitrary", "arbitrary"), vmem_limit_bytes=VMEM_LIMIT_BYTES),
        name="positions",
    )(eidx)


def _sc_mesh():
    return plsc.VectorSubcoreMesh(core_axis_name="c", subcore_axis_name="s")


def _sc_worker(n_workers_per_core=SC_SUBCORES):
    return lax.axis_index("c") * n_workers_per_core + lax.axis_index("s")


def _sc_invert(pos, n_rows):
    n_slots = pos.shape[0]
    n_tokens = n_slots // 2
    assert n_tokens & (n_tokens - 1) == 0
    per_worker = n_rows // SC_WORKERS
    assert per_worker * SC_WORKERS == n_rows and per_worker % SC_LANES == 0 and n_slots % SC_LANES == 0

    def body(pos_hbm, inv_hbm, pos_v, inv_v):
        base = _sc_worker() * per_worker
        pltpu.sync_copy(pos_hbm, pos_v)

        @pl.loop(0, per_worker, step=SC_LANES)
        def _(j):
            inv_v[pl.ds(j, SC_LANES)] = (base + j + lax.iota(jnp.int32, SC_LANES)) & (n_tokens - 1)

        @pl.loop(0, n_slots, step=SC_LANES)
        def _(s):
            local = pos_v[pl.ds(s, SC_LANES)] - base
            mine = (local >= 0) & (local < per_worker)
            token = (s + lax.iota(jnp.int32, SC_LANES)) & (n_tokens - 1)
            plsc.store_scatter(inv_v, [jnp.where(mine, local, 0)], token, mask=mine)

        pltpu.sync_copy(inv_v, inv_hbm.at[pl.ds(base, per_worker)])

    return pl.kernel(
        body, out_type=jax.ShapeDtypeStruct((n_rows,), jnp.int32), mesh=_sc_mesh(),
        scratch_types=[pltpu.VMEM((n_slots,), jnp.int32), pltpu.VMEM((per_worker,), jnp.int32)],
        compiler_params=pltpu.CompilerParams(needs_layout_passes=False),
        name="sc_invert",
    )(pos)


def _sc_gather(table, idx):
    n_rows, width = idx.shape[0], table.shape[1]
    per_worker = n_rows // SC_WORKERS
    assert per_worker * SC_WORKERS == n_rows and per_worker % SC_GATHER_ROWS == 0

    def body(table_hbm, idx_hbm, out_hbm, idx_v, rows_v, sem):
        base = _sc_worker() * per_worker

        @pl.loop(0, per_worker, step=SC_GATHER_ROWS)
        def _(j):
            pltpu.sync_copy(idx_hbm.at[pl.ds(base + j, SC_GATHER_ROWS)], idx_v)
            pltpu.async_copy(table_hbm.at[idx_v], rows_v, sem).wait()
            pltpu.sync_copy(rows_v, out_hbm.at[pl.ds(base + j, SC_GATHER_ROWS)])

    return pl.kernel(
        body, out_type=jax.ShapeDtypeStruct((n_rows, width), table.dtype), mesh=_sc_mesh(),
        scratch_types=[pltpu.VMEM((SC_GATHER_ROWS,), jnp.int32), pltpu.VMEM((SC_GATHER_ROWS, width), table.dtype),
                       pltpu.SemaphoreType.DMA],
        name="sc_gather",
    )(table, idx)


def _experts_kernel(te_ref, xs_ref, wg_ref, wu_ref, wd_ref, ys_ref, wg_bf, wu_bf, wd_bf):
    i = pl.program_id(0)
    expert = te_ref[i]

    @pl.when((i == 0) | (expert != te_ref[jnp.maximum(i - 1, 0)]))
    def _():
        wg_bf[...] = wg_ref[0, 0].astype(BF16)
        wu_bf[...] = wu_ref[0, 0].astype(BF16)
        wd_bf[...] = wd_ref[0, 0].astype(BF16)

    @pl.when(expert < N_EXPERTS)
    def _():
        x = _unpack_bf16_pairs(xs_ref[...]).astype(BF16)
        gate = jnp.dot(x, wg_bf[...], preferred_element_type=F32)
        up = jnp.dot(x, wu_bf[...], preferred_element_type=F32)
        he = (gate * _sigmoid(gate) * up).astype(BF16)
        ys_ref[...] = _pack_bf16_pairs(jnp.dot(he, wd_bf[...], preferred_element_type=F32))


def _experts(tile_expert, xs, layer, w_gate, w_up, w_down):
    n_rows, half = xs.shape
    _, n_e, d, de = w_gate.shape

    def w_map(i, te):
        return (layer, jnp.minimum(te[i], n_e - 1), 0, 0)

    rows = pl.BlockSpec((EXPERT_ROW_TILE, half), lambda i, te: (i, 0))
    return pl.pallas_call(
        _experts_kernel,
        grid_spec=pltpu.PrefetchScalarGridSpec(
            num_scalar_prefetch=1,
            grid=(n_rows // EXPERT_ROW_TILE,),
            in_specs=[rows, pl.BlockSpec((1, 1, d, de), w_map), pl.BlockSpec((1, 1, d, de), w_map),
                      pl.BlockSpec((1, 1, de, d), w_map)],
            out_specs=rows,
            scratch_shapes=[pltpu.VMEM((d, de), BF16), pltpu.VMEM((d, de), BF16), pltpu.VMEM((de, d), BF16)]),
        out_shape=jax.ShapeDtypeStruct((n_rows, half), jnp.int32),
        compiler_params=pltpu.CompilerParams(
            dimension_semantics=("arbitrary",), vmem_limit_bytes=VMEM_LIMIT_BYTES),
        name="experts",
    )(tile_expert, xs, w_gate, w_up, w_down)


def _final_kernel(y0_ref, y1_ref, gate_ref, h1_ref, gt_ref, fg_ref, *out_refs):
    o_ref = out_refs[-1]
    h2 = _moe_residual(y0_ref, y1_ref, gate_ref, h1_ref[0], gt_ref[0])
    ms = jnp.mean(h2 * h2, axis=-1, keepdims=True)
    o_ref[0] = h2 * lax.rsqrt(ms + RMS_EPS) * fg_ref[...]


def _moe_out_specs(b, t, d):
    nt = t // TOKEN_TILE
    return [pl.BlockSpec((TOKEN_TILE, d // 2), lambda i, j: (i * nt + j, 0)),
            pl.BlockSpec((TOKEN_TILE, d // 2), lambda i, j: (b * nt + i * nt + j, 0)),
            pl.BlockSpec((TOKEN_TILE, 2), lambda i, j: (i * nt + j, 0))]


def _final(yg, gates, h1, mod, final_g, batch0, n_batch, out_prev):
    b, t, d = h1.shape
    tile = pl.BlockSpec((1, TOKEN_TILE, d), lambda i, j: (i, j, 0))
    in_specs = _moe_out_specs(b, t, d) + [tile, _mod_spec3(5, d, batch0), pl.BlockSpec((1, d), lambda i, j: (0, 0))]
    args = [yg, yg, gates, h1, mod, final_g]
    aliases = {}
    if out_prev is not None:
        in_specs.append(pl.BlockSpec(memory_space=pl.ANY))
        aliases = {len(args): 0}
        args.append(out_prev)
    return pl.pallas_call(
        _final_kernel,
        grid=(b, t // TOKEN_TILE),
        in_specs=in_specs,
        out_specs=pl.BlockSpec((1, TOKEN_TILE, d), lambda i, j: (i + batch0, j, 0)),
        out_shape=jax.ShapeDtypeStruct((n_batch, t, d), F32),
        input_output_aliases=aliases,
        compiler_params=pltpu.CompilerParams(
            dimension_semantics=("arbitrary", "arbitrary"), vmem_limit_bytes=VMEM_LIMIT_BYTES),
        name="final",
    )(*args)


def _moe(hm, eidx, gates, layer, w_gate, w_up, w_down):
    n_slots = 2 * hm.shape[0]
    n_rows = n_slots + N_EXPERTS * EXPERT_ROW_TILE
    pos, ends = _positions(eidx)
    pos = pos.reshape(n_slots)
    tile_start = jnp.arange(n_rows // EXPERT_ROW_TILE, dtype=jnp.int32) * EXPERT_ROW_TILE
    tile_expert = jnp.sum(tile_start[:, None] >= ends[None, :, 0], axis=1).astype(jnp.int32)
    xs = _sc_gather(hm, _sc_invert(pos, n_rows))
    ys = _experts(tile_expert, xs, layer, w_gate, w_up, w_down)
    return _sc_gather(ys, pos), gates.T


def kernel(x, c, ctx, c_ctx, ada_w, ada_b, norm_g, rw_mu, rw_w_rkv, rw_w0, rw_w1, rw_w2, rw_a0, rw_a1, rw_a2, rw_g1, rw_g2, rw_k_k, rw_k_a, rw_r_k, rw_gn_w, rw_gn_b, rw_w_o, sc_w_in, sc_conv, sc_w_out, router_w, router_b, moe_w_gate, moe_w_up, moe_w_down, final_g):
    b, t, d = x.shape
    ctx_len = ctx.shape[1]
    depth = ada_w.shape[0]
    assert d == D_MODEL and depth == 2 and t % POSITION_TILE == 0

    mod_rows = 16
    cc = jnp.concatenate([c, c_ctx[None, :], jnp.zeros((mod_rows - b - 1, d), F32)], axis=0)
    mod = _ada(cc, ada_w, ada_b).reshape(depth, mod_rows, 1, 6 * d)

    row = lambda a: a.reshape(1, d)
    router_wt = jnp.pad(router_w, ((0, 0), (0, LANES - N_EXPERTS)))
    router_bc = router_b.reshape(N_EXPERTS, 1)

    w1 = jnp.concatenate([rw_w1[0, 0], rw_w1[0, 1]], axis=1).astype(BF16)
    a1 = jnp.concatenate([rw_a1[0, 0], rw_a1[0, 1]], axis=1).astype(BF16)

    def pad_dirs(w):
        z = jnp.zeros_like(w[0])
        return jnp.stack([jnp.concatenate([w[0], z], axis=0), jnp.concatenate([z, w[1]], axis=0)]).astype(BF16)

    (r, v, g, bonus, cs0, cs1, al0, al1, kd0, kd1, b0, b1) = _rwkv_pre(
        ctx, x, mod[0], row(norm_g[0, 0]), rw_mu[0], rw_w_rkv[0].astype(BF16), w1,
        pad_dirs(rw_w2[0]), rw_w0[0], a1, pad_dirs(rw_a2[0]), rw_a0[0],
        rw_g1[0].astype(BF16), rw_g2[0].astype(BF16), row(rw_k_k[0]), row(rw_k_a[0]), row(rw_r_k[0]))
    yf, yr = _wkv(r, v, cs0, al0, kd0, b0, cs1, al1, kd1, b1, ctx_len)
    assert b % N_STREAMS == 0
    nb = b // N_STREAMS
    w_o, w_in, w_out = rw_w_o[0].astype(BF16), sc_w_in[0].astype(BF16), sc_w_out[0].astype(BF16)
    out = None
    for batch0 in range(0, b, nb):
        h1, hm, eidx, gates = _readout(yf, yr, bonus, g, x, mod[0], ctx_len, row(rw_gn_w[0]), row(rw_gn_b[0]),
                                       w_o, row(norm_g[0, 1]), router_wt, router_bc, batch0, nb)
        yg, gates = _moe(hm, eidx, gates, 0, moe_w_gate, moe_w_up, moe_w_down)
        h1, hm, eidx, gates = _sconv(yg, gates, h1, mod[0], mod[1], row(norm_g[1, 0]), w_in, sc_conv[0], w_out,
                                     row(norm_g[1, 1]), router_wt, router_bc, batch0)
        yg, gates = _moe(hm, eidx, gates, 1, moe_w_gate, moe_w_up, moe_w_down)
        out = _final(yg, gates, h1, mod[1], row(final_g), batch0, b, out)
    return out
```

```python
import functools

import jax
import jax.numpy as jnp
from jax import lax
from jax.experimental import pallas as pl
from jax.experimental.pallas import tpu as pltpu
from jax.experimental.pallas import tpu_sc as plsc

F32 = jnp.float32
BF16 = jnp.bfloat16

D_MODEL = 1024
HEAD_DIM = 64
N_HEADS = D_MODEL // HEAD_DIM
LANES = 128
N_PAIRS = D_MODEL // LANES
CHUNK = 64
N_EXPERTS = 16
EXPERTS_PER_GROUP = 4
N_GROUPS = N_EXPERTS // EXPERTS_PER_GROUP
RMS_EPS = 1e-6
GN_EPS = 64e-5
L2_EPS = 1e-12

TOKEN_TILE = 256
EXPERT_ROW_TILE = 512
EXPERT_COL_TILE = 256
POSITION_TILE = 1024
SC_CORES, SC_SUBCORES, SC_LANES = 2, 16, 16
SC_WORKERS = SC_CORES * SC_SUBCORES
N_STREAMS = 2
SC_GATHER_ROWS = 128
PAIRS_PER_STEP = 8
CHUNKS_PER_STEP = 4
ADA_COL_TILE = 1536
VMEM_LIMIT_BYTES = 56 * 1024 * 1024


def _sigmoid(x):
    return 1.0 / (1.0 + jnp.exp(-x))


def _mm(a, b):
    return jnp.dot(a.astype(BF16), b.astype(BF16), preferred_element_type=F32)


def _mm_nt(a, b):
    return lax.dot_general(a.astype(BF16), b.astype(BF16), (((1,), (1,)), ((), ())),
                           preferred_element_type=F32)


def _split2(x):
    hi = x.astype(BF16)
    return hi, (x - hi.astype(F32)).astype(BF16)


def _head_sum(x):
    rows = x.shape[0]
    left = lax.broadcasted_iota(jnp.int32, (rows, LANES), 1) < HEAD_DIM
    outs = []
    for j in range(N_PAIRS):
        xb = x[:, LANES * j:LANES * (j + 1)]
        sa = jnp.sum(jnp.where(left, xb, 0.0), axis=-1, keepdims=True)
        sb = jnp.sum(jnp.where(left, 0.0, xb), axis=-1, keepdims=True)
        outs.append(jnp.where(left, sa, sb))
    return jnp.concatenate(outs, axis=-1)


def _rms_modulate(x, g, shift, scale):
    ms = jnp.mean(x * x, axis=-1, keepdims=True)
    xn = x * lax.rsqrt(ms + RMS_EPS) * g
    return xn * (1.0 + scale) + shift


def _row_neighbours(x, row_len):
    rows = x.shape[0]
    pos = lax.broadcasted_iota(jnp.int32, x.shape, 0) & (row_len - 1)
    prev = jnp.where(pos == 0, 0.0, pltpu.roll(x, 1, 0))
    nxt = jnp.where(pos == row_len - 1, 0.0, pltpu.roll(x, rows - 1, 0))
    return prev, nxt


def _ada_kernel(c_ref, w_ref, b_ref, o_ref):
    c = c_ref[...]
    s = c * _sigmoid(c)
    o_ref[0] = _mm(s, w_ref[0]) + b_ref[0]


def _ada(cc, ada_w, ada_b):
    depth, d, n = ada_w.shape
    rows = cc.shape[0]
    return pl.pallas_call(
        _ada_kernel,
        grid=(depth, n // ADA_COL_TILE),
        in_specs=[
            pl.BlockSpec((rows, d), lambda l, j: (0, 0)),
            pl.BlockSpec((1, d, ADA_COL_TILE), lambda l, j: (l, 0, j)),
            pl.BlockSpec((1, 1, ADA_COL_TILE), lambda l, j: (l, 0, j)),
        ],
        out_specs=pl.BlockSpec((1, rows, ADA_COL_TILE), lambda l, j: (l, 0, j)),
        out_shape=jax.ShapeDtypeStruct((depth, rows, n), F32),
        compiler_params=pltpu.CompilerParams(
            dimension_semantics=("arbitrary", "arbitrary"), vmem_limit_bytes=VMEM_LIMIT_BYTES),
        name="ada",
    )(cc, ada_w, ada_b.reshape(depth, 1, n))


def _rwkv_pre_kernel(ctx_len, ctx_ref, x_ref, sh_ref, sc_ref, ng_ref, mu_ref, wrkv_ref, w1_ref, w2_ref, w0_ref,
                     a1_ref, a2_ref, a0_ref, g1_ref, g2_ref, kk_ref, ka_ref, rk_ref,
                     r_out, v_out, g_out, bon_out, cs0_out, cs1_out, al0_out, al1_out, kd0_out, kd1_out,
                     b0_out, b1_out):
    j = pl.program_id(1)
    row_len = jnp.where(j == 0, ctx_len, CHUNK)
    xin = jnp.where(j == 0, ctx_ref[0], x_ref[0])
    xn = _rms_modulate(xin, ng_ref[...], sh_ref[0], sc_ref[0])
    prev, nxt = _row_neighbours(xn, row_len)
    xx = 0.5 * (prev + nxt) - xn

    def mix(i):
        return xn + xx * mu_ref[i:i + 1, :]

    r = _mm(mix(0), wrkv_ref[0])
    k = _mm(mix(1), wrkv_ref[1])
    v = _mm(mix(2), wrkv_ref[2])
    r_out[0] = r.astype(r_out.dtype)
    v_out[0] = v.astype(v_out.dtype)
    g_out[0] = _mm(_sigmoid(_mm(mix(5), g1_ref[...])), g2_ref[...]).astype(g_out.dtype)

    kk = k * kk_ref[...]
    kk = kk * lax.rsqrt(jnp.maximum(_head_sum(kk * kk), L2_EPS * L2_EPS))

    wl = jnp.tanh(_mm(mix(3), w1_ref[...]))
    al = _mm(mix(4), a1_ref[...])
    rows = xn.shape[0]
    ti = lax.broadcasted_iota(jnp.int32, (rows, rows), 0)
    tj = lax.broadcasted_iota(jnp.int32, (rows, rows), 1)
    same_chunk = (ti ^ tj) < CHUNK
    bonus_dot = None
    for p, (cs_out, al_out, kd_out, b_out) in enumerate(((cs0_out, al0_out, kd0_out, b0_out),
                                                         (cs1_out, al1_out, kd1_out, b1_out))):
        z = w0_ref[p:p + 1, :] + _mm(wl, w2_ref[p])
        lw = -jnp.exp(-0.5) * _sigmoid(z)
        tri = jnp.where(same_chunk & ((tj >= ti) if p else (tj <= ti)), 1.0, 0.0).astype(BF16)
        hi, lo = _split2(lw)
        cs_out[0] = jnp.dot(tri, hi, preferred_element_type=F32) + jnp.dot(tri, lo, preferred_element_type=F32)
        al_out[0] = (-kk * jnp.exp(-lw)).astype(al_out.dtype)
        a = _sigmoid(a0_ref[p:p + 1, :] + _mm(al, a2_ref[p]))
        kd = k * (1.0 + (a - 1.0) * ka_ref[...])
        kd_out[0] = kd.astype(kd_out.dtype)
        b_out[0] = (kk * a).astype(b_out.dtype)
        t = r * kd * rk_ref[...]
        bonus_dot = t if bonus_dot is None else bonus_dot + t
    bon_out[0] = (_head_sum(bonus_dot) * v).astype(bon_out.dtype)


def _rwkv_pre(ctx, x, mod, norm_g, mu, w_rkv, w1, w2, w0, a1, a2, a0, g1, g2, k_k, k_a, r_k):
    n_batch, t, d = x.shape
    ctx_len = ctx.shape[1]
    b, tt = n_batch, ctx_len + t
    assert ctx_len == TOKEN_TILE and t % TOKEN_TILE == 0

    def mod_spec(part):
        return pl.BlockSpec((1, 1, d), lambda i, j: (jnp.where(j == 0, n_batch, i), 0, part))

    def full(a):
        nd = a.ndim
        return pl.BlockSpec(a.shape, lambda i, j: (0,) * nd)

    tile = pl.BlockSpec((1, TOKEN_TILE, d), lambda i, j: (i, j, 0))
    params = (norm_g, mu, w_rkv, w1, w2, w0, a1, a2, a0, g1, g2, k_k, k_a, r_k)
    out_dtypes = (BF16,) * 4 + (F32, F32) + (BF16,) * 6
    return pl.pallas_call(
        functools.partial(_rwkv_pre_kernel, ctx_len),
        grid=(b, tt // TOKEN_TILE),
        in_specs=[pl.BlockSpec((1, TOKEN_TILE, d), lambda i, j: (i, 0, 0)),
                  pl.BlockSpec((1, TOKEN_TILE, d), lambda i, j: (i, jnp.maximum(j - 1, 0), 0)),
                  mod_spec(0), mod_spec(1)] + [full(a) for a in params],
        out_specs=[tile] * len(out_dtypes),
        out_shape=[jax.ShapeDtypeStruct((b, tt, d), dt) for dt in out_dtypes],
        compiler_params=pltpu.CompilerParams(
            dimension_semantics=("arbitrary", "arbitrary"), vmem_limit_bytes=VMEM_LIMIT_BYTES),
        name="rwkv_pre",
    )(ctx, x, mod, mod, *params)


def _expand(x, left):
    return jnp.concatenate([jnp.where(left, x, 0.0), jnp.where(left, 0.0, x)], axis=0)


def _dot(a, b):
    return jnp.dot(a, b, preferred_element_type=F32)


def _chunk_steps(chains):
    assert CHUNK == HEAD_DIM
    c = CHUNK
    lane = lax.broadcasted_iota(jnp.int32, (c, LANES), 1)
    left = lane < HEAD_DIM
    tt = lax.broadcasted_iota(jnp.int32, (c, LANES), 0)
    jj = lane & (c - 1)
    diag = jj == tt
    tri = {False: (jj < tt, jj <= tt), True: (jj > tt, jj >= tt)}

    def bd(x):
        return _expand(x, left).astype(BF16)

    def rows(*xs):
        return jnp.concatenate(xs, axis=0).astype(BF16)

    nt_dims = (((1,), (1,)), ((), ()))
    eye = jnp.where(diag, 1.0, 0.0).astype(BF16)

    prep = []
    for cs, alw, r, kd, v, be, h, reverse in chains:
        tot = cs[0:1, :] if reverse else cs[c - 1:c, :]
        e_pos, e_neg, e_rem = jnp.exp(cs), jnp.exp(-cs), jnp.exp(tot - cs)
        at, rt = alw * e_pos, r * e_pos
        q = dict(a_e=bd(at), rt=rt, v_e=bd(v), g_tot=jnp.exp(tot), h=h, tri=tri[reverse])
        q["hat_t"] = lax.dot_general(eye, jnp.concatenate([bd(be * e_rem), bd(kd * e_rem)], axis=0), nt_dims,
                                     preferred_element_type=F32)
        q["sc"] = lax.dot_general(rows(at, rt), jnp.concatenate([bd(be * e_neg), bd(kd * e_neg)], axis=0),
                                  nt_dims, preferred_element_type=F32)
        prep.append(q)

    for q in prep:
        (strict, incl), sc = q["tri"], q["sc"]
        q["n"] = jnp.where(strict, sc[:c, :LANES], 0.0)
        a_ak = jnp.where(strict, sc[:c, LANES:], 0.0)
        q["a_rb"] = jnp.where(incl, sc[c:, :LANES], 0.0)
        a_rk = jnp.where(incl, sc[c:, LANES:], 0.0)
        q["bh_t"] = q["hat_t"][:, :LANES]
        q["vv"] = _dot(rows(a_ak, q["hat_t"][:, LANES:], a_rk), q["v_e"])

    for q in prep:
        q["t"] = jnp.where(diag, 1.0, 0.0) + q["n"]
        q["p"] = _dot(q["n"].astype(BF16), bd(q["n"]))
    order = 4
    while order < c:
        for q in prep:
            x = _dot(rows(q["p"], q["t"]), bd(q["p"]))
            q["p"] = x[:c]
            q["t"] = q["t"] + x[c:]
        order *= 2
    for q in prep:
        q["t"] = q["t"] + _dot(q["t"].astype(BF16), bd(q["p"]))

    for q in prep:
        xu = _dot(q["t"].astype(BF16), jnp.concatenate([q["a_e"], bd(q["vv"][:c])], axis=1))
        q["au_e"] = jnp.concatenate([bd(xu[:, :LANES]), bd(xu[:, LANES:])], axis=1)
    for q in prep:
        z = _dot(rows(q["bh_t"], q["a_rb"]), q["au_e"])
        q["m"] = z[:c, :LANES] + jnp.where(diag, q["g_tot"], 0.0)
        q["g"] = z[:c, LANES:] + q["vv"][c:2 * c]
        q["r_hat"] = q["rt"] + z[c:, :LANES]
        q["yi"] = z[c:, LANES:] + q["vv"][2 * c:]
    outs = []
    for q in prep:
        h = outs[q["h"]][1] if isinstance(q["h"], int) else q["h"]
        o = _dot(rows(q["r_hat"], q["m"]), bd(h))
        outs.append((o[:c] + q["yi"], o[c:] + q["g"]))
    return outs


def _wkv_kernel(rf_ref, vf_ref, csf_ref, af_ref, kdf_ref, bf_ref,
                rr_ref, vr_ref, csr_ref, ar_ref, kdr_ref, br_ref,
                yf_ref, yr_ref, hf_ref, hr_ref):
    @pl.when(pl.program_id(2) == 0)
    def _():
        hf_ref[...] = jnp.zeros_like(hf_ref)
        hr_ref[...] = jnp.zeros_like(hr_ref)

    dirs = ((rf_ref, vf_ref, csf_ref, af_ref, kdf_ref, bf_ref, yf_ref, hf_ref),
            (rr_ref, vr_ref, csr_ref, ar_ref, kdr_ref, br_ref, yr_ref, hr_ref))
    chains, dests = [], []
    per_chunk = 2 * PAIRS_PER_STEP
    for u in range(CHUNKS_PER_STEP):
        for p in range(PAIRS_PER_STEP):
            sl = slice(p * LANES, (p + 1) * LANES)
            for reverse, (r_ref, v_ref, cs_ref, a_ref, kd_ref, b_ref, y_ref, h_ref) in enumerate(dirs):
                at = CHUNKS_PER_STEP - 1 - u if reverse else u
                tm = slice(at * CHUNK, (at + 1) * CHUNK)
                h = h_ref[p] if u == 0 else len(chains) - per_chunk
                chains.append((cs_ref[0, tm, sl], a_ref[0, tm, sl].astype(F32), r_ref[0, tm, sl].astype(F32),
                               kd_ref[0, tm, sl].astype(F32), v_ref[0, tm, sl].astype(F32),
                               b_ref[0, tm, sl].astype(F32), h, bool(reverse)))
                dests.append((y_ref, tm, sl, h_ref if u == CHUNKS_PER_STEP - 1 else None, p))
    for (y, h_new), (y_ref, tm, sl, h_ref, p) in zip(_chunk_steps(chains), dests):
        y_ref[0, tm, sl] = y
        if h_ref is not None:
            h_ref[p] = h_new


def _wkv(r, v, cs0, al0, kd0, b0, cs1, al1, kd1, b1, ctx_len):
    b, tt, d = r.shape
    step_rows = CHUNKS_PER_STEP * CHUNK
    n_steps, n_ctx = tt // step_rows, ctx_len // step_rows
    assert n_steps * step_rows == tt and n_ctx * step_rows == ctx_len
    width = PAIRS_PER_STEP * LANES

    def fwd_map(i, j, s):
        return (i, s, j)

    def rev_map(i, j, s):
        return (i, jnp.where(s < n_ctx, n_ctx - 1 - s, n_steps - 1 - (s - n_ctx)), j)

    fwd = pl.BlockSpec((1, step_rows, width), fwd_map)
    rev = pl.BlockSpec((1, step_rows, width), rev_map)
    return pl.pallas_call(
        _wkv_kernel,
        grid=(b, d // width, n_steps),
        in_specs=[fwd] * 6 + [rev] * 6,
        out_specs=[fwd, rev],
        out_shape=[jax.ShapeDtypeStruct((b, tt, d), F32)] * 2,
        scratch_shapes=[pltpu.VMEM((PAIRS_PER_STEP, HEAD_DIM, LANES), F32)] * 2,
        compiler_params=pltpu.CompilerParams(
            dimension_semantics=("arbitrary", "arbitrary", "arbitrary"), vmem_limit_bytes=VMEM_LIMIT_BYTES),
        name="wkv",
    )(r, v, cs0, al0, kd0, b0, r, v, cs1, al1, kd1, b1)


def _pack_bf16_pairs(x):
    half = x.shape[1] // 2
    lo = pltpu.bitcast(x[:, :half].astype(BF16).astype(F32), jnp.int32)
    hi = pltpu.bitcast(x[:, half:].astype(BF16).astype(F32), jnp.int32)
    return lax.shift_right_logical(lo, jnp.int32(16)) | (hi & jnp.int32(-65536))


def _unpack_bf16_pairs(w):
    lo = pltpu.bitcast(lax.shift_left(w, jnp.int32(16)), F32)
    hi = pltpu.bitcast(w & jnp.int32(-65536), F32)
    return jnp.concatenate([lo, hi], axis=1)


def _route(h1, ng, shift, scale, rwt_ref, rb_ref, hm_out, eidx_out, gate_out):
    hm = _rms_modulate(h1, ng, shift, scale)
    hm_out[...] = _pack_bf16_pairs(hm)
    w_hi, w_mid = _split2(rwt_ref[...])
    h_hi, h_mid = _split2(hm)
    logits_t = _dot(h_hi, w_hi) + _dot(h_mid, w_hi) + _dot(h_hi, w_mid)
    logits = logits_t.T[:N_EXPERTS]
    s = _sigmoid(logits)
    sel = s + rb_ref[...]
    assert EXPERTS_PER_GROUP == 4 and N_GROUPS == 4
    cands = []
    for g in range(N_GROUPS):
        m = [sel[e:e + 1, :] for e in range(g * EXPERTS_PER_GROUP, (g + 1) * EXPERTS_PER_GROUP)]
        sg = [s[e:e + 1, :] for e in range(g * EXPERTS_PER_GROUP, (g + 1) * EXPERTS_PER_GROUP)]
        pairs = [m[i] + m[k] for i in range(4) for k in range(i + 1, 4)]
        score = jnp.maximum(jnp.maximum(jnp.maximum(pairs[0], pairs[1]), jnp.maximum(pairs[2], pairs[3])),
                            jnp.maximum(pairs[4], pairs[5]))
        chosen = []
        for i in range(4):
            ahead = [((m[k] >= m[i]) if k < i else (m[k] > m[i])).astype(jnp.int32) for k in range(4) if k != i]
            chosen.append(ahead[0] + ahead[1] + ahead[2] < 2)
        base = g * EXPERTS_PER_GROUP
        lo_idx = jnp.where(chosen[0], base, jnp.where(chosen[1], base + 1, base + 2))
        lo_gate = jnp.where(chosen[0], sg[0], jnp.where(chosen[1], sg[1], sg[2]))
        hi_idx = jnp.where(chosen[3], base + 3, jnp.where(chosen[2], base + 2, base + 1))
        hi_gate = jnp.where(chosen[3], sg[3], jnp.where(chosen[2], sg[2], sg[1]))
        cands.append((score, lo_idx, hi_idx, lo_gate, hi_gate))

    def better(x, y):
        win = y[0] > x[0]
        return tuple(jnp.where(win, yv, xv) for xv, yv in zip(x, y))

    _, e_lo, e_hi, g_lo, g_hi = better(better(cands[0], cands[1]), better(cands[2], cands[3]))
    eidx_out[...] = jnp.concatenate([e_lo, e_hi], axis=0)
    gate_out[...] = jnp.concatenate([g_lo, g_hi], axis=0) / (g_lo + g_hi)


def _readout_kernel(yf_ref, yr_ref, bon_ref, g_ref, h_ref, gt_ref, gnw_ref, gnb_ref, wo_ref,
                    ng_ref, sh_ref, sc_ref, rwt_ref, rb_ref, h1_out, hm_out, eidx_out, gate_out):
    y = yf_ref[0] + yr_ref[0] + bon_ref[0].astype(F32)
    mean = _head_sum(y) * (1.0 / HEAD_DIM)
    yc = y - mean
    var = _head_sum(yc * yc) * (1.0 / HEAD_DIM)
    yn = yc * lax.rsqrt(var + GN_EPS) * gnw_ref[...] + gnb_ref[...]
    out = _mm(yn * g_ref[0].astype(F32), wo_ref[...])
    h1 = h_ref[0] + gt_ref[0] * out
    h1_out[0] = h1
    _route(h1, ng_ref[...], sh_ref[0], sc_ref[0], rwt_ref, rb_ref, hm_out, eidx_out, gate_out)


def _mod_spec3(part, d, batch0=0):
    return pl.BlockSpec((1, 1, d), lambda i, j: (i + batch0, 0, part))


def _route_out_specs(b, t, d):
    nt = t // TOKEN_TILE
    specs = [pl.BlockSpec((1, TOKEN_TILE, d), lambda i, j: (i, j, 0)),
             pl.BlockSpec((TOKEN_TILE, d // 2), lambda i, j: (i * nt + j, 0)),
             pl.BlockSpec((2, TOKEN_TILE), lambda i, j: (0, i * nt + j)),
             pl.BlockSpec((2, TOKEN_TILE), lambda i, j: (0, i * nt + j))]
    shapes = [jax.ShapeDtypeStruct((b, t, d), F32), jax.ShapeDtypeStruct((b * t, d // 2), jnp.int32),
              jax.ShapeDtypeStruct((2, b * t), jnp.int32), jax.ShapeDtypeStruct((2, b * t), F32)]
    return specs, shapes


def _full2(a):
    nd = a.ndim
    return pl.BlockSpec(a.shape, lambda i, j: (0,) * nd)


def _readout(yf, yr, bonus, g, h, mod, ctx_len, gn_w, gn_b, w_o, norm_g, router_wt, router_b, batch0, b):
    _, t, d = h.shape
    off = ctx_len // TOKEN_TILE
    nt = t // TOKEN_TILE
    shifted = pl.BlockSpec((1, TOKEN_TILE, d), lambda i, j: (i + batch0, j + off, 0))
    tile = pl.BlockSpec((1, TOKEN_TILE, d), lambda i, j: (i + batch0, j, 0))
    params_a = (gn_w, gn_b, w_o, norm_g)
    params_b = (router_wt, router_b)
    out_specs, out_shape = _route_out_specs(b, t, d)
    return pl.pallas_call(
        _readout_kernel,
        grid=(b, nt),
        in_specs=[shifted] * 4 + [tile, _mod_spec3(2, d, batch0)] + [_full2(a) for a in params_a]
        + [_mod_spec3(3, d, batch0), _mod_spec3(4, d, batch0)] + [_full2(a) for a in params_b],
        out_specs=out_specs,
        out_shape=out_shape,
        compiler_params=pltpu.CompilerParams(
            dimension_semantics=("arbitrary", "arbitrary"), vmem_limit_bytes=VMEM_LIMIT_BYTES),
        name="readout",
    )(yf, yr, bonus, g, h, mod, *params_a, mod, mod, *params_b)


def _moe_residual(y0_ref, y1_ref, gate_ref, h1, gt):
    gates = gate_ref[...]
    y = gates[:, 0:1] * _unpack_bf16_pairs(y0_ref[...]) + gates[:, 1:2] * _unpack_bf16_pairs(y1_ref[...])
    return h1 + gt * y


def _sconv_kernel(y0_ref, y1_ref, gate_ref, hp_ref, gtp_ref, sh1_ref, sc1_ref, gt_ref, ng1_ref, win_ref, cw_ref,
                  wout_ref, ng2_ref, sh2_ref, sc2_ref, rwt_ref, rb_ref, h1_out, hm_out, eidx_out, gate_out):
    h = _moe_residual(y0_ref, y1_ref, gate_ref, hp_ref[0], gtp_ref[0])
    d = h.shape[1]
    xn = _rms_modulate(h, ng1_ref[...], sh1_ref[0], sc1_ref[0]).astype(BF16)
    bg = jnp.dot(xn, win_ref[:, 0:d], preferred_element_type=F32)
    u = (jnp.dot(xn, win_ref[:, d:2 * d], preferred_element_type=F32)
         * jnp.dot(xn, win_ref[:, 2 * d:3 * d], preferred_element_type=F32))
    prev, nxt = _row_neighbours(u, CHUNK)
    conv = cw_ref[0:1, :] * prev + cw_ref[1:2, :] * u + cw_ref[2:3, :] * nxt
    h1 = h + gt_ref[0] * _mm(bg * conv, wout_ref[...])
    h1_out[0] = h1
    _route(h1, ng2_ref[...], sh2_ref[0], sc2_ref[0], rwt_ref, rb_ref, hm_out, eidx_out, gate_out)


def _sconv(yg, gates, h_prev, mod_prev, mod, norm_g1, w_in, conv_w, w_out, norm_g2, router_wt, router_b, batch0):
    b, t, d = h_prev.shape
    nt = t // TOKEN_TILE
    tile = pl.BlockSpec((1, TOKEN_TILE, d), lambda i, j: (i, j, 0))
    params_a = (norm_g1, w_in, conv_w, w_out, norm_g2)
    params_b = (router_wt, router_b)
    out_specs, out_shape = _route_out_specs(b, t, d)
    return pl.pallas_call(
        _sconv_kernel,
        grid=(b, nt),
        in_specs=_moe_out_specs(b, t, d) + [tile, _mod_spec3(5, d, batch0)]
        + [_mod_spec3(0, d, batch0), _mod_spec3(1, d, batch0), _mod_spec3(2, d, batch0)]
        + [_full2(a) for a in params_a]
        + [_mod_spec3(3, d, batch0), _mod_spec3(4, d, batch0)] + [_full2(a) for a in params_b],
        out_specs=out_specs,
        out_shape=out_shape,
        compiler_params=pltpu.CompilerParams(
            dimension_semantics=("arbitrary", "arbitrary"), vmem_limit_bytes=VMEM_LIMIT_BYTES),
        name="sconv",
    )(yg, yg, gates, h_prev, mod_prev, mod, mod, mod, *params_a, mod, mod, *params_b)


def _positions_kernel(eidx_ref, pos_ref, ends_ref, run_ref):
    phase, i = pl.program_id(0), pl.program_id(1)
    tp = eidx_ref.shape[1]
    expert = lax.broadcasted_iota(jnp.int32, (N_EXPERTS, tp), 0)
    onehot = [jnp.where(eidx_ref[k:k + 1, :] == expert, 1.0, 0.0) for k in range(2)]

    @pl.when((phase == 0) & (i == 0))
    def _():
        run_ref[...] = jnp.zeros_like(run_ref)

    @pl.when(phase == 0)
    def _():
        run_ref[...] += jnp.sum(onehot[0] + onehot[1], axis=1, keepdims=True)

    @pl.when((phase == 1) & (i == 0))
    def _():
        padded = jnp.floor((run_ref[...] + (EXPERT_ROW_TILE - 1)) * (1.0 / EXPERT_ROW_TILE)) * EXPERT_ROW_TILE
        ei = lax.broadcasted_iota(jnp.int32, (N_EXPERTS, N_EXPERTS), 0)
        ej = lax.broadcasted_iota(jnp.int32, (N_EXPERTS, N_EXPERTS), 1)
        below = jnp.where(ej < ei, 1.0, 0.0)
        starts = jnp.dot(below, jnp.broadcast_to(padded, (N_EXPERTS, LANES)), precision=lax.Precision.HIGHEST,
                         preferred_element_type=F32)
        ends_ref[...] = (starts + padded).astype(jnp.int32)
        run_ref[...] = starts[:, 0:1]

    @pl.when(phase == 1)
    def _():
        ti = lax.broadcasted_iota(jnp.int32, (tp, tp), 0)
        tj = lax.broadcasted_iota(jnp.int32, (tp, tp), 1)
        before = jnp.where(ti < tj, 1.0, 0.0).astype(BF16)
        run = run_ref[...]
        for k in range(2):
            prefix = jnp.dot(onehot[k].astype(BF16), before, preferred_element_type=F32)
            pos_ref[k:k + 1, :] = jnp.sum(onehot[k] * (prefix + run), axis=0, keepdims=True).astype(jnp.int32)
            run = run + jnp.sum(onehot[k], axis=1, keepdims=True)
        run_ref[...] = run


def _positions(eidx):
    n = eidx.shape[1]
    nt = n // POSITION_TILE
    return pl.pallas_call(
        _positions_kernel,
        grid=(2, nt),
        in_specs=[pl.BlockSpec((2, POSITION_TILE), lambda p, i: (0, i))],
        out_specs=[pl.BlockSpec((2, POSITION_TILE), lambda p, i: (0, i * p)),
                   pl.BlockSpec((N_EXPERTS, LANES), lambda p, i: (0, 0))],
        out_shape=[jax.ShapeDtypeStruct((2, n), jnp.int32), jax.ShapeDtypeStruct((N_EXPERTS, LANES), jnp.int32)],
        scratch_shapes=[pltpu.VMEM((N_EXPERTS, 1), F32)],
        compiler_params=pltpu.CompilerParams(
            dimension_semantics=("arbitrary", "arbitrary"), vmem_limit_bytes=VMEM_LIMIT_BYTES),
        name="positions",
    )(eidx)


def _sc_mesh():
    return plsc.VectorSubcoreMesh(core_axis_name="c", subcore_axis_name="s")


def _sc_worker(n_workers_per_core=SC_SUBCORES):
    return lax.axis_index("c") * n_workers_per_core + lax.axis_index("s")


def _sc_invert(pos, n_rows):
    n_slots = pos.shape[0]
    n_tokens = n_slots // 2
    assert n_tokens & (n_tokens - 1) == 0
    per_worker = n_rows // SC_WORKERS
    assert per_worker * SC_WORKERS == n_rows and per_worker % SC_LANES == 0 and n_slots % SC_LANES == 0

    def body(pos_hbm, inv_hbm, pos_v, inv_v):
        base = _sc_worker() * per_worker
        pltpu.sync_copy(pos_hbm, pos_v)

        @pl.loop(0, per_worker, step=SC_LANES)
        def _(j):
            inv_v[pl.ds(j, SC_LANES)] = (base + j + lax.iota(jnp.int32, SC_LANES)) & (n_tokens - 1)

        @pl.loop(0, n_slots, step=SC_LANES)
        def _(s):
            local = pos_v[pl.ds(s, SC_LANES)] - base
            mine = (local >= 0) & (local < per_worker)
            token = (s + lax.iota(jnp.int32, SC_LANES)) & (n_tokens - 1)
            plsc.store_scatter(inv_v, [jnp.where(mine, local, 0)], token, mask=mine)

        pltpu.sync_copy(inv_v, inv_hbm.at[pl.ds(base, per_worker)])

    return pl.kernel(
        body, out_type=jax.ShapeDtypeStruct((n_rows,), jnp.int32), mesh=_sc_mesh(),
        scratch_types=[pltpu.VMEM((n_slots,), jnp.int32), pltpu.VMEM((per_worker,), jnp.int32)],
        compiler_params=pltpu.CompilerParams(needs_layout_passes=False),
        name="sc_invert",
    )(pos)


def _sc_gather(table, idx):
    n_rows, width = idx.shape[0], table.shape[1]
    per_worker = n_rows // SC_WORKERS
    assert per_worker * SC_WORKERS == n_rows and per_worker % SC_GATHER_ROWS == 0

    def body(table_hbm, idx_hbm, out_hbm, idx_v, rows_v, sem):
        base = _sc_worker() * per_worker

        @pl.loop(0, per_worker, step=SC_GATHER_ROWS)
        def _(j):
            pltpu.sync_copy(idx_hbm.at[pl.ds(base + j, SC_GATHER_ROWS)], idx_v)
            pltpu.async_copy(table_hbm.at[idx_v], rows_v, sem).wait()
            pltpu.sync_copy(rows_v, out_hbm.at[pl.ds(base + j, SC_GATHER_ROWS)])

    return pl.kernel(
        body, out_type=jax.ShapeDtypeStruct((n_rows, width), table.dtype), mesh=_sc_mesh(),
        scratch_types=[pltpu.VMEM((SC_GATHER_ROWS,), jnp.int32), pltpu.VMEM((SC_GATHER_ROWS, width), table.dtype),
                       pltpu.SemaphoreType.DMA],
        name="sc_gather",
    )(table, idx)


def _experts_kernel(te_ref, xs_ref, wg_ref, wu_ref, wd_ref, ys_ref, wg_bf, wu_bf, wd_bf):
    i = pl.program_id(0)
    expert = te_ref[i]

    @pl.when((i == 0) | (expert != te_ref[jnp.maximum(i - 1, 0)]))
    def _():
        wg_bf[...] = wg_ref[0, 0].astype(BF16)
        wu_bf[...] = wu_ref[0, 0].astype(BF16)
        wd_bf[...] = wd_ref[0, 0].astype(BF16)

    @pl.when(expert < N_EXPERTS)
    def _():
        x = _unpack_bf16_pairs(xs_ref[...]).astype(BF16)
        de = wg_bf.shape[1]
        cols = [slice(n, n + EXPERT_COL_TILE) for n in range(0, de, EXPERT_COL_TILE)]
        gates = [(_dot(x, wg_bf[:, c]), _dot(x, wu_bf[:, c])) for c in cols]
        hes = [(gate * _sigmoid(gate) * up).astype(BF16) for gate, up in gates]
        y = _dot(hes[0], wd_bf[cols[0], :])
        for he, c in zip(hes[1:], cols[1:]):
            y = y + _dot(he, wd_bf[c, :])
        ys_ref[...] = _pack_bf16_pairs(y)


def _experts(tile_expert, xs, layer, w_gate, w_up, w_down):
    n_rows, half = xs.shape
    _, n_e, d, de = w_gate.shape

    def w_map(i, te):
        return (layer, jnp.minimum(te[i], n_e - 1), 0, 0)

    rows = pl.BlockSpec((EXPERT_ROW_TILE, half), lambda i, te: (i, 0))
    return pl.pallas_call(
        _experts_kernel,
        grid_spec=pltpu.PrefetchScalarGridSpec(
            num_scalar_prefetch=1,
            grid=(n_rows // EXPERT_ROW_TILE,),
            in_specs=[rows, pl.BlockSpec((1, 1, d, de), w_map), pl.BlockSpec((1, 1, d, de), w_map),
                      pl.BlockSpec((1, 1, de, d), w_map)],
            out_specs=rows,
            scratch_shapes=[pltpu.VMEM((d, de), BF16), pltpu.VMEM((d, de), BF16), pltpu.VMEM((de, d), BF16)]),
        out_shape=jax.ShapeDtypeStruct((n_rows, half), jnp.int32),
        compiler_params=pltpu.CompilerParams(
            dimension_semantics=("arbitrary",), vmem_limit_bytes=VMEM_LIMIT_BYTES),
        name="experts",
    )(tile_expert, xs, w_gate, w_up, w_down)


def _final_kernel(y0_ref, y1_ref, gate_ref, h1_ref, gt_ref, fg_ref, *out_refs):
    o_ref = out_refs[-1]
    h2 = _moe_residual(y0_ref, y1_ref, gate_ref, h1_ref[0], gt_ref[0])
    ms = jnp.mean(h2 * h2, axis=-1, keepdims=True)
    o_ref[0] = h2 * lax.rsqrt(ms + RMS_EPS) * fg_ref[...]


def _moe_out_specs(b, t, d):
    nt = t // TOKEN_TILE
    return [pl.BlockSpec((TOKEN_TILE, d // 2), lambda i, j: (i * nt + j, 0)),
            pl.BlockSpec((TOKEN_TILE, d // 2), lambda i, j: (b * nt + i * nt + j, 0)),
            pl.BlockSpec((TOKEN_TILE, 2), lambda i, j: (i * nt + j, 0))]


def _final(yg, gates, h1, mod, final_g, batch0, n_batch, out_prev):
    b, t, d = h1.shape
    tile = pl.BlockSpec((1, TOKEN_TILE, d), lambda i, j: (i, j, 0))
    in_specs = _moe_out_specs(b, t, d) + [tile, _mod_spec3(5, d, batch0), pl.BlockSpec((1, d), lambda i, j: (0, 0))]
    args = [yg, yg, gates, h1, mod, final_g]
    aliases = {}
    if out_prev is not None:
        in_specs.append(pl.BlockSpec(memory_space=pl.ANY))
        aliases = {len(args): 0}
        args.append(out_prev)
    return pl.pallas_call(
        _final_kernel,
        grid=(b, t // TOKEN_TILE),
        in_specs=in_specs,
        out_specs=pl.BlockSpec((1, TOKEN_TILE, d), lambda i, j: (i + batch0, j, 0)),
        out_shape=jax.ShapeDtypeStruct((n_batch, t, d), F32),
        input_output_aliases=aliases,
        compiler_params=pltpu.CompilerParams(
            dimension_semantics=("arbitrary", "arbitrary"), vmem_limit_bytes=VMEM_LIMIT_BYTES),
        name="final",
    )(*args)


def _moe(hm, eidx, gates, layer, w_gate, w_up, w_down):
    n_slots = 2 * hm.shape[0]
    n_rows = n_slots + N_EXPERTS * EXPERT_ROW_TILE
    pos, ends = _positions(eidx)
    pos = pos.reshape(n_slots)
    tile_start = jnp.arange(n_rows // EXPERT_ROW_TILE, dtype=jnp.int32) * EXPERT_ROW_TILE
    tile_expert = jnp.sum(tile_start[:, None] >= ends[None, :, 0], axis=1).astype(jnp.int32)
    xs = _sc_gather(hm, _sc_invert(pos, n_rows))
    ys = _experts(tile_expert, xs, layer, w_gate, w_up, w_down)
    return _sc_gather(ys, pos), gates.T


def kernel(x, c, ctx, c_ctx, ada_w, ada_b, norm_g, rw_mu, rw_w_rkv, rw_w0, rw_w1, rw_w2, rw_a0, rw_a1, rw_a2, rw_g1, rw_g2, rw_k_k, rw_k_a, rw_r_k, rw_gn_w, rw_gn_b, rw_w_o, sc_w_in, sc_conv, sc_w_out, router_w, router_b, moe_w_gate, moe_w_up, moe_w_down, final_g):
    b, t, d = x.shape
    ctx_len = ctx.shape[1]
    depth = ada_w.shape[0]
    assert d == D_MODEL and depth == 2 and t % POSITION_TILE == 0

    mod_rows = 16
    cc = jnp.concatenate([c, c_ctx[None, :], jnp.zeros((mod_rows - b - 1, d), F32)], axis=0)
    mod = _ada(cc, ada_w, ada_b).reshape(depth, mod_rows, 1, 6 * d)

    row = lambda a: a.reshape(1, d)
    router_wt = jnp.pad(router_w, ((0, 0), (0, LANES - N_EXPERTS)))
    router_bc = router_b.reshape(N_EXPERTS, 1)

    w1 = jnp.concatenate([rw_w1[0, 0], rw_w1[0, 1]], axis=1).astype(BF16)
    a1 = jnp.concatenate([rw_a1[0, 0], rw_a1[0, 1]], axis=1).astype(BF16)

    def pad_dirs(w):
        z = jnp.zeros_like(w[0])
        return jnp.stack([jnp.concatenate([w[0], z], axis=0), jnp.concatenate([z, w[1]], axis=0)]).astype(BF16)

    (r, v, g, bonus, cs0, cs1, al0, al1, kd0, kd1, b0, b1) = _rwkv_pre(
        ctx, x, mod[0], row(norm_g[0, 0]), rw_mu[0], rw_w_rkv[0].astype(BF16), w1,
        pad_dirs(rw_w2[0]), rw_w0[0], a1, pad_dirs(rw_a2[0]), rw_a0[0],
        rw_g1[0].astype(BF16), rw_g2[0].astype(BF16), row(rw_k_k[0]), row(rw_k_a[0]), row(rw_r_k[0]))
    yf, yr = _wkv(r, v, cs0, al0, kd0, b0, cs1, al1, kd1, b1, ctx_len)
    assert b % N_STREAMS == 0
    nb = b // N_STREAMS
    w_o, w_in, w_out = rw_w_o[0].astype(BF16), sc_w_in[0].astype(BF16), sc_w_out[0].astype(BF16)
    out = None
    for batch0 in range(0, b, nb):
        h1, hm, eidx, gates = _readout(yf, yr, bonus, g, x, mod[0], ctx_len, row(rw_gn_w[0]), row(rw_gn_b[0]),
                                       w_o, row(norm_g[0, 1]), router_wt, router_bc, batch0, nb)
        yg, gates = _moe(hm, eidx, gates, 0, moe_w_gate, moe_w_up, moe_w_down)
        h1, hm, eidx, gates = _sconv(yg, gates, h1, mod[0], mod[1], row(norm_g[1, 0]), w_in, sc_conv[0], w_out,
                                     row(norm_g[1, 1]), router_wt, router_bc, batch0)
        yg, gates = _moe(hm, eidx, gates, 1, moe_w_gate, moe_w_up, moe_w_down)
        out = _final(yg, gates, h1, mod[1], row(final_g), batch0, b, out)
    return out
```

```python
import functools

import jax
import jax.numpy as jnp
from jax import lax
from jax.experimental import pallas as pl
from jax.experimental.pallas import tpu as pltpu
from jax.experimental.pallas import tpu_sc as plsc

F32 = jnp.float32
BF16 = jnp.bfloat16

D_MODEL = 1024
HEAD_DIM = 64
N_HEADS = D_MODEL // HEAD_DIM
LANES = 128
N_PAIRS = D_MODEL // LANES
CHUNK = 64
N_EXPERTS = 16
EXPERTS_PER_GROUP = 4
N_GROUPS = N_EXPERTS // EXPERTS_PER_GROUP
RMS_EPS = 1e-6
GN_EPS = 64e-5
L2_EPS = 1e-12

TOKEN_TILE = 256
EXPERT_ROW_TILE = 512
EXPERT_COL_TILE = 256
POSITION_TILE = 1024
SC_CORES, SC_SUBCORES, SC_LANES = 2, 16, 16
SC_WORKERS = SC_CORES * SC_SUBCORES
N_STREAMS = 2
SC_GATHER_ROWS = 128
PAIRS_PER_STEP = 8
CHUNKS_PER_STEP = 4
ADA_COL_TILE = 1536
VMEM_LIMIT_BYTES = 56 * 1024 * 1024


def _sigmoid(x):
    return 1.0 / (1.0 + jnp.exp(-x))


def _mm(a, b):
    return jnp.dot(a.astype(BF16), b.astype(BF16), preferred_element_type=F32)


def _mm_nt(a, b):
    return lax.dot_general(a.astype(BF16), b.astype(BF16), (((1,), (1,)), ((), ())),
                           preferred_element_type=F32)


def _split2(x):
    hi = x.astype(BF16)
    return hi, (x - hi.astype(F32)).astype(BF16)


def _head_sum(x):
    rows = x.shape[0]
    left = lax.broadcasted_iota(jnp.int32, (rows, LANES), 1) < HEAD_DIM
    outs = []
    for j in range(N_PAIRS):
        xb = x[:, LANES * j:LANES * (j + 1)]
        sa = jnp.sum(jnp.where(left, xb, 0.0), axis=-1, keepdims=True)
        sb = jnp.sum(jnp.where(left, 0.0, xb), axis=-1, keepdims=True)
        outs.append(jnp.where(left, sa, sb))
    return jnp.concatenate(outs, axis=-1)


def _rms_modulate(x, g, shift, scale):
    ms = jnp.mean(x * x, axis=-1, keepdims=True)
    xn = x * lax.rsqrt(ms + RMS_EPS) * g
    return xn * (1.0 + scale) + shift


def _row_neighbours(x, row_len):
    rows = x.shape[0]
    pos = lax.broadcasted_iota(jnp.int32, x.shape, 0) & (row_len - 1)
    prev = jnp.where(pos == 0, 0.0, pltpu.roll(x, 1, 0))
    nxt = jnp.where(pos == row_len - 1, 0.0, pltpu.roll(x, rows - 1, 0))
    return prev, nxt


def _ada_kernel(c_ref, w_ref, b_ref, o_ref):
    c = c_ref[...]
    s = c * _sigmoid(c)
    o_ref[0] = _mm(s, w_ref[0]) + b_ref[0]


def _ada(cc, ada_w, ada_b):
    depth, d, n = ada_w.shape
    rows = cc.shape[0]
    return pl.pallas_call(
        _ada_kernel,
        grid=(depth, n // ADA_COL_TILE),
        in_specs=[
            pl.BlockSpec((rows, d), lambda l, j: (0, 0)),
            pl.BlockSpec((1, d, ADA_COL_TILE), lambda l, j: (l, 0, j)),
            pl.BlockSpec((1, 1, ADA_COL_TILE), lambda l, j: (l, 0, j)),
        ],
        out_specs=pl.BlockSpec((1, rows, ADA_COL_TILE), lambda l, j: (l, 0, j)),
        out_shape=jax.ShapeDtypeStruct((depth, rows, n), F32),
        compiler_params=pltpu.CompilerParams(
            dimension_semantics=("arbitrary", "arbitrary"), vmem_limit_bytes=VMEM_LIMIT_BYTES),
        name="ada",
    )(cc, ada_w, ada_b.reshape(depth, 1, n))


def _rwkv_pre_kernel(ctx_len, ctx_ref, x_ref, sh_ref, sc_ref, ng_ref, mu_ref, wrkv_ref, w1_ref, w2_ref, w0_ref,
                     a1_ref, a2_ref, a0_ref, g1_ref, g2_ref, kk_ref, ka_ref, rk_ref,
                     r_out, v_out, g_out, bon_out, cs0_out, cs1_out, al0_out, al1_out, kd0_out, kd1_out,
                     b0_out, b1_out):
    j = pl.program_id(1)
    row_len = jnp.where(j == 0, ctx_len, CHUNK)
    xin = jnp.where(j == 0, ctx_ref[0], x_ref[0])
    xn = _rms_modulate(xin, ng_ref[...], sh_ref[0], sc_ref[0])
    prev, nxt = _row_neighbours(xn, row_len)
    xx = 0.5 * (prev + nxt) - xn

    def mix(i):
        return xn + xx * mu_ref[i:i + 1, :]

    r = _mm(mix(0), wrkv_ref[0])
    k = _mm(mix(1), wrkv_ref[1])
    v = _mm(mix(2), wrkv_ref[2])
    r_out[0] = r.astype(r_out.dtype)
    v_out[0] = v.astype(v_out.dtype)
    g_out[0] = _mm(_sigmoid(_mm(mix(5), g1_ref[...])), g2_ref[...]).astype(g_out.dtype)

    kk = k * kk_ref[...]
    kk = kk * lax.rsqrt(jnp.maximum(_head_sum(kk * kk), L2_EPS * L2_EPS))

    wl = jnp.tanh(_mm(mix(3), w1_ref[...]))
    al = _mm(mix(4), a1_ref[...])
    rows = xn.shape[0]
    ti = lax.broadcasted_iota(jnp.int32, (rows, rows), 0)
    tj = lax.broadcasted_iota(jnp.int32, (rows, rows), 1)
    same_chunk = (ti ^ tj) < CHUNK
    bonus_dot = None
    for p, (cs_out, al_out, kd_out, b_out) in enumerate(((cs0_out, al0_out, kd0_out, b0_out),
                                                         (cs1_out, al1_out, kd1_out, b1_out))):
        z = w0_ref[p:p + 1, :] + _mm(wl, w2_ref[p])
        lw = -jnp.exp(-0.5) * _sigmoid(z)
        tri = jnp.where(same_chunk & ((tj >= ti) if p else (tj <= ti)), 1.0, 0.0).astype(BF16)
        hi, lo = _split2(lw)
        cs_out[0] = jnp.dot(tri, hi, preferred_element_type=F32) + jnp.dot(tri, lo, preferred_element_type=F32)
        al_out[0] = (-kk * jnp.exp(-lw)).astype(al_out.dtype)
        a = _sigmoid(a0_ref[p:p + 1, :] + _mm(al, a2_ref[p]))
        kd = k * (1.0 + (a - 1.0) * ka_ref[...])
        kd_out[0] = kd.astype(kd_out.dtype)
        b_out[0] = (kk * a).astype(b_out.dtype)
        t = r * kd * rk_ref[...]
        bonus_dot = t if bonus_dot is None else bonus_dot + t
    bon_out[0] = (_head_sum(bonus_dot) * v).astype(bon_out.dtype)


def _rwkv_pre(ctx, x, mod, norm_g, mu, w_rkv, w1, w2, w0, a1, a2, a0, g1, g2, k_k, k_a, r_k, batch0, b):
    n_batch, t, d = x.shape
    ctx_len = ctx.shape[1]
    tt = ctx_len + t
    assert ctx_len == TOKEN_TILE and t % TOKEN_TILE == 0

    def mod_spec(part):
        return pl.BlockSpec((1, 1, d), lambda i, j: (jnp.where(j == 0, n_batch, i + batch0), 0, part))

    def full(a):
        nd = a.ndim
        return pl.BlockSpec(a.shape, lambda i, j: (0,) * nd)

    tile = pl.BlockSpec((1, TOKEN_TILE, d), lambda i, j: (i, j, 0))
    params = (norm_g, mu, w_rkv, w1, w2, w0, a1, a2, a0, g1, g2, k_k, k_a, r_k)
    out_dtypes = (BF16,) * 4 + (F32, F32) + (BF16,) * 6
    return pl.pallas_call(
        functools.partial(_rwkv_pre_kernel, ctx_len),
        grid=(b, tt // TOKEN_TILE),
        in_specs=[pl.BlockSpec((1, TOKEN_TILE, d), lambda i, j: (i + batch0, 0, 0)),
                  pl.BlockSpec((1, TOKEN_TILE, d), lambda i, j: (i + batch0, jnp.maximum(j - 1, 0), 0)),
                  mod_spec(0), mod_spec(1)] + [full(a) for a in params],
        out_specs=[tile] * len(out_dtypes),
        out_shape=[jax.ShapeDtypeStruct((b, tt, d), dt) for dt in out_dtypes],
        compiler_params=pltpu.CompilerParams(
            dimension_semantics=("arbitrary", "arbitrary"), vmem_limit_bytes=VMEM_LIMIT_BYTES),
        name="rwkv_pre",
    )(ctx, x, mod, mod, *params)


def _expand(x, left):
    return jnp.concatenate([jnp.where(left, x, 0.0), jnp.where(left, 0.0, x)], axis=0)


def _dot(a, b):
    return jnp.dot(a, b, preferred_element_type=F32)


def _chunk_steps(chains):
    assert CHUNK == HEAD_DIM
    c = CHUNK
    lane = lax.broadcasted_iota(jnp.int32, (c, LANES), 1)
    left = lane < HEAD_DIM
    tt = lax.broadcasted_iota(jnp.int32, (c, LANES), 0)
    jj = lane & (c - 1)
    diag = jj == tt
    tri = {False: (jj < tt, jj <= tt), True: (jj > tt, jj >= tt)}

    def bd(x):
        return _expand(x, left).astype(BF16)

    def rows(*xs):
        return jnp.concatenate(xs, axis=0).astype(BF16)

    nt_dims = (((1,), (1,)), ((), ()))
    eye = jnp.where(diag, 1.0, 0.0).astype(BF16)

    prep = []
    for cs, alw, r, kd, v, be, h, reverse in chains:
        tot = cs[0:1, :] if reverse else cs[c - 1:c, :]
        e_pos, e_neg, e_rem = jnp.exp(cs), jnp.exp(-cs), jnp.exp(tot - cs)
        at, rt = alw * e_pos, r * e_pos
        q = dict(a_e=bd(at), rt=rt, v_e=bd(v), g_tot=jnp.exp(tot), h=h, tri=tri[reverse])
        q["hat_t"] = lax.dot_general(eye, jnp.concatenate([bd(be * e_rem), bd(kd * e_rem)], axis=0), nt_dims,
                                     preferred_element_type=F32)
        q["sc"] = lax.dot_general(rows(at, rt), jnp.concatenate([bd(be * e_neg), bd(kd * e_neg)], axis=0),
                                  nt_dims, preferred_element_type=F32)
        prep.append(q)

    for q in prep:
        (strict, incl), sc = q["tri"], q["sc"]
        q["n"] = jnp.where(strict, sc[:c, :LANES], 0.0)
        a_ak = jnp.where(strict, sc[:c, LANES:], 0.0)
        q["a_rb"] = jnp.where(incl, sc[c:, :LANES], 0.0)
        a_rk = jnp.where(incl, sc[c:, LANES:], 0.0)
        q["bh_t"] = q["hat_t"][:, :LANES]
        q["vv"] = _dot(rows(a_ak, q["hat_t"][:, LANES:], a_rk), q["v_e"])

    for q in prep:
        q["t"] = jnp.where(diag, 1.0, 0.0) + q["n"]
        q["p"] = _dot(q["n"].astype(BF16), bd(q["n"]))
    order = 4
    while order < c:
        for q in prep:
            x = _dot(rows(q["p"], q["t"]), bd(q["p"]))
            q["p"] = x[:c]
            q["t"] = q["t"] + x[c:]
        order *= 2
    for q in prep:
        q["t"] = q["t"] + _dot(q["t"].astype(BF16), bd(q["p"]))

    for q in prep:
        xu = _dot(q["t"].astype(BF16), jnp.concatenate([q["a_e"], bd(q["vv"][:c])], axis=1))
        q["au_e"] = jnp.concatenate([bd(xu[:, :LANES]), bd(xu[:, LANES:])], axis=1)
    for q in prep:
        z = _dot(rows(q["bh_t"], q["a_rb"]), q["au_e"])
        q["m"] = z[:c, :LANES] + jnp.where(diag, q["g_tot"], 0.0)
        q["g"] = z[:c, LANES:] + q["vv"][c:2 * c]
        q["r_hat"] = q["rt"] + z[c:, :LANES]
        q["yi"] = z[c:, LANES:] + q["vv"][2 * c:]
    outs = []
    for q in prep:
        h = outs[q["h"]][1] if isinstance(q["h"], int) else q["h"]
        o = _dot(rows(q["r_hat"], q["m"]), bd(h))
        outs.append((o[:c] + q["yi"], o[c:] + q["g"]))
    return outs


def _wkv_kernel(rf_ref, vf_ref, csf_ref, af_ref, kdf_ref, bf_ref,
                rr_ref, vr_ref, csr_ref, ar_ref, kdr_ref, br_ref,
                yf_ref, yr_ref, hf_ref, hr_ref):
    @pl.when(pl.program_id(2) == 0)
    def _():
        hf_ref[...] = jnp.zeros_like(hf_ref)
        hr_ref[...] = jnp.zeros_like(hr_ref)

    dirs = ((rf_ref, vf_ref, csf_ref, af_ref, kdf_ref, bf_ref, yf_ref, hf_ref),
            (rr_ref, vr_ref, csr_ref, ar_ref, kdr_ref, br_ref, yr_ref, hr_ref))
    chains, dests = [], []
    per_chunk = 2 * PAIRS_PER_STEP
    for u in range(CHUNKS_PER_STEP):
        for p in range(PAIRS_PER_STEP):
            sl = slice(p * LANES, (p + 1) * LANES)
            for reverse, (r_ref, v_ref, cs_ref, a_ref, kd_ref, b_ref, y_ref, h_ref) in enumerate(dirs):
                at = CHUNKS_PER_STEP - 1 - u if reverse else u
                tm = slice(at * CHUNK, (at + 1) * CHUNK)
                h = h_ref[p] if u == 0 else len(chains) - per_chunk
                chains.append((cs_ref[0, tm, sl], a_ref[0, tm, sl].astype(F32), r_ref[0, tm, sl].astype(F32),
                               kd_ref[0, tm, sl].astype(F32), v_ref[0, tm, sl].astype(F32),
                               b_ref[0, tm, sl].astype(F32), h, bool(reverse)))
                dests.append((y_ref, tm, sl, h_ref if u == CHUNKS_PER_STEP - 1 else None, p))
    for (y, h_new), (y_ref, tm, sl, h_ref, p) in zip(_chunk_steps(chains), dests):
        y_ref[0, tm, sl] = y
        if h_ref is not None:
            h_ref[p] = h_new


def _wkv(r, v, cs0, al0, kd0, b0, cs1, al1, kd1, b1, ctx_len):
    b, tt, d = r.shape
    step_rows = CHUNKS_PER_STEP * CHUNK
    n_steps, n_ctx = tt // step_rows, ctx_len // step_rows
    assert n_steps * step_rows == tt and n_ctx * step_rows == ctx_len
    width = PAIRS_PER_STEP * LANES

    def fwd_map(i, j, s):
        return (i, s, j)

    def rev_map(i, j, s):
        return (i, jnp.where(s < n_ctx, n_ctx - 1 - s, n_steps - 1 - (s - n_ctx)), j)

    fwd = pl.BlockSpec((1, step_rows, width), fwd_map)
    rev = pl.BlockSpec((1, step_rows, width), rev_map)
    return pl.pallas_call(
        _wkv_kernel,
        grid=(b, d // width, n_steps),
        in_specs=[fwd] * 6 + [rev] * 6,
        out_specs=[fwd, rev],
        out_shape=[jax.ShapeDtypeStruct((b, tt, d), F32)] * 2,
        scratch_shapes=[pltpu.VMEM((PAIRS_PER_STEP, HEAD_DIM, LANES), F32)] * 2,
        compiler_params=pltpu.CompilerParams(
            dimension_semantics=("arbitrary", "arbitrary", "arbitrary"), vmem_limit_bytes=VMEM_LIMIT_BYTES),
        name="wkv",
    )(r, v, cs0, al0, kd0, b0, r, v, cs1, al1, kd1, b1)


def _pack_bf16_pairs(x):
    half = x.shape[1] // 2
    lo = pltpu.bitcast(x[:, :half].astype(BF16).astype(F32), jnp.int32)
    hi = pltpu.bitcast(x[:, half:].astype(BF16).astype(F32), jnp.int32)
    return lax.shift_right_logical(lo, jnp.int32(16)) | (hi & jnp.int32(-65536))


def _unpack_bf16_pairs(w):
    lo = pltpu.bitcast(lax.shift_left(w, jnp.int32(16)), F32)
    hi = pltpu.bitcast(w & jnp.int32(-65536), F32)
    return jnp.concatenate([lo, hi], axis=1)


def _route(h1, ng, shift, scale, rwt_ref, rb_ref, hm_out, eidx_out, gate_out):
    hm = _rms_modulate(h1, ng, shift, scale)
    hm_out[...] = _pack_bf16_pairs(hm)
    w_hi, w_mid = _split2(rwt_ref[...])
    h_hi, h_mid = _split2(hm)
    logits_t = _dot(h_hi, w_hi) + _dot(h_mid, w_hi) + _dot(h_hi, w_mid)
    logits = logits_t.T[:N_EXPERTS]
    s = _sigmoid(logits)
    sel = s + rb_ref[...]
    assert EXPERTS_PER_GROUP == 4 and N_GROUPS == 4
    cands = []
    for g in range(N_GROUPS):
        m = [sel[e:e + 1, :] for e in range(g * EXPERTS_PER_GROUP, (g + 1) * EXPERTS_PER_GROUP)]
        sg = [s[e:e + 1, :] for e in range(g * EXPERTS_PER_GROUP, (g + 1) * EXPERTS_PER_GROUP)]
        pairs = [m[i] + m[k] for i in range(4) for k in range(i + 1, 4)]
        score = jnp.maximum(jnp.maximum(jnp.maximum(pairs[0], pairs[1]), jnp.maximum(pairs[2], pairs[3])),
                            jnp.maximum(pairs[4], pairs[5]))
        chosen = []
        for i in range(4):
            ahead = [((m[k] >= m[i]) if k < i else (m[k] > m[i])).astype(jnp.int32) for k in range(4) if k != i]
            chosen.append(ahead[0] + ahead[1] + ahead[2] < 2)
        base = g * EXPERTS_PER_GROUP
        lo_idx = jnp.where(chosen[0], base, jnp.where(chosen[1], base + 1, base + 2))
        lo_gate = jnp.where(chosen[0], sg[0], jnp.where(chosen[1], sg[1], sg[2]))
        hi_idx = jnp.where(chosen[3], base + 3, jnp.where(chosen[2], base + 2, base + 1))
        hi_gate = jnp.where(chosen[3], sg[3], jnp.where(chosen[2], sg[2], sg[1]))
        cands.append((score, lo_idx, hi_idx, lo_gate, hi_gate))

    def better(x, y):
        win = y[0] > x[0]
        return tuple(jnp.where(win, yv, xv) for xv, yv in zip(x, y))

    _, e_lo, e_hi, g_lo, g_hi = better(better(cands[0], cands[1]), better(cands[2], cands[3]))
    eidx_out[...] = jnp.concatenate([e_lo, e_hi], axis=0)
    gate_out[...] = jnp.concatenate([g_lo, g_hi], axis=0) / (g_lo + g_hi)


def _readout_kernel(yf_ref, yr_ref, bon_ref, g_ref, h_ref, gt_ref, gnw_ref, gnb_ref, wo_ref,
                    ng_ref, sh_ref, sc_ref, rwt_ref, rb_ref, h1_out, hm_out, eidx_out, gate_out):
    y = yf_ref[0] + yr_ref[0] + bon_ref[0].astype(F32)
    mean = _head_sum(y) * (1.0 / HEAD_DIM)
    yc = y - mean
    var = _head_sum(yc * yc) * (1.0 / HEAD_DIM)
    yn = yc * lax.rsqrt(var + GN_EPS) * gnw_ref[...] + gnb_ref[...]
    out = _mm(yn * g_ref[0].astype(F32), wo_ref[...])
    h1 = h_ref[0] + gt_ref[0] * out
    h1_out[0] = h1
    _route(h1, ng_ref[...], sh_ref[0], sc_ref[0], rwt_ref, rb_ref, hm_out, eidx_out, gate_out)


def _mod_spec3(part, d, batch0=0):
    return pl.BlockSpec((1, 1, d), lambda i, j: (i + batch0, 0, part))


def _route_out_specs(b, t, d):
    nt = t // TOKEN_TILE
    specs = [pl.BlockSpec((1, TOKEN_TILE, d), lambda i, j: (i, j, 0)),
             pl.BlockSpec((TOKEN_TILE, d // 2), lambda i, j: (i * nt + j, 0)),
             pl.BlockSpec((2, TOKEN_TILE), lambda i, j: (0, i * nt + j)),
             pl.BlockSpec((2, TOKEN_TILE), lambda i, j: (0, i * nt + j))]
    shapes = [jax.ShapeDtypeStruct((b, t, d), F32), jax.ShapeDtypeStruct((b * t, d // 2), jnp.int32),
              jax.ShapeDtypeStruct((2, b * t), jnp.int32), jax.ShapeDtypeStruct((2, b * t), F32)]
    return specs, shapes


def _full2(a):
    nd = a.ndim
    return pl.BlockSpec(a.shape, lambda i, j: (0,) * nd)


def _readout(yf, yr, bonus, g, h, mod, ctx_len, gn_w, gn_b, w_o, norm_g, router_wt, router_b, batch0, b):
    _, t, d = h.shape
    off = ctx_len // TOKEN_TILE
    nt = t // TOKEN_TILE
    shifted = pl.BlockSpec((1, TOKEN_TILE, d), lambda i, j: (i, j + off, 0))
    tile = pl.BlockSpec((1, TOKEN_TILE, d), lambda i, j: (i + batch0, j, 0))
    params_a = (gn_w, gn_b, w_o, norm_g)
    params_b = (router_wt, router_b)
    out_specs, out_shape = _route_out_specs(b, t, d)
    return pl.pallas_call(
        _readout_kernel,
        grid=(b, nt),
        in_specs=[shifted] * 4 + [tile, _mod_spec3(2, d, batch0)] + [_full2(a) for a in params_a]
        + [_mod_spec3(3, d, batch0), _mod_spec3(4, d, batch0)] + [_full2(a) for a in params_b],
        out_specs=out_specs,
        out_shape=out_shape,
        compiler_params=pltpu.CompilerParams(
            dimension_semantics=("arbitrary", "arbitrary"), vmem_limit_bytes=VMEM_LIMIT_BYTES),
        name="readout",
    )(yf, yr, bonus, g, h, mod, *params_a, mod, mod, *params_b)


def _moe_residual(y0_ref, y1_ref, gate_ref, h1, gt):
    gates = gate_ref[...]
    y = gates[:, 0:1] * _unpack_bf16_pairs(y0_ref[...]) + gates[:, 1:2] * _unpack_bf16_pairs(y1_ref[...])
    return h1 + gt * y


def _sconv_kernel(y0_ref, y1_ref, gate_ref, hp_ref, gtp_ref, sh1_ref, sc1_ref, gt_ref, ng1_ref, win_ref, cw_ref,
                  wout_ref, ng2_ref, sh2_ref, sc2_ref, rwt_ref, rb_ref, h1_out, hm_out, eidx_out, gate_out):
    h = _moe_residual(y0_ref, y1_ref, gate_ref, hp_ref[0], gtp_ref[0])
    d = h.shape[1]
    xn = _rms_modulate(h, ng1_ref[...], sh1_ref[0], sc1_ref[0]).astype(BF16)
    bg = jnp.dot(xn, win_ref[:, 0:d], preferred_element_type=F32)
    u = (jnp.dot(xn, win_ref[:, d:2 * d], preferred_element_type=F32)
         * jnp.dot(xn, win_ref[:, 2 * d:3 * d], preferred_element_type=F32))
    prev, nxt = _row_neighbours(u, CHUNK)
    conv = cw_ref[0:1, :] * prev + cw_ref[1:2, :] * u + cw_ref[2:3, :] * nxt
    h1 = h + gt_ref[0] * _mm(bg * conv, wout_ref[...])
    h1_out[0] = h1
    _route(h1, ng2_ref[...], sh2_ref[0], sc2_ref[0], rwt_ref, rb_ref, hm_out, eidx_out, gate_out)


def _sconv(yg, gates, h_prev, mod_prev, mod, norm_g1, w_in, conv_w, w_out, norm_g2, router_wt, router_b, batch0):
    b, t, d = h_prev.shape
    nt = t // TOKEN_TILE
    tile = pl.BlockSpec((1, TOKEN_TILE, d), lambda i, j: (i, j, 0))
    params_a = (norm_g1, w_in, conv_w, w_out, norm_g2)
    params_b = (router_wt, router_b)
    out_specs, out_shape = _route_out_specs(b, t, d)
    return pl.pallas_call(
        _sconv_kernel,
        grid=(b, nt),
        in_specs=_moe_out_specs(b, t, d) + [tile, _mod_spec3(5, d, batch0)]
        + [_mod_spec3(0, d, batch0), _mod_spec3(1, d, batch0), _mod_spec3(2, d, batch0)]
        + [_full2(a) for a in params_a]
        + [_mod_spec3(3, d, batch0), _mod_spec3(4, d, batch0)] + [_full2(a) for a in params_b],
        out_specs=out_specs,
        out_shape=out_shape,
        compiler_params=pltpu.CompilerParams(
            dimension_semantics=("arbitrary", "arbitrary"), vmem_limit_bytes=VMEM_LIMIT_BYTES),
        name="sconv",
    )(yg, yg, gates, h_prev, mod_prev, mod, mod, mod, *params_a, mod, mod, *params_b)


def _positions_kernel(eidx_ref, pos_ref, ends_ref, run_ref):
    phase, i = pl.program_id(0), pl.program_id(1)
    tp = eidx_ref.shape[1]
    expert = lax.broadcasted_iota(jnp.int32, (N_EXPERTS, tp), 0)
    onehot = [jnp.where(eidx_ref[k:k + 1, :] == expert, 1.0, 0.0) for k in range(2)]

    @pl.when((phase == 0) & (i == 0))
    def _():
        run_ref[...] = jnp.zeros_like(run_ref)

    @pl.when(phase == 0)
    def _():
        run_ref[...] += jnp.sum(onehot[0] + onehot[1], axis=1, keepdims=True)

    @pl.when((phase == 1) & (i == 0))
    def _():
        padded = jnp.floor((run_ref[...] + (EXPERT_ROW_TILE - 1)) * (1.0 / EXPERT_ROW_TILE)) * EXPERT_ROW_TILE
        ei = lax.broadcasted_iota(jnp.int32, (N_EXPERTS, N_EXPERTS), 0)
        ej = lax.broadcasted_iota(jnp.int32, (N_EXPERTS, N_EXPERTS), 1)
        below = jnp.where(ej < ei, 1.0, 0.0)
        starts = jnp.dot(below, jnp.broadcast_to(padded, (N_EXPERTS, LANES)), precision=lax.Precision.HIGHEST,
                         preferred_element_type=F32)
        ends_ref[...] = (starts + padded).astype(jnp.int32)
        run_ref[...] = starts[:, 0:1]

    @pl.when(phase == 1)
    def _():
        ti = lax.broadcasted_iota(jnp.int32, (tp, tp), 0)
        tj = lax.broadcasted_iota(jnp.int32, (tp, tp), 1)
        before = jnp.where(ti < tj, 1.0, 0.0).astype(BF16)
        run = run_ref[...]
        for k in range(2):
            prefix = jnp.dot(onehot[k].astype(BF16), before, preferred_element_type=F32)
            pos_ref[k:k + 1, :] = jnp.sum(onehot[k] * (prefix + run), axis=0, keepdims=True).astype(jnp.int32)
            run = run + jnp.sum(onehot[k], axis=1, keepdims=True)
        run_ref[...] = run


def _positions(eidx):
    n = eidx.shape[1]
    nt = n // POSITION_TILE
    return pl.pallas_call(
        _positions_kernel,
        grid=(2, nt),
        in_specs=[pl.BlockSpec((2, POSITION_TILE), lambda p, i: (0, i))],
        out_specs=[pl.BlockSpec((2, POSITION_TILE), lambda p, i: (0, i * p)),
                   pl.BlockSpec((N_EXPERTS, LANES), lambda p, i: (0, 0))],
        out_shape=[jax.ShapeDtypeStruct((2, n), jnp.int32), jax.ShapeDtypeStruct((N_EXPERTS, LANES), jnp.int32)],
        scratch_shapes=[pltpu.VMEM((N_EXPERTS, 1), F32)],
        compiler_params=pltpu.CompilerParams(
            dimension_semantics=("arbitrary", "arbitrary"), vmem_limit_bytes=VMEM_LIMIT_BYTES),
        name="positions",
    )(eidx)


def _sc_mesh():
    return plsc.VectorSubcoreMesh(core_axis_name="c", subcore_axis_name="s")


def _sc_worker(n_workers_per_core=SC_SUBCORES):
    return lax.axis_index("c") * n_workers_per_core + lax.axis_index("s")


def _sc_invert(pos, n_rows):
    n_slots = pos.shape[0]
    n_tokens = n_slots // 2
    assert n_tokens & (n_tokens - 1) == 0
    per_worker = n_rows // SC_WORKERS
    assert per_worker * SC_WORKERS == n_rows and per_worker % SC_LANES == 0 and n_slots % SC_LANES == 0

    def body(pos_hbm, inv_hbm, pos_v, inv_v):
        base = _sc_worker() * per_worker
        pltpu.sync_copy(pos_hbm, pos_v)

        @pl.loop(0, per_worker, step=SC_LANES)
        def _(j):
            inv_v[pl.ds(j, SC_LANES)] = (base + j + lax.iota(jnp.int32, SC_LANES)) & (n_tokens - 1)

        @pl.loop(0, n_slots, step=SC_LANES)
        def _(s):
            local = pos_v[pl.ds(s, SC_LANES)] - base
            mine = (local >= 0) & (local < per_worker)
            token = (s + lax.iota(jnp.int32, SC_LANES)) & (n_tokens - 1)
            plsc.store_scatter(inv_v, [jnp.where(mine, local, 0)], token, mask=mine)

        pltpu.sync_copy(inv_v, inv_hbm.at[pl.ds(base, per_worker)])

    return pl.kernel(
        body, out_type=jax.ShapeDtypeStruct((n_rows,), jnp.int32), mesh=_sc_mesh(),
        scratch_types=[pltpu.VMEM((n_slots,), jnp.int32), pltpu.VMEM((per_worker,), jnp.int32)],
        compiler_params=pltpu.CompilerParams(needs_layout_passes=False),
        name="sc_invert",
    )(pos)


def _sc_gather(table, idx):
    n_rows, width = idx.shape[0], table.shape[1]
    per_worker = n_rows // SC_WORKERS
    assert per_worker * SC_WORKERS == n_rows and per_worker % SC_GATHER_ROWS == 0

    def body(table_hbm, idx_hbm, out_hbm, idx_v, rows_v, sem):
        base = _sc_worker() * per_worker

        @pl.loop(0, per_worker, step=SC_GATHER_ROWS)
        def _(j):
            pltpu.sync_copy(idx_hbm.at[pl.ds(base + j, SC_GATHER_ROWS)], idx_v)
            pltpu.async_copy(table_hbm.at[idx_v], rows_v, sem).wait()
            pltpu.sync_copy(rows_v, out_hbm.at[pl.ds(base + j, SC_GATHER_ROWS)])

    return pl.kernel(
        body, out_type=jax.ShapeDtypeStruct((n_rows, width), table.dtype), mesh=_sc_mesh(),
        scratch_types=[pltpu.VMEM((SC_GATHER_ROWS,), jnp.int32), pltpu.VMEM((SC_GATHER_ROWS, width), table.dtype),
                       pltpu.SemaphoreType.DMA],
        name="sc_gather",
    )(table, idx)


def _experts_kernel(te_ref, xs_ref, wg_ref, wu_ref, wd_ref, ys_ref, wg_bf, wu_bf, wd_bf):
    i = pl.program_id(0)
    expert = te_ref[i]

    @pl.when((i == 0) | (expert != te_ref[jnp.maximum(i - 1, 0)]))
    def _():
        wg_bf[...] = wg_ref[0, 0].astype(BF16)
        wu_bf[...] = wu_ref[0, 0].astype(BF16)
        wd_bf[...] = wd_ref[0, 0].astype(BF16)

    @pl.when(expert < N_EXPERTS)
    def _():
        x = _unpack_bf16_pairs(xs_ref[...]).astype(BF16)
        de = wg_bf.shape[1]
        cols = [slice(n, n + EXPERT_COL_TILE) for n in range(0, de, EXPERT_COL_TILE)]
        gates = [(_dot(x, wg_bf[:, c]), _dot(x, wu_bf[:, c])) for c in cols]
        hes = [(gate * _sigmoid(gate) * up).astype(BF16) for gate, up in gates]
        y = _dot(hes[0], wd_bf[cols[0], :])
        for he, c in zip(hes[1:], cols[1:]):
            y = y + _dot(he, wd_bf[c, :])
        ys_ref[...] = _pack_bf16_pairs(y)


def _experts(tile_expert, xs, layer, w_gate, w_up, w_down):
    n_rows, half = xs.shape
    _, n_e, d, de = w_gate.shape

    def w_map(i, te):
        return (layer, jnp.minimum(te[i], n_e - 1), 0, 0)

    rows = pl.BlockSpec((EXPERT_ROW_TILE, half), lambda i, te: (i, 0))
    return pl.pallas_call(
        _experts_kernel,
        grid_spec=pltpu.PrefetchScalarGridSpec(
            num_scalar_prefetch=1,
            grid=(n_rows // EXPERT_ROW_TILE,),
            in_specs=[rows, pl.BlockSpec((1, 1, d, de), w_map), pl.BlockSpec((1, 1, d, de), w_map),
                      pl.BlockSpec((1, 1, de, d), w_map)],
            out_specs=rows,
            scratch_shapes=[pltpu.VMEM((d, de), BF16), pltpu.VMEM((d, de), BF16), pltpu.VMEM((de, d), BF16)]),
        out_shape=jax.ShapeDtypeStruct((n_rows, half), jnp.int32),
        compiler_params=pltpu.CompilerParams(
            dimension_semantics=("arbitrary",), vmem_limit_bytes=VMEM_LIMIT_BYTES),
        name="experts",
    )(tile_expert, xs, w_gate, w_up, w_down)


def _final_kernel(y0_ref, y1_ref, gate_ref, h1_ref, gt_ref, fg_ref, *out_refs):
    o_ref = out_refs[-1]
    h2 = _moe_residual(y0_ref, y1_ref, gate_ref, h1_ref[0], gt_ref[0])
    ms = jnp.mean(h2 * h2, axis=-1, keepdims=True)
    o_ref[0] = h2 * lax.rsqrt(ms + RMS_EPS) * fg_ref[...]


def _moe_out_specs(b, t, d):
    nt = t // TOKEN_TILE
    return [pl.BlockSpec((TOKEN_TILE, d // 2), lambda i, j: (i * nt + j, 0)),
            pl.BlockSpec((TOKEN_TILE, d // 2), lambda i, j: (b * nt + i * nt + j, 0)),
            pl.BlockSpec((TOKEN_TILE, 2), lambda i, j: (i * nt + j, 0))]


def _final(yg, gates, h1, mod, final_g, batch0, n_batch, out_prev):
    b, t, d = h1.shape
    tile = pl.BlockSpec((1, TOKEN_TILE, d), lambda i, j: (i, j, 0))
    in_specs = _moe_out_specs(b, t, d) + [tile, _mod_spec3(5, d, batch0), pl.BlockSpec((1, d), lambda i, j: (0, 0))]
    args = [yg, yg, gates, h1, mod, final_g]
    aliases = {}
    if out_prev is not None:
        in_specs.append(pl.BlockSpec(memory_space=pl.ANY))
        aliases = {len(args): 0}
        args.append(out_prev)
    return pl.pallas_call(
        _final_kernel,
        grid=(b, t // TOKEN_TILE),
        in_specs=in_specs,
        out_specs=pl.BlockSpec((1, TOKEN_TILE, d), lambda i, j: (i + batch0, j, 0)),
        out_shape=jax.ShapeDtypeStruct((n_batch, t, d), F32),
        input_output_aliases=aliases,
        compiler_params=pltpu.CompilerParams(
            dimension_semantics=("arbitrary", "arbitrary"), vmem_limit_bytes=VMEM_LIMIT_BYTES),
        name="final",
    )(*args)


def _moe(hm, eidx, gates, layer, w_gate, w_up, w_down):
    n_slots = 2 * hm.shape[0]
    n_rows = n_slots + N_EXPERTS * EXPERT_ROW_TILE
    pos, ends = _positions(eidx)
    pos = pos.reshape(n_slots)
    tile_start = jnp.arange(n_rows // EXPERT_ROW_TILE, dtype=jnp.int32) * EXPERT_ROW_TILE
    tile_expert = jnp.sum(tile_start[:, None] >= ends[None, :, 0], axis=1).astype(jnp.int32)
    xs = _sc_gather(hm, _sc_invert(pos, n_rows))
    ys = _experts(tile_expert, xs, layer, w_gate, w_up, w_down)
    return _sc_gather(ys, pos), gates.T


def kernel(x, c, ctx, c_ctx, ada_w, ada_b, norm_g, rw_mu, rw_w_rkv, rw_w0, rw_w1, rw_w2, rw_a0, rw_a1, rw_a2, rw_g1, rw_g2, rw_k_k, rw_k_a, rw_r_k, rw_gn_w, rw_gn_b, rw_w_o, sc_w_in, sc_conv, sc_w_out, router_w, router_b, moe_w_gate, moe_w_up, moe_w_down, final_g):
    b, t, d = x.shape
    ctx_len = ctx.shape[1]
    depth = ada_w.shape[0]
    assert d == D_MODEL and depth == 2 and t % POSITION_TILE == 0

    mod_rows = 16
    cc = jnp.concatenate([c, c_ctx[None, :], jnp.zeros((mod_rows - b - 1, d), F32)], axis=0)
    mod = _ada(cc, ada_w, ada_b).reshape(depth, mod_rows, 1, 6 * d)

    row = lambda a: a.reshape(1, d)
    router_wt = jnp.pad(router_w, ((0, 0), (0, LANES - N_EXPERTS)))
    router_bc = router_b.reshape(N_EXPERTS, 1)

    w1 = jnp.concatenate([rw_w1[0, 0], rw_w1[0, 1]], axis=1).astype(BF16)
    a1 = jnp.concatenate([rw_a1[0, 0], rw_a1[0, 1]], axis=1).astype(BF16)

    def pad_dirs(w):
        z = jnp.zeros_like(w[0])
        return jnp.stack([jnp.concatenate([w[0], z], axis=0), jnp.concatenate([z, w[1]], axis=0)]).astype(BF16)

    assert b % N_STREAMS == 0
    nb = b // N_STREAMS
    pre_params = (row(norm_g[0, 0]), rw_mu[0], rw_w_rkv[0].astype(BF16), w1, pad_dirs(rw_w2[0]), rw_w0[0], a1,
                  pad_dirs(rw_a2[0]), rw_a0[0], rw_g1[0].astype(BF16), rw_g2[0].astype(BF16), row(rw_k_k[0]),
                  row(rw_k_a[0]), row(rw_r_k[0]))
    w_o, w_in, w_out = rw_w_o[0].astype(BF16), sc_w_in[0].astype(BF16), sc_w_out[0].astype(BF16)
    out = None
    for batch0 in range(0, b, nb):
        (r, v, g, bonus, cs0, cs1, al0, al1, kd0, kd1, b0, b1) = _rwkv_pre(ctx, x, mod[0], *pre_params, batch0, nb)
        yf, yr = _wkv(r, v, cs0, al0, kd0, b0, cs1, al1, kd1, b1, ctx_len)
        h1, hm, eidx, gates = _readout(yf, yr, bonus, g, x, mod[0], ctx_len, row(rw_gn_w[0]), row(rw_gn_b[0]),
                                       w_o, row(norm_g[0, 1]), router_wt, router_bc, batch0, nb)
        yg, gates = _moe(hm, eidx, gates, 0, moe_w_gate, moe_w_up, moe_w_down)
        h1, hm, eidx, gates = _sconv(yg, gates, h1, mod[0], mod[1], row(norm_g[1, 0]), w_in, sc_conv[0], w_out,
                                     row(norm_g[1, 1]), router_wt, router_bc, batch0)
        yg, gates = _moe(hm, eidx, gates, 1, moe_w_gate, moe_w_up, moe_w_down)
        out = _final(yg, gates, h1, mod[1], row(final_g), batch0, b, out)
    return out
```

```python
import functools

import jax
import jax.numpy as jnp
from jax import lax
from jax.experimental import pallas as pl
from jax.experimental.pallas import tpu as pltpu
from jax.experimental.pallas import tpu_sc as plsc

F32 = jnp.float32
BF16 = jnp.bfloat16

D_MODEL = 1024
HEAD_DIM = 64
N_HEADS = D_MODEL // HEAD_DIM
LANES = 128
N_PAIRS = D_MODEL // LANES
CHUNK = 64
N_EXPERTS = 16
EXPERTS_PER_GROUP = 4
N_GROUPS = N_EXPERTS // EXPERTS_PER_GROUP
RMS_EPS = 1e-6
GN_EPS = 64e-5
L2_EPS = 1e-12

TOKEN_TILE = 256
WIDE_TOKEN_TILE = 512
MIX_SUB_TILES = 2
PRE_SUB_TILES = 2
EXPERT_ROW_TILE = 512
POSITION_TILE = 1024
SC_CORES, SC_SUBCORES, SC_LANES = 2, 16, 16
SC_WORKERS = SC_CORES * SC_SUBCORES
N_STREAMS = 2
SC_GATHER_ROWS = 128
PAIRS_PER_STEP = 8
CHUNKS_PER_STEP = 4
ADA_COL_TILE = 1536
VMEM_LIMIT_BYTES = 56 * 1024 * 1024


def _sigmoid(x):
    return 1.0 / (1.0 + jnp.exp(-x))


def _mm(a, b):
    return jnp.dot(a.astype(BF16), b.astype(BF16), preferred_element_type=F32)


def _mm_nt(a, b):
    return lax.dot_general(a.astype(BF16), b.astype(BF16), (((1,), (1,)), ((), ())),
                           preferred_element_type=F32)


def _split2(x):
    hi = x.astype(BF16)
    return hi, (x - hi.astype(F32)).astype(BF16)


def _head_sum(x):
    rows = x.shape[0]
    left = lax.broadcasted_iota(jnp.int32, (rows, LANES), 1) < HEAD_DIM
    outs = []
    for j in range(N_PAIRS):
        xb = x[:, LANES * j:LANES * (j + 1)]
        sa = jnp.sum(jnp.where(left, xb, 0.0), axis=-1, keepdims=True)
        sb = jnp.sum(jnp.where(left, 0.0, xb), axis=-1, keepdims=True)
        outs.append(jnp.where(left, sa, sb))
    return jnp.concatenate(outs, axis=-1)


def _rms_modulate(x, g, shift, scale):
    ms = jnp.mean(x * x, axis=-1, keepdims=True)
    xn = x * lax.rsqrt(ms + RMS_EPS) * g
    return xn * (1.0 + scale) + shift


def _row_neighbours(x, row_len):
    rows = x.shape[0]
    pos = lax.broadcasted_iota(jnp.int32, x.shape, 0) & (row_len - 1)
    prev = jnp.where(pos == 0, 0.0, pltpu.roll(x, 1, 0))
    nxt = jnp.where(pos == row_len - 1, 0.0, pltpu.roll(x, rows - 1, 0))
    return prev, nxt


def _ada_kernel(c_ref, w_ref, b_ref, o_ref):
    c = c_ref[...]
    s = c * _sigmoid(c)
    o_ref[0] = _mm(s, w_ref[0]) + b_ref[0]


def _ada(cc, ada_w, ada_b):
    depth, d, n = ada_w.shape
    rows = cc.shape[0]
    return pl.pallas_call(
        _ada_kernel,
        grid=(depth, n // ADA_COL_TILE),
        in_specs=[
            pl.BlockSpec((rows, d), lambda l, j: (0, 0)),
            pl.BlockSpec((1, d, ADA_COL_TILE), lambda l, j: (l, 0, j)),
            pl.BlockSpec((1, 1, ADA_COL_TILE), lambda l, j: (l, 0, j)),
        ],
        out_specs=pl.BlockSpec((1, rows, ADA_COL_TILE), lambda l, j: (l, 0, j)),
        out_shape=jax.ShapeDtypeStruct((depth, rows, n), F32),
        compiler_params=pltpu.CompilerParams(
            dimension_semantics=("arbitrary", "arbitrary"), vmem_limit_bytes=VMEM_LIMIT_BYTES),
        name="ada",
    )(cc, ada_w, ada_b.reshape(depth, 1, n))


def _rwkv_pre_kernel(ctx_len, ctx_ref, x_ref, sh_ref, sc_ref, ng_ref, mu_ref, wrkv_ref, w1_ref, w2_ref, w0_ref,
                     a1_ref, a2_ref, a0_ref, g1_ref, g2_ref, kk_ref, ka_ref, rk_ref,
                     r_out, v_out, g_out, bon_out, cs0_out, cs1_out, al0_out, al1_out, kd0_out, kd1_out,
                     b0_out, b1_out):
    j = pl.program_id(1)
    row_len = jnp.where(j == 0, ctx_len, CHUNK)
    xin = jnp.where(j == 0, ctx_ref[0], x_ref[0])
    xn = _rms_modulate(xin, ng_ref[...], sh_ref[0], sc_ref[0])
    prev, nxt = _row_neighbours(xn, row_len)
    xx = 0.5 * (prev + nxt) - xn

    sub = xn.shape[0] // PRE_SUB_TILES
    projected = []
    for s in range(PRE_SUB_TILES):
        rs = slice(s * sub, (s + 1) * sub)
        xn_s, xx_s = xn[rs], xx[rs]

        def mix(i, xn_s=xn_s, xx_s=xx_s):
            return xn_s + xx_s * mu_ref[i:i + 1, :]

        r = _mm(mix(0), wrkv_ref[0])
        k = _mm(mix(1), wrkv_ref[1])
        v = _mm(mix(2), wrkv_ref[2])
        g = _mm(_sigmoid(_mm(mix(5), g1_ref[...])), g2_ref[...])
        wl = jnp.tanh(_mm(mix(3), w1_ref[...]))
        al = _mm(mix(4), a1_ref[...])
        projected.append((rs, r, k, v, g, [_mm(wl, w2_ref[p]) for p in range(2)],
                          [_mm(al, a2_ref[p]) for p in range(2)]))

    ti = lax.broadcasted_iota(jnp.int32, (sub, sub), 0)
    tj = lax.broadcasted_iota(jnp.int32, (sub, sub), 1)
    same_chunk = (ti ^ tj) < CHUNK
    dir_outs = ((cs0_out, al0_out, kd0_out, b0_out), (cs1_out, al1_out, kd1_out, b1_out))
    for rs, r, k, v, g, zs, a_logits in projected:
        r_out[0, rs, :] = r.astype(r_out.dtype)
        v_out[0, rs, :] = v.astype(v_out.dtype)
        g_out[0, rs, :] = g.astype(g_out.dtype)
        kk = k * kk_ref[...]
        kk = kk * lax.rsqrt(jnp.maximum(_head_sum(kk * kk), L2_EPS * L2_EPS))
        bonus_dot = None
        for p, (cs_out, al_out, kd_out, b_out) in enumerate(dir_outs):
            z = w0_ref[p:p + 1, :] + zs[p]
            lw = -jnp.exp(-0.5) * _sigmoid(z)
            tri = jnp.where(same_chunk & ((tj >= ti) if p else (tj <= ti)), 1.0, 0.0).astype(BF16)
            hi, lo = _split2(lw)
            cs_out[0, rs, :] = _dot(tri, hi) + _dot(tri, lo)
            al_out[0, rs, :] = (-kk * jnp.exp(-lw)).astype(al_out.dtype)
            a = _sigmoid(a0_ref[p:p + 1, :] + a_logits[p])
            kd = k * (1.0 + (a - 1.0) * ka_ref[...])
            kd_out[0, rs, :] = kd.astype(kd_out.dtype)
            b_out[0, rs, :] = (kk * a).astype(b_out.dtype)
            t = r * kd * rk_ref[...]
            bonus_dot = t if bonus_dot is None else bonus_dot + t
        bon_out[0, rs, :] = (_head_sum(bonus_dot) * v).astype(bon_out.dtype)


def _rwkv_pre(ctx, x, mod, norm_g, mu, w_rkv, w1, w2, w0, a1, a2, a0, g1, g2, k_k, k_a, r_k, batch0, b):
    n_batch, t, d = x.shape
    ctx_len = ctx.shape[1]
    tt = ctx_len + t
    assert ctx_len == TOKEN_TILE and t % TOKEN_TILE == 0

    def mod_spec(part):
        return pl.BlockSpec((1, 1, d), lambda i, j: (jnp.where(j == 0, n_batch, i + batch0), 0, part))

    def full(a):
        nd = a.ndim
        return pl.BlockSpec(a.shape, lambda i, j: (0,) * nd)

    tile = pl.BlockSpec((1, TOKEN_TILE, d), lambda i, j: (i, j, 0))
    params = (norm_g, mu, w_rkv, w1, w2, w0, a1, a2, a0, g1, g2, k_k, k_a, r_k)
    out_dtypes = (BF16,) * 4 + (F32, F32) + (BF16,) * 6
    return pl.pallas_call(
        functools.partial(_rwkv_pre_kernel, ctx_len),
        grid=(b, tt // TOKEN_TILE),
        in_specs=[pl.BlockSpec((1, TOKEN_TILE, d), lambda i, j: (i + batch0, 0, 0)),
                  pl.BlockSpec((1, TOKEN_TILE, d), lambda i, j: (i + batch0, jnp.maximum(j - 1, 0), 0)),
                  mod_spec(0), mod_spec(1)] + [full(a) for a in params],
        out_specs=[tile] * len(out_dtypes),
        out_shape=[jax.ShapeDtypeStruct((b, tt, d), dt) for dt in out_dtypes],
        compiler_params=pltpu.CompilerParams(
            dimension_semantics=("arbitrary", "arbitrary"), vmem_limit_bytes=VMEM_LIMIT_BYTES),
        name="rwkv_pre",
    )(ctx, x, mod, mod, *params)


def _expand(x, left):
    return jnp.concatenate([jnp.where(left, x, 0.0), jnp.where(left, 0.0, x)], axis=0)


def _dot(a, b):
    return jnp.dot(a, b, preferred_element_type=F32)


def _chunk_steps(chains):
    assert CHUNK == HEAD_DIM
    c = CHUNK
    lane = lax.broadcasted_iota(jnp.int32, (c, LANES), 1)
    left = lane < HEAD_DIM
    tt = lax.broadcasted_iota(jnp.int32, (c, LANES), 0)
    jj = lane & (c - 1)
    diag = jj == tt
    tri = {False: (jj < tt, jj <= tt), True: (jj > tt, jj >= tt)}

    def bd(x):
        return _expand(x, left).astype(BF16)

    def rows(*xs):
        return jnp.concatenate(xs, axis=0).astype(BF16)

    nt_dims = (((1,), (1,)), ((), ()))
    eye = jnp.where(diag, 1.0, 0.0).astype(BF16)

    prep = []
    for cs, alw, r, kd, v, be, h, reverse in chains:
        tot = cs[0:1, :] if reverse else cs[c - 1:c, :]
        e_pos, e_neg, e_rem = jnp.exp(cs), jnp.exp(-cs), jnp.exp(tot - cs)
        at, rt = alw * e_pos, r * e_pos
        q = dict(a_e=bd(at), rt=rt, v_e=bd(v), g_tot=jnp.exp(tot), h=h, tri=tri[reverse])
        q["hat_t"] = lax.dot_general(eye, jnp.concatenate([bd(be * e_rem), bd(kd * e_rem)], axis=0), nt_dims,
                                     preferred_element_type=F32)
        q["sc"] = lax.dot_general(rows(at, rt), jnp.concatenate([bd(be * e_neg), bd(kd * e_neg)], axis=0),
                                  nt_dims, preferred_element_type=F32)
        prep.append(q)

    for q in prep:
        (strict, incl), sc = q["tri"], q["sc"]
        q["n"] = jnp.where(strict, sc[:c, :LANES], 0.0)
        a_ak = jnp.where(strict, sc[:c, LANES:], 0.0)
        q["a_rb"] = jnp.where(incl, sc[c:, :LANES], 0.0)
        a_rk = jnp.where(incl, sc[c:, LANES:], 0.0)
        q["bh_t"] = q["hat_t"][:, :LANES]
        q["vv"] = _dot(rows(a_ak, q["hat_t"][:, LANES:], a_rk), q["v_e"])

    for q in prep:
        q["t"] = jnp.where(diag, 1.0, 0.0) + q["n"]
        q["p"] = _dot(q["n"].astype(BF16), bd(q["n"]))
    order = 4
    while order < c:
        for q in prep:
            x = _dot(rows(q["p"], q["t"]), bd(q["p"]))
            q["p"] = x[:c]
            q["t"] = q["t"] + x[c:]
        order *= 2
    for q in prep:
        q["t"] = q["t"] + _dot(q["t"].astype(BF16), bd(q["p"]))

    for q in prep:
        xu = _dot(q["t"].astype(BF16), jnp.concatenate([q["a_e"], bd(q["vv"][:c])], axis=1))
        q["au_e"] = jnp.concatenate([bd(xu[:, :LANES]), bd(xu[:, LANES:])], axis=1)
    for q in prep:
        z = _dot(rows(q["bh_t"], q["a_rb"]), q["au_e"])
        q["m"] = z[:c, :LANES] + jnp.where(diag, q["g_tot"], 0.0)
        q["g"] = z[:c, LANES:] + q["vv"][c:2 * c]
        q["r_hat"] = q["rt"] + z[c:, :LANES]
        q["yi"] = z[c:, LANES:] + q["vv"][2 * c:]
    outs = []
    for q in prep:
        h = outs[q["h"]][1] if isinstance(q["h"], int) else q["h"]
        o = _dot(rows(q["r_hat"], q["m"]), bd(h))
        outs.append((o[:c] + q["yi"], o[c:] + q["g"]))
    return outs


def _wkv_kernel(rf_ref, vf_ref, csf_ref, af_ref, kdf_ref, bf_ref,
                rr_ref, vr_ref, csr_ref, ar_ref, kdr_ref, br_ref,
                yf_ref, yr_ref, hf_ref, hr_ref):
    @pl.when(pl.program_id(2) == 0)
    def _():
        hf_ref[...] = jnp.zeros_like(hf_ref)
        hr_ref[...] = jnp.zeros_like(hr_ref)

    dirs = ((rf_ref, vf_ref, csf_ref, af_ref, kdf_ref, bf_ref, yf_ref, hf_ref),
            (rr_ref, vr_ref, csr_ref, ar_ref, kdr_ref, br_ref, yr_ref, hr_ref))
    chains, dests = [], []
    per_chunk = 2 * PAIRS_PER_STEP
    for u in range(CHUNKS_PER_STEP):
        for p in range(PAIRS_PER_STEP):
            sl = slice(p * LANES, (p + 1) * LANES)
            for reverse, (r_ref, v_ref, cs_ref, a_ref, kd_ref, b_ref, y_ref, h_ref) in enumerate(dirs):
                at = CHUNKS_PER_STEP - 1 - u if reverse else u
                tm = slice(at * CHUNK, (at + 1) * CHUNK)
                h = h_ref[p] if u == 0 else len(chains) - per_chunk
                chains.append((cs_ref[0, tm, sl], a_ref[0, tm, sl].astype(F32), r_ref[0, tm, sl].astype(F32),
                               kd_ref[0, tm, sl].astype(F32), v_ref[0, tm, sl].astype(F32),
                               b_ref[0, tm, sl].astype(F32), h, bool(reverse)))
                dests.append((y_ref, tm, sl, h_ref if u == CHUNKS_PER_STEP - 1 else None, p))
    for (y, h_new), (y_ref, tm, sl, h_ref, p) in zip(_chunk_steps(chains), dests):
        y_ref[0, tm, sl] = y
        if h_ref is not None:
            h_ref[p] = h_new


def _wkv(r, v, cs0, al0, kd0, b0, cs1, al1, kd1, b1, ctx_len):
    b, tt, d = r.shape
    step_rows = CHUNKS_PER_STEP * CHUNK
    n_steps, n_ctx = tt // step_rows, ctx_len // step_rows
    assert n_steps * step_rows == tt and n_ctx * step_rows == ctx_len
    width = PAIRS_PER_STEP * LANES

    def fwd_map(i, j, s):
        return (i, s, j)

    def rev_map(i, j, s):
        return (i, jnp.where(s < n_ctx, n_ctx - 1 - s, n_steps - 1 - (s - n_ctx)), j)

    fwd = pl.BlockSpec((1, step_rows, width), fwd_map)
    rev = pl.BlockSpec((1, step_rows, width), rev_map)
    return pl.pallas_call(
        _wkv_kernel,
        grid=(b, d // width, n_steps),
        in_specs=[fwd] * 6 + [rev] * 6,
        out_specs=[fwd, rev],
        out_shape=[jax.ShapeDtypeStruct((b, tt, d), F32)] * 2,
        scratch_shapes=[pltpu.VMEM((PAIRS_PER_STEP, HEAD_DIM, LANES), F32)] * 2,
        compiler_params=pltpu.CompilerParams(
            dimension_semantics=("arbitrary", "arbitrary", "arbitrary"), vmem_limit_bytes=VMEM_LIMIT_BYTES),
        name="wkv",
    )(r, v, cs0, al0, kd0, b0, r, v, cs1, al1, kd1, b1)


def _pack_bf16_pairs(x):
    half = x.shape[1] // 2
    lo = pltpu.bitcast(x[:, :half].astype(BF16).astype(F32), jnp.int32)
    hi = pltpu.bitcast(x[:, half:].astype(BF16).astype(F32), jnp.int32)
    return lax.shift_right_logical(lo, jnp.int32(16)) | (hi & jnp.int32(-65536))


def _unpack_bf16_pairs(w):
    lo = pltpu.bitcast(lax.shift_left(w, jnp.int32(16)), F32)
    hi = pltpu.bitcast(w & jnp.int32(-65536), F32)
    return jnp.concatenate([lo, hi], axis=1)


def _route(h1, ng, shift, scale, rwt_ref, rb_ref, hm_out, eidx_out, gate_out, rs=slice(None)):
    hm = _rms_modulate(h1, ng, shift, scale)
    hm_out[rs, :] = _pack_bf16_pairs(hm)
    w_hi, w_mid = _split2(rwt_ref[...])
    h_hi, h_mid = _split2(hm)
    logits_t = _dot(h_hi, w_hi) + _dot(h_mid, w_hi) + _dot(h_hi, w_mid)
    logits = logits_t.T[:N_EXPERTS]
    s = _sigmoid(logits)
    sel = s + rb_ref[...]
    assert EXPERTS_PER_GROUP == 4 and N_GROUPS == 4
    cands = []
    for g in range(N_GROUPS):
        m = [sel[e:e + 1, :] for e in range(g * EXPERTS_PER_GROUP, (g + 1) * EXPERTS_PER_GROUP)]
        sg = [s[e:e + 1, :] for e in range(g * EXPERTS_PER_GROUP, (g + 1) * EXPERTS_PER_GROUP)]
        pairs = [m[i] + m[k] for i in range(4) for k in range(i + 1, 4)]
        score = jnp.maximum(jnp.maximum(jnp.maximum(pairs[0], pairs[1]), jnp.maximum(pairs[2], pairs[3])),
                            jnp.maximum(pairs[4], pairs[5]))
        chosen = []
        for i in range(4):
            ahead = [((m[k] >= m[i]) if k < i else (m[k] > m[i])).astype(jnp.int32) for k in range(4) if k != i]
            chosen.append(ahead[0] + ahead[1] + ahead[2] < 2)
        base = g * EXPERTS_PER_GROUP
        lo_idx = jnp.where(chosen[0], base, jnp.where(chosen[1], base + 1, base + 2))
        lo_gate = jnp.where(chosen[0], sg[0], jnp.where(chosen[1], sg[1], sg[2]))
        hi_idx = jnp.where(chosen[3], base + 3, jnp.where(chosen[2], base + 2, base + 1))
        hi_gate = jnp.where(chosen[3], sg[3], jnp.where(chosen[2], sg[2], sg[1]))
        cands.append((score, lo_idx, hi_idx, lo_gate, hi_gate))

    def better(x, y):
        win = y[0] > x[0]
        return tuple(jnp.where(win, yv, xv) for xv, yv in zip(x, y))

    _, e_lo, e_hi, g_lo, g_hi = better(better(cands[0], cands[1]), better(cands[2], cands[3]))
    eidx_out[:, rs] = jnp.concatenate([e_lo, e_hi], axis=0)
    gate_out[:, rs] = jnp.concatenate([g_lo, g_hi], axis=0) / (g_lo + g_hi)


def _readout_kernel(yf_ref, yr_ref, bon_ref, g_ref, h_ref, gt_ref, gnw_ref, gnb_ref, wo_ref,
                    ng_ref, sh_ref, sc_ref, rwt_ref, rb_ref, h1_out, hm_out, eidx_out, gate_out):
    sub = h_ref.shape[1] // MIX_SUB_TILES
    tiles = [slice(s * sub, (s + 1) * sub) for s in range(MIX_SUB_TILES)]
    gated, h1s = [], []
    for rs in tiles:
        y = yf_ref[0, rs, :] + yr_ref[0, rs, :] + bon_ref[0, rs, :].astype(F32)
        mean = _head_sum(y) * (1.0 / HEAD_DIM)
        yc = y - mean
        var = _head_sum(yc * yc) * (1.0 / HEAD_DIM)
        yn = yc * lax.rsqrt(var + GN_EPS) * gnw_ref[...] + gnb_ref[...]
        gated.append((yn * g_ref[0, rs, :].astype(F32)).astype(BF16))
    for rs, lhs in zip(tiles, gated):
        h1 = h_ref[0, rs, :] + gt_ref[0] * _dot(lhs, wo_ref[...])
        h1_out[0, rs, :] = h1
        h1s.append(h1)
    for rs, h1 in zip(tiles, h1s):
        _route(h1, ng_ref[...], sh_ref[0], sc_ref[0], rwt_ref, rb_ref, hm_out, eidx_out, gate_out, rs)


def _mod_spec3(part, d, batch0=0):
    return pl.BlockSpec((1, 1, d), lambda i, j: (i + batch0, 0, part))


def _route_out_specs(b, t, d, tile=TOKEN_TILE):
    nt = t // tile
    specs = [pl.BlockSpec((1, tile, d), lambda i, j: (i, j, 0)),
             pl.BlockSpec((tile, d // 2), lambda i, j: (i * nt + j, 0)),
             pl.BlockSpec((2, tile), lambda i, j: (0, i * nt + j)),
             pl.BlockSpec((2, tile), lambda i, j: (0, i * nt + j))]
    shapes = [jax.ShapeDtypeStruct((b, t, d), F32), jax.ShapeDtypeStruct((b * t, d // 2), jnp.int32),
              jax.ShapeDtypeStruct((2, b * t), jnp.int32), jax.ShapeDtypeStruct((2, b * t), F32)]
    return specs, shapes


def _full2(a):
    nd = a.ndim
    return pl.BlockSpec(a.shape, lambda i, j: (0,) * nd)


def _readout(yf, yr, bonus, g, h, mod, ctx_len, gn_w, gn_b, w_o, norm_g, router_wt, router_b, batch0, b):
    _, t, d = h.shape
    off = ctx_len // TOKEN_TILE
    nt = t // TOKEN_TILE
    shifted = pl.BlockSpec((1, TOKEN_TILE, d), lambda i, j: (i, j + off, 0))
    tile = pl.BlockSpec((1, TOKEN_TILE, d), lambda i, j: (i + batch0, j, 0))
    params_a = (gn_w, gn_b, w_o, norm_g)
    params_b = (router_wt, router_b)
    out_specs, out_shape = _route_out_specs(b, t, d)
    return pl.pallas_call(
        _readout_kernel,
        grid=(b, nt),
        in_specs=[shifted] * 4 + [tile, _mod_spec3(2, d, batch0)] + [_full2(a) for a in params_a]
        + [_mod_spec3(3, d, batch0), _mod_spec3(4, d, batch0)] + [_full2(a) for a in params_b],
        out_specs=out_specs,
        out_shape=out_shape,
        compiler_params=pltpu.CompilerParams(
            dimension_semantics=("arbitrary", "arbitrary"), vmem_limit_bytes=VMEM_LIMIT_BYTES),
        name="readout",
    )(yf, yr, bonus, g, h, mod, *params_a, mod, mod, *params_b)


def _moe_residual(y0_ref, y1_ref, gate_ref, h1, gt, rs=slice(None)):
    gates = gate_ref[rs, :]
    y = gates[:, 0:1] * _unpack_bf16_pairs(y0_ref[rs, :]) + gates[:, 1:2] * _unpack_bf16_pairs(y1_ref[rs, :])
    return h1 + gt * y


def _sconv_kernel(y0_ref, y1_ref, gate_ref, hp_ref, gtp_ref, sh1_ref, sc1_ref, gt_ref, ng1_ref, win_ref, cw_ref,
                  wout_ref, ng2_ref, sh2_ref, sc2_ref, rwt_ref, rb_ref, h1_out, hm_out, eidx_out, gate_out):
    d = hp_ref.shape[2]
    sub = hp_ref.shape[1] // MIX_SUB_TILES
    tiles = [slice(s * sub, (s + 1) * sub) for s in range(MIX_SUB_TILES)]
    hs, xns, gated, h1s = [], [], [], []
    for rs in tiles:
        h = _moe_residual(y0_ref, y1_ref, gate_ref, hp_ref[0, rs, :], gtp_ref[0], rs)
        hs.append(h)
        xns.append(_rms_modulate(h, ng1_ref[...], sh1_ref[0], sc1_ref[0]).astype(BF16))
    for xn in xns:
        bg = _dot(xn, win_ref[:, 0:d])
        u = _dot(xn, win_ref[:, d:2 * d]) * _dot(xn, win_ref[:, 2 * d:3 * d])
        prev, nxt = _row_neighbours(u, CHUNK)
        conv = cw_ref[0:1, :] * prev + cw_ref[1:2, :] * u + cw_ref[2:3, :] * nxt
        gated.append((bg * conv).astype(BF16))
    for rs, h, lhs in zip(tiles, hs, gated):
        h1 = h + gt_ref[0] * _dot(lhs, wout_ref[...])
        h1_out[0, rs, :] = h1
        h1s.append(h1)
    for rs, h1 in zip(tiles, h1s):
        _route(h1, ng2_ref[...], sh2_ref[0], sc2_ref[0], rwt_ref, rb_ref, hm_out, eidx_out, gate_out, rs)


def _sconv(yg, gates, h_prev, mod_prev, mod, norm_g1, w_in, conv_w, w_out, norm_g2, router_wt, router_b, batch0):
    b, t, d = h_prev.shape
    nt = t // WIDE_TOKEN_TILE
    tile = pl.BlockSpec((1, WIDE_TOKEN_TILE, d), lambda i, j: (i, j, 0))
    params_a = (norm_g1, w_in, conv_w, w_out, norm_g2)
    params_b = (router_wt, router_b)
    out_specs, out_shape = _route_out_specs(b, t, d, WIDE_TOKEN_TILE)
    return pl.pallas_call(
        _sconv_kernel,
        grid=(b, nt),
        in_specs=_moe_out_specs(b, t, d, WIDE_TOKEN_TILE) + [tile, _mod_spec3(5, d, batch0)]
        + [_mod_spec3(0, d, batch0), _mod_spec3(1, d, batch0), _mod_spec3(2, d, batch0)]
        + [_full2(a) for a in params_a]
        + [_mod_spec3(3, d, batch0), _mod_spec3(4, d, batch0)] + [_full2(a) for a in params_b],
        out_specs=out_specs,
        out_shape=out_shape,
        compiler_params=pltpu.CompilerParams(
            dimension_semantics=("arbitrary", "arbitrary"), vmem_limit_bytes=VMEM_LIMIT_BYTES),
        name="sconv",
    )(yg, yg, gates, h_prev, mod_prev, mod, mod, mod, *params_a, mod, mod, *params_b)


def _positions_kernel(eidx_ref, pos_ref, ends_ref, run_ref):
    phase, i = pl.program_id(0), pl.program_id(1)
    tp = eidx_ref.shape[1]
    expert = lax.broadcasted_iota(jnp.int32, (N_EXPERTS, tp), 0)
    onehot = [jnp.where(eidx_ref[k:k + 1, :] == expert, 1.0, 0.0) for k in range(2)]

    @pl.when((phase == 0) & (i == 0))
    def _():
        run_ref[...] = jnp.zeros_like(run_ref)

    @pl.when(phase == 0)
    def _():
        run_ref[...] += jnp.sum(onehot[0] + onehot[1], axis=1, keepdims=True)

    @pl.when((phase == 1) & (i == 0))
    def _():
        padded = jnp.floor((run_ref[...] + (EXPERT_ROW_TILE - 1)) * (1.0 / EXPERT_ROW_TILE)) * EXPERT_ROW_TILE
        ei = lax.broadcasted_iota(jnp.int32, (N_EXPERTS, N_EXPERTS), 0)
        ej = lax.broadcasted_iota(jnp.int32, (N_EXPERTS, N_EXPERTS), 1)
        below = jnp.where(ej < ei, 1.0, 0.0)
        starts = jnp.dot(below, jnp.broadcast_to(padded, (N_EXPERTS, LANES)), precision=lax.Precision.HIGHEST,
                         preferred_element_type=F32)
        ends_ref[...] = (starts + padded).astype(jnp.int32)
        run_ref[...] = starts[:, 0:1]

    @pl.when(phase == 1)
    def _():
        ti = lax.broadcasted_iota(jnp.int32, (tp, tp), 0)
        tj = lax.broadcasted_iota(jnp.int32, (tp, tp), 1)
        before = jnp.where(ti < tj, 1.0, 0.0).astype(BF16)
        run = run_ref[...]
        for k in range(2):
            prefix = jnp.dot(onehot[k].astype(BF16), before, preferred_element_type=F32)
            pos_ref[k:k + 1, :] = jnp.sum(onehot[k] * (prefix + run), axis=0, keepdims=True).astype(jnp.int32)
            run = run + jnp.sum(onehot[k], axis=1, keepdims=True)
        run_ref[...] = run


def _positions(eidx):
    n = eidx.shape[1]
    nt = n // POSITION_TILE
    return pl.pallas_call(
        _positions_kernel,
        grid=(2, nt),
        in_specs=[pl.BlockSpec((2, POSITION_TILE), lambda p, i: (0, i))],
        out_specs=[pl.BlockSpec((2, POSITION_TILE), lambda p, i: (0, i * p)),
                   pl.BlockSpec((N_EXPERTS, LANES), lambda p, i: (0, 0))],
        out_shape=[jax.ShapeDtypeStruct((2, n), jnp.int32), jax.ShapeDtypeStruct((N_EXPERTS, LANES), jnp.int32)],
        scratch_shapes=[pltpu.VMEM((N_EXPERTS, 1), F32)],
        compiler_params=pltpu.CompilerParams(
            dimension_semantics=("arbitrary", "arbitrary"), vmem_limit_bytes=VMEM_LIMIT_BYTES),
        name="positions",
    )(eidx)


def _sc_mesh():
    return plsc.VectorSubcoreMesh(core_axis_name="c", subcore_axis_name="s")


def _sc_worker(n_workers_per_core=SC_SUBCORES):
    return lax.axis_index("c") * n_workers_per_core + lax.axis_index("s")


def _sc_invert(pos, n_rows):
    n_slots = pos.shape[0]
    n_tokens = n_slots // 2
    assert n_tokens & (n_tokens - 1) == 0
    per_worker = n_rows // SC_WORKERS
    assert per_worker * SC_WORKERS == n_rows and per_worker % SC_LANES == 0 and n_slots % SC_LANES == 0

    def body(pos_hbm, inv_hbm, pos_v, inv_v):
        base = _sc_worker() * per_worker
        pltpu.sync_copy(pos_hbm, pos_v)

        @pl.loop(0, per_worker, step=SC_LANES)
        def _(j):
            inv_v[pl.ds(j, SC_LANES)] = (base + j + lax.iota(jnp.int32, SC_LANES)) & (n_tokens - 1)

        @pl.loop(0, n_slots, step=SC_LANES)
        def _(s):
            local = pos_v[pl.ds(s, SC_LANES)] - base
            mine = (local >= 0) & (local < per_worker)
            token = (s + lax.iota(jnp.int32, SC_LANES)) & (n_tokens - 1)
            plsc.store_scatter(inv_v, [jnp.where(mine, local, 0)], token, mask=mine)

        pltpu.sync_copy(inv_v, inv_hbm.at[pl.ds(base, per_worker)])

    return pl.kernel(
        body, out_type=jax.ShapeDtypeStruct((n_rows,), jnp.int32), mesh=_sc_mesh(),
        scratch_types=[pltpu.VMEM((n_slots,), jnp.int32), pltpu.VMEM((per_worker,), jnp.int32)],
        compiler_params=pltpu.CompilerParams(needs_layout_passes=False),
        name="sc_invert",
    )(pos)


def _sc_gather(table, idx):
    n_rows, width = idx.shape[0], table.shape[1]
    per_worker = n_rows // SC_WORKERS
    assert per_worker * SC_WORKERS == n_rows and per_worker % SC_GATHER_ROWS == 0

    def body(table_hbm, idx_hbm, out_hbm, idx_v, rows_v, sem):
        base = _sc_worker() * per_worker

        @pl.loop(0, per_worker, step=SC_GATHER_ROWS)
        def _(j):
            pltpu.sync_copy(idx_hbm.at[pl.ds(base + j, SC_GATHER_ROWS)], idx_v)
            pltpu.async_copy(table_hbm.at[idx_v], rows_v, sem).wait()
            pltpu.sync_copy(rows_v, out_hbm.at[pl.ds(base + j, SC_GATHER_ROWS)])

    return pl.kernel(
        body, out_type=jax.ShapeDtypeStruct((n_rows, width), table.dtype), mesh=_sc_mesh(),
        scratch_types=[pltpu.VMEM((SC_GATHER_ROWS,), jnp.int32), pltpu.VMEM((SC_GATHER_ROWS, width), table.dtype),
                       pltpu.SemaphoreType.DMA],
        name="sc_gather",
    )(table, idx)


def _experts_kernel(te_ref, xs_ref, wg_ref, wu_ref, wd_ref, ys_ref, wg_bf, wu_bf, wd_bf):
    i = pl.program_id(0)
    expert = te_ref[i]

    @pl.when((i == 0) | (expert != te_ref[jnp.maximum(i - 1, 0)]))
    def _():
        wg_bf[...] = wg_ref[0, 0].astype(BF16)
        wu_bf[...] = wu_ref[0, 0].astype(BF16)
        wd_bf[...] = wd_ref[0, 0].astype(BF16)

    @pl.when(expert < N_EXPERTS)
    def _():
        sub = xs_ref.shape[0] // MIX_SUB_TILES
        tiles = [slice(s * sub, (s + 1) * sub) for s in range(MIX_SUB_TILES)]
        xs = [_unpack_bf16_pairs(xs_ref[rs, :]).astype(BF16) for rs in tiles]
        hes = []
        for x in xs:
            gate, up = _dot(x, wg_bf[...]), _dot(x, wu_bf[...])
            hes.append((gate * _sigmoid(gate) * up).astype(BF16))
        for rs, he in zip(tiles, hes):
            ys_ref[rs, :] = _pack_bf16_pairs(_dot(he, wd_bf[...]))


def _experts(tile_expert, xs, layer, w_gate, w_up, w_down):
    n_rows, half = xs.shape
    _, n_e, d, de = w_gate.shape

    def w_map(i, te):
        return (layer, jnp.minimum(te[i], n_e - 1), 0, 0)

    rows = pl.BlockSpec((EXPERT_ROW_TILE, half), lambda i, te: (i, 0))
    return pl.pallas_call(
        _experts_kernel,
        grid_spec=pltpu.PrefetchScalarGridSpec(
            num_scalar_prefetch=1,
            grid=(n_rows // EXPERT_ROW_TILE,),
            in_specs=[rows, pl.BlockSpec((1, 1, d, de), w_map), pl.BlockSpec((1, 1, d, de), w_map),
                      pl.BlockSpec((1, 1, de, d), w_map)],
            out_specs=rows,
            scratch_shapes=[pltpu.VMEM((d, de), BF16), pltpu.VMEM((d, de), BF16), pltpu.VMEM((de, d), BF16)]),
        out_shape=jax.ShapeDtypeStruct((n_rows, half), jnp.int32),
        compiler_params=pltpu.CompilerParams(
            dimension_semantics=("arbitrary",), vmem_limit_bytes=VMEM_LIMIT_BYTES),
        name="experts",
    )(tile_expert, xs, w_gate, w_up, w_down)


def _final_kernel(y0_ref, y1_ref, gate_ref, h1_ref, gt_ref, fg_ref, *out_refs):
    o_ref = out_refs[-1]
    h2 = _moe_residual(y0_ref, y1_ref, gate_ref, h1_ref[0], gt_ref[0])
    ms = jnp.mean(h2 * h2, axis=-1, keepdims=True)
    o_ref[0] = h2 * lax.rsqrt(ms + RMS_EPS) * fg_ref[...]


def _moe_out_specs(b, t, d, tile=TOKEN_TILE):
    nt = t // tile
    return [pl.BlockSpec((tile, d // 2), lambda i, j: (i * nt + j, 0)),
            pl.BlockSpec((tile, d // 2), lambda i, j: (b * nt + i * nt + j, 0)),
            pl.BlockSpec((tile, 2), lambda i, j: (i * nt + j, 0))]


def _final(yg, gates, h1, mod, final_g, batch0, n_batch, out_prev):
    b, t, d = h1.shape
    tile = pl.BlockSpec((1, WIDE_TOKEN_TILE, d), lambda i, j: (i, j, 0))
    in_specs = _moe_out_specs(b, t, d, WIDE_TOKEN_TILE) + [tile, _mod_spec3(5, d, batch0),
                                                           pl.BlockSpec((1, d), lambda i, j: (0, 0))]
    args = [yg, yg, gates, h1, mod, final_g]
    aliases = {}
    if out_prev is not None:
        in_specs.append(pl.BlockSpec(memory_space=pl.ANY))
        aliases = {len(args): 0}
        args.append(out_prev)
    return pl.pallas_call(
        _final_kernel,
        grid=(b, t // WIDE_TOKEN_TILE),
        in_specs=in_specs,
        out_specs=pl.BlockSpec((1, WIDE_TOKEN_TILE, d), lambda i, j: (i + batch0, j, 0)),
        out_shape=jax.ShapeDtypeStruct((n_batch, t, d), F32),
        input_output_aliases=aliases,
        compiler_params=pltpu.CompilerParams(
            dimension_semantics=("arbitrary", "arbitrary"), vmem_limit_bytes=VMEM_LIMIT_BYTES),
        name="final",
    )(*args)


def _moe(hm, eidx, gates, layer, w_gate, w_up, w_down):
    n_slots = 2 * hm.shape[0]
    n_rows = n_slots + N_EXPERTS * EXPERT_ROW_TILE
    pos, ends = _positions(eidx)
    pos = pos.reshape(n_slots)
    tile_start = jnp.arange(n_rows // EXPERT_ROW_TILE, dtype=jnp.int32) * EXPERT_ROW_TILE
    tile_expert = jnp.sum(tile_start[:, None] >= ends[None, :, 0], axis=1).astype(jnp.int32)
    xs = _sc_gather(hm, _sc_invert(pos, n_rows))
    ys = _experts(tile_expert, xs, layer, w_gate, w_up, w_down)
    return _sc_gather(ys, pos), gates.T


def kernel(x, c, ctx, c_ctx, ada_w, ada_b, norm_g, rw_mu, rw_w_rkv, rw_w0, rw_w1, rw_w2, rw_a0, rw_a1, rw_a2, rw_g1, rw_g2, rw_k_k, rw_k_a, rw_r_k, rw_gn_w, rw_gn_b, rw_w_o, sc_w_in, sc_conv, sc_w_out, router_w, router_b, moe_w_gate, moe_w_up, moe_w_down, final_g):
    b, t, d = x.shape
    ctx_len = ctx.shape[1]
    depth = ada_w.shape[0]
    assert d == D_MODEL and depth == 2 and t % POSITION_TILE == 0

    mod_rows = 16
    cc = jnp.concatenate([c, c_ctx[None, :], jnp.zeros((mod_rows - b - 1, d), F32)], axis=0)
    mod = _ada(cc, ada_w, ada_b).reshape(depth, mod_rows, 1, 6 * d)

    row = lambda a: a.reshape(1, d)
    router_wt = jnp.pad(router_w, ((0, 0), (0, LANES - N_EXPERTS)))
    router_bc = router_b.reshape(N_EXPERTS, 1)

    w1 = jnp.concatenate([rw_w1[0, 0], rw_w1[0, 1]], axis=1).astype(BF16)
    a1 = jnp.concatenate([rw_a1[0, 0], rw_a1[0, 1]], axis=1).astype(BF16)

    def pad_dirs(w):
        z = jnp.zeros_like(w[0])
        return jnp.stack([jnp.concatenate([w[0], z], axis=0), jnp.concatenate([z, w[1]], axis=0)]).astype(BF16)

    assert b % N_STREAMS == 0
    nb = b // N_STREAMS
    pre_params = (row(norm_g[0, 0]), rw_mu[0], rw_w_rkv[0].astype(BF16), w1, pad_dirs(rw_w2[0]), rw_w0[0], a1,
                  pad_dirs(rw_a2[0]), rw_a0[0], rw_g1[0].astype(BF16), rw_g2[0].astype(BF16), row(rw_k_k[0]),
                  row(rw_k_a[0]), row(rw_r_k[0]))
    w_o, w_in, w_out = rw_w_o[0].astype(BF16), sc_w_in[0].astype(BF16), sc_w_out[0].astype(BF16)
    out = None
    for batch0 in range(0, b, nb):
        (r, v, g, bonus, cs0, cs1, al0, al1, kd0, kd1, b0, b1) = _rwkv_pre(ctx, x, mod[0], *pre_params, batch0, nb)
        yf, yr = _wkv(r, v, cs0, al0, kd0, b0, cs1, al1, kd1, b1, ctx_len)
        h1, hm, eidx, gates = _readout(yf, yr, bonus, g, x, mod[0], ctx_len, row(rw_gn_w[0]), row(rw_gn_b[0]),
                                       w_o, row(norm_g[0, 1]), router_wt, router_bc, batch0, nb)
        yg, gates = _moe(hm, eidx, gates, 0, moe_w_gate, moe_w_up, moe_w_down)
        h1, hm, eidx, gates = _sconv(yg, gates, h1, mod[0], mod[1], row(norm_g[1, 0]), w_in, sc_conv[0], w_out,
                                     row(norm_g[1, 1]), router_wt, router_bc, batch0)
        yg, gates = _moe(hm, eidx, gates, 1, moe_w_gate, moe_w_up, moe_w_down)
        out = _final(yg, gates, h1, mod[1], row(final_g), batch0, b, out)
    return out
```

```python
import functools

import jax
import jax.numpy as jnp
from jax import lax
from jax.experimental import pallas as pl
from jax.experimental.pallas import tpu as pltpu
from jax.experimental.pallas import tpu_sc as plsc

F32 = jnp.float32
BF16 = jnp.bfloat16

D_MODEL = 1024
HEAD_DIM = 64
N_HEADS = D_MODEL // HEAD_DIM
LANES = 128
N_PAIRS = D_MODEL // LANES
CHUNK = 64
N_EXPERTS = 16
EXPERTS_PER_GROUP = 4
N_GROUPS = N_EXPERTS // EXPERTS_PER_GROUP
RMS_EPS = 1e-6
GN_EPS = 64e-5
L2_EPS = 1e-12

TOKEN_TILE = 256
WIDE_TOKEN_TILE = 512
MIX_SUB_TILES = 2
PRE_SUB_TILES = 2
EXPERT_ROW_TILE = 512
POSITION_TILE = 1024
SC_CORES, SC_SUBCORES, SC_LANES = 2, 16, 16
SC_WORKERS = SC_CORES * SC_SUBCORES
N_STREAMS = 2
SC_GATHER_ROWS = 128
PAIRS_PER_STEP = 8
STAGE_LAG = 2
CHUNKS_PER_STEP = 4
ADA_COL_TILE = 1536
VMEM_LIMIT_BYTES = 56 * 1024 * 1024


def _sigmoid(x):
    return 1.0 / (1.0 + jnp.exp(-x))


def _mm(a, b):
    return jnp.dot(a.astype(BF16), b.astype(BF16), preferred_element_type=F32)


def _mm_nt(a, b):
    return lax.dot_general(a.astype(BF16), b.astype(BF16), (((1,), (1,)), ((), ())),
                           preferred_element_type=F32)


def _split2(x):
    hi = x.astype(BF16)
    return hi, (x - hi.astype(F32)).astype(BF16)


def _head_sum(x):
    rows = x.shape[0]
    left = lax.broadcasted_iota(jnp.int32, (rows, LANES), 1) < HEAD_DIM
    outs = []
    for j in range(N_PAIRS):
        xb = x[:, LANES * j:LANES * (j + 1)]
        sa = jnp.sum(jnp.where(left, xb, 0.0), axis=-1, keepdims=True)
        sb = jnp.sum(jnp.where(left, 0.0, xb), axis=-1, keepdims=True)
        outs.append(jnp.where(left, sa, sb))
    return jnp.concatenate(outs, axis=-1)


def _rms_modulate(x, g, shift, scale):
    ms = jnp.mean(x * x, axis=-1, keepdims=True)
    xn = x * lax.rsqrt(ms + RMS_EPS) * g
    return xn * (1.0 + scale) + shift


def _row_neighbours(x, row_len):
    rows = x.shape[0]
    pos = lax.broadcasted_iota(jnp.int32, x.shape, 0) & (row_len - 1)
    prev = jnp.where(pos == 0, 0.0, pltpu.roll(x, 1, 0))
    nxt = jnp.where(pos == row_len - 1, 0.0, pltpu.roll(x, rows - 1, 0))
    return prev, nxt


def _ada_kernel(c_ref, w_ref, b_ref, o_ref):
    c = c_ref[...]
    s = c * _sigmoid(c)
    o_ref[0] = _mm(s, w_ref[0]) + b_ref[0]


def _ada(cc, ada_w, ada_b):
    depth, d, n = ada_w.shape
    rows = cc.shape[0]
    return pl.pallas_call(
        _ada_kernel,
        grid=(depth, n // ADA_COL_TILE),
        in_specs=[
            pl.BlockSpec((rows, d), lambda l, j: (0, 0)),
            pl.BlockSpec((1, d, ADA_COL_TILE), lambda l, j: (l, 0, j)),
            pl.BlockSpec((1, 1, ADA_COL_TILE), lambda l, j: (l, 0, j)),
        ],
        out_specs=pl.BlockSpec((1, rows, ADA_COL_TILE), lambda l, j: (l, 0, j)),
        out_shape=jax.ShapeDtypeStruct((depth, rows, n), F32),
        compiler_params=pltpu.CompilerParams(
            dimension_semantics=("arbitrary", "arbitrary"), vmem_limit_bytes=VMEM_LIMIT_BYTES),
        name="ada",
    )(cc, ada_w, ada_b.reshape(depth, 1, n))


def _rwkv_pre_kernel(ctx_len, ctx_ref, x_ref, sh_ref, sc_ref, ng_ref, mu_ref, wrkv_ref, w1_ref, w2_ref, w0_ref,
                     a1_ref, a2_ref, a0_ref, g1_ref, g2_ref, kk_ref, ka_ref, rk_ref,
                     r_out, v_out, g_out, bon_out, k_out, kk_out, cs0_out, cs1_out, a0_out, a1_out):
    j = pl.program_id(1)
    row_len = jnp.where(j == 0, ctx_len, CHUNK)
    xin = jnp.where(j == 0, ctx_ref[0], x_ref[0])
    xn = _rms_modulate(xin, ng_ref[...], sh_ref[0], sc_ref[0])
    prev, nxt = _row_neighbours(xn, row_len)
    xx = 0.5 * (prev + nxt) - xn

    sub = xn.shape[0] // PRE_SUB_TILES
    projected = []
    for s in range(PRE_SUB_TILES):
        rs = slice(s * sub, (s + 1) * sub)
        xn_s, xx_s = xn[rs], xx[rs]

        def mix(i, xn_s=xn_s, xx_s=xx_s):
            return xn_s + xx_s * mu_ref[i:i + 1, :]

        r = _mm(mix(0), wrkv_ref[0])
        k = _mm(mix(1), wrkv_ref[1])
        v = _mm(mix(2), wrkv_ref[2])
        g = _mm(_sigmoid(_mm(mix(5), g1_ref[...])), g2_ref[...])
        wl = jnp.tanh(_mm(mix(3), w1_ref[...]))
        al = _mm(mix(4), a1_ref[...])
        projected.append((rs, r, k, v, g, [_mm(wl, w2_ref[p]) for p in range(2)],
                          [_mm(al, a2_ref[p]) for p in range(2)]))

    ti = lax.broadcasted_iota(jnp.int32, (sub, sub), 0)
    tj = lax.broadcasted_iota(jnp.int32, (sub, sub), 1)
    same_chunk = (ti ^ tj) < CHUNK
    dir_outs = ((cs0_out, a0_out), (cs1_out, a1_out))
    for rs, r, k, v, g, zs, a_logits in projected:
        r_out[0, rs, :] = r.astype(r_out.dtype)
        v_out[0, rs, :] = v.astype(v_out.dtype)
        g_out[0, rs, :] = g.astype(g_out.dtype)
        k_out[0, rs, :] = k.astype(k_out.dtype)
        kk = k * kk_ref[...]
        kk_out[0, rs, :] = (kk * lax.rsqrt(jnp.maximum(_head_sum(kk * kk), L2_EPS * L2_EPS))).astype(kk_out.dtype)
        a_sum = None
        for p, (cs_out, a_out) in enumerate(dir_outs):
            z = w0_ref[p:p + 1, :] + zs[p]
            lw = -jnp.exp(-0.5) * _sigmoid(z)
            tri = jnp.where(same_chunk & ((tj >= ti) if p else (tj <= ti)), 1.0, 0.0).astype(BF16)
            hi, lo = _split2(lw)
            cs_out[0, rs, :] = _dot(tri, hi) + _dot(tri, lo)
            a = _sigmoid(a0_ref[p:p + 1, :] + a_logits[p])
            a_out[0, rs, :] = a.astype(a_out.dtype)
            a_sum = a if a_sum is None else a_sum + a
        k_dirs = k * (2.0 + (a_sum - 2.0) * ka_ref[...])
        bon_out[0, rs, :] = (_head_sum(r * rk_ref[...] * k_dirs) * v).astype(bon_out.dtype)


def _rwkv_pre(ctx, x, mod, norm_g, mu, w_rkv, w1, w2, w0, a1, a2, a0, g1, g2, k_k, k_a, r_k, batch0, b):
    n_batch, t, d = x.shape
    ctx_len = ctx.shape[1]
    tt = ctx_len + t
    assert ctx_len == TOKEN_TILE and t % TOKEN_TILE == 0

    def mod_spec(part):
        return pl.BlockSpec((1, 1, d), lambda i, j: (jnp.where(j == 0, n_batch, i + batch0), 0, part))

    def full(a):
        nd = a.ndim
        return pl.BlockSpec(a.shape, lambda i, j: (0,) * nd)

    tile = pl.BlockSpec((1, TOKEN_TILE, d), lambda i, j: (i, j, 0))
    latent = pl.BlockSpec((1, TOKEN_TILE, d), lambda i, j: (i, jnp.maximum(j - 1, 0), 0))
    params = (norm_g, mu, w_rkv, w1, w2, w0, a1, a2, a0, g1, g2, k_k, k_a, r_k)
    out_dtypes = (BF16,) * 6 + (F32, F32) + (BF16,) * 2
    out_specs = [tile, tile, latent, latent] + [tile] * 6
    return pl.pallas_call(
        functools.partial(_rwkv_pre_kernel, ctx_len),
        grid=(b, tt // TOKEN_TILE),
        in_specs=[pl.BlockSpec((1, TOKEN_TILE, d), lambda i, j: (i + batch0, 0, 0)),
                  pl.BlockSpec((1, TOKEN_TILE, d), lambda i, j: (i + batch0, jnp.maximum(j - 1, 0), 0)),
                  mod_spec(0), mod_spec(1)] + [full(a) for a in params],
        out_specs=out_specs,
        out_shape=[jax.ShapeDtypeStruct((b, t if spec is latent else tt, d), dt)
                   for spec, dt in zip(out_specs, out_dtypes)],
        compiler_params=pltpu.CompilerParams(
            dimension_semantics=("arbitrary", "arbitrary"), vmem_limit_bytes=VMEM_LIMIT_BYTES),
        name="rwkv_pre",
    )(ctx, x, mod, mod, *params)


def _expand(x, left):
    return jnp.concatenate([jnp.where(left, x, 0.0), jnp.where(left, 0.0, x)], axis=0)


def _dot(a, b):
    return jnp.dot(a, b, preferred_element_type=F32)


def _chunk_steps(chains, group_size):
    assert CHUNK == HEAD_DIM
    c = CHUNK
    lane = lax.broadcasted_iota(jnp.int32, (c, LANES), 1)
    left = lane < HEAD_DIM
    tt = lax.broadcasted_iota(jnp.int32, (c, LANES), 0)
    jj = lane & (c - 1)
    diag = jj == tt
    tri = {False: (jj < tt, jj <= tt), True: (jj > tt, jj >= tt)}

    def bd(x):
        return _expand(x, left).astype(BF16)

    def fold_t(x):
        xt = _expand(x, left).T
        return xt[:c] + xt[c:]

    def rows(*xs):
        return jnp.concatenate(xs, axis=0).astype(BF16)

    nt_dims = (((1,), (1,)), ((), ()))

    def s_prep(q):
        cs, r, v, k, kk, a, ka, h, reverse = q.pop("chain")
        tot = cs[0:1, :] if reverse else cs[c - 1:c, :]
        e_pos, e_neg, e_rem = jnp.exp(cs), jnp.exp(-cs), jnp.exp(tot - cs)
        first = tt == (c - 1 if reverse else 0)
        cs_prev = jnp.where(first, 0.0, pltpu.roll(cs, c - 1 if reverse else 1, 0))
        at, rt = -kk * jnp.exp(cs_prev), r * e_pos
        kd, be = k * (1.0 + (a - 1.0) * ka), kk * a
        q.update(a_e=bd(at), rt=rt, v_e=bd(v), g_tot=jnp.exp(tot), h=h, tri=tri[reverse])
        q["hat_t"] = jnp.concatenate([fold_t(be * e_rem), fold_t(kd * e_rem)], axis=1)
        q["sc"] = lax.dot_general(rows(at, rt), jnp.concatenate([bd(be * e_neg), bd(kd * e_neg)], axis=0),
                                  nt_dims, preferred_element_type=F32)

    def s_mask(q):
        (strict, incl), sc = q["tri"], q["sc"]
        q["n"] = jnp.where(strict, sc[:c, :LANES], 0.0)
        a_ak = jnp.where(strict, sc[:c, LANES:], 0.0)
        q["a_rb"] = jnp.where(incl, sc[c:, :LANES], 0.0)
        a_rk = jnp.where(incl, sc[c:, LANES:], 0.0)
        q["bh_t"] = q["hat_t"][:, :LANES]
        q["vv"] = _dot(rows(a_ak, q["hat_t"][:, LANES:], a_rk), q["v_e"])

    def s_square(q):
        q["t"] = jnp.where(diag, 1.0, 0.0) + q["n"]
        q["p"] = _dot(q["n"].astype(BF16), bd(q["n"]))

    def s_level(q):
        x = _dot(rows(q["p"], q["t"]), bd(q["p"]))
        q["p"] = x[:c]
        q["t"] = q["t"] + x[c:]

    def s_last_level(q):
        q["t"] = q["t"] + _dot(q["t"].astype(BF16), bd(q["p"]))

    def s_solve(q):
        xu = _dot(q["t"].astype(BF16), jnp.concatenate([q["a_e"], bd(q["vv"][:c])], axis=1))
        q["au_e"] = jnp.concatenate([bd(xu[:, :LANES]), bd(xu[:, LANES:])], axis=1)

    def s_affine(q):
        z = _dot(rows(q["bh_t"], q["a_rb"]), q["au_e"])
        q["m"] = z[:c, :LANES] + jnp.where(diag, q["g_tot"], 0.0)
        q["g"] = z[:c, LANES:] + q["vv"][c:2 * c]
        q["r_hat"] = q["rt"] + z[c:, :LANES]
        q["yi"] = z[c:, LANES:] + q["vv"][2 * c:]

    n_levels = (c // 4).bit_length() - 1
    stages = [s_prep, s_mask, s_square] + [s_level] * n_levels + [s_last_level, s_solve, s_affine]
    qs = [dict(chain=ch) for ch in chains]
    groups = [qs[i:i + group_size] for i in range(0, len(qs), group_size)]
    for tau in range(len(stages) + (len(groups) - 1) * STAGE_LAG):
        for gi, group in enumerate(groups):
            si = tau - gi * STAGE_LAG
            if 0 <= si < len(stages):
                for q in group:
                    stages[si](q)
    outs = []
    for q in qs:
        h = outs[q["h"]][1] if isinstance(q["h"], int) else q["h"]
        o = _dot(rows(q["r_hat"], q["m"]), bd(h))
        outs.append((o[:c] + q["yi"], o[c:] + q["g"]))
    return outs


def _wkv_kernel(ka_ref, rf_ref, vf_ref, kf_ref, kkf_ref, csf_ref, af_ref,
                rr_ref, vr_ref, kr_ref, kkr_ref, csr_ref, ar_ref,
                yf_ref, yr_ref, hf_ref, hr_ref):
    @pl.when(pl.program_id(2) == 0)
    def _():
        hf_ref[...] = jnp.zeros_like(hf_ref)
        hr_ref[...] = jnp.zeros_like(hr_ref)

    dirs = ((rf_ref, vf_ref, kf_ref, kkf_ref, csf_ref, af_ref, yf_ref, hf_ref),
            (rr_ref, vr_ref, kr_ref, kkr_ref, csr_ref, ar_ref, yr_ref, hr_ref))
    chains, dests = [], []
    per_chunk = 2 * PAIRS_PER_STEP
    for u in range(CHUNKS_PER_STEP):
        for p in range(PAIRS_PER_STEP):
            sl = slice(p * LANES, (p + 1) * LANES)
            for reverse, (r_ref, v_ref, k_ref, kk_ref, cs_ref, a_ref, y_ref, h_ref) in enumerate(dirs):
                at = CHUNKS_PER_STEP - 1 - u if reverse else u
                tm = slice(at * CHUNK, (at + 1) * CHUNK)
                h = h_ref[p] if u == 0 else len(chains) - per_chunk
                chains.append((cs_ref[0, tm, sl], r_ref[0, tm, sl].astype(F32), v_ref[0, tm, sl].astype(F32),
                               k_ref[0, tm, sl].astype(F32), kk_ref[0, tm, sl].astype(F32),
                               a_ref[0, tm, sl].astype(F32), ka_ref[:, sl], h, bool(reverse)))
                dests.append((y_ref, tm, sl, h_ref if u == CHUNKS_PER_STEP - 1 else None, p))
    for (y, h_new), (y_ref, tm, sl, h_ref, p) in zip(_chunk_steps(chains, per_chunk), dests):
        y_ref[0, tm, sl] = y
        if h_ref is not None:
            h_ref[p] = h_new


def _wkv(k_a, r, v, k, kk, cs0, a0, cs1, a1, ctx_len):
    b, tt, d = r.shape
    step_rows = CHUNKS_PER_STEP * CHUNK
    n_steps, n_ctx = tt // step_rows, ctx_len // step_rows
    assert n_steps * step_rows == tt and n_ctx * step_rows == ctx_len
    width = PAIRS_PER_STEP * LANES

    def fwd_map(i, j, s):
        return (i, s, j)

    def rev_map(i, j, s):
        return (i, jnp.where(s < n_ctx, n_ctx - 1 - s, n_steps - 1 - (s - n_ctx)), j)

    n_lat = n_steps - n_ctx

    def fwd_out(i, j, s):
        return (i, jnp.maximum(s - n_ctx, 0), j)

    def rev_out(i, j, s):
        return (i, jnp.where(s < n_ctx, n_lat - 1, n_lat - 1 - (s - n_ctx)), j)

    fwd = pl.BlockSpec((1, step_rows, width), fwd_map)
    rev = pl.BlockSpec((1, step_rows, width), rev_map)
    return pl.pallas_call(
        _wkv_kernel,
        grid=(b, d // width, n_steps),
        in_specs=[pl.BlockSpec((1, width), lambda i, j, s: (0, j))] + [fwd] * 6 + [rev] * 6,
        out_specs=[pl.BlockSpec((1, step_rows, width), fwd_out), pl.BlockSpec((1, step_rows, width), rev_out)],
        out_shape=[jax.ShapeDtypeStruct((b, tt - ctx_len, d), F32)] * 2,
        scratch_shapes=[pltpu.VMEM((PAIRS_PER_STEP, HEAD_DIM, LANES), F32)] * 2,
        compiler_params=pltpu.CompilerParams(
            dimension_semantics=("arbitrary", "arbitrary", "arbitrary"), vmem_limit_bytes=VMEM_LIMIT_BYTES),
        name="wkv",
    )(k_a, r, v, k, kk, cs0, a0, r, v, k, kk, cs1, a1)


def _pack_bf16_pairs(x):
    half = x.shape[1] // 2
    lo = pltpu.bitcast(x[:, :half].astype(BF16).astype(F32), jnp.int32)
    hi = pltpu.bitcast(x[:, half:].astype(BF16).astype(F32), jnp.int32)
    return lax.shift_right_logical(lo, jnp.int32(16)) | (hi & jnp.int32(-65536))


def _unpack_bf16_pairs(w):
    lo = pltpu.bitcast(lax.shift_left(w, jnp.int32(16)), F32)
    hi = pltpu.bitcast(w & jnp.int32(-65536), F32)
    return jnp.concatenate([lo, hi], axis=1)


def _route(h1, ng, shift, scale, rwt_ref, rb_ref, hm_out, eidx_out, gate_out, rs=slice(None)):
    hm = _rms_modulate(h1, ng, shift, scale)
    hm_out[rs, :] = _pack_bf16_pairs(hm)
    w_hi, w_mid = _split2(rwt_ref[...])
    h_hi, h_mid = _split2(hm)
    logits_t = _dot(h_hi, w_hi) + _dot(h_mid, w_hi) + _dot(h_hi, w_mid)
    logits = logits_t.T[:N_EXPERTS]
    s = _sigmoid(logits)
    sel = s + rb_ref[...]
    assert EXPERTS_PER_GROUP == 4 and N_GROUPS == 4
    cands = []
    for g in range(N_GROUPS):
        m = [sel[e:e + 1, :] for e in range(g * EXPERTS_PER_GROUP, (g + 1) * EXPERTS_PER_GROUP)]
        sg = [s[e:e + 1, :] for e in range(g * EXPERTS_PER_GROUP, (g + 1) * EXPERTS_PER_GROUP)]
        pairs = [m[i] + m[k] for i in range(4) for k in range(i + 1, 4)]
        score = jnp.maximum(jnp.maximum(jnp.maximum(pairs[0], pairs[1]), jnp.maximum(pairs[2], pairs[3])),
                            jnp.maximum(pairs[4], pairs[5]))
        chosen = []
        for i in range(4):
            ahead = [((m[k] >= m[i]) if k < i else (m[k] > m[i])).astype(jnp.int32) for k in range(4) if k != i]
            chosen.append(ahead[0] + ahead[1] + ahead[2] < 2)
        base = g * EXPERTS_PER_GROUP
        lo_idx = jnp.where(chosen[0], base, jnp.where(chosen[1], base + 1, base + 2))
        lo_gate = jnp.where(chosen[0], sg[0], jnp.where(chosen[1], sg[1], sg[2]))
        hi_idx = jnp.where(chosen[3], base + 3, jnp.where(chosen[2], base + 2, base + 1))
        hi_gate = jnp.where(chosen[3], sg[3], jnp.where(chosen[2], sg[2], sg[1]))
        cands.append((score, lo_idx, hi_idx, lo_gate, hi_gate))

    def better(x, y):
        win = y[0] > x[0]
        return tuple(jnp.where(win, yv, xv) for xv, yv in zip(x, y))

    _, e_lo, e_hi, g_lo, g_hi = better(better(cands[0], cands[1]), better(cands[2], cands[3]))
    eidx_out[:, rs] = jnp.concatenate([e_lo, e_hi], axis=0)
    gate_out[:, rs] = jnp.concatenate([g_lo, g_hi], axis=0) / (g_lo + g_hi)


def _readout_kernel(yf_ref, yr_ref, bon_ref, g_ref, h_ref, gt_ref, gnw_ref, gnb_ref, wo_ref,
                    ng_ref, sh_ref, sc_ref, rwt_ref, rb_ref, h1_out, hm_out, eidx_out, gate_out):
    sub = h_ref.shape[1] // MIX_SUB_TILES
    tiles = [slice(s * sub, (s + 1) * sub) for s in range(MIX_SUB_TILES)]
    gated, h1s = [], []
    for rs in tiles:
        y = yf_ref[0, rs, :] + yr_ref[0, rs, :] + bon_ref[0, rs, :].astype(F32)
        mean = _head_sum(y) * (1.0 / HEAD_DIM)
        yc = y - mean
        var = _head_sum(yc * yc) * (1.0 / HEAD_DIM)
        yn = yc * lax.rsqrt(var + GN_EPS) * gnw_ref[...] + gnb_ref[...]
        gated.append((yn * g_ref[0, rs, :].astype(F32)).astype(BF16))
    for rs, lhs in zip(tiles, gated):
        h1 = h_ref[0, rs, :] + gt_ref[0] * _dot(lhs, wo_ref[...])
        h1_out[0, rs, :] = h1
        h1s.append(h1)
    for rs, h1 in zip(tiles, h1s):
        _route(h1, ng_ref[...], sh_ref[0], sc_ref[0], rwt_ref, rb_ref, hm_out, eidx_out, gate_out, rs)


def _mod_spec3(part, d, batch0=0):
    return pl.BlockSpec((1, 1, d), lambda i, j: (i + batch0, 0, part))


def _route_out_specs(b, t, d, tile=TOKEN_TILE):
    nt = t // tile
    specs = [pl.BlockSpec((1, tile, d), lambda i, j: (i, j, 0)),
             pl.BlockSpec((tile, d // 2), lambda i, j: (i * nt + j, 0)),
             pl.BlockSpec((2, tile), lambda i, j: (0, i * nt + j)),
             pl.BlockSpec((2, tile), lambda i, j: (0, i * nt + j))]
    shapes = [jax.ShapeDtypeStruct((b, t, d), F32), jax.ShapeDtypeStruct((b * t, d // 2), jnp.int32),
              jax.ShapeDtypeStruct((2, b * t), jnp.int32), jax.ShapeDtypeStruct((2, b * t), F32)]
    return specs, shapes


def _full2(a):
    nd = a.ndim
    return pl.BlockSpec(a.shape, lambda i, j: (0,) * nd)


def _readout(yf, yr, bonus, g, h, mod, gn_w, gn_b, w_o, norm_g, router_wt, router_b, batch0, b):
    _, t, d = h.shape
    nt = t // WIDE_TOKEN_TILE
    local = pl.BlockSpec((1, WIDE_TOKEN_TILE, d), lambda i, j: (i, j, 0))
    tile = pl.BlockSpec((1, WIDE_TOKEN_TILE, d), lambda i, j: (i + batch0, j, 0))
    params_a = (gn_w, gn_b, w_o, norm_g)
    params_b = (router_wt, router_b)
    out_specs, out_shape = _route_out_specs(b, t, d, WIDE_TOKEN_TILE)
    return pl.pallas_call(
        _readout_kernel,
        grid=(b, nt),
        in_specs=[local] * 4 + [tile, _mod_spec3(2, d, batch0)] + [_full2(a) for a in params_a]
        + [_mod_spec3(3, d, batch0), _mod_spec3(4, d, batch0)] + [_full2(a) for a in params_b],
        out_specs=out_specs,
        out_shape=out_shape,
        compiler_params=pltpu.CompilerParams(
            dimension_semantics=("arbitrary", "arbitrary"), vmem_limit_bytes=VMEM_LIMIT_BYTES),
        name="readout",
    )(yf, yr, bonus, g, h, mod, *params_a, mod, mod, *params_b)


def _moe_residual(y0_ref, y1_ref, gate_ref, h1, gt, rs=slice(None)):
    gates = gate_ref[rs, :]
    y = gates[:, 0:1] * _unpack_bf16_pairs(y0_ref[rs, :]) + gates[:, 1:2] * _unpack_bf16_pairs(y1_ref[rs, :])
    return h1 + gt * y


def _sconv_kernel(y0_ref, y1_ref, gate_ref, hp_ref, gtp_ref, sh1_ref, sc1_ref, gt_ref, ng1_ref, win_ref, cw_ref,
                  wout_ref, ng2_ref, sh2_ref, sc2_ref, rwt_ref, rb_ref, h1_out, hm_out, eidx_out, gate_out):
    d = hp_ref.shape[2]
    sub = hp_ref.shape[1] // MIX_SUB_TILES
    tiles = [slice(s * sub, (s + 1) * sub) for s in range(MIX_SUB_TILES)]
    hs, xns, gated, h1s = [], [], [], []
    for rs in tiles:
        h = _moe_residual(y0_ref, y1_ref, gate_ref, hp_ref[0, rs, :], gtp_ref[0], rs)
        hs.append(h)
        xns.append(_rms_modulate(h, ng1_ref[...], sh1_ref[0], sc1_ref[0]).astype(BF16))
    for xn in xns:
        bg = _dot(xn, win_ref[:, 0:d])
        u = _dot(xn, win_ref[:, d:2 * d]) * _dot(xn, win_ref[:, 2 * d:3 * d])
        prev, nxt = _row_neighbours(u, CHUNK)
        conv = cw_ref[0:1, :] * prev + cw_ref[1:2, :] * u + cw_ref[2:3, :] * nxt
        gated.append((bg * conv).astype(BF16))
    for rs, h, lhs in zip(tiles, hs, gated):
        h1 = h + gt_ref[0] * _dot(lhs, wout_ref[...])
        h1_out[0, rs, :] = h1
        h1s.append(h1)
    for rs, h1 in zip(tiles, h1s):
        _route(h1, ng2_ref[...], sh2_ref[0], sc2_ref[0], rwt_ref, rb_ref, hm_out, eidx_out, gate_out, rs)


def _sconv(yg, gates, h_prev, mod_prev, mod, norm_g1, w_in, conv_w, w_out, norm_g2, router_wt, router_b, batch0):
    b, t, d = h_prev.shape
    nt = t // WIDE_TOKEN_TILE
    tile = pl.BlockSpec((1, WIDE_TOKEN_TILE, d), lambda i, j: (i, j, 0))
    params_a = (norm_g1, w_in, conv_w, w_out, norm_g2)
    params_b = (router_wt, router_b)
    out_specs, out_shape = _route_out_specs(b, t, d, WIDE_TOKEN_TILE)
    return pl.pallas_call(
        _sconv_kernel,
        grid=(b, nt),
        in_specs=_moe_out_specs(b, t, d, WIDE_TOKEN_TILE) + [tile, _mod_spec3(5, d, batch0)]
        + [_mod_spec3(0, d, batch0), _mod_spec3(1, d, batch0), _mod_spec3(2, d, batch0)]
        + [_full2(a) for a in params_a]
        + [_mod_spec3(3, d, batch0), _mod_spec3(4, d, batch0)] + [_full2(a) for a in params_b],
        out_specs=out_specs,
        out_shape=out_shape,
        compiler_params=pltpu.CompilerParams(
            dimension_semantics=("arbitrary", "arbitrary"), vmem_limit_bytes=VMEM_LIMIT_BYTES),
        name="sconv",
    )(yg, yg, gates, h_prev, mod_prev, mod, mod, mod, *params_a, mod, mod, *params_b)


def _positions_kernel(eidx_ref, pos_ref, ends_ref, run_ref):
    phase, i = pl.program_id(0), pl.program_id(1)
    tp = eidx_ref.shape[1]
    expert = lax.broadcasted_iota(jnp.int32, (N_EXPERTS, tp), 0)
    onehot = [jnp.where(eidx_ref[k:k + 1, :] == expert, 1.0, 0.0) for k in range(2)]

    @pl.when((phase == 0) & (i == 0))
    def _():
        run_ref[...] = jnp.zeros_like(run_ref)

    @pl.when(phase == 0)
    def _():
        run_ref[...] += jnp.sum(onehot[0] + onehot[1], axis=1, keepdims=True)

    @pl.when((phase == 1) & (i == 0))
    def _():
        padded = jnp.floor((run_ref[...] + (EXPERT_ROW_TILE - 1)) * (1.0 / EXPERT_ROW_TILE)) * EXPERT_ROW_TILE
        ei = lax.broadcasted_iota(jnp.int32, (N_EXPERTS, N_EXPERTS), 0)
        ej = lax.broadcasted_iota(jnp.int32, (N_EXPERTS, N_EXPERTS), 1)
        below = jnp.where(ej < ei, 1.0, 0.0)
        starts = jnp.dot(below, jnp.broadcast_to(padded, (N_EXPERTS, LANES)), precision=lax.Precision.HIGHEST,
                         preferred_element_type=F32)
        ends_ref[...] = (starts + padded).astype(jnp.int32)
        run_ref[...] = starts[:, 0:1]

    @pl.when(phase == 1)
    def _():
        ti = lax.broadcasted_iota(jnp.int32, (tp, tp), 0)
        tj = lax.broadcasted_iota(jnp.int32, (tp, tp), 1)
        before = jnp.where(ti < tj, 1.0, 0.0).astype(BF16)
        run = run_ref[...]
        for k in range(2):
            prefix = jnp.dot(onehot[k].astype(BF16), before, preferred_element_type=F32)
            pos_ref[k:k + 1, :] = jnp.sum(onehot[k] * (prefix + run), axis=0, keepdims=True).astype(jnp.int32)
            run = run + jnp.sum(onehot[k], axis=1, keepdims=True)
        run_ref[...] = run


def _positions(eidx):
    n = eidx.shape[1]
    nt = n // POSITION_TILE
    return pl.pallas_call(
        _positions_kernel,
        grid=(2, nt),
        in_specs=[pl.BlockSpec((2, POSITION_TILE), lambda p, i: (0, i))],
        out_specs=[pl.BlockSpec((2, POSITION_TILE), lambda p, i: (0, i * p)),
                   pl.BlockSpec((N_EXPERTS, LANES), lambda p, i: (0, 0))],
        out_shape=[jax.ShapeDtypeStruct((2, n), jnp.int32), jax.ShapeDtypeStruct((N_EXPERTS, LANES), jnp.int32)],
        scratch_shapes=[pltpu.VMEM((N_EXPERTS, 1), F32)],
        compiler_params=pltpu.CompilerParams(
            dimension_semantics=("arbitrary", "arbitrary"), vmem_limit_bytes=VMEM_LIMIT_BYTES),
        name="positions",
    )(eidx)


def _sc_mesh():
    return plsc.VectorSubcoreMesh(core_axis_name="c", subcore_axis_name="s")


def _sc_worker(n_workers_per_core=SC_SUBCORES):
    return lax.axis_index("c") * n_workers_per_core + lax.axis_index("s")


def _sc_invert(pos, n_rows):
    n_slots = pos.shape[0]
    n_tokens = n_slots // 2
    assert n_tokens & (n_tokens - 1) == 0
    per_worker = n_rows // SC_WORKERS
    assert per_worker * SC_WORKERS == n_rows and per_worker % SC_LANES == 0 and n_slots % SC_LANES == 0

    def body(pos_hbm, inv_hbm, pos_v, inv_v):
        base = _sc_worker() * per_worker
        pltpu.sync_copy(pos_hbm, pos_v)

        @pl.loop(0, per_worker, step=SC_LANES)
        def _(j):
            inv_v[pl.ds(j, SC_LANES)] = (base + j + lax.iota(jnp.int32, SC_LANES)) & (n_tokens - 1)

        @pl.loop(0, n_slots, step=SC_LANES)
        def _(s):
            local = pos_v[pl.ds(s, SC_LANES)] - base
            mine = (local >= 0) & (local < per_worker)
            token = (s + lax.iota(jnp.int32, SC_LANES)) & (n_tokens - 1)
            plsc.store_scatter(inv_v, [jnp.where(mine, local, 0)], token, mask=mine)

        pltpu.sync_copy(inv_v, inv_hbm.at[pl.ds(base, per_worker)])

    return pl.kernel(
        body, out_type=jax.ShapeDtypeStruct((n_rows,), jnp.int32), mesh=_sc_mesh(),
        scratch_types=[pltpu.VMEM((n_slots,), jnp.int32), pltpu.VMEM((per_worker,), jnp.int32)],
        compiler_params=pltpu.CompilerParams(needs_layout_passes=False),
        name="sc_invert",
    )(pos)


def _sc_gather(table, idx):
    n_rows, width = idx.shape[0], table.shape[1]
    per_worker = n_rows // SC_WORKERS
    assert per_worker * SC_WORKERS == n_rows and per_worker % SC_GATHER_ROWS == 0

    def body(table_hbm, idx_hbm, out_hbm, idx_v, rows_v, sem):
        base = _sc_worker() * per_worker

        @pl.loop(0, per_worker, step=SC_GATHER_ROWS)
        def _(j):
            pltpu.sync_copy(idx_hbm.at[pl.ds(base + j, SC_GATHER_ROWS)], idx_v)
            pltpu.async_copy(table_hbm.at[idx_v], rows_v, sem).wait()
            pltpu.sync_copy(rows_v, out_hbm.at[pl.ds(base + j, SC_GATHER_ROWS)])

    return pl.kernel(
        body, out_type=jax.ShapeDtypeStruct((n_rows, width), table.dtype), mesh=_sc_mesh(),
        scratch_types=[pltpu.VMEM((SC_GATHER_ROWS,), jnp.int32), pltpu.VMEM((SC_GATHER_ROWS, width), table.dtype),
                       pltpu.SemaphoreType.DMA],
        name="sc_gather",
    )(table, idx)


def _experts_kernel(te_ref, xs_ref, wg_ref, wu_ref, wd_ref, ys_ref, wg_bf, wu_bf, wd_bf):
    i = pl.program_id(0)
    expert = te_ref[i]

    @pl.when((i == 0) | (expert != te_ref[jnp.maximum(i - 1, 0)]))
    def _():
        wg_bf[...] = wg_ref[0, 0].astype(BF16)
        wu_bf[...] = wu_ref[0, 0].astype(BF16)
        wd_bf[...] = wd_ref[0, 0].astype(BF16)

    @pl.when(expert < N_EXPERTS)
    def _():
        sub = xs_ref.shape[0] // MIX_SUB_TILES
        tiles = [slice(s * sub, (s + 1) * sub) for s in range(MIX_SUB_TILES)]
        xs = [_unpack_bf16_pairs(xs_ref[rs, :]).astype(BF16) for rs in tiles]
        hes = []
        for x in xs:
            gate, up = _dot(x, wg_bf[...]), _dot(x, wu_bf[...])
            hes.append((gate * _sigmoid(gate) * up).astype(BF16))
        for rs, he in zip(tiles, hes):
            ys_ref[rs, :] = _pack_bf16_pairs(_dot(he, wd_bf[...]))


def _experts(tile_expert, xs, layer, w_gate, w_up, w_down):
    n_rows, half = xs.shape
    _, n_e, d, de = w_gate.shape

    def w_map(i, te):
        return (layer, jnp.minimum(te[i], n_e - 1), 0, 0)

    rows = pl.BlockSpec((EXPERT_ROW_TILE, half), lambda i, te: (i, 0))
    return pl.pallas_call(
        _experts_kernel,
        grid_spec=pltpu.PrefetchScalarGridSpec(
            num_scalar_prefetch=1,
            grid=(n_rows // EXPERT_ROW_TILE,),
            in_specs=[rows, pl.BlockSpec((1, 1, d, de), w_map), pl.BlockSpec((1, 1, d, de), w_map),
                      pl.BlockSpec((1, 1, de, d), w_map)],
            out_specs=rows,
            scratch_shapes=[pltpu.VMEM((d, de), BF16), pltpu.VMEM((d, de), BF16), pltpu.VMEM((de, d), BF16)]),
        out_shape=jax.ShapeDtypeStruct((n_rows, half), jnp.int32),
        compiler_params=pltpu.CompilerParams(
            dimension_semantics=("arbitrary",), vmem_limit_bytes=VMEM_LIMIT_BYTES),
        name="experts",
    )(tile_expert, xs, w_gate, w_up, w_down)


def _final_kernel(y0_ref, y1_ref, gate_ref, h1_ref, gt_ref, fg_ref, *out_refs):
    o_ref = out_refs[-1]
    h2 = _moe_residual(y0_ref, y1_ref, gate_ref, h1_ref[0], gt_ref[0])
    ms = jnp.mean(h2 * h2, axis=-1, keepdims=True)
    o_ref[0] = h2 * lax.rsqrt(ms + RMS_EPS) * fg_ref[...]


def _moe_out_specs(b, t, d, tile=TOKEN_TILE):
    nt = t // tile
    return [pl.BlockSpec((tile, d // 2), lambda i, j: (i * nt + j, 0)),
            pl.BlockSpec((tile, d // 2), lambda i, j: (b * nt + i * nt + j, 0)),
            pl.BlockSpec((tile, 2), lambda i, j: (i * nt + j, 0))]


def _final(yg, gates, h1, mod, final_g, batch0, n_batch, out_prev):
    b, t, d = h1.shape
    tile = pl.BlockSpec((1, WIDE_TOKEN_TILE, d), lambda i, j: (i, j, 0))
    in_specs = _moe_out_specs(b, t, d, WIDE_TOKEN_TILE) + [tile, _mod_spec3(5, d, batch0),
                                                           pl.BlockSpec((1, d), lambda i, j: (0, 0))]
    args = [yg, yg, gates, h1, mod, final_g]
    aliases = {}
    if out_prev is not None:
        in_specs.append(pl.BlockSpec(memory_space=pl.ANY))
        aliases = {len(args): 0}
        args.append(out_prev)
    return pl.pallas_call(
        _final_kernel,
        grid=(b, t // WIDE_TOKEN_TILE),
        in_specs=in_specs,
        out_specs=pl.BlockSpec((1, WIDE_TOKEN_TILE, d), lambda i, j: (i + batch0, j, 0)),
        out_shape=jax.ShapeDtypeStruct((n_batch, t, d), F32),
        input_output_aliases=aliases,
        compiler_params=pltpu.CompilerParams(
            dimension_semantics=("arbitrary", "arbitrary"), vmem_limit_bytes=VMEM_LIMIT_BYTES),
        name="final",
    )(*args)


def _moe(hm, eidx, gates, layer, w_gate, w_up, w_down):
    n_slots = 2 * hm.shape[0]
    n_rows = n_slots + N_EXPERTS * EXPERT_ROW_TILE
    pos, ends = _positions(eidx)
    pos = pos.reshape(n_slots)
    tile_start = jnp.arange(n_rows // EXPERT_ROW_TILE, dtype=jnp.int32) * EXPERT_ROW_TILE
    tile_expert = jnp.sum(tile_start[:, None] >= ends[None, :, 0], axis=1).astype(jnp.int32)
    xs = _sc_gather(hm, _sc_invert(pos, n_rows))
    ys = _experts(tile_expert, xs, layer, w_gate, w_up, w_down)
    return _sc_gather(ys, pos), gates.T


def kernel(x, c, ctx, c_ctx, ada_w, ada_b, norm_g, rw_mu, rw_w_rkv, rw_w0, rw_w1, rw_w2, rw_a0, rw_a1, rw_a2, rw_g1, rw_g2, rw_k_k, rw_k_a, rw_r_k, rw_gn_w, rw_gn_b, rw_w_o, sc_w_in, sc_conv, sc_w_out, router_w, router_b, moe_w_gate, moe_w_up, moe_w_down, final_g):
    b, t, d = x.shape
    ctx_len = ctx.shape[1]
    depth = ada_w.shape[0]
    assert d == D_MODEL and depth == 2 and t % POSITION_TILE == 0

    mod_rows = 16
    cc = jnp.concatenate([c, c_ctx[None, :], jnp.zeros((mod_rows - b - 1, d), F32)], axis=0)
    mod = _ada(cc, ada_w, ada_b).reshape(depth, mod_rows, 1, 6 * d)

    row = lambda a: a.reshape(1, d)
    router_wt = jnp.pad(router_w, ((0, 0), (0, LANES - N_EXPERTS)))
    router_bc = router_b.reshape(N_EXPERTS, 1)

    w1 = jnp.concatenate([rw_w1[0, 0], rw_w1[0, 1]], axis=1).astype(BF16)
    a1 = jnp.concatenate([rw_a1[0, 0], rw_a1[0, 1]], axis=1).astype(BF16)

    def pad_dirs(w):
        z = jnp.zeros_like(w[0])
        return jnp.stack([jnp.concatenate([w[0], z], axis=0), jnp.concatenate([z, w[1]], axis=0)]).astype(BF16)

    assert b % N_STREAMS == 0
    nb = b // N_STREAMS
    pre_params = (row(norm_g[0, 0]), rw_mu[0], rw_w_rkv[0].astype(BF16), w1, pad_dirs(rw_w2[0]), rw_w0[0], a1,
                  pad_dirs(rw_a2[0]), rw_a0[0], rw_g1[0].astype(BF16), rw_g2[0].astype(BF16), row(rw_k_k[0]),
                  row(rw_k_a[0]), row(rw_r_k[0]))
    w_o, w_in, w_out = rw_w_o[0].astype(BF16), sc_w_in[0].astype(BF16), sc_w_out[0].astype(BF16)
    out = None
    for batch0 in range(0, b, nb):
        r, v, g, bonus, k, kk, cs0, cs1, a0, a1 = _rwkv_pre(ctx, x, mod[0], *pre_params, batch0, nb)
        yf, yr = _wkv(row(rw_k_a[0]), r, v, k, kk, cs0, a0, cs1, a1, ctx_len)
        h1, hm, eidx, gates = _readout(yf, yr, bonus, g, x, mod[0], row(rw_gn_w[0]), row(rw_gn_b[0]),
                                       w_o, row(norm_g[0, 1]), router_wt, router_bc, batch0, nb)
        yg, gates = _moe(hm, eidx, gates, 0, moe_w_gate, moe_w_up, moe_w_down)
        h1, hm, eidx, gates = _sconv(yg, gates, h1, mod[0], mod[1], row(norm_g[1, 0]), w_in, sc_conv[0], w_out,
                                     row(norm_g[1, 1]), router_wt, router_bc, batch0)
        yg, gates = _moe(hm, eidx, gates, 1, moe_w_gate, moe_w_up, moe_w_down)
        out = _final(yg, gates, h1, mod[1], row(final_g), batch0, b, out)
    return out
```

```python
import functools

import jax
import jax.numpy as jnp
from jax import lax
from jax.experimental import pallas as pl
from jax.experimental.pallas import tpu as pltpu
from jax.experimental.pallas import tpu_sc as plsc

F32 = jnp.float32
BF16 = jnp.bfloat16

D_MODEL = 1024
HEAD_DIM = 64
N_HEADS = D_MODEL // HEAD_DIM
LANES = 128
N_PAIRS = D_MODEL // LANES
CHUNK = 64
N_EXPERTS = 16
EXPERTS_PER_GROUP = 4
N_GROUPS = N_EXPERTS // EXPERTS_PER_GROUP
RMS_EPS = 1e-6
GN_EPS = 64e-5
L2_EPS = 1e-12

TOKEN_TILE = 256
WIDE_TOKEN_TILE = 512
MIX_SUB_TILES = 2
PRE_SUB_TILES = 2
EXPERT_ROW_TILE = 512
POSITION_TILE = 1024
SC_CORES, SC_SUBCORES, SC_LANES = 2, 16, 16
SC_WORKERS = SC_CORES * SC_SUBCORES
N_STREAMS = 2
SC_GATHER_ROWS = 128
PAIRS_PER_STEP = 8
STAGE_LAG = 2
CHUNKS_PER_STEP = 4
ADA_COL_TILE = 1536
VMEM_LIMIT_BYTES = 56 * 1024 * 1024


def _sigmoid(x):
    return 1.0 / (1.0 + jnp.exp(-x))


def _mm(a, b):
    return jnp.dot(a.astype(BF16), b.astype(BF16), preferred_element_type=F32)


def _mm_nt(a, b):
    return lax.dot_general(a.astype(BF16), b.astype(BF16), (((1,), (1,)), ((), ())),
                           preferred_element_type=F32)


def _split2(x):
    hi = x.astype(BF16)
    return hi, (x - hi.astype(F32)).astype(BF16)


def _head_sum(x):
    rows = x.shape[0]
    left = lax.broadcasted_iota(jnp.int32, (rows, LANES), 1) < HEAD_DIM
    outs = []
    for j in range(N_PAIRS):
        xb = x[:, LANES * j:LANES * (j + 1)]
        sa = jnp.sum(jnp.where(left, xb, 0.0), axis=-1, keepdims=True)
        sb = jnp.sum(jnp.where(left, 0.0, xb), axis=-1, keepdims=True)
        outs.append(jnp.where(left, sa, sb))
    return jnp.concatenate(outs, axis=-1)


def _rms_modulate(x, g, shift, scale):
    ms = jnp.mean(x * x, axis=-1, keepdims=True)
    xn = x * lax.rsqrt(ms + RMS_EPS) * g
    return xn * (1.0 + scale) + shift


def _row_neighbours(x, row_len):
    rows = x.shape[0]
    pos = lax.broadcasted_iota(jnp.int32, x.shape, 0) & (row_len - 1)
    prev = jnp.where(pos == 0, 0.0, pltpu.roll(x, 1, 0))
    nxt = jnp.where(pos == row_len - 1, 0.0, pltpu.roll(x, rows - 1, 0))
    return prev, nxt


def _ada_kernel(c_ref, w_ref, b_ref, o_ref):
    c = c_ref[...]
    s = c * _sigmoid(c)
    o_ref[0] = _mm(s, w_ref[0]) + b_ref[0]


def _ada(cc, ada_w, ada_b):
    depth, d, n = ada_w.shape
    rows = cc.shape[0]
    return pl.pallas_call(
        _ada_kernel,
        grid=(depth, n // ADA_COL_TILE),
        in_specs=[
            pl.BlockSpec((rows, d), lambda l, j: (0, 0)),
            pl.BlockSpec((1, d, ADA_COL_TILE), lambda l, j: (l, 0, j)),
            pl.BlockSpec((1, 1, ADA_COL_TILE), lambda l, j: (l, 0, j)),
        ],
        out_specs=pl.BlockSpec((1, rows, ADA_COL_TILE), lambda l, j: (l, 0, j)),
        out_shape=jax.ShapeDtypeStruct((depth, rows, n), F32),
        compiler_params=pltpu.CompilerParams(
            dimension_semantics=("arbitrary", "arbitrary"), vmem_limit_bytes=VMEM_LIMIT_BYTES),
        name="ada",
    )(cc, ada_w, ada_b.reshape(depth, 1, n))


def _rwkv_pre_kernel(ctx_len, ctx_ref, x_ref, sh_ref, sc_ref, ng_ref, mu_ref, wrkv_ref, w1_ref, w2_ref, w0_ref,
                     a1_ref, a2_ref, a0_ref, g1_ref, g2_ref, kk_ref, ka_ref, rk_ref,
                     r_out, v_out, g_out, bon_out, k_out, kk_out, cs0_out, cs1_out, a0_out, a1_out):
    j = pl.program_id(1)
    row_len = jnp.where(j == 0, ctx_len, CHUNK)
    xin = jnp.where(j == 0, ctx_ref[0], x_ref[0])
    xn = _rms_modulate(xin, ng_ref[...], sh_ref[0], sc_ref[0])
    prev, nxt = _row_neighbours(xn, row_len)
    xx = 0.5 * (prev + nxt) - xn

    sub = xn.shape[0] // PRE_SUB_TILES
    projected = []
    for s in range(PRE_SUB_TILES):
        rs = slice(s * sub, (s + 1) * sub)
        xn_s, xx_s = xn[rs], xx[rs]

        def mix(i, xn_s=xn_s, xx_s=xx_s):
            return xn_s + xx_s * mu_ref[i:i + 1, :]

        r = _mm(mix(0), wrkv_ref[0])
        k = _mm(mix(1), wrkv_ref[1])
        v = _mm(mix(2), wrkv_ref[2])
        g = _mm(_sigmoid(_mm(mix(5), g1_ref[...])), g2_ref[...])
        wl = jnp.tanh(_mm(mix(3), w1_ref[...]))
        al = _mm(mix(4), a1_ref[...])
        projected.append((rs, r, k, v, g, [_mm(wl, w2_ref[p]) for p in range(2)],
                          [_mm(al, a2_ref[p]) for p in range(2)]))

    ti = lax.broadcasted_iota(jnp.int32, (sub, sub), 0)
    tj = lax.broadcasted_iota(jnp.int32, (sub, sub), 1)
    same_chunk = (ti ^ tj) < CHUNK
    dir_outs = ((cs0_out, a0_out), (cs1_out, a1_out))
    for rs, r, k, v, g, zs, a_logits in projected:
        r_out[0, rs, :] = r.astype(r_out.dtype)
        v_out[0, rs, :] = v.astype(v_out.dtype)
        g_out[0, rs, :] = g.astype(g_out.dtype)
        k_out[0, rs, :] = k.astype(k_out.dtype)
        kk = k * kk_ref[...]
        kk_out[0, rs, :] = (kk * lax.rsqrt(jnp.maximum(_head_sum(kk * kk), L2_EPS * L2_EPS))).astype(kk_out.dtype)
        a_sum = None
        for p, (cs_out, a_out) in enumerate(dir_outs):
            z = w0_ref[p:p + 1, :] + zs[p]
            lw = -jnp.exp(-0.5) * _sigmoid(z)
            tri = jnp.where(same_chunk & ((tj >= ti) if p else (tj <= ti)), 1.0, 0.0).astype(BF16)
            hi, lo = _split2(lw)
            cs_out[0, rs, :] = _dot(tri, hi) + _dot(tri, lo)
            a = _sigmoid(a0_ref[p:p + 1, :] + a_logits[p])
            a_out[0, rs, :] = a.astype(a_out.dtype)
            a_sum = a if a_sum is None else a_sum + a
        k_dirs = k * (2.0 + (a_sum - 2.0) * ka_ref[...])
        bon_out[0, rs, :] = (_head_sum(r * rk_ref[...] * k_dirs) * v).astype(bon_out.dtype)


def _rwkv_pre(ctx, x, mod, norm_g, mu, w_rkv, w1, w2, w0, a1, a2, a0, g1, g2, k_k, k_a, r_k, batch0, b):
    n_batch, t, d = x.shape
    ctx_len = ctx.shape[1]
    tt = ctx_len + t
    assert ctx_len == TOKEN_TILE and t % TOKEN_TILE == 0

    def mod_spec(part):
        return pl.BlockSpec((1, 1, d), lambda i, j: (jnp.where(j == 0, n_batch, i + batch0), 0, part))

    def full(a):
        nd = a.ndim
        return pl.BlockSpec(a.shape, lambda i, j: (0,) * nd)

    tile = pl.BlockSpec((1, TOKEN_TILE, d), lambda i, j: (i, j, 0))
    latent = pl.BlockSpec((1, TOKEN_TILE, d), lambda i, j: (i, jnp.maximum(j - 1, 0), 0))
    params = (norm_g, mu, w_rkv, w1, w2, w0, a1, a2, a0, g1, g2, k_k, k_a, r_k)
    out_dtypes = (BF16,) * 6 + (F32, F32) + (BF16,) * 2
    out_specs = [tile, tile, latent, latent] + [tile] * 6
    return pl.pallas_call(
        functools.partial(_rwkv_pre_kernel, ctx_len),
        grid=(b, tt // TOKEN_TILE),
        in_specs=[pl.BlockSpec((1, TOKEN_TILE, d), lambda i, j: (i + batch0, 0, 0)),
                  pl.BlockSpec((1, TOKEN_TILE, d), lambda i, j: (i + batch0, jnp.maximum(j - 1, 0), 0)),
                  mod_spec(0), mod_spec(1)] + [full(a) for a in params],
        out_specs=out_specs,
        out_shape=[jax.ShapeDtypeStruct((b, t if spec is latent else tt, d), dt)
                   for spec, dt in zip(out_specs, out_dtypes)],
        compiler_params=pltpu.CompilerParams(
            dimension_semantics=("arbitrary", "arbitrary"), vmem_limit_bytes=VMEM_LIMIT_BYTES),
        name="rwkv_pre",
    )(ctx, x, mod, mod, *params)


def _expand(x, left):
    return jnp.concatenate([jnp.where(left, x, 0.0), jnp.where(left, 0.0, x)], axis=0)


def _dot(a, b):
    return jnp.dot(a, b, preferred_element_type=F32)


def _chunk_steps(chains, group_size):
    assert CHUNK == HEAD_DIM
    c = CHUNK
    lane = lax.broadcasted_iota(jnp.int32, (c, LANES), 1)
    left = lane < HEAD_DIM
    tt = lax.broadcasted_iota(jnp.int32, (c, LANES), 0)
    jj = lane & (c - 1)
    diag = jj == tt
    tri = {False: (jj < tt, jj <= tt), True: (jj > tt, jj >= tt)}

    def bd(x):
        return _expand(x, left).astype(BF16)

    def fold_t(x):
        xt = _expand(x, left).T
        return xt[:c] + xt[c:]

    def rows(*xs):
        return jnp.concatenate(xs, axis=0).astype(BF16)

    nt_dims = (((1,), (1,)), ((), ()))

    def s_prep(q):
        cs, r, v, k, kk, a, ka, h, reverse = q.pop("chain")
        tot = cs[0:1, :] if reverse else cs[c - 1:c, :]
        e_pos, e_neg, e_rem = jnp.exp(cs), jnp.exp(-cs), jnp.exp(tot - cs)
        first = tt == (c - 1 if reverse else 0)
        cs_prev = jnp.where(first, 0.0, pltpu.roll(cs, c - 1 if reverse else 1, 0))
        at, rt = -kk * jnp.exp(cs_prev), r * e_pos
        kd, be = k * (1.0 + (a - 1.0) * ka), kk * a
        q.update(a_e=bd(at), rt=rt, v_e=bd(v), g_tot=jnp.exp(tot), h=h, tri=tri[reverse])
        q["hat_t"] = jnp.concatenate([fold_t(be * e_rem), fold_t(kd * e_rem)], axis=1)
        q["sc"] = lax.dot_general(rows(at, rt), jnp.concatenate([bd(be * e_neg), bd(kd * e_neg)], axis=0),
                                  nt_dims, preferred_element_type=F32)

    def s_mask(q):
        (strict, incl), sc = q["tri"], q["sc"]
        q["n"] = jnp.where(strict, sc[:c, :LANES], 0.0)
        a_ak = jnp.where(strict, sc[:c, LANES:], 0.0)
        q["a_rb"] = jnp.where(incl, sc[c:, :LANES], 0.0)
        a_rk = jnp.where(incl, sc[c:, LANES:], 0.0)
        q["bh_t"] = q["hat_t"][:, :LANES]
        q["vv"] = _dot(rows(a_ak, q["hat_t"][:, LANES:], a_rk), q["v_e"])

    def s_square(q):
        q["t"] = jnp.where(diag, 1.0, 0.0) + q["n"]
        q["p"] = _dot(q["n"].astype(BF16), bd(q["n"]))

    def s_level(q):
        x = _dot(rows(q["p"], q["t"]), bd(q["p"]))
        q["p"] = x[:c]
        q["t"] = q["t"] + x[c:]

    def s_last_level(q):
        q["t"] = q["t"] + _dot(q["t"].astype(BF16), bd(q["p"]))

    def s_solve(q):
        xu = _dot(q["t"].astype(BF16), jnp.concatenate([q["a_e"], bd(q["vv"][:c])], axis=1))
        q["au_e"] = jnp.concatenate([bd(xu[:, :LANES]), bd(xu[:, LANES:])], axis=1)

    def s_affine(q):
        z = _dot(rows(q["bh_t"], q["a_rb"]), q["au_e"])
        q["m"] = z[:c, :LANES] + jnp.where(diag, q["g_tot"], 0.0)
        q["g"] = z[:c, LANES:] + q["vv"][c:2 * c]
        q["r_hat"] = q["rt"] + z[c:, :LANES]
        q["yi"] = z[c:, LANES:] + q["vv"][2 * c:]

    n_levels = (c // 4).bit_length() - 1
    stages = [s_prep, s_mask, s_square] + [s_level] * n_levels + [s_last_level, s_solve, s_affine]
    qs = [dict(chain=ch) for ch in chains]
    groups = [qs[i:i + group_size] for i in range(0, len(qs), group_size)]
    for tau in range(len(stages) + (len(groups) - 1) * STAGE_LAG):
        for gi, group in enumerate(groups):
            si = tau - gi * STAGE_LAG
            if 0 <= si < len(stages):
                for q in group:
                    stages[si](q)
    outs = []
    for q in qs:
        h = outs[q["h"]][1] if isinstance(q["h"], int) else q["h"]
        o = _dot(rows(q["r_hat"], q["m"]), bd(h))
        outs.append((o[:c] + q["yi"], o[c:] + q["g"]))
    return outs


def _wkv_kernel(ka_ref, rf_ref, vf_ref, kf_ref, kkf_ref, csf_ref, af_ref,
                rr_ref, vr_ref, kr_ref, kkr_ref, csr_ref, ar_ref,
                yf_ref, yr_ref, hf_ref, hr_ref):
    @pl.when(pl.program_id(2) == 0)
    def _():
        hf_ref[...] = jnp.zeros_like(hf_ref)
        hr_ref[...] = jnp.zeros_like(hr_ref)

    dirs = ((rf_ref, vf_ref, kf_ref, kkf_ref, csf_ref, af_ref, yf_ref, hf_ref),
            (rr_ref, vr_ref, kr_ref, kkr_ref, csr_ref, ar_ref, yr_ref, hr_ref))
    chains, dests = [], []
    per_chunk = 2 * PAIRS_PER_STEP
    for u in range(CHUNKS_PER_STEP):
        for p in range(PAIRS_PER_STEP):
            sl = slice(p * LANES, (p + 1) * LANES)
            for reverse, (r_ref, v_ref, k_ref, kk_ref, cs_ref, a_ref, y_ref, h_ref) in enumerate(dirs):
                at = CHUNKS_PER_STEP - 1 - u if reverse else u
                tm = slice(at * CHUNK, (at + 1) * CHUNK)
                h = h_ref[p] if u == 0 else len(chains) - per_chunk
                chains.append((cs_ref[0, tm, sl], r_ref[0, tm, sl].astype(F32), v_ref[0, tm, sl].astype(F32),
                               k_ref[0, tm, sl].astype(F32), kk_ref[0, tm, sl].astype(F32),
                               a_ref[0, tm, sl].astype(F32), ka_ref[:, sl], h, bool(reverse)))
                dests.append((y_ref, tm, sl, h_ref if u == CHUNKS_PER_STEP - 1 else None, p))
    for (y, h_new), (y_ref, tm, sl, h_ref, p) in zip(_chunk_steps(chains, per_chunk), dests):
        y_ref[0, tm, sl] = y
        if h_ref is not None:
            h_ref[p] = h_new


def _wkv(k_a, r, v, k, kk, cs0, a0, cs1, a1, ctx_len):
    b, tt, d = r.shape
    step_rows = CHUNKS_PER_STEP * CHUNK
    n_steps, n_ctx = tt // step_rows, ctx_len // step_rows
    assert n_steps * step_rows == tt and n_ctx * step_rows == ctx_len
    width = PAIRS_PER_STEP * LANES

    def fwd_map(i, j, s):
        return (i, s, j)

    def rev_map(i, j, s):
        return (i, jnp.where(s < n_ctx, n_ctx - 1 - s, n_steps - 1 - (s - n_ctx)), j)

    n_lat = n_steps - n_ctx

    def fwd_out(i, j, s):
        return (i, jnp.maximum(s - n_ctx, 0), j)

    def rev_out(i, j, s):
        return (i, jnp.where(s < n_ctx, n_lat - 1, n_lat - 1 - (s - n_ctx)), j)

    fwd = pl.BlockSpec((1, step_rows, width), fwd_map)
    rev = pl.BlockSpec((1, step_rows, width), rev_map)
    return pl.pallas_call(
        _wkv_kernel,
        grid=(b, d // width, n_steps),
        in_specs=[pl.BlockSpec((1, width), lambda i, j, s: (0, j))] + [fwd] * 6 + [rev] * 6,
        out_specs=[pl.BlockSpec((1, step_rows, width), fwd_out), pl.BlockSpec((1, step_rows, width), rev_out)],
        out_shape=[jax.ShapeDtypeStruct((b, tt - ctx_len, d), F32)] * 2,
        scratch_shapes=[pltpu.VMEM((PAIRS_PER_STEP, HEAD_DIM, LANES), F32)] * 2,
        compiler_params=pltpu.CompilerParams(
            dimension_semantics=("arbitrary", "arbitrary", "arbitrary"), vmem_limit_bytes=VMEM_LIMIT_BYTES),
        name="wkv",
    )(k_a, r, v, k, kk, cs0, a0, r, v, k, kk, cs1, a1)


def _pack_bf16_pairs(x):
    half = x.shape[1] // 2
    lo = pltpu.bitcast(x[:, :half].astype(BF16).astype(F32), jnp.int32)
    hi = pltpu.bitcast(x[:, half:].astype(BF16).astype(F32), jnp.int32)
    return lax.shift_right_logical(lo, jnp.int32(16)) | (hi & jnp.int32(-65536))


def _unpack_bf16_pairs(w):
    lo = pltpu.bitcast(lax.shift_left(w, jnp.int32(16)), F32)
    hi = pltpu.bitcast(w & jnp.int32(-65536), F32)
    return jnp.concatenate([lo, hi], axis=1)


def _route(h1, ng, shift, scale, rwt_ref, rb_ref, hm_out, eidx_out, gate_out, rs=slice(None)):
    hm = _rms_modulate(h1, ng, shift, scale)
    hm_out[rs, :] = _pack_bf16_pairs(hm)
    w_hi, w_mid = _split2(rwt_ref[...])
    h_hi, h_mid = _split2(hm)
    logits_t = _dot(h_hi, w_hi) + _dot(h_mid, w_hi) + _dot(h_hi, w_mid)
    logits = logits_t.T[:N_EXPERTS]
    s = _sigmoid(logits)
    sel = s + rb_ref[...]
    assert EXPERTS_PER_GROUP == 4 and N_GROUPS == 4
    cands = []
    for g in range(N_GROUPS):
        m = [sel[e:e + 1, :] for e in range(g * EXPERTS_PER_GROUP, (g + 1) * EXPERTS_PER_GROUP)]
        sg = [s[e:e + 1, :] for e in range(g * EXPERTS_PER_GROUP, (g + 1) * EXPERTS_PER_GROUP)]
        pairs = [m[i] + m[k] for i in range(4) for k in range(i + 1, 4)]
        score = jnp.maximum(jnp.maximum(jnp.maximum(pairs[0], pairs[1]), jnp.maximum(pairs[2], pairs[3])),
                            jnp.maximum(pairs[4], pairs[5]))
        chosen = []
        for i in range(4):
            ahead = [((m[k] >= m[i]) if k < i else (m[k] > m[i])).astype(jnp.int32) for k in range(4) if k != i]
            chosen.append(ahead[0] + ahead[1] + ahead[2] < 2)
        base = g * EXPERTS_PER_GROUP
        lo_idx = jnp.where(chosen[0], base, jnp.where(chosen[1], base + 1, base + 2))
        lo_gate = jnp.where(chosen[0], sg[0], jnp.where(chosen[1], sg[1], sg[2]))
        hi_idx = jnp.where(chosen[3], base + 3, jnp.where(chosen[2], base + 2, base + 1))
        hi_gate = jnp.where(chosen[3], sg[3], jnp.where(chosen[2], sg[2], sg[1]))
        cands.append((score, lo_idx, hi_idx, lo_gate, hi_gate))

    def better(x, y):
        win = y[0] > x[0]
        return tuple(jnp.where(win, yv, xv) for xv, yv in zip(x, y))

    _, e_lo, e_hi, g_lo, g_hi = better(better(cands[0], cands[1]), better(cands[2], cands[3]))
    eidx_out[:, rs] = jnp.concatenate([e_lo, e_hi], axis=0)
    gate_out[:, rs] = jnp.concatenate([g_lo, g_hi], axis=0) / (g_lo + g_hi)


def _readout_kernel(yf_ref, yr_ref, bon_ref, g_ref, h_ref, gt_ref, gnw_ref, gnb_ref, wo_ref,
                    ng_ref, sh_ref, sc_ref, rwt_ref, rb_ref, h1_out, hm_out, eidx_out, gate_out):
    sub = h_ref.shape[1] // MIX_SUB_TILES
    tiles = [slice(s * sub, (s + 1) * sub) for s in range(MIX_SUB_TILES)]
    gated, h1s = [], []
    for rs in tiles:
        y = yf_ref[0, rs, :] + yr_ref[0, rs, :] + bon_ref[0, rs, :].astype(F32)
        mean = _head_sum(y) * (1.0 / HEAD_DIM)
        yc = y - mean
        var = _head_sum(yc * yc) * (1.0 / HEAD_DIM)
        yn = yc * lax.rsqrt(var + GN_EPS) * gnw_ref[...] + gnb_ref[...]
        gated.append((yn * g_ref[0, rs, :].astype(F32)).astype(BF16))
    for rs, lhs in zip(tiles, gated):
        h1 = h_ref[0, rs, :] + gt_ref[0] * _dot(lhs, wo_ref[...])
        h1_out[0, rs, :] = h1
        h1s.append(h1)
    for rs, h1 in zip(tiles, h1s):
        _route(h1, ng_ref[...], sh_ref[0], sc_ref[0], rwt_ref, rb_ref, hm_out, eidx_out, gate_out, rs)


def _mod_spec3(part, d, batch0=0):
    return pl.BlockSpec((1, 1, d), lambda i, j: (i + batch0, 0, part))


def _route_out_specs(b, t, d, tile=TOKEN_TILE):
    nt = t // tile
    specs = [pl.BlockSpec((1, tile, d), lambda i, j: (i, j, 0)),
             pl.BlockSpec((tile, d // 2), lambda i, j: (i * nt + j, 0)),
             pl.BlockSpec((2, tile), lambda i, j: (0, i * nt + j)),
             pl.BlockSpec((2, tile), lambda i, j: (0, i * nt + j))]
    shapes = [jax.ShapeDtypeStruct((b, t, d), F32), jax.ShapeDtypeStruct((b * t, d // 2), jnp.int32),
              jax.ShapeDtypeStruct((2, b * t), jnp.int32), jax.ShapeDtypeStruct((2, b * t), F32)]
    return specs, shapes


def _full2(a):
    nd = a.ndim
    return pl.BlockSpec(a.shape, lambda i, j: (0,) * nd)


def _readout(yf, yr, bonus, g, h, mod, gn_w, gn_b, w_o, norm_g, router_wt, router_b, batch0, b):
    _, t, d = h.shape
    nt = t // WIDE_TOKEN_TILE
    local = pl.BlockSpec((1, WIDE_TOKEN_TILE, d), lambda i, j: (i, j, 0))
    tile = pl.BlockSpec((1, WIDE_TOKEN_TILE, d), lambda i, j: (i + batch0, j, 0))
    params_a = (gn_w, gn_b, w_o, norm_g)
    params_b = (router_wt, router_b)
    out_specs, out_shape = _route_out_specs(b, t, d, WIDE_TOKEN_TILE)
    return pl.pallas_call(
        _readout_kernel,
        grid=(b, nt),
        in_specs=[local] * 4 + [tile, _mod_spec3(2, d, batch0)] + [_full2(a) for a in params_a]
        + [_mod_spec3(3, d, batch0), _mod_spec3(4, d, batch0)] + [_full2(a) for a in params_b],
        out_specs=out_specs,
        out_shape=out_shape,
        compiler_params=pltpu.CompilerParams(
            dimension_semantics=("arbitrary", "arbitrary"), vmem_limit_bytes=VMEM_LIMIT_BYTES),
        name="readout",
    )(yf, yr, bonus, g, h, mod, *params_a, mod, mod, *params_b)


def _moe_residual(y0_ref, y1_ref, gate_ref, h1, gt, rs=slice(None)):
    gates = gate_ref[rs, :]
    y = gates[:, 0:1] * _unpack_bf16_pairs(y0_ref[rs, :]) + gates[:, 1:2] * _unpack_bf16_pairs(y1_ref[rs, :])
    return h1 + gt * y


def _sconv_kernel(y0_ref, y1_ref, gate_ref, hp_ref, gtp_ref, sh1_ref, sc1_ref, gt_ref, ng1_ref, win_ref, cw_ref,
                  wout_ref, ng2_ref, sh2_ref, sc2_ref, rwt_ref, rb_ref, h1_out, hm_out, eidx_out, gate_out):
    d = hp_ref.shape[2]
    sub = hp_ref.shape[1] // MIX_SUB_TILES
    tiles = [slice(s * sub, (s + 1) * sub) for s in range(MIX_SUB_TILES)]
    hs, xns, gated, h1s = [], [], [], []
    for rs in tiles:
        h = _moe_residual(y0_ref, y1_ref, gate_ref, hp_ref[0, rs, :], gtp_ref[0], rs)
        hs.append(h)
        xns.append(_rms_modulate(h, ng1_ref[...], sh1_ref[0], sc1_ref[0]).astype(BF16))
    for xn in xns:
        bg = _dot(xn, win_ref[:, 0:d])
        u = _dot(xn, win_ref[:, d:2 * d]) * _dot(xn, win_ref[:, 2 * d:3 * d])
        prev, nxt = _row_neighbours(u, CHUNK)
        conv = cw_ref[0:1, :] * prev + cw_ref[1:2, :] * u + cw_ref[2:3, :] * nxt
        gated.append((bg * conv).astype(BF16))
    for rs, h, lhs in zip(tiles, hs, gated):
        h1 = h + gt_ref[0] * _dot(lhs, wout_ref[...])
        h1_out[0, rs, :] = h1
        h1s.append(h1)
    for rs, h1 in zip(tiles, h1s):
        _route(h1, ng2_ref[...], sh2_ref[0], sc2_ref[0], rwt_ref, rb_ref, hm_out, eidx_out, gate_out, rs)


def _sconv(yg, gates, h_prev, mod_prev, mod, norm_g1, w_in, conv_w, w_out, norm_g2, router_wt, router_b, batch0):
    b, t, d = h_prev.shape
    nt = t // WIDE_TOKEN_TILE
    tile = pl.BlockSpec((1, WIDE_TOKEN_TILE, d), lambda i, j: (i, j, 0))
    params_a = (norm_g1, w_in, conv_w, w_out, norm_g2)
    params_b = (router_wt, router_b)
    out_specs, out_shape = _route_out_specs(b, t, d, WIDE_TOKEN_TILE)
    return pl.pallas_call(
        _sconv_kernel,
        grid=(b, nt),
        in_specs=_moe_out_specs(b, t, d, WIDE_TOKEN_TILE) + [tile, _mod_spec3(5, d, batch0)]
        + [_mod_spec3(0, d, batch0), _mod_spec3(1, d, batch0), _mod_spec3(2, d, batch0)]
        + [_full2(a) for a in params_a]
        + [_mod_spec3(3, d, batch0), _mod_spec3(4, d, batch0)] + [_full2(a) for a in params_b],
        out_specs=out_specs,
        out_shape=out_shape,
        compiler_params=pltpu.CompilerParams(
            dimension_semantics=("arbitrary", "arbitrary"), vmem_limit_bytes=VMEM_LIMIT_BYTES),
        name="sconv",
    )(yg, yg, gates, h_prev, mod_prev, mod, mod, mod, *params_a, mod, mod, *params_b)


def _positions_kernel(eidx_ref, pos_ref, ends_ref, run_ref):
    phase, i = pl.program_id(0), pl.program_id(1)
    tp = eidx_ref.shape[1]
    expert = lax.broadcasted_iota(jnp.int32, (N_EXPERTS, tp), 0)
    onehot = [jnp.where(eidx_ref[k:k + 1, :] == expert, 1.0, 0.0) for k in range(2)]

    @pl.when((phase == 0) & (i == 0))
    def _():
        run_ref[...] = jnp.zeros_like(run_ref)

    @pl.when(phase == 0)
    def _():
        run_ref[...] += jnp.sum(onehot[0] + onehot[1], axis=1, keepdims=True)

    @pl.when((phase == 1) & (i == 0))
    def _():
        padded = jnp.floor((run_ref[...] + (EXPERT_ROW_TILE - 1)) * (1.0 / EXPERT_ROW_TILE)) * EXPERT_ROW_TILE
        ei = lax.broadcasted_iota(jnp.int32, (N_EXPERTS, N_EXPERTS), 0)
        ej = lax.broadcasted_iota(jnp.int32, (N_EXPERTS, N_EXPERTS), 1)
        below = jnp.where(ej < ei, 1.0, 0.0)
        starts = jnp.dot(below, jnp.broadcast_to(padded, (N_EXPERTS, LANES)), precision=lax.Precision.HIGHEST,
                         preferred_element_type=F32)
        ends_ref[...] = (starts + padded).astype(jnp.int32)
        run_ref[...] = starts[:, 0:1]

    @pl.when(phase == 1)
    def _():
        ti = lax.broadcasted_iota(jnp.int32, (tp, tp), 0)
        tj = lax.broadcasted_iota(jnp.int32, (tp, tp), 1)
        before = jnp.where(ti < tj, 1.0, 0.0).astype(BF16)
        run = run_ref[...]
        for k in range(2):
            prefix = jnp.dot(onehot[k].astype(BF16), before, preferred_element_type=F32)
            pos_ref[k:k + 1, :] = jnp.sum(onehot[k] * (prefix + run), axis=0, keepdims=True).astype(jnp.int32)
            run = run + jnp.sum(onehot[k], axis=1, keepdims=True)
        run_ref[...] = run


def _positions(eidx):
    n = eidx.shape[1]
    nt = n // POSITION_TILE
    return pl.pallas_call(
        _positions_kernel,
        grid=(2, nt),
        in_specs=[pl.BlockSpec((2, POSITION_TILE), lambda p, i: (0, i))],
        out_specs=[pl.BlockSpec((2, POSITION_TILE), lambda p, i: (0, i * p)),
                   pl.BlockSpec((N_EXPERTS, LANES), lambda p, i: (0, 0))],
        out_shape=[jax.ShapeDtypeStruct((2, n), jnp.int32), jax.ShapeDtypeStruct((N_EXPERTS, LANES), jnp.int32)],
        scratch_shapes=[pltpu.VMEM((N_EXPERTS, 1), F32)],
        compiler_params=pltpu.CompilerParams(
            dimension_semantics=("arbitrary", "arbitrary"), vmem_limit_bytes=VMEM_LIMIT_BYTES),
        name="positions",
    )(eidx)


def _sc_mesh():
    return plsc.VectorSubcoreMesh(core_axis_name="c", subcore_axis_name="s")


def _sc_worker(n_workers_per_core=SC_SUBCORES):
    return lax.axis_index("c") * n_workers_per_core + lax.axis_index("s")


def _sc_invert(pos, n_rows):
    n_slots = pos.shape[0]
    n_tokens = n_slots // 2
    assert n_tokens & (n_tokens - 1) == 0
    per_worker = n_rows // SC_WORKERS
    assert per_worker * SC_WORKERS == n_rows and per_worker % SC_LANES == 0 and n_slots % SC_LANES == 0

    def body(pos_hbm, inv_hbm, pos_v, inv_v):
        base = _sc_worker() * per_worker
        pltpu.sync_copy(pos_hbm, pos_v)

        @pl.loop(0, per_worker, step=SC_LANES)
        def _(j):
            inv_v[pl.ds(j, SC_LANES)] = (base + j + lax.iota(jnp.int32, SC_LANES)) & (n_tokens - 1)

        @pl.loop(0, n_slots, step=SC_LANES)
        def _(s):
            local = pos_v[pl.ds(s, SC_LANES)] - base
            mine = (local >= 0) & (local < per_worker)
            token = (s + lax.iota(jnp.int32, SC_LANES)) & (n_tokens - 1)
            plsc.store_scatter(inv_v, [jnp.where(mine, local, 0)], token, mask=mine)

        pltpu.sync_copy(inv_v, inv_hbm.at[pl.ds(base, per_worker)])

    return pl.kernel(
        body, out_type=jax.ShapeDtypeStruct((n_rows,), jnp.int32), mesh=_sc_mesh(),
        scratch_types=[pltpu.VMEM((n_slots,), jnp.int32), pltpu.VMEM((per_worker,), jnp.int32)],
        compiler_params=pltpu.CompilerParams(needs_layout_passes=False),
        name="sc_invert",
    )(pos)


def _sc_gather(table, idx):
    n_rows, width = idx.shape[0], table.shape[1]
    per_worker = n_rows // SC_WORKERS
    assert per_worker * SC_WORKERS == n_rows and per_worker % SC_GATHER_ROWS == 0

    def body(table_hbm, idx_hbm, out_hbm, idx_v, rows_v, sem):
        base = _sc_worker() * per_worker

        @pl.loop(0, per_worker, step=SC_GATHER_ROWS)
        def _(j):
            pltpu.sync_copy(idx_hbm.at[pl.ds(base + j, SC_GATHER_ROWS)], idx_v)
            pltpu.async_copy(table_hbm.at[idx_v], rows_v, sem).wait()
            pltpu.sync_copy(rows_v, out_hbm.at[pl.ds(base + j, SC_GATHER_ROWS)])

    return pl.kernel(
        body, out_type=jax.ShapeDtypeStruct((n_rows, width), table.dtype), mesh=_sc_mesh(),
        scratch_types=[pltpu.VMEM((SC_GATHER_ROWS,), jnp.int32), pltpu.VMEM((SC_GATHER_ROWS, width), table.dtype),
                       pltpu.SemaphoreType.DMA],
        name="sc_gather",
    )(table, idx)


def _experts_kernel(layer, te_ref, seg_ref, nxt_ref, xs_ref, wg_hbm, wu_hbm, wd_hbm, ys_ref,
                    wg_f, wu_f, wd_f, wg_bf, wu_bf, wd_bf, sems):
    i = pl.program_id(0)
    expert = te_ref[i]
    held = jnp.minimum(expert, N_EXPERTS - 1)
    slot = seg_ref[i] & 1
    new_segment = (i == 0) | (seg_ref[i] != seg_ref[jnp.maximum(i - 1, 0)])

    def weight_copies(e, s):
        return [pltpu.make_async_copy(w_hbm.at[layer, e], w_f.at[s], sems.at[k, s])
                for k, (w_hbm, w_f) in enumerate(((wg_hbm, wg_f), (wu_hbm, wu_f), (wd_hbm, wd_f)))]

    @pl.when(i == 0)
    def _():
        for cp in weight_copies(held, slot):
            cp.start()

    @pl.when(new_segment)
    def _():
        for cp in weight_copies(held, slot):
            cp.wait()
        wg_bf[...] = wg_f[slot].astype(BF16)
        wu_bf[...] = wu_f[slot].astype(BF16)
        wd_bf[...] = wd_f[slot].astype(BF16)

        @pl.when(nxt_ref[i] >= 0)
        def _():
            for cp in weight_copies(nxt_ref[i], 1 - slot):
                cp.start()

    @pl.when(expert < N_EXPERTS)
    def _():
        sub = xs_ref.shape[0] // MIX_SUB_TILES
        tiles = [slice(s * sub, (s + 1) * sub) for s in range(MIX_SUB_TILES)]
        xs = [_unpack_bf16_pairs(xs_ref[rs, :]).astype(BF16) for rs in tiles]
        hes = []
        for x in xs:
            gate, up = _dot(x, wg_bf[...]), _dot(x, wu_bf[...])
            hes.append((gate * _sigmoid(gate) * up).astype(BF16))
        for rs, he in zip(tiles, hes):
            ys_ref[rs, :] = _pack_bf16_pairs(_dot(he, wd_bf[...]))


def _experts(tile_expert, xs, layer, w_gate, w_up, w_down):
    n_rows, half = xs.shape
    _, n_e, d, de = w_gate.shape
    held = jnp.minimum(tile_expert, n_e - 1)
    segment = jnp.cumsum(jnp.concatenate([jnp.zeros((1,), jnp.int32), (held[1:] != held[:-1]).astype(jnp.int32)]))
    after = jnp.searchsorted(held, held, side="right")
    next_expert = jnp.where(after < held.shape[0], held[jnp.minimum(after, held.shape[0] - 1)], -1).astype(jnp.int32)

    rows = pl.BlockSpec((EXPERT_ROW_TILE, half), lambda i, te, seg, nxt: (i, 0))
    hbm = pl.BlockSpec(memory_space=pl.ANY)
    return pl.pallas_call(
        functools.partial(_experts_kernel, layer),
        grid_spec=pltpu.PrefetchScalarGridSpec(
            num_scalar_prefetch=3,
            grid=(n_rows // EXPERT_ROW_TILE,),
            in_specs=[rows, hbm, hbm, hbm],
            out_specs=rows,
            scratch_shapes=[pltpu.VMEM((2, d, de), F32), pltpu.VMEM((2, d, de), F32), pltpu.VMEM((2, de, d), F32),
                            pltpu.VMEM((d, de), BF16), pltpu.VMEM((d, de), BF16), pltpu.VMEM((de, d), BF16),
                            pltpu.SemaphoreType.DMA((3, 2))]),
        out_shape=jax.ShapeDtypeStruct((n_rows, half), jnp.int32),
        compiler_params=pltpu.CompilerParams(
            dimension_semantics=("arbitrary",), vmem_limit_bytes=VMEM_LIMIT_BYTES),
        name="experts",
    )(tile_expert, segment.astype(jnp.int32), next_expert, xs, w_gate, w_up, w_down)


def _final_kernel(y0_ref, y1_ref, gate_ref, h1_ref, gt_ref, fg_ref, *out_refs):
    o_ref = out_refs[-1]
    h2 = _moe_residual(y0_ref, y1_ref, gate_ref, h1_ref[0], gt_ref[0])
    ms = jnp.mean(h2 * h2, axis=-1, keepdims=True)
    o_ref[0] = h2 * lax.rsqrt(ms + RMS_EPS) * fg_ref[...]


def _moe_out_specs(b, t, d, tile=TOKEN_TILE):
    nt = t // tile
    return [pl.BlockSpec((tile, d // 2), lambda i, j: (i * nt + j, 0)),
            pl.BlockSpec((tile, d // 2), lambda i, j: (b * nt + i * nt + j, 0)),
            pl.BlockSpec((tile, 2), lambda i, j: (i * nt + j, 0))]


def _final(yg, gates, h1, mod, final_g, batch0, n_batch, out_prev):
    b, t, d = h1.shape
    tile = pl.BlockSpec((1, WIDE_TOKEN_TILE, d), lambda i, j: (i, j, 0))
    in_specs = _moe_out_specs(b, t, d, WIDE_TOKEN_TILE) + [tile, _mod_spec3(5, d, batch0),
                                                           pl.BlockSpec((1, d), lambda i, j: (0, 0))]
    args = [yg, yg, gates, h1, mod, final_g]
    aliases = {}
    if out_prev is not None:
        in_specs.append(pl.BlockSpec(memory_space=pl.ANY))
        aliases = {len(args): 0}
        args.append(out_prev)
    return pl.pallas_call(
        _final_kernel,
        grid=(b, t // WIDE_TOKEN_TILE),
        in_specs=in_specs,
        out_specs=pl.BlockSpec((1, WIDE_TOKEN_TILE, d), lambda i, j: (i + batch0, j, 0)),
        out_shape=jax.ShapeDtypeStruct((n_batch, t, d), F32),
        input_output_aliases=aliases,
        compiler_params=pltpu.CompilerParams(
            dimension_semantics=("arbitrary", "arbitrary"), vmem_limit_bytes=VMEM_LIMIT_BYTES),
        name="final",
    )(*args)


def _moe(hm, eidx, gates, layer, w_gate, w_up, w_down):
    n_slots = 2 * hm.shape[0]
    n_rows = n_slots + N_EXPERTS * EXPERT_ROW_TILE
    pos, ends = _positions(eidx)
    pos = pos.reshape(n_slots)
    tile_start = jnp.arange(n_rows // EXPERT_ROW_TILE, dtype=jnp.int32) * EXPERT_ROW_TILE
    tile_expert = jnp.sum(tile_start[:, None] >= ends[None, :, 0], axis=1).astype(jnp.int32)
    xs = _sc_gather(hm, _sc_invert(pos, n_rows))
    ys = _experts(tile_expert, xs, layer, w_gate, w_up, w_down)
    return _sc_gather(ys, pos), gates.T


def kernel(x, c, ctx, c_ctx, ada_w, ada_b, norm_g, rw_mu, rw_w_rkv, rw_w0, rw_w1, rw_w2, rw_a0, rw_a1, rw_a2, rw_g1, rw_g2, rw_k_k, rw_k_a, rw_r_k, rw_gn_w, rw_gn_b, rw_w_o, sc_w_in, sc_conv, sc_w_out, router_w, router_b, moe_w_gate, moe_w_up, moe_w_down, final_g):
    b, t, d = x.shape
    ctx_len = ctx.shape[1]
    depth = ada_w.shape[0]
    assert d == D_MODEL and depth == 2 and t % POSITION_TILE == 0

    mod_rows = 16
    cc = jnp.concatenate([c, c_ctx[None, :], jnp.zeros((mod_rows - b - 1, d), F32)], axis=0)
    mod = _ada(cc, ada_w, ada_b).reshape(depth, mod_rows, 1, 6 * d)

    row = lambda a: a.reshape(1, d)
    router_wt = jnp.pad(router_w, ((0, 0), (0, LANES - N_EXPERTS)))
    router_bc = router_b.reshape(N_EXPERTS, 1)

    w1 = jnp.concatenate([rw_w1[0, 0], rw_w1[0, 1]], axis=1).astype(BF16)
    a1 = jnp.concatenate([rw_a1[0, 0], rw_a1[0, 1]], axis=1).astype(BF16)

    def pad_dirs(w):
        z = jnp.zeros_like(w[0])
        return jnp.stack([jnp.concatenate([w[0], z], axis=0), jnp.concatenate([z, w[1]], axis=0)]).astype(BF16)

    assert b % N_STREAMS == 0
    nb = b // N_STREAMS
    pre_params = (row(norm_g[0, 0]), rw_mu[0], rw_w_rkv[0].astype(BF16), w1, pad_dirs(rw_w2[0]), rw_w0[0], a1,
                  pad_dirs(rw_a2[0]), rw_a0[0], rw_g1[0].astype(BF16), rw_g2[0].astype(BF16), row(rw_k_k[0]),
                  row(rw_k_a[0]), row(rw_r_k[0]))
    w_o, w_in, w_out = rw_w_o[0].astype(BF16), sc_w_in[0].astype(BF16), sc_w_out[0].astype(BF16)
    out = None
    for batch0 in range(0, b, nb):
        r, v, g, bonus, k, kk, cs0, cs1, a0, a1 = _rwkv_pre(ctx, x, mod[0], *pre_params, batch0, nb)
        yf, yr = _wkv(row(rw_k_a[0]), r, v, k, kk, cs0, a0, cs1, a1, ctx_len)
        h1, hm, eidx, gates = _readout(yf, yr, bonus, g, x, mod[0], row(rw_gn_w[0]), row(rw_gn_b[0]),
                                       w_o, row(norm_g[0, 1]), router_wt, router_bc, batch0, nb)
        yg, gates = _moe(hm, eidx, gates, 0, moe_w_gate, moe_w_up, moe_w_down)
        h1, hm, eidx, gates = _sconv(yg, gates, h1, mod[0], mod[1], row(norm_g[1, 0]), w_in, sc_conv[0], w_out,
                                     row(norm_g[1, 1]), router_wt, router_bc, batch0)
        yg, gates = _moe(hm, eidx, gates, 1, moe_w_gate, moe_w_up, moe_w_down)
        out = _final(yg, gates, h1, mod[1], row(final_g), batch0, b, out)
    return out
```

```python
import functools

import jax
import jax.numpy as jnp
from jax import lax
from jax.experimental import pallas as pl
from jax.experimental.pallas import tpu as pltpu
from jax.experimental.pallas import tpu_sc as plsc

F32 = jnp.float32
BF16 = jnp.bfloat16

D_MODEL = 1024
HEAD_DIM = 64
N_HEADS = D_MODEL // HEAD_DIM
LANES = 128
N_PAIRS = D_MODEL // LANES
CHUNK = 64
N_EXPERTS = 16
EXPERTS_PER_GROUP = 4
N_GROUPS = N_EXPERTS // EXPERTS_PER_GROUP
RMS_EPS = 1e-6
GN_EPS = 64e-5
L2_EPS = 1e-12

TOKEN_TILE = 256
WIDE_TOKEN_TILE = 512
MIX_SUB_TILES = 2
PRE_SUB_TILES = 2
EXPERT_ROW_TILE = 512
POSITION_TILE = 1024
SC_CORES, SC_SUBCORES, SC_LANES = 2, 16, 16
SC_WORKERS = SC_CORES * SC_SUBCORES
N_STREAMS = 2
SC_GATHER_ROWS = 128
PAIRS_PER_STEP = 8
STAGE_LAG = 2
CHUNKS_PER_STEP = 4
ADA_COL_TILE = 1536
VMEM_LIMIT_BYTES = 56 * 1024 * 1024


def _sigmoid(x):
    return 1.0 / (1.0 + jnp.exp(-x))


def _mm(a, b):
    return jnp.dot(a.astype(BF16), b.astype(BF16), preferred_element_type=F32)


def _mm_nt(a, b):
    return lax.dot_general(a.astype(BF16), b.astype(BF16), (((1,), (1,)), ((), ())),
                           preferred_element_type=F32)


def _split2(x):
    hi = x.astype(BF16)
    return hi, (x - hi.astype(F32)).astype(BF16)


def _head_sum(x):
    rows = x.shape[0]
    left = lax.broadcasted_iota(jnp.int32, (rows, LANES), 1) < HEAD_DIM
    outs = []
    for j in range(N_PAIRS):
        xb = x[:, LANES * j:LANES * (j + 1)]
        sa = jnp.sum(jnp.where(left, xb, 0.0), axis=-1, keepdims=True)
        sb = jnp.sum(jnp.where(left, 0.0, xb), axis=-1, keepdims=True)
        outs.append(jnp.where(left, sa, sb))
    return jnp.concatenate(outs, axis=-1)


def _rms_modulate(x, g, shift, scale):
    ms = jnp.mean(x * x, axis=-1, keepdims=True)
    xn = x * lax.rsqrt(ms + RMS_EPS) * g
    return xn * (1.0 + scale) + shift


def _row_neighbours(x, row_len):
    rows = x.shape[0]
    pos = lax.broadcasted_iota(jnp.int32, x.shape, 0) & (row_len - 1)
    prev = jnp.where(pos == 0, 0.0, pltpu.roll(x, 1, 0))
    nxt = jnp.where(pos == row_len - 1, 0.0, pltpu.roll(x, rows - 1, 0))
    return prev, nxt


def _ada_kernel(c_ref, w_ref, b_ref, o_ref):
    c = c_ref[...]
    s = c * _sigmoid(c)
    o_ref[0] = _mm(s, w_ref[0]) + b_ref[0]


def _ada(cc, ada_w, ada_b):
    depth, d, n = ada_w.shape
    rows = cc.shape[0]
    return pl.pallas_call(
        _ada_kernel,
        grid=(depth, n // ADA_COL_TILE),
        in_specs=[
            pl.BlockSpec((rows, d), lambda l, j: (0, 0)),
            pl.BlockSpec((1, d, ADA_COL_TILE), lambda l, j: (l, 0, j)),
            pl.BlockSpec((1, 1, ADA_COL_TILE), lambda l, j: (l, 0, j)),
        ],
        out_specs=pl.BlockSpec((1, rows, ADA_COL_TILE), lambda l, j: (l, 0, j)),
        out_shape=jax.ShapeDtypeStruct((depth, rows, n), F32),
        compiler_params=pltpu.CompilerParams(
            dimension_semantics=("arbitrary", "arbitrary"), vmem_limit_bytes=VMEM_LIMIT_BYTES),
        name="ada",
    )(cc, ada_w, ada_b.reshape(depth, 1, n))


def _rwkv_pre_kernel(ctx_len, ctx_ref, x_ref, sh_ref, sc_ref, ng_ref, mu_ref, wrkv_ref, w1_ref, w2_ref, w0_ref,
                     a1_ref, a2_ref, a0_ref, g1_ref, g2_ref, kk_ref, ka_ref, rk_ref,
                     r_out, v_out, g_out, bon_out, k_out, kk_out, cs0_out, cs1_out, a0_out, a1_out):
    j = pl.program_id(1)
    row_len = jnp.where(j == 0, ctx_len, CHUNK)
    xin = jnp.where(j == 0, ctx_ref[0], x_ref[0])
    xn = _rms_modulate(xin, ng_ref[...], sh_ref[0], sc_ref[0])
    prev, nxt = _row_neighbours(xn, row_len)
    xx = 0.5 * (prev + nxt) - xn

    sub = xn.shape[0] // PRE_SUB_TILES
    projected = []
    for s in range(PRE_SUB_TILES):
        rs = slice(s * sub, (s + 1) * sub)
        xn_s, xx_s = xn[rs], xx[rs]

        def mix(i, xn_s=xn_s, xx_s=xx_s):
            return xn_s + xx_s * mu_ref[i:i + 1, :]

        r = _mm(mix(0), wrkv_ref[0])
        k = _mm(mix(1), wrkv_ref[1])
        v = _mm(mix(2), wrkv_ref[2])
        g = _mm(_sigmoid(_mm(mix(5), g1_ref[...])), g2_ref[...])
        wl = jnp.tanh(_mm(mix(3), w1_ref[...]))
        al = _mm(mix(4), a1_ref[...])
        projected.append((rs, r, k, v, g, [_mm(wl, w2_ref[p]) for p in range(2)],
                          [_mm(al, a2_ref[p]) for p in range(2)]))

    ti = lax.broadcasted_iota(jnp.int32, (sub, sub), 0)
    tj = lax.broadcasted_iota(jnp.int32, (sub, sub), 1)
    same_chunk = (ti ^ tj) < CHUNK
    dir_outs = ((cs0_out, a0_out), (cs1_out, a1_out))
    for rs, r, k, v, g, zs, a_logits in projected:
        r_out[0, rs, :] = r.astype(r_out.dtype)
        v_out[0, rs, :] = v.astype(v_out.dtype)
        g_out[0, rs, :] = g.astype(g_out.dtype)
        k_out[0, rs, :] = k.astype(k_out.dtype)
        kk = k * kk_ref[...]
        kk_out[0, rs, :] = (kk * lax.rsqrt(jnp.maximum(_head_sum(kk * kk), L2_EPS * L2_EPS))).astype(kk_out.dtype)
        a_sum = None
        for p, (cs_out, a_out) in enumerate(dir_outs):
            z = w0_ref[p:p + 1, :] + zs[p]
            lw = -jnp.exp(-0.5) * _sigmoid(z)
            tri = jnp.where(same_chunk & ((tj >= ti) if p else (tj <= ti)), 1.0, 0.0).astype(BF16)
            hi, lo = _split2(lw)
            cs_out[0, rs, :] = _dot(tri, hi) + _dot(tri, lo)
            a = _sigmoid(a0_ref[p:p + 1, :] + a_logits[p])
            a_out[0, rs, :] = a.astype(a_out.dtype)
            a_sum = a if a_sum is None else a_sum + a
        k_dirs = k * (2.0 + (a_sum - 2.0) * ka_ref[...])
        bon_out[0, rs, :] = (_head_sum(r * rk_ref[...] * k_dirs) * v).astype(bon_out.dtype)


def _rwkv_pre(ctx, x, mod, norm_g, mu, w_rkv, w1, w2, w0, a1, a2, a0, g1, g2, k_k, k_a, r_k, batch0, b):
    n_batch, t, d = x.shape
    ctx_len = ctx.shape[1]
    tt = ctx_len + t
    assert ctx_len == TOKEN_TILE and t % TOKEN_TILE == 0

    def mod_spec(part):
        return pl.BlockSpec((1, 1, d), lambda i, j: (jnp.where(j == 0, n_batch, i + batch0), 0, part))

    def full(a):
        nd = a.ndim
        return pl.BlockSpec(a.shape, lambda i, j: (0,) * nd)

    tile = pl.BlockSpec((1, TOKEN_TILE, d), lambda i, j: (i, j, 0))
    latent = pl.BlockSpec((1, TOKEN_TILE, d), lambda i, j: (i, jnp.maximum(j - 1, 0), 0))
    params = (norm_g, mu, w_rkv, w1, w2, w0, a1, a2, a0, g1, g2, k_k, k_a, r_k)
    out_dtypes = (BF16,) * 6 + (F32, F32) + (BF16,) * 2
    out_specs = [tile, tile, latent, latent] + [tile] * 6
    return pl.pallas_call(
        functools.partial(_rwkv_pre_kernel, ctx_len),
        grid=(b, tt // TOKEN_TILE),
        in_specs=[pl.BlockSpec((1, TOKEN_TILE, d), lambda i, j: (i + batch0, 0, 0)),
                  pl.BlockSpec((1, TOKEN_TILE, d), lambda i, j: (i + batch0, jnp.maximum(j - 1, 0), 0)),
                  mod_spec(0), mod_spec(1)] + [full(a) for a in params],
        out_specs=out_specs,
        out_shape=[jax.ShapeDtypeStruct((b, t if spec is latent else tt, d), dt)
                   for spec, dt in zip(out_specs, out_dtypes)],
        compiler_params=pltpu.CompilerParams(
            dimension_semantics=("arbitrary", "arbitrary"), vmem_limit_bytes=VMEM_LIMIT_BYTES),
        name="rwkv_pre",
    )(ctx, x, mod, mod, *params)


def _expand(x, left):
    return jnp.concatenate([jnp.where(left, x, 0.0), jnp.where(left, 0.0, x)], axis=0)


def _dot(a, b):
    return jnp.dot(a, b, preferred_element_type=F32)


def _chunk_steps(chains, group_size):
    assert CHUNK == HEAD_DIM
    c = CHUNK
    lane = lax.broadcasted_iota(jnp.int32, (c, LANES), 1)
    left = lane < HEAD_DIM
    tt = lax.broadcasted_iota(jnp.int32, (c, LANES), 0)
    jj = lane & (c - 1)
    diag = jj == tt
    tri = {False: (jj < tt, jj <= tt), True: (jj > tt, jj >= tt)}

    def bd(x):
        return _expand(x, left).astype(BF16)

    def fold_t(x):
        xt = _expand(x, left).T
        return xt[:c] + xt[c:]

    def rows(*xs):
        return jnp.concatenate(xs, axis=0).astype(BF16)

    nt_dims = (((1,), (1,)), ((), ()))

    def s_prep(q):
        cs, r, v, k, kk, a, ka, h, reverse = q.pop("chain")
        tot = cs[0:1, :] if reverse else cs[c - 1:c, :]
        e_pos, e_neg, e_rem = jnp.exp(cs), jnp.exp(-cs), jnp.exp(tot - cs)
        first = tt == (c - 1 if reverse else 0)
        cs_prev = jnp.where(first, 0.0, pltpu.roll(cs, c - 1 if reverse else 1, 0))
        at, rt = -kk * jnp.exp(cs_prev), r * e_pos
        kd, be = k * (1.0 + (a - 1.0) * ka), kk * a
        q.update(a_e=bd(at), rt=rt, v_e=bd(v), g_tot=jnp.exp(tot), h=h, tri=tri[reverse])
        q["hat_t"] = jnp.concatenate([fold_t(be * e_rem), fold_t(kd * e_rem)], axis=1)
        q["sc"] = lax.dot_general(rows(at, rt), jnp.concatenate([bd(be * e_neg), bd(kd * e_neg)], axis=0),
                                  nt_dims, preferred_element_type=F32)

    def s_mask(q):
        (strict, incl), sc = q["tri"], q["sc"]
        q["n"] = jnp.where(strict, sc[:c, :LANES], 0.0)
        a_ak = jnp.where(strict, sc[:c, LANES:], 0.0)
        q["a_rb"] = jnp.where(incl, sc[c:, :LANES], 0.0)
        a_rk = jnp.where(incl, sc[c:, LANES:], 0.0)
        q["bh_t"] = q["hat_t"][:, :LANES]
        q["vv"] = _dot(rows(a_ak, q["hat_t"][:, LANES:], a_rk), q["v_e"])

    def s_square(q):
        q["t"] = jnp.where(diag, 1.0, 0.0) + q["n"]
        q["p"] = _dot(q["n"].astype(BF16), bd(q["n"]))

    def s_level(q):
        x = _dot(rows(q["p"], q["t"]), bd(q["p"]))
        q["p"] = x[:c]
        q["t"] = q["t"] + x[c:]

    def s_last_level(q):
        q["t"] = q["t"] + _dot(q["t"].astype(BF16), bd(q["p"]))

    def s_solve(q):
        xu = _dot(q["t"].astype(BF16), jnp.concatenate([q["a_e"], bd(q["vv"][:c])], axis=1))
        q["au_e"] = jnp.concatenate([bd(xu[:, :LANES]), bd(xu[:, LANES:])], axis=1)

    def s_affine(q):
        z = _dot(rows(q["bh_t"], q["a_rb"]), q["au_e"])
        q["m"] = z[:c, :LANES] + jnp.where(diag, q["g_tot"], 0.0)
        q["g"] = z[:c, LANES:] + q["vv"][c:2 * c]
        q["r_hat"] = q["rt"] + z[c:, :LANES]
        q["yi"] = z[c:, LANES:] + q["vv"][2 * c:]

    n_levels = (c // 4).bit_length() - 1
    stages = [s_prep, s_mask, s_square] + [s_level] * n_levels + [s_last_level, s_solve, s_affine]
    qs = [dict(chain=ch) for ch in chains]
    groups = [qs[i:i + group_size] for i in range(0, len(qs), group_size)]
    for tau in range(len(stages) + (len(groups) - 1) * STAGE_LAG):
        for gi, group in enumerate(groups):
            si = tau - gi * STAGE_LAG
            if 0 <= si < len(stages):
                for q in group:
                    stages[si](q)
    outs = []
    for q in qs:
        h = outs[q["h"]][1] if isinstance(q["h"], int) else q["h"]
        o = _dot(rows(q["r_hat"], q["m"]), bd(h))
        outs.append((o[:c] + q["yi"], o[c:] + q["g"]))
    return outs


def _wkv_kernel(ka_ref, rf_ref, vf_ref, kf_ref, kkf_ref, csf_ref, af_ref,
                rr_ref, vr_ref, kr_ref, kkr_ref, csr_ref, ar_ref,
                yf_ref, yr_ref, hf_ref, hr_ref):
    @pl.when(pl.program_id(2) == 0)
    def _():
        hf_ref[...] = jnp.zeros_like(hf_ref)
        hr_ref[...] = jnp.zeros_like(hr_ref)

    dirs = ((rf_ref, vf_ref, kf_ref, kkf_ref, csf_ref, af_ref, yf_ref, hf_ref),
            (rr_ref, vr_ref, kr_ref, kkr_ref, csr_ref, ar_ref, yr_ref, hr_ref))
    chains, dests = [], []
    per_chunk = 2 * PAIRS_PER_STEP
    for u in range(CHUNKS_PER_STEP):
        for p in range(PAIRS_PER_STEP):
            sl = slice(p * LANES, (p + 1) * LANES)
            for reverse, (r_ref, v_ref, k_ref, kk_ref, cs_ref, a_ref, y_ref, h_ref) in enumerate(dirs):
                at = CHUNKS_PER_STEP - 1 - u if reverse else u
                tm = slice(at * CHUNK, (at + 1) * CHUNK)
                h = h_ref[p] if u == 0 else len(chains) - per_chunk
                chains.append((cs_ref[0, tm, sl], r_ref[0, tm, sl].astype(F32), v_ref[0, tm, sl].astype(F32),
                               k_ref[0, tm, sl].astype(F32), kk_ref[0, tm, sl].astype(F32),
                               a_ref[0, tm, sl].astype(F32), ka_ref[:, sl], h, bool(reverse)))
                dests.append((y_ref, tm, sl, h_ref if u == CHUNKS_PER_STEP - 1 else None, p))
    for (y, h_new), (y_ref, tm, sl, h_ref, p) in zip(_chunk_steps(chains, per_chunk), dests):
        y_ref[0, tm, sl] = y
        if h_ref is not None:
            h_ref[p] = h_new


def _wkv(k_a, r, v, k, kk, cs0, a0, cs1, a1, ctx_len):
    b, tt, d = r.shape
    step_rows = CHUNKS_PER_STEP * CHUNK
    n_steps, n_ctx = tt // step_rows, ctx_len // step_rows
    assert n_steps * step_rows == tt and n_ctx * step_rows == ctx_len
    width = PAIRS_PER_STEP * LANES

    def fwd_map(i, j, s):
        return (i, s, j)

    def rev_map(i, j, s):
        return (i, jnp.where(s < n_ctx, n_ctx - 1 - s, n_steps - 1 - (s - n_ctx)), j)

    n_lat = n_steps - n_ctx

    def fwd_out(i, j, s):
        return (i, jnp.maximum(s - n_ctx, 0), j)

    def rev_out(i, j, s):
        return (i, jnp.where(s < n_ctx, n_lat - 1, n_lat - 1 - (s - n_ctx)), j)

    fwd = pl.BlockSpec((1, step_rows, width), fwd_map)
    rev = pl.BlockSpec((1, step_rows, width), rev_map)
    return pl.pallas_call(
        _wkv_kernel,
        grid=(b, d // width, n_steps),
        in_specs=[pl.BlockSpec((1, width), lambda i, j, s: (0, j))] + [fwd] * 6 + [rev] * 6,
        out_specs=[pl.BlockSpec((1, step_rows, width), fwd_out), pl.BlockSpec((1, step_rows, width), rev_out)],
        out_shape=[jax.ShapeDtypeStruct((b, tt - ctx_len, d), F32)] * 2,
        scratch_shapes=[pltpu.VMEM((PAIRS_PER_STEP, HEAD_DIM, LANES), F32)] * 2,
        compiler_params=pltpu.CompilerParams(
            dimension_semantics=("arbitrary", "arbitrary", "arbitrary"), vmem_limit_bytes=VMEM_LIMIT_BYTES),
        name="wkv",
    )(k_a, r, v, k, kk, cs0, a0, r, v, k, kk, cs1, a1)


def _pack_bf16_pairs(x):
    half = x.shape[1] // 2
    lo = pltpu.bitcast(x[:, :half].astype(BF16).astype(F32), jnp.int32)
    hi = pltpu.bitcast(x[:, half:].astype(BF16).astype(F32), jnp.int32)
    return lax.shift_right_logical(lo, jnp.int32(16)) | (hi & jnp.int32(-65536))


def _unpack_bf16_pairs(w):
    lo = pltpu.bitcast(lax.shift_left(w, jnp.int32(16)), F32)
    hi = pltpu.bitcast(w & jnp.int32(-65536), F32)
    return jnp.concatenate([lo, hi], axis=1)


def _route(h1, ng, shift, scale, rwt_ref, rb_ref, hm_out, eidx_out, gate_out, rs=slice(None)):
    hm = _rms_modulate(h1, ng, shift, scale)
    hm_out[rs, :] = _pack_bf16_pairs(hm)
    w_hi, w_mid = _split2(rwt_ref[...])
    h_hi, h_mid = _split2(hm)
    logits_t = _dot(h_hi, w_hi) + _dot(h_mid, w_hi) + _dot(h_hi, w_mid)
    logits = logits_t.T[:N_EXPERTS]
    s = _sigmoid(logits)
    sel = s + rb_ref[...]
    assert EXPERTS_PER_GROUP == 4 and N_GROUPS == 4
    cands = []
    for g in range(N_GROUPS):
        m = [sel[e:e + 1, :] for e in range(g * EXPERTS_PER_GROUP, (g + 1) * EXPERTS_PER_GROUP)]
        sg = [s[e:e + 1, :] for e in range(g * EXPERTS_PER_GROUP, (g + 1) * EXPERTS_PER_GROUP)]
        pairs = [m[i] + m[k] for i in range(4) for k in range(i + 1, 4)]
        score = jnp.maximum(jnp.maximum(jnp.maximum(pairs[0], pairs[1]), jnp.maximum(pairs[2], pairs[3])),
                            jnp.maximum(pairs[4], pairs[5]))
        chosen = []
        for i in range(4):
            ahead = [((m[k] >= m[i]) if k < i else (m[k] > m[i])).astype(jnp.int32) for k in range(4) if k != i]
            chosen.append(ahead[0] + ahead[1] + ahead[2] < 2)
        base = g * EXPERTS_PER_GROUP
        lo_idx = jnp.where(chosen[0], base, jnp.where(chosen[1], base + 1, base + 2))
        lo_gate = jnp.where(chosen[0], sg[0], jnp.where(chosen[1], sg[1], sg[2]))
        hi_idx = jnp.where(chosen[3], base + 3, jnp.where(chosen[2], base + 2, base + 1))
        hi_gate = jnp.where(chosen[3], sg[3], jnp.where(chosen[2], sg[2], sg[1]))
        cands.append((score, lo_idx, hi_idx, lo_gate, hi_gate))

    def better(x, y):
        win = y[0] > x[0]
        return tuple(jnp.where(win, yv, xv) for xv, yv in zip(x, y))

    _, e_lo, e_hi, g_lo, g_hi = better(better(cands[0], cands[1]), better(cands[2], cands[3]))
    eidx_out[:, rs] = jnp.concatenate([e_lo, e_hi], axis=0)
    gate_out[:, rs] = jnp.concatenate([g_lo, g_hi], axis=0) / (g_lo + g_hi)


def _readout_kernel(yf_ref, yr_ref, bon_ref, g_ref, h_ref, gt_ref, gnw_ref, gnb_ref, wo_ref,
                    ng_ref, sh_ref, sc_ref, rwt_ref, rb_ref, h1_out, hm_out, eidx_out, gate_out):
    sub = h_ref.shape[1] // MIX_SUB_TILES
    tiles = [slice(s * sub, (s + 1) * sub) for s in range(MIX_SUB_TILES)]
    gated, h1s = [], []
    for rs in tiles:
        y = yf_ref[0, rs, :] + yr_ref[0, rs, :] + bon_ref[0, rs, :].astype(F32)
        mean = _head_sum(y) * (1.0 / HEAD_DIM)
        yc = y - mean
        var = _head_sum(yc * yc) * (1.0 / HEAD_DIM)
        yn = yc * lax.rsqrt(var + GN_EPS) * gnw_ref[...] + gnb_ref[...]
        gated.append((yn * g_ref[0, rs, :].astype(F32)).astype(BF16))
    for rs, lhs in zip(tiles, gated):
        h1 = h_ref[0, rs, :] + gt_ref[0] * _dot(lhs, wo_ref[...])
        h1_out[0, rs, :] = h1
        h1s.append(h1)
    for rs, h1 in zip(tiles, h1s):
        _route(h1, ng_ref[...], sh_ref[0], sc_ref[0], rwt_ref, rb_ref, hm_out, eidx_out, gate_out, rs)


def _mod_spec3(part, d, batch0=0):
    return pl.BlockSpec((1, 1, d), lambda i, j: (i + batch0, 0, part))


def _route_out_specs(b, t, d, tile=TOKEN_TILE):
    nt = t // tile
    specs = [pl.BlockSpec((1, tile, d), lambda i, j: (i, j, 0)),
             pl.BlockSpec((tile, d // 2), lambda i, j: (i * nt + j, 0)),
             pl.BlockSpec((2, tile), lambda i, j: (0, i * nt + j)),
             pl.BlockSpec((2, tile), lambda i, j: (0, i * nt + j))]
    shapes = [jax.ShapeDtypeStruct((b, t, d), F32), jax.ShapeDtypeStruct((b * t, d // 2), jnp.int32),
              jax.ShapeDtypeStruct((2, b * t), jnp.int32), jax.ShapeDtypeStruct((2, b * t), F32)]
    return specs, shapes


def _full2(a):
    nd = a.ndim
    return pl.BlockSpec(a.shape, lambda i, j: (0,) * nd)


def _readout(yf, yr, bonus, g, h, mod, gn_w, gn_b, w_o, norm_g, router_wt, router_b, batch0, b):
    _, t, d = h.shape
    nt = t // WIDE_TOKEN_TILE
    local = pl.BlockSpec((1, WIDE_TOKEN_TILE, d), lambda i, j: (i, j, 0))
    tile = pl.BlockSpec((1, WIDE_TOKEN_TILE, d), lambda i, j: (i + batch0, j, 0))
    params_a = (gn_w, gn_b, w_o, norm_g)
    params_b = (router_wt, router_b)
    out_specs, out_shape = _route_out_specs(b, t, d, WIDE_TOKEN_TILE)
    return pl.pallas_call(
        _readout_kernel,
        grid=(b, nt),
        in_specs=[local] * 4 + [tile, _mod_spec3(2, d, batch0)] + [_full2(a) for a in params_a]
        + [_mod_spec3(3, d, batch0), _mod_spec3(4, d, batch0)] + [_full2(a) for a in params_b],
        out_specs=out_specs,
        out_shape=out_shape,
        compiler_params=pltpu.CompilerParams(
            dimension_semantics=("arbitrary", "arbitrary"), vmem_limit_bytes=VMEM_LIMIT_BYTES),
        name="readout",
    )(yf, yr, bonus, g, h, mod, *params_a, mod, mod, *params_b)


def _moe_residual(y0_ref, y1_ref, gate_ref, h1, gt, rs=slice(None)):
    gates = gate_ref[rs, :]
    y = gates[:, 0:1] * _unpack_bf16_pairs(y0_ref[rs, :]) + gates[:, 1:2] * _unpack_bf16_pairs(y1_ref[rs, :])
    return h1 + gt * y


def _sconv_kernel(y0_ref, y1_ref, gate_ref, hp_ref, gtp_ref, sh1_ref, sc1_ref, gt_ref, ng1_ref, win_ref, cw_ref,
                  wout_ref, ng2_ref, sh2_ref, sc2_ref, rwt_ref, rb_ref, h1_out, hm_out, eidx_out, gate_out):
    d = hp_ref.shape[2]
    sub = hp_ref.shape[1] // MIX_SUB_TILES
    tiles = [slice(s * sub, (s + 1) * sub) for s in range(MIX_SUB_TILES)]
    hs, xns, gated, h1s = [], [], [], []
    for rs in tiles:
        h = _moe_residual(y0_ref, y1_ref, gate_ref, hp_ref[0, rs, :], gtp_ref[0], rs)
        hs.append(h)
        xns.append(_rms_modulate(h, ng1_ref[...], sh1_ref[0], sc1_ref[0]).astype(BF16))
    for xn in xns:
        bg = _dot(xn, win_ref[:, 0:d])
        u = _dot(xn, win_ref[:, d:2 * d]) * _dot(xn, win_ref[:, 2 * d:3 * d])
        prev, nxt = _row_neighbours(u, CHUNK)
        conv = cw_ref[0:1, :] * prev + cw_ref[1:2, :] * u + cw_ref[2:3, :] * nxt
        gated.append((bg * conv).astype(BF16))
    for rs, h, lhs in zip(tiles, hs, gated):
        h1 = h + gt_ref[0] * _dot(lhs, wout_ref[...])
        h1_out[0, rs, :] = h1
        h1s.append(h1)
    for rs, h1 in zip(tiles, h1s):
        _route(h1, ng2_ref[...], sh2_ref[0], sc2_ref[0], rwt_ref, rb_ref, hm_out, eidx_out, gate_out, rs)


def _sconv(yg, gates, h_prev, mod_prev, mod, norm_g1, w_in, conv_w, w_out, norm_g2, router_wt, router_b, batch0):
    b, t, d = h_prev.shape
    nt = t // WIDE_TOKEN_TILE
    tile = pl.BlockSpec((1, WIDE_TOKEN_TILE, d), lambda i, j: (i, j, 0))
    params_a = (norm_g1, w_in, conv_w, w_out, norm_g2)
    params_b = (router_wt, router_b)
    out_specs, out_shape = _route_out_specs(b, t, d, WIDE_TOKEN_TILE)
    return pl.pallas_call(
        _sconv_kernel,
        grid=(b, nt),
        in_specs=_moe_out_specs(b, t, d, WIDE_TOKEN_TILE) + [tile, _mod_spec3(5, d, batch0)]
        + [_mod_spec3(0, d, batch0), _mod_spec3(1, d, batch0), _mod_spec3(2, d, batch0)]
        + [_full2(a) for a in params_a]
        + [_mod_spec3(3, d, batch0), _mod_spec3(4, d, batch0)] + [_full2(a) for a in params_b],
        out_specs=out_specs,
        out_shape=out_shape,
        compiler_params=pltpu.CompilerParams(
            dimension_semantics=("arbitrary", "arbitrary"), vmem_limit_bytes=VMEM_LIMIT_BYTES),
        name="sconv",
    )(yg, yg, gates, h_prev, mod_prev, mod, mod, mod, *params_a, mod, mod, *params_b)


def _positions_kernel(eidx_ref, pos_ref, ends_ref, run_ref):
    phase, i = pl.program_id(0), pl.program_id(1)
    tp = eidx_ref.shape[1]
    expert = lax.broadcasted_iota(jnp.int32, (N_EXPERTS, tp), 0)
    onehot = [jnp.where(eidx_ref[k:k + 1, :] == expert, 1.0, 0.0) for k in range(2)]

    @pl.when((phase == 0) & (i == 0))
    def _():
        run_ref[...] = jnp.zeros_like(run_ref)

    @pl.when(phase == 0)
    def _():
        run_ref[...] += jnp.sum(onehot[0] + onehot[1], axis=1, keepdims=True)

    @pl.when((phase == 1) & (i == 0))
    def _():
        padded = jnp.floor((run_ref[...] + (EXPERT_ROW_TILE - 1)) * (1.0 / EXPERT_ROW_TILE)) * EXPERT_ROW_TILE
        ei = lax.broadcasted_iota(jnp.int32, (N_EXPERTS, N_EXPERTS), 0)
        ej = lax.broadcasted_iota(jnp.int32, (N_EXPERTS, N_EXPERTS), 1)
        below = jnp.where(ej < ei, 1.0, 0.0)
        starts = jnp.dot(below, jnp.broadcast_to(padded, (N_EXPERTS, LANES)), precision=lax.Precision.HIGHEST,
                         preferred_element_type=F32)
        ends_ref[...] = (starts + padded).astype(jnp.int32)
        run_ref[...] = starts[:, 0:1]

    @pl.when(phase == 1)
    def _():
        ti = lax.broadcasted_iota(jnp.int32, (tp, tp), 0)
        tj = lax.broadcasted_iota(jnp.int32, (tp, tp), 1)
        before = jnp.where(ti < tj, 1.0, 0.0).astype(BF16)
        run = run_ref[...]
        for k in range(2):
            prefix = jnp.dot(onehot[k].astype(BF16), before, preferred_element_type=F32)
            pos_ref[k:k + 1, :] = jnp.sum(onehot[k] * (prefix + run), axis=0, keepdims=True).astype(jnp.int32)
            run = run + jnp.sum(onehot[k], axis=1, keepdims=True)
        run_ref[...] = run


def _positions(eidx):
    n = eidx.shape[1]
    nt = n // POSITION_TILE
    return pl.pallas_call(
        _positions_kernel,
        grid=(2, nt),
        in_specs=[pl.BlockSpec((2, POSITION_TILE), lambda p, i: (0, i))],
        out_specs=[pl.BlockSpec((2, POSITION_TILE), lambda p, i: (0, i * p)),
                   pl.BlockSpec((N_EXPERTS, LANES), lambda p, i: (0, 0))],
        out_shape=[jax.ShapeDtypeStruct((2, n), jnp.int32), jax.ShapeDtypeStruct((N_EXPERTS, LANES), jnp.int32)],
        scratch_shapes=[pltpu.VMEM((N_EXPERTS, 1), F32)],
        compiler_params=pltpu.CompilerParams(
            dimension_semantics=("arbitrary", "arbitrary"), vmem_limit_bytes=VMEM_LIMIT_BYTES),
        name="positions",
    )(eidx)


def _sc_mesh():
    return plsc.VectorSubcoreMesh(core_axis_name="c", subcore_axis_name="s")


def _sc_worker(n_workers_per_core=SC_SUBCORES):
    return lax.axis_index("c") * n_workers_per_core + lax.axis_index("s")


def _sc_sort_rows(table, pos, n_rows):
    n_slots, width = pos.shape[0], table.shape[1]
    n_tokens = n_slots // 2
    assert n_tokens & (n_tokens - 1) == 0
    per_worker = n_rows // SC_WORKERS
    assert per_worker * SC_WORKERS == n_rows and per_worker % SC_GATHER_ROWS == 0 and n_slots % SC_LANES == 0

    def body(table_hbm, pos_hbm, out_hbm, pos_v, inv_v, idx_v, rows_v, sem):
        base = _sc_worker() * per_worker
        pltpu.sync_copy(pos_hbm, pos_v)

        @pl.loop(0, per_worker, step=SC_LANES)
        def _(j):
            inv_v[pl.ds(j, SC_LANES)] = (base + j + lax.iota(jnp.int32, SC_LANES)) & (n_tokens - 1)

        @pl.loop(0, n_slots, step=SC_LANES)
        def _(s):
            local = pos_v[pl.ds(s, SC_LANES)] - base
            mine = (local >= 0) & (local < per_worker)
            token = (s + lax.iota(jnp.int32, SC_LANES)) & (n_tokens - 1)
            plsc.store_scatter(inv_v, [jnp.where(mine, local, 0)], token, mask=mine)

        @pl.loop(0, per_worker, step=SC_GATHER_ROWS)
        def _(j):
            @pl.loop(0, SC_GATHER_ROWS, step=SC_LANES)
            def _(q):
                idx_v[pl.ds(q, SC_LANES)] = inv_v[pl.ds(j + q, SC_LANES)]

            pltpu.async_copy(table_hbm.at[idx_v], rows_v, sem).wait()
            pltpu.sync_copy(rows_v, out_hbm.at[pl.ds(base + j, SC_GATHER_ROWS)])

    return pl.kernel(
        body, out_type=jax.ShapeDtypeStruct((n_rows, width), table.dtype), mesh=_sc_mesh(),
        scratch_types=[pltpu.VMEM((n_slots,), jnp.int32), pltpu.VMEM((per_worker,), jnp.int32),
                       pltpu.VMEM((SC_GATHER_ROWS,), jnp.int32), pltpu.VMEM((SC_GATHER_ROWS, width), table.dtype),
                       pltpu.SemaphoreType.DMA],
        compiler_params=pltpu.CompilerParams(needs_layout_passes=False),
        name="sc_sort_rows",
    )(table, pos)


def _sc_gather(table, idx):
    n_rows, width = idx.shape[0], table.shape[1]
    per_worker = n_rows // SC_WORKERS
    assert per_worker * SC_WORKERS == n_rows and per_worker % SC_GATHER_ROWS == 0

    def body(table_hbm, idx_hbm, out_hbm, idx_v, rows_v, sem):
        base = _sc_worker() * per_worker

        @pl.loop(0, per_worker, step=SC_GATHER_ROWS)
        def _(j):
            pltpu.sync_copy(idx_hbm.at[pl.ds(base + j, SC_GATHER_ROWS)], idx_v)
            pltpu.async_copy(table_hbm.at[idx_v], rows_v, sem).wait()
            pltpu.sync_copy(rows_v, out_hbm.at[pl.ds(base + j, SC_GATHER_ROWS)])

    return pl.kernel(
        body, out_type=jax.ShapeDtypeStruct((n_rows, width), table.dtype), mesh=_sc_mesh(),
        scratch_types=[pltpu.VMEM((SC_GATHER_ROWS,), jnp.int32), pltpu.VMEM((SC_GATHER_ROWS, width), table.dtype),
                       pltpu.SemaphoreType.DMA],
        name="sc_gather",
    )(table, idx)


def _experts_kernel(layer, te_ref, seg_ref, nxt_ref, xs_ref, wg_hbm, wu_hbm, wd_hbm, ys_ref,
                    wg_f, wu_f, wd_f, wg_bf, wu_bf, wd_bf, sems):
    i = pl.program_id(0)
    expert = te_ref[i]
    held = jnp.minimum(expert, N_EXPERTS - 1)
    slot = seg_ref[i] & 1
    new_segment = (i == 0) | (seg_ref[i] != seg_ref[jnp.maximum(i - 1, 0)])

    def weight_copies(e, s):
        return [pltpu.make_async_copy(w_hbm.at[layer, e], w_f.at[s], sems.at[k, s])
                for k, (w_hbm, w_f) in enumerate(((wg_hbm, wg_f), (wu_hbm, wu_f), (wd_hbm, wd_f)))]

    @pl.when(i == 0)
    def _():
        for cp in weight_copies(held, slot):
            cp.start()

    @pl.when(new_segment)
    def _():
        for cp in weight_copies(held, slot):
            cp.wait()
        wg_bf[...] = wg_f[slot].astype(BF16)
        wu_bf[...] = wu_f[slot].astype(BF16)
        wd_bf[...] = wd_f[slot].astype(BF16)

        @pl.when(nxt_ref[i] >= 0)
        def _():
            for cp in weight_copies(nxt_ref[i], 1 - slot):
                cp.start()

    @pl.when(expert < N_EXPERTS)
    def _():
        sub = xs_ref.shape[0] // MIX_SUB_TILES
        tiles = [slice(s * sub, (s + 1) * sub) for s in range(MIX_SUB_TILES)]
        xs = [_unpack_bf16_pairs(xs_ref[rs, :]).astype(BF16) for rs in tiles]
        hes = []
        for x in xs:
            gate, up = _dot(x, wg_bf[...]), _dot(x, wu_bf[...])
            hes.append((gate * _sigmoid(gate) * up).astype(BF16))
        for rs, he in zip(tiles, hes):
            ys_ref[rs, :] = _pack_bf16_pairs(_dot(he, wd_bf[...]))


def _experts(tile_expert, xs, layer, w_gate, w_up, w_down):
    n_rows, half = xs.shape
    _, n_e, d, de = w_gate.shape
    held = jnp.minimum(tile_expert, n_e - 1)
    segment = jnp.cumsum(jnp.concatenate([jnp.zeros((1,), jnp.int32), (held[1:] != held[:-1]).astype(jnp.int32)]))
    after = jnp.sum(held[None, :] <= held[:, None], axis=1)
    next_expert = jnp.where(after < held.shape[0], held[jnp.minimum(after, held.shape[0] - 1)], -1).astype(jnp.int32)

    rows = pl.BlockSpec((EXPERT_ROW_TILE, half), lambda i, te, seg, nxt: (i, 0))
    hbm = pl.BlockSpec(memory_space=pl.ANY)
    return pl.pallas_call(
        functools.partial(_experts_kernel, layer),
        grid_spec=pltpu.PrefetchScalarGridSpec(
            num_scalar_prefetch=3,
            grid=(n_rows // EXPERT_ROW_TILE,),
            in_specs=[rows, hbm, hbm, hbm],
            out_specs=rows,
            scratch_shapes=[pltpu.VMEM((2, d, de), F32), pltpu.VMEM((2, d, de), F32), pltpu.VMEM((2, de, d), F32),
                            pltpu.VMEM((d, de), BF16), pltpu.VMEM((d, de), BF16), pltpu.VMEM((de, d), BF16),
                            pltpu.SemaphoreType.DMA((3, 2))]),
        out_shape=jax.ShapeDtypeStruct((n_rows, half), jnp.int32),
        compiler_params=pltpu.CompilerParams(
            dimension_semantics=("arbitrary",), vmem_limit_bytes=VMEM_LIMIT_BYTES),
        name="experts",
    )(tile_expert, segment.astype(jnp.int32), next_expert, xs, w_gate, w_up, w_down)


def _final_kernel(y0_ref, y1_ref, gate_ref, h1_ref, gt_ref, fg_ref, *out_refs):
    o_ref = out_refs[-1]
    h2 = _moe_residual(y0_ref, y1_ref, gate_ref, h1_ref[0], gt_ref[0])
    ms = jnp.mean(h2 * h2, axis=-1, keepdims=True)
    o_ref[0] = h2 * lax.rsqrt(ms + RMS_EPS) * fg_ref[...]


def _moe_out_specs(b, t, d, tile=TOKEN_TILE):
    nt = t // tile
    return [pl.BlockSpec((tile, d // 2), lambda i, j: (i * nt + j, 0)),
            pl.BlockSpec((tile, d // 2), lambda i, j: (b * nt + i * nt + j, 0)),
            pl.BlockSpec((tile, 2), lambda i, j: (i * nt + j, 0))]


def _final(yg, gates, h1, mod, final_g, batch0, n_batch, out_prev):
    b, t, d = h1.shape
    tile = pl.BlockSpec((1, WIDE_TOKEN_TILE, d), lambda i, j: (i, j, 0))
    in_specs = _moe_out_specs(b, t, d, WIDE_TOKEN_TILE) + [tile, _mod_spec3(5, d, batch0),
                                                           pl.BlockSpec((1, d), lambda i, j: (0, 0))]
    args = [yg, yg, gates, h1, mod, final_g]
    aliases = {}
    if out_prev is not None:
        in_specs.append(pl.BlockSpec(memory_space=pl.ANY))
        aliases = {len(args): 0}
        args.append(out_prev)
    return pl.pallas_call(
        _final_kernel,
        grid=(b, t // WIDE_TOKEN_TILE),
        in_specs=in_specs,
        out_specs=pl.BlockSpec((1, WIDE_TOKEN_TILE, d), lambda i, j: (i + batch0, j, 0)),
        out_shape=jax.ShapeDtypeStruct((n_batch, t, d), F32),
        input_output_aliases=aliases,
        compiler_params=pltpu.CompilerParams(
            dimension_semantics=("arbitrary", "arbitrary"), vmem_limit_bytes=VMEM_LIMIT_BYTES),
        name="final",
    )(*args)


def _moe(hm, eidx, gates, layer, w_gate, w_up, w_down):
    n_slots = 2 * hm.shape[0]
    n_rows = n_slots + N_EXPERTS * EXPERT_ROW_TILE
    pos, ends = _positions(eidx)
    pos = pos.reshape(n_slots)
    tile_start = jnp.arange(n_rows // EXPERT_ROW_TILE, dtype=jnp.int32) * EXPERT_ROW_TILE
    tile_expert = jnp.sum(tile_start[:, None] >= ends[None, :, 0], axis=1).astype(jnp.int32)
    xs = _sc_sort_rows(hm, pos, n_rows)
    ys = _experts(tile_expert, xs, layer, w_gate, w_up, w_down)
    return _sc_gather(ys, pos), gates.T


def kernel(x, c, ctx, c_ctx, ada_w, ada_b, norm_g, rw_mu, rw_w_rkv, rw_w0, rw_w1, rw_w2, rw_a0, rw_a1, rw_a2, rw_g1, rw_g2, rw_k_k, rw_k_a, rw_r_k, rw_gn_w, rw_gn_b, rw_w_o, sc_w_in, sc_conv, sc_w_out, router_w, router_b, moe_w_gate, moe_w_up, moe_w_down, final_g):
    b, t, d = x.shape
    ctx_len = ctx.shape[1]
    depth = ada_w.shape[0]
    assert d == D_MODEL and depth == 2 and t % POSITION_TILE == 0

    mod_rows = 16
    cc = jnp.concatenate([c, c_ctx[None, :], jnp.zeros((mod_rows - b - 1, d), F32)], axis=0)
    mod = _ada(cc, ada_w, ada_b).reshape(depth, mod_rows, 1, 6 * d)

    row = lambda a: a.reshape(1, d)
    router_wt = jnp.pad(router_w, ((0, 0), (0, LANES - N_EXPERTS)))
    router_bc = router_b.reshape(N_EXPERTS, 1)

    w1 = jnp.concatenate([rw_w1[0, 0], rw_w1[0, 1]], axis=1).astype(BF16)
    a1 = jnp.concatenate([rw_a1[0, 0], rw_a1[0, 1]], axis=1).astype(BF16)

    def pad_dirs(w):
        z = jnp.zeros_like(w[0])
        return jnp.stack([jnp.concatenate([w[0], z], axis=0), jnp.concatenate([z, w[1]], axis=0)]).astype(BF16)

    assert b % N_STREAMS == 0
    nb = b // N_STREAMS
    pre_params = (row(norm_g[0, 0]), rw_mu[0], rw_w_rkv[0].astype(BF16), w1, pad_dirs(rw_w2[0]), rw_w0[0], a1,
                  pad_dirs(rw_a2[0]), rw_a0[0], rw_g1[0].astype(BF16), rw_g2[0].astype(BF16), row(rw_k_k[0]),
                  row(rw_k_a[0]), row(rw_r_k[0]))
    w_o, w_in, w_out = rw_w_o[0].astype(BF16), sc_w_in[0].astype(BF16), sc_w_out[0].astype(BF16)
    out = None
    for batch0 in range(0, b, nb):
        r, v, g, bonus, k, kk, cs0, cs1, a0, a1 = _rwkv_pre(ctx, x, mod[0], *pre_params, batch0, nb)
        yf, yr = _wkv(row(rw_k_a[0]), r, v, k, kk, cs0, a0, cs1, a1, ctx_len)
        h1, hm, eidx, gates = _readout(yf, yr, bonus, g, x, mod[0], row(rw_gn_w[0]), row(rw_gn_b[0]),
                                       w_o, row(norm_g[0, 1]), router_wt, router_bc, batch0, nb)
        yg, gates = _moe(hm, eidx, gates, 0, moe_w_gate, moe_w_up, moe_w_down)
        h1, hm, eidx, gates = _sconv(yg, gates, h1, mod[0], mod[1], row(norm_g[1, 0]), w_in, sc_conv[0], w_out,
                                     row(norm_g[1, 1]), router_wt, router_bc, batch0)
        yg, gates = _moe(hm, eidx, gates, 1, moe_w_gate, moe_w_up, moe_w_down)
        out = _final(yg, gates, h1, mod[1], row(final_g), batch0, b, out)
    return out
```

```python
import functools

import jax
import jax.numpy as jnp
from jax import lax
from jax.experimental import pallas as pl
from jax.experimental.pallas import tpu as pltpu
from jax.experimental.pallas import tpu_sc as plsc

F32 = jnp.float32
BF16 = jnp.bfloat16

D_MODEL = 1024
HEAD_DIM = 64
N_HEADS = D_MODEL // HEAD_DIM
LANES = 128
N_PAIRS = D_MODEL // LANES
CHUNK = 64
N_EXPERTS = 16
EXPERTS_PER_GROUP = 4
N_GROUPS = N_EXPERTS // EXPERTS_PER_GROUP
RMS_EPS = 1e-6
GN_EPS = 64e-5
L2_EPS = 1e-12

TOKEN_TILE = 256
WIDE_TOKEN_TILE = 512
MIX_SUB_TILES = 2
PRE_SUB_TILES = 2
EXPERT_ROW_TILE = 512
POSITION_TILE = 1024
SC_CORES, SC_SUBCORES, SC_LANES = 2, 16, 16
SC_WORKERS = SC_CORES * SC_SUBCORES
N_STREAMS = 2
SC_GATHER_ROWS = 128
PAIRS_PER_STEP = 8
STAGE_LAG = 2
CHUNKS_PER_STEP = 4
ADA_COL_TILE = 1536
VMEM_LIMIT_BYTES = 56 * 1024 * 1024


def _sigmoid(x):
    return 1.0 / (1.0 + jnp.exp(-x))


def _mm(a, b):
    return jnp.dot(a.astype(BF16), b.astype(BF16), preferred_element_type=F32)


def _mm_nt(a, b):
    return lax.dot_general(a.astype(BF16), b.astype(BF16), (((1,), (1,)), ((), ())),
                           preferred_element_type=F32)


def _split2(x):
    hi = x.astype(BF16)
    return hi, (x - hi.astype(F32)).astype(BF16)


def _head_sum(x):
    rows = x.shape[0]
    left = lax.broadcasted_iota(jnp.int32, (rows, LANES), 1) < HEAD_DIM
    outs = []
    for j in range(N_PAIRS):
        xb = x[:, LANES * j:LANES * (j + 1)]
        sa = jnp.sum(jnp.where(left, xb, 0.0), axis=-1, keepdims=True)
        sb = jnp.sum(jnp.where(left, 0.0, xb), axis=-1, keepdims=True)
        outs.append(jnp.where(left, sa, sb))
    return jnp.concatenate(outs, axis=-1)


def _rms_modulate(x, g, shift, scale):
    ms = jnp.mean(x * x, axis=-1, keepdims=True)
    xn = x * lax.rsqrt(ms + RMS_EPS) * g
    return xn * (1.0 + scale) + shift


def _row_neighbours(x, row_len):
    rows = x.shape[0]
    pos = lax.broadcasted_iota(jnp.int32, x.shape, 0) & (row_len - 1)
    prev = jnp.where(pos == 0, 0.0, pltpu.roll(x, 1, 0))
    nxt = jnp.where(pos == row_len - 1, 0.0, pltpu.roll(x, rows - 1, 0))
    return prev, nxt


def _ada_kernel(c_ref, w_ref, b_ref, o_ref):
    c = c_ref[...]
    s = c * _sigmoid(c)
    o_ref[0] = _mm(s, w_ref[0]) + b_ref[0]


def _ada(cc, ada_w, ada_b):
    depth, d, n = ada_w.shape
    rows = cc.shape[0]
    return pl.pallas_call(
        _ada_kernel,
        grid=(depth, n // ADA_COL_TILE),
        in_specs=[
            pl.BlockSpec((rows, d), lambda l, j: (0, 0)),
            pl.BlockSpec((1, d, ADA_COL_TILE), lambda l, j: (l, 0, j)),
            pl.BlockSpec((1, 1, ADA_COL_TILE), lambda l, j: (l, 0, j)),
        ],
        out_specs=pl.BlockSpec((1, rows, ADA_COL_TILE), lambda l, j: (l, 0, j)),
        out_shape=jax.ShapeDtypeStruct((depth, rows, n), F32),
        compiler_params=pltpu.CompilerParams(
            dimension_semantics=("arbitrary", "arbitrary"), vmem_limit_bytes=VMEM_LIMIT_BYTES),
        name="ada",
    )(cc, ada_w, ada_b.reshape(depth, 1, n))


def _rwkv_pre_kernel(ctx_len, ctx_ref, x_ref, sh_ref, sc_ref, ng_ref, mu_ref, wrkv_ref, w1_ref, w2_ref, w0_ref,
                     a1_ref, a2_ref, a0_ref, g1_ref, g2_ref, kk_ref, ka_ref, rk_ref,
                     r_out, v_out, g_out, bon_out, k_out, kk_out, cs0_out, cs1_out, a0_out, a1_out):
    j = pl.program_id(1)
    row_len = jnp.where(j == 0, ctx_len, CHUNK)
    xin = jnp.where(j == 0, ctx_ref[0], x_ref[0])
    xn = _rms_modulate(xin, ng_ref[...], sh_ref[0], sc_ref[0])
    prev, nxt = _row_neighbours(xn, row_len)
    xx = 0.5 * (prev + nxt) - xn

    sub = xn.shape[0] // PRE_SUB_TILES
    projected = []
    for s in range(PRE_SUB_TILES):
        rs = slice(s * sub, (s + 1) * sub)
        xn_s, xx_s = xn[rs], xx[rs]

        def mix(i, xn_s=xn_s, xx_s=xx_s):
            return xn_s + xx_s * mu_ref[i:i + 1, :]

        r = _mm(mix(0), wrkv_ref[0])
        k = _mm(mix(1), wrkv_ref[1])
        v = _mm(mix(2), wrkv_ref[2])
        g = _mm(_sigmoid(_mm(mix(5), g1_ref[...])), g2_ref[...])
        wl = jnp.tanh(_mm(mix(3), w1_ref[...]))
        al = _mm(mix(4), a1_ref[...])
        projected.append((rs, r, k, v, g, [_mm(wl, w2_ref[p]) for p in range(2)],
                          [_mm(al, a2_ref[p]) for p in range(2)]))

    ti = lax.broadcasted_iota(jnp.int32, (sub, sub), 0)
    tj = lax.broadcasted_iota(jnp.int32, (sub, sub), 1)
    same_chunk = (ti ^ tj) < CHUNK
    dir_outs = ((cs0_out, a0_out), (cs1_out, a1_out))
    for rs, r, k, v, g, zs, a_logits in projected:
        r_out[0, rs, :] = r.astype(r_out.dtype)
        v_out[0, rs, :] = v.astype(v_out.dtype)
        g_out[0, rs, :] = g.astype(g_out.dtype)
        k_out[0, rs, :] = k.astype(k_out.dtype)
        kk = k * kk_ref[...]
        kk_out[0, rs, :] = (kk * lax.rsqrt(jnp.maximum(_head_sum(kk * kk), L2_EPS * L2_EPS))).astype(kk_out.dtype)
        a_sum = None
        for p, (cs_out, a_out) in enumerate(dir_outs):
            z = w0_ref[p:p + 1, :] + zs[p]
            lw = -jnp.exp(-0.5) * _sigmoid(z)
            tri = jnp.where(same_chunk & ((tj >= ti) if p else (tj <= ti)), 1.0, 0.0).astype(BF16)
            hi, lo = _split2(lw)
            cs_out[0, rs, :] = _dot(tri, hi) + _dot(tri, lo)
            a = _sigmoid(a0_ref[p:p + 1, :] + a_logits[p])
            a_out[0, rs, :] = a.astype(a_out.dtype)
            a_sum = a if a_sum is None else a_sum + a
        k_dirs = k * (2.0 + (a_sum - 2.0) * ka_ref[...])
        bon_out[0, rs, :] = (_head_sum(r * rk_ref[...] * k_dirs) * v).astype(bon_out.dtype)


def _rwkv_pre(ctx, x, mod, norm_g, mu, w_rkv, w1, w2, w0, a1, a2, a0, g1, g2, k_k, k_a, r_k, batch0, b):
    n_batch, t, d = x.shape
    ctx_len = ctx.shape[1]
    tt = ctx_len + t
    assert ctx_len == TOKEN_TILE and t % TOKEN_TILE == 0

    def mod_spec(part):
        return pl.BlockSpec((1, 1, d), lambda i, j: (jnp.where(j == 0, n_batch, i + batch0), 0, part))

    def full(a):
        nd = a.ndim
        return pl.BlockSpec(a.shape, lambda i, j: (0,) * nd)

    tile = pl.BlockSpec((1, TOKEN_TILE, d), lambda i, j: (i, j, 0))
    latent = pl.BlockSpec((1, TOKEN_TILE, d), lambda i, j: (i, jnp.maximum(j - 1, 0), 0))
    params = (norm_g, mu, w_rkv, w1, w2, w0, a1, a2, a0, g1, g2, k_k, k_a, r_k)
    out_dtypes = (BF16,) * 6 + (F32, F32) + (BF16,) * 2
    out_specs = [tile, tile, latent, latent] + [tile] * 6
    return pl.pallas_call(
        functools.partial(_rwkv_pre_kernel, ctx_len),
        grid=(b, tt // TOKEN_TILE),
        in_specs=[pl.BlockSpec((1, TOKEN_TILE, d), lambda i, j: (i + batch0, 0, 0)),
                  pl.BlockSpec((1, TOKEN_TILE, d), lambda i, j: (i + batch0, jnp.maximum(j - 1, 0), 0)),
                  mod_spec(0), mod_spec(1)] + [full(a) for a in params],
        out_specs=out_specs,
        out_shape=[jax.ShapeDtypeStruct((b, t if spec is latent else tt, d), dt)
                   for spec, dt in zip(out_specs, out_dtypes)],
        compiler_params=pltpu.CompilerParams(
            dimension_semantics=("arbitrary", "arbitrary"), vmem_limit_bytes=VMEM_LIMIT_BYTES),
        name="rwkv_pre",
    )(ctx, x, mod, mod, *params)


def _expand(x, left):
    return jnp.concatenate([jnp.where(left, x, 0.0), jnp.where(left, 0.0, x)], axis=0)


def _dot(a, b):
    return jnp.dot(a, b, preferred_element_type=F32)


def _chunk_steps(chains, group_size):
    assert CHUNK == HEAD_DIM
    c = CHUNK
    lane = lax.broadcasted_iota(jnp.int32, (c, LANES), 1)
    left = lane < HEAD_DIM
    tt = lax.broadcasted_iota(jnp.int32, (c, LANES), 0)
    jj = lane & (c - 1)
    diag = jj == tt
    tri = {False: (jj < tt, jj <= tt), True: (jj > tt, jj >= tt)}

    def bd(x):
        return _expand(x, left).astype(BF16)

    def fold_t(x):
        xt = _expand(x, left).T
        return xt[:c] + xt[c:]

    def rows(*xs):
        return jnp.concatenate(xs, axis=0).astype(BF16)

    nt_dims = (((1,), (1,)), ((), ()))

    def s_prep(q):
        cs, r, v, k, kk, a, ka, h, reverse = q.pop("chain")
        tot = cs[0:1, :] if reverse else cs[c - 1:c, :]
        e_pos, e_neg, e_rem = jnp.exp(cs), jnp.exp(-cs), jnp.exp(tot - cs)
        first = tt == (c - 1 if reverse else 0)
        cs_prev = jnp.where(first, 0.0, pltpu.roll(cs, c - 1 if reverse else 1, 0))
        at, rt = -kk * jnp.exp(cs_prev), r * e_pos
        kd, be = k * (1.0 + (a - 1.0) * ka), kk * a
        q.update(a_e=bd(at), rt=rt, v_e=bd(v), g_tot=jnp.exp(tot), h=h, tri=tri[reverse])
        q["hat_t"] = jnp.concatenate([fold_t(be * e_rem), fold_t(kd * e_rem)], axis=1)
        q["sc"] = lax.dot_general(rows(at, rt), jnp.concatenate([bd(be * e_neg), bd(kd * e_neg)], axis=0),
                                  nt_dims, preferred_element_type=F32)

    def s_mask(q):
        (strict, incl), sc = q["tri"], q["sc"]
        q["n"] = jnp.where(strict, sc[:c, :LANES], 0.0)
        a_ak = jnp.where(strict, sc[:c, LANES:], 0.0)
        q["a_rb"] = jnp.where(incl, sc[c:, :LANES], 0.0)
        a_rk = jnp.where(incl, sc[c:, LANES:], 0.0)
        q["bh_t"] = q["hat_t"][:, :LANES]
        q["vv"] = _dot(rows(a_ak, q["hat_t"][:, LANES:], a_rk), q["v_e"])

    def s_square(q):
        q["t"] = jnp.where(diag, 1.0, 0.0) + q["n"]
        q["p"] = _dot(q["n"].astype(BF16), bd(q["n"]))

    def s_level(q):
        x = _dot(rows(q["p"], q["t"]), bd(q["p"]))
        q["p"] = x[:c]
        q["t"] = q["t"] + x[c:]

    def s_last_level(q):
        q["t"] = q["t"] + _dot(q["t"].astype(BF16), bd(q["p"]))

    def s_solve(q):
        xu = _dot(q["t"].astype(BF16), jnp.concatenate([q["a_e"], bd(q["vv"][:c])], axis=1))
        q["au_e"] = jnp.concatenate([bd(xu[:, :LANES]), bd(xu[:, LANES:])], axis=1)

    def s_affine(q):
        z = _dot(rows(q["bh_t"], q["a_rb"]), q["au_e"])
        q["m"] = z[:c, :LANES] + jnp.where(diag, q["g_tot"], 0.0)
        q["g"] = z[:c, LANES:] + q["vv"][c:2 * c]
        q["r_hat"] = q["rt"] + z[c:, :LANES]
        q["yi"] = z[c:, LANES:] + q["vv"][2 * c:]

    n_levels = (c // 4).bit_length() - 1
    stages = [s_prep, s_mask, s_square] + [s_level] * n_levels + [s_last_level, s_solve, s_affine]
    qs = [dict(chain=ch) for ch in chains]
    groups = [qs[i:i + group_size] for i in range(0, len(qs), group_size)]
    for tau in range(len(stages) + (len(groups) - 1) * STAGE_LAG):
        for gi, group in enumerate(groups):
            si = tau - gi * STAGE_LAG
            if 0 <= si < len(stages):
                for q in group:
                    stages[si](q)
    outs = []
    for q in qs:
        h = outs[q["h"]][1] if isinstance(q["h"], int) else q["h"]
        o = _dot(rows(q["r_hat"], q["m"]), bd(h))
        outs.append((o[:c] + q["yi"], o[c:] + q["g"]))
    return outs


def _wkv_kernel(ka_ref, rf_ref, vf_ref, kf_ref, kkf_ref, csf_ref, af_ref,
                rr_ref, vr_ref, kr_ref, kkr_ref, csr_ref, ar_ref,
                yf_ref, yr_ref, hf_ref, hr_ref):
    @pl.when(pl.program_id(2) == 0)
    def _():
        hf_ref[...] = jnp.zeros_like(hf_ref)
        hr_ref[...] = jnp.zeros_like(hr_ref)

    dirs = ((rf_ref, vf_ref, kf_ref, kkf_ref, csf_ref, af_ref, yf_ref, hf_ref),
            (rr_ref, vr_ref, kr_ref, kkr_ref, csr_ref, ar_ref, yr_ref, hr_ref))
    chains, dests = [], []
    per_chunk = 2 * PAIRS_PER_STEP
    for u in range(CHUNKS_PER_STEP):
        for p in range(PAIRS_PER_STEP):
            sl = slice(p * LANES, (p + 1) * LANES)
            for reverse, (r_ref, v_ref, k_ref, kk_ref, cs_ref, a_ref, y_ref, h_ref) in enumerate(dirs):
                at = CHUNKS_PER_STEP - 1 - u if reverse else u
                tm = slice(at * CHUNK, (at + 1) * CHUNK)
                h = h_ref[p] if u == 0 else len(chains) - per_chunk
                chains.append((cs_ref[0, tm, sl], r_ref[0, tm, sl].astype(F32), v_ref[0, tm, sl].astype(F32),
                               k_ref[0, tm, sl].astype(F32), kk_ref[0, tm, sl].astype(F32),
                               a_ref[0, tm, sl].astype(F32), ka_ref[:, sl], h, bool(reverse)))
                dests.append((y_ref, tm, sl, h_ref if u == CHUNKS_PER_STEP - 1 else None, p))
    for (y, h_new), (y_ref, tm, sl, h_ref, p) in zip(_chunk_steps(chains, per_chunk), dests):
        y_ref[0, tm, sl] = y
        if h_ref is not None:
            h_ref[p] = h_new


def _wkv(k_a, r, v, k, kk, cs0, a0, cs1, a1, ctx_len):
    b, tt, d = r.shape
    step_rows = CHUNKS_PER_STEP * CHUNK
    n_steps, n_ctx = tt // step_rows, ctx_len // step_rows
    assert n_steps * step_rows == tt and n_ctx * step_rows == ctx_len
    width = PAIRS_PER_STEP * LANES

    def fwd_map(i, j, s):
        return (i, s, j)

    def rev_map(i, j, s):
        return (i, jnp.where(s < n_ctx, n_ctx - 1 - s, n_steps - 1 - (s - n_ctx)), j)

    n_lat = n_steps - n_ctx

    def fwd_out(i, j, s):
        return (i, jnp.maximum(s - n_ctx, 0), j)

    def rev_out(i, j, s):
        return (i, jnp.where(s < n_ctx, n_lat - 1, n_lat - 1 - (s - n_ctx)), j)

    fwd = pl.BlockSpec((1, step_rows, width), fwd_map)
    rev = pl.BlockSpec((1, step_rows, width), rev_map)
    return pl.pallas_call(
        _wkv_kernel,
        grid=(b, d // width, n_steps),
        in_specs=[pl.BlockSpec((1, width), lambda i, j, s: (0, j))] + [fwd] * 6 + [rev] * 6,
        out_specs=[pl.BlockSpec((1, step_rows, width), fwd_out), pl.BlockSpec((1, step_rows, width), rev_out)],
        out_shape=[jax.ShapeDtypeStruct((b, tt - ctx_len, d), F32)] * 2,
        scratch_shapes=[pltpu.VMEM((PAIRS_PER_STEP, HEAD_DIM, LANES), F32)] * 2,
        compiler_params=pltpu.CompilerParams(
            dimension_semantics=("arbitrary", "arbitrary", "arbitrary"), vmem_limit_bytes=VMEM_LIMIT_BYTES),
        name="wkv",
    )(k_a, r, v, k, kk, cs0, a0, r, v, k, kk, cs1, a1)


def _pack_bf16_pairs(x):
    half = x.shape[1] // 2
    lo = pltpu.bitcast(x[:, :half].astype(BF16).astype(F32), jnp.int32)
    hi = pltpu.bitcast(x[:, half:].astype(BF16).astype(F32), jnp.int32)
    return lax.shift_right_logical(lo, jnp.int32(16)) | (hi & jnp.int32(-65536))


def _unpack_bf16_pairs(w):
    lo = pltpu.bitcast(lax.shift_left(w, jnp.int32(16)), F32)
    hi = pltpu.bitcast(w & jnp.int32(-65536), F32)
    return jnp.concatenate([lo, hi], axis=1)


def _route(h1, ng, shift, scale, rwt_ref, rb_ref, hm_out, eidx_out, gate_out, rs=slice(None)):
    hm = _rms_modulate(h1, ng, shift, scale)
    hm_out[rs, :] = _pack_bf16_pairs(hm)
    w_hi, w_mid = _split2(rwt_ref[...])
    h_hi, h_mid = _split2(hm)
    logits_t = _dot(h_hi, w_hi) + _dot(h_mid, w_hi) + _dot(h_hi, w_mid)
    logits = logits_t.T[:N_EXPERTS]
    s = _sigmoid(logits)
    sel = s + rb_ref[...]
    assert EXPERTS_PER_GROUP == 4 and N_GROUPS == 4
    cands = []
    for g in range(N_GROUPS):
        m = [sel[e:e + 1, :] for e in range(g * EXPERTS_PER_GROUP, (g + 1) * EXPERTS_PER_GROUP)]
        sg = [s[e:e + 1, :] for e in range(g * EXPERTS_PER_GROUP, (g + 1) * EXPERTS_PER_GROUP)]
        pairs = [m[i] + m[k] for i in range(4) for k in range(i + 1, 4)]
        score = jnp.maximum(jnp.maximum(jnp.maximum(pairs[0], pairs[1]), jnp.maximum(pairs[2], pairs[3])),
                            jnp.maximum(pairs[4], pairs[5]))
        chosen = []
        for i in range(4):
            ahead = [((m[k] >= m[i]) if k < i else (m[k] > m[i])).astype(jnp.int32) for k in range(4) if k != i]
            chosen.append(ahead[0] + ahead[1] + ahead[2] < 2)
        base = g * EXPERTS_PER_GROUP
        lo_idx = jnp.where(chosen[0], base, jnp.where(chosen[1], base + 1, base + 2))
        lo_gate = jnp.where(chosen[0], sg[0], jnp.where(chosen[1], sg[1], sg[2]))
        hi_idx = jnp.where(chosen[3], base + 3, jnp.where(chosen[2], base + 2, base + 1))
        hi_gate = jnp.where(chosen[3], sg[3], jnp.where(chosen[2], sg[2], sg[1]))
        cands.append((score, lo_idx, hi_idx, lo_gate, hi_gate))

    def better(x, y):
        win = y[0] > x[0]
        return tuple(jnp.where(win, yv, xv) for xv, yv in zip(x, y))

    _, e_lo, e_hi, g_lo, g_hi = better(better(cands[0], cands[1]), better(cands[2], cands[3]))
    eidx_out[:, rs] = jnp.concatenate([e_lo, e_hi], axis=0)
    gate_out[:, rs] = jnp.concatenate([g_lo, g_hi], axis=0) / (g_lo + g_hi)


def _readout_kernel(yf_ref, yr_ref, bon_ref, g_ref, h_ref, gt_ref, gnw_ref, gnb_ref, wo_ref,
                    ng_ref, sh_ref, sc_ref, rwt_ref, rb_ref, h1_out, hm_out, eidx_out, gate_out):
    sub = h_ref.shape[1] // MIX_SUB_TILES
    tiles = [slice(s * sub, (s + 1) * sub) for s in range(MIX_SUB_TILES)]
    gated, h1s = [], []
    for rs in tiles:
        y = yf_ref[0, rs, :] + yr_ref[0, rs, :] + bon_ref[0, rs, :].astype(F32)
        mean = _head_sum(y) * (1.0 / HEAD_DIM)
        yc = y - mean
        var = _head_sum(yc * yc) * (1.0 / HEAD_DIM)
        yn = yc * lax.rsqrt(var + GN_EPS) * gnw_ref[...] + gnb_ref[...]
        gated.append((yn * g_ref[0, rs, :].astype(F32)).astype(BF16))
    for rs, lhs in zip(tiles, gated):
        h1 = h_ref[0, rs, :] + gt_ref[0] * _dot(lhs, wo_ref[...])
        h1_out[0, rs, :] = h1
        h1s.append(h1)
    for rs, h1 in zip(tiles, h1s):
        _route(h1, ng_ref[...], sh_ref[0], sc_ref[0], rwt_ref, rb_ref, hm_out, eidx_out, gate_out, rs)


def _mod_spec3(part, d, batch0=0):
    return pl.BlockSpec((1, 1, d), lambda i, j: (i + batch0, 0, part))


def _route_out_specs(b, t, d, tile=TOKEN_TILE):
    nt = t // tile
    specs = [pl.BlockSpec((1, tile, d), lambda i, j: (i, j, 0)),
             pl.BlockSpec((tile, d // 2), lambda i, j: (i * nt + j, 0)),
             pl.BlockSpec((2, tile), lambda i, j: (0, i * nt + j)),
             pl.BlockSpec((2, tile), lambda i, j: (0, i * nt + j))]
    shapes = [jax.ShapeDtypeStruct((b, t, d), F32), jax.ShapeDtypeStruct((b * t, d // 2), jnp.int32),
              jax.ShapeDtypeStruct((2, b * t), jnp.int32), jax.ShapeDtypeStruct((2, b * t), F32)]
    return specs, shapes


def _full2(a):
    nd = a.ndim
    return pl.BlockSpec(a.shape, lambda i, j: (0,) * nd)


def _readout(yf, yr, bonus, g, h, mod, gn_w, gn_b, w_o, norm_g, router_wt, router_b, batch0, b):
    _, t, d = h.shape
    nt = t // WIDE_TOKEN_TILE
    local = pl.BlockSpec((1, WIDE_TOKEN_TILE, d), lambda i, j: (i, j, 0))
    tile = pl.BlockSpec((1, WIDE_TOKEN_TILE, d), lambda i, j: (i + batch0, j, 0))
    params_a = (gn_w, gn_b, w_o, norm_g)
    params_b = (router_wt, router_b)
    out_specs, out_shape = _route_out_specs(b, t, d, WIDE_TOKEN_TILE)
    return pl.pallas_call(
        _readout_kernel,
        grid=(b, nt),
        in_specs=[local] * 4 + [tile, _mod_spec3(2, d, batch0)] + [_full2(a) for a in params_a]
        + [_mod_spec3(3, d, batch0), _mod_spec3(4, d, batch0)] + [_full2(a) for a in params_b],
        out_specs=out_specs,
        out_shape=out_shape,
        compiler_params=pltpu.CompilerParams(
            dimension_semantics=("arbitrary", "arbitrary"), vmem_limit_bytes=VMEM_LIMIT_BYTES),
        name="readout",
    )(yf, yr, bonus, g, h, mod, *params_a, mod, mod, *params_b)


def _moe_residual(y0_ref, y1_ref, gate_ref, h1, gt, rs=slice(None)):
    gates = gate_ref[rs, :]
    y = gates[:, 0:1] * _unpack_bf16_pairs(y0_ref[rs, :]) + gates[:, 1:2] * _unpack_bf16_pairs(y1_ref[rs, :])
    return h1 + gt * y


def _sconv_kernel(y0_ref, y1_ref, gate_ref, hp_ref, gtp_ref, sh1_ref, sc1_ref, gt_ref, ng1_ref, win_ref, cw_ref,
                  wout_ref, ng2_ref, sh2_ref, sc2_ref, rwt_ref, rb_ref, h1_out, hm_out, eidx_out, gate_out):
    d = hp_ref.shape[2]
    sub = hp_ref.shape[1] // MIX_SUB_TILES
    tiles = [slice(s * sub, (s + 1) * sub) for s in range(MIX_SUB_TILES)]
    hs, xns, gated, h1s = [], [], [], []
    for rs in tiles:
        h = _moe_residual(y0_ref, y1_ref, gate_ref, hp_ref[0, rs, :], gtp_ref[0], rs)
        hs.append(h)
        xns.append(_rms_modulate(h, ng1_ref[...], sh1_ref[0], sc1_ref[0]).astype(BF16))
    for xn in xns:
        bg = _dot(xn, win_ref[:, 0:d])
        u = _dot(xn, win_ref[:, d:2 * d]) * _dot(xn, win_ref[:, 2 * d:3 * d])
        prev, nxt = _row_neighbours(u, CHUNK)
        conv = cw_ref[0:1, :] * prev + cw_ref[1:2, :] * u + cw_ref[2:3, :] * nxt
        gated.append((bg * conv).astype(BF16))
    for rs, h, lhs in zip(tiles, hs, gated):
        h1 = h + gt_ref[0] * _dot(lhs, wout_ref[...])
        h1_out[0, rs, :] = h1
        h1s.append(h1)
    for rs, h1 in zip(tiles, h1s):
        _route(h1, ng2_ref[...], sh2_ref[0], sc2_ref[0], rwt_ref, rb_ref, hm_out, eidx_out, gate_out, rs)


def _sconv(yg, gates, h_prev, mod_prev, mod, norm_g1, w_in, conv_w, w_out, norm_g2, router_wt, router_b, batch0):
    b, t, d = h_prev.shape
    nt = t // WIDE_TOKEN_TILE
    tile = pl.BlockSpec((1, WIDE_TOKEN_TILE, d), lambda i, j: (i, j, 0))
    params_a = (norm_g1, w_in, conv_w, w_out, norm_g2)
    params_b = (router_wt, router_b)
    out_specs, out_shape = _route_out_specs(b, t, d, WIDE_TOKEN_TILE)
    return pl.pallas_call(
        _sconv_kernel,
        grid=(b, nt),
        in_specs=_moe_out_specs(b, t, d, WIDE_TOKEN_TILE) + [tile, _mod_spec3(5, d, batch0)]
        + [_mod_spec3(0, d, batch0), _mod_spec3(1, d, batch0), _mod_spec3(2, d, batch0)]
        + [_full2(a) for a in params_a]
        + [_mod_spec3(3, d, batch0), _mod_spec3(4, d, batch0)] + [_full2(a) for a in params_b],
        out_specs=out_specs,
        out_shape=out_shape,
        compiler_params=pltpu.CompilerParams(
            dimension_semantics=("arbitrary", "arbitrary"), vmem_limit_bytes=VMEM_LIMIT_BYTES),
        name="sconv",
    )(yg, yg, gates, h_prev, mod_prev, mod, mod, mod, *params_a, mod, mod, *params_b)


def _positions_kernel(eidx_ref, pos_ref, ends_ref, run_ref):
    phase, i = pl.program_id(0), pl.program_id(1)
    tp = eidx_ref.shape[1]
    expert = lax.broadcasted_iota(jnp.int32, (N_EXPERTS, tp), 0)
    onehot = [jnp.where(eidx_ref[k:k + 1, :] == expert, 1.0, 0.0) for k in range(2)]

    @pl.when((phase == 0) & (i == 0))
    def _():
        run_ref[...] = jnp.zeros_like(run_ref)

    @pl.when(phase == 0)
    def _():
        run_ref[...] += jnp.sum(onehot[0] + onehot[1], axis=1, keepdims=True)

    @pl.when((phase == 1) & (i == 0))
    def _():
        padded = jnp.floor((run_ref[...] + (EXPERT_ROW_TILE - 1)) * (1.0 / EXPERT_ROW_TILE)) * EXPERT_ROW_TILE
        ei = lax.broadcasted_iota(jnp.int32, (N_EXPERTS, N_EXPERTS), 0)
        ej = lax.broadcasted_iota(jnp.int32, (N_EXPERTS, N_EXPERTS), 1)
        below = jnp.where(ej < ei, 1.0, 0.0)
        starts = jnp.dot(below, jnp.broadcast_to(padded, (N_EXPERTS, LANES)), precision=lax.Precision.HIGHEST,
                         preferred_element_type=F32)
        ends_ref[...] = (starts + padded).astype(jnp.int32)
        run_ref[...] = starts[:, 0:1]

    @pl.when(phase == 1)
    def _():
        ti = lax.broadcasted_iota(jnp.int32, (tp, tp), 0)
        tj = lax.broadcasted_iota(jnp.int32, (tp, tp), 1)
        before = jnp.where(ti < tj, 1.0, 0.0).astype(BF16)
        run = run_ref[...]
        for k in range(2):
            prefix = jnp.dot(onehot[k].astype(BF16), before, preferred_element_type=F32)
            pos_ref[k:k + 1, :] = jnp.sum(onehot[k] * (prefix + run), axis=0, keepdims=True).astype(jnp.int32)
            run = run + jnp.sum(onehot[k], axis=1, keepdims=True)
        run_ref[...] = run


def _positions(eidx):
    n = eidx.shape[1]
    nt = n // POSITION_TILE
    return pl.pallas_call(
        _positions_kernel,
        grid=(2, nt),
        in_specs=[pl.BlockSpec((2, POSITION_TILE), lambda p, i: (0, i))],
        out_specs=[pl.BlockSpec((2, POSITION_TILE), lambda p, i: (0, i * p)),
                   pl.BlockSpec((N_EXPERTS, LANES), lambda p, i: (0, 0))],
        out_shape=[jax.ShapeDtypeStruct((2, n), jnp.int32), jax.ShapeDtypeStruct((N_EXPERTS, LANES), jnp.int32)],
        scratch_shapes=[pltpu.VMEM((N_EXPERTS, 1), F32)],
        compiler_params=pltpu.CompilerParams(
            dimension_semantics=("arbitrary", "arbitrary"), vmem_limit_bytes=VMEM_LIMIT_BYTES),
        name="positions",
    )(eidx)


def _sc_mesh():
    return plsc.VectorSubcoreMesh(core_axis_name="c", subcore_axis_name="s")


def _sc_worker(n_workers_per_core=SC_SUBCORES):
    return lax.axis_index("c") * n_workers_per_core + lax.axis_index("s")


def _sc_sort_rows(table, pos, n_rows):
    n_slots, width = pos.shape[0], table.shape[1]
    n_tokens = n_slots // 2
    assert n_tokens & (n_tokens - 1) == 0
    per_worker = n_rows // SC_WORKERS
    assert per_worker * SC_WORKERS == n_rows and per_worker % SC_GATHER_ROWS == 0 and n_slots % SC_LANES == 0

    def body(table_hbm, pos_hbm, out_hbm, pos_v, inv_v, idx_v, rows_v, sem):
        base = _sc_worker() * per_worker
        pltpu.sync_copy(pos_hbm, pos_v)

        @pl.loop(0, per_worker, step=SC_LANES)
        def _(j):
            inv_v[pl.ds(j, SC_LANES)] = (base + j + lax.iota(jnp.int32, SC_LANES)) & (n_tokens - 1)

        @pl.loop(0, n_slots, step=SC_LANES)
        def _(s):
            local = pos_v[pl.ds(s, SC_LANES)] - base
            mine = (local >= 0) & (local < per_worker)
            token = (s + lax.iota(jnp.int32, SC_LANES)) & (n_tokens - 1)
            plsc.store_scatter(inv_v, [jnp.where(mine, local, 0)], token, mask=mine)

        @pl.loop(0, per_worker, step=SC_GATHER_ROWS)
        def _(j):
            @pl.loop(0, SC_GATHER_ROWS, step=SC_LANES)
            def _(q):
                idx_v[pl.ds(q, SC_LANES)] = inv_v[pl.ds(j + q, SC_LANES)]

            pltpu.async_copy(table_hbm.at[idx_v], rows_v, sem).wait()
            pltpu.sync_copy(rows_v, out_hbm.at[pl.ds(base + j, SC_GATHER_ROWS)])

    return pl.kernel(
        body, out_type=jax.ShapeDtypeStruct((n_rows, width), table.dtype), mesh=_sc_mesh(),
        scratch_types=[pltpu.VMEM((n_slots,), jnp.int32), pltpu.VMEM((per_worker,), jnp.int32),
                       pltpu.VMEM((SC_GATHER_ROWS,), jnp.int32), pltpu.VMEM((SC_GATHER_ROWS, width), table.dtype),
                       pltpu.SemaphoreType.DMA],
        compiler_params=pltpu.CompilerParams(needs_layout_passes=False),
        name="sc_sort_rows",
    )(table, pos)


def _sc_gather(table, idx):
    n_rows, width = idx.shape[0], table.shape[1]
    per_worker = n_rows // SC_WORKERS
    assert per_worker * SC_WORKERS == n_rows and per_worker % SC_GATHER_ROWS == 0

    def body(table_hbm, idx_hbm, out_hbm, idx_v, rows_v, sem):
        base = _sc_worker() * per_worker

        @pl.loop(0, per_worker, step=SC_GATHER_ROWS)
        def _(j):
            pltpu.sync_copy(idx_hbm.at[pl.ds(base + j, SC_GATHER_ROWS)], idx_v)
            pltpu.async_copy(table_hbm.at[idx_v], rows_v, sem).wait()
            pltpu.sync_copy(rows_v, out_hbm.at[pl.ds(base + j, SC_GATHER_ROWS)])

    return pl.kernel(
        body, out_type=jax.ShapeDtypeStruct((n_rows, width), table.dtype), mesh=_sc_mesh(),
        scratch_types=[pltpu.VMEM((SC_GATHER_ROWS,), jnp.int32), pltpu.VMEM((SC_GATHER_ROWS, width), table.dtype),
                       pltpu.SemaphoreType.DMA],
        name="sc_gather",
    )(table, idx)


def _experts_kernel(layer, te_ref, seg_ref, nxt_ref, xs_ref, wg_hbm, wu_hbm, wd_hbm, ys_ref,
                    wg_f, wu_f, wd_f, wg_bf, wu_bf, wd_bf, sems):
    i = pl.program_id(0)
    expert = te_ref[i]
    held = jnp.minimum(expert, N_EXPERTS - 1)
    slot = seg_ref[i] & 1
    new_segment = (i == 0) | (seg_ref[i] != seg_ref[jnp.maximum(i - 1, 0)])

    def weight_copies(e, s):
        return [pltpu.make_async_copy(w_hbm.at[layer, e], w_f.at[s], sems.at[k, s])
                for k, (w_hbm, w_f) in enumerate(((wg_hbm, wg_f), (wu_hbm, wu_f), (wd_hbm, wd_f)))]

    @pl.when(i == 0)
    def _():
        for cp in weight_copies(held, slot):
            cp.start()

    @pl.when(new_segment)
    def _():
        for cp in weight_copies(held, slot):
            cp.wait()
        wg_bf[...] = wg_f[slot].astype(BF16)
        wu_bf[...] = wu_f[slot].astype(BF16)
        wd_bf[...] = wd_f[slot].astype(BF16)

        @pl.when(nxt_ref[i] >= 0)
        def _():
            for cp in weight_copies(nxt_ref[i], 1 - slot):
                cp.start()

    @pl.when(expert < N_EXPERTS)
    def _():
        sub = xs_ref.shape[0] // MIX_SUB_TILES
        tiles = [slice(s * sub, (s + 1) * sub) for s in range(MIX_SUB_TILES)]
        xs = [_unpack_bf16_pairs(xs_ref[rs, :]).astype(BF16) for rs in tiles]
        hes = []
        for x in xs:
            gate, up = _dot(x, wg_bf[...]), _dot(x, wu_bf[...])
            hes.append((gate * _sigmoid(gate) * up).astype(BF16))
        for rs, he in zip(tiles, hes):
            ys_ref[rs, :] = _pack_bf16_pairs(_dot(he, wd_bf[...]))


def _experts(tile_expert, xs, layer, w_gate, w_up, w_down):
    n_rows, half = xs.shape
    _, n_e, d, de = w_gate.shape
    held = jnp.minimum(tile_expert, n_e - 1)
    segment = jnp.cumsum(jnp.concatenate([jnp.zeros((1,), jnp.int32), (held[1:] != held[:-1]).astype(jnp.int32)]))
    after = jnp.sum(held[None, :] <= held[:, None], axis=1)
    next_expert = jnp.where(after < held.shape[0], held[jnp.minimum(after, held.shape[0] - 1)], -1).astype(jnp.int32)

    rows = pl.BlockSpec((EXPERT_ROW_TILE, half), lambda i, te, seg, nxt: (i, 0))
    hbm = pl.BlockSpec(memory_space=pl.ANY)
    return pl.pallas_call(
        functools.partial(_experts_kernel, layer),
        grid_spec=pltpu.PrefetchScalarGridSpec(
            num_scalar_prefetch=3,
            grid=(n_rows // EXPERT_ROW_TILE,),
            in_specs=[rows, hbm, hbm, hbm],
            out_specs=rows,
            scratch_shapes=[pltpu.VMEM((2, d, de), F32), pltpu.VMEM((2, d, de), F32), pltpu.VMEM((2, de, d), F32),
                            pltpu.VMEM((d, de), BF16), pltpu.VMEM((d, de), BF16), pltpu.VMEM((de, d), BF16),
                            pltpu.SemaphoreType.DMA((3, 2))]),
        out_shape=jax.ShapeDtypeStruct((n_rows, half), jnp.int32),
        compiler_params=pltpu.CompilerParams(
            dimension_semantics=("arbitrary",), vmem_limit_bytes=VMEM_LIMIT_BYTES),
        name="experts",
    )(tile_expert, segment.astype(jnp.int32), next_expert, xs, w_gate, w_up, w_down)


def _final_kernel(y0_ref, y1_ref, gate_ref, h1_ref, gt_ref, fg_ref, *out_refs):
    o_ref = out_refs[-1]
    h2 = _moe_residual(y0_ref, y1_ref, gate_ref, h1_ref[0], gt_ref[0])
    ms = jnp.mean(h2 * h2, axis=-1, keepdims=True)
    o_ref[0] = h2 * lax.rsqrt(ms + RMS_EPS) * fg_ref[...]


def _moe_out_specs(b, t, d, tile=TOKEN_TILE):
    nt = t // tile
    return [pl.BlockSpec((tile, d // 2), lambda i, j: (i * nt + j, 0)),
            pl.BlockSpec((tile, d // 2), lambda i, j: (b * nt + i * nt + j, 0)),
            pl.BlockSpec((tile, 2), lambda i, j: (i * nt + j, 0))]


def _final(yg, gates, h1, mod, final_g, batch0, n_batch, out_prev):
    b, t, d = h1.shape
    tile = pl.BlockSpec((1, WIDE_TOKEN_TILE, d), lambda i, j: (i, j, 0))
    in_specs = _moe_out_specs(b, t, d, WIDE_TOKEN_TILE) + [tile, _mod_spec3(5, d, batch0),
                                                           pl.BlockSpec((1, d), lambda i, j: (0, 0))]
    args = [yg, yg, gates, h1, mod, final_g]
    aliases = {}
    if out_prev is not None:
        in_specs.append(pl.BlockSpec(memory_space=pl.ANY))
        aliases = {len(args): 0}
        args.append(out_prev)
    return pl.pallas_call(
        _final_kernel,
        grid=(b, t // WIDE_TOKEN_TILE),
        in_specs=in_specs,
        out_specs=pl.BlockSpec((1, WIDE_TOKEN_TILE, d), lambda i, j: (i + batch0, j, 0)),
        out_shape=jax.ShapeDtypeStruct((n_batch, t, d), F32),
        input_output_aliases=aliases,
        compiler_params=pltpu.CompilerParams(
            dimension_semantics=("arbitrary", "arbitrary"), vmem_limit_bytes=VMEM_LIMIT_BYTES),
        name="final",
    )(*args)


def _moe_sort(hm, eidx):
    n_slots = 2 * hm.shape[0]
    n_rows = n_slots + N_EXPERTS * EXPERT_ROW_TILE
    pos, ends = _positions(eidx)
    pos = pos.reshape(n_slots)
    tile_start = jnp.arange(n_rows // EXPERT_ROW_TILE, dtype=jnp.int32) * EXPERT_ROW_TILE
    tile_expert = jnp.sum(tile_start[:, None] >= ends[None, :, 0], axis=1).astype(jnp.int32)
    return _sc_sort_rows(hm, pos, n_rows), pos, tile_expert


def _moe_apply(sorted_rows, gates, layer, w_gate, w_up, w_down):
    xs, pos, tile_expert = sorted_rows
    ys = _experts(tile_expert, xs, layer, w_gate, w_up, w_down)
    return _sc_gather(ys, pos), gates.T


def kernel(x, c, ctx, c_ctx, ada_w, ada_b, norm_g, rw_mu, rw_w_rkv, rw_w0, rw_w1, rw_w2, rw_a0, rw_a1, rw_a2, rw_g1, rw_g2, rw_k_k, rw_k_a, rw_r_k, rw_gn_w, rw_gn_b, rw_w_o, sc_w_in, sc_conv, sc_w_out, router_w, router_b, moe_w_gate, moe_w_up, moe_w_down, final_g):
    b, t, d = x.shape
    ctx_len = ctx.shape[1]
    depth = ada_w.shape[0]
    assert d == D_MODEL and depth == 2 and t % POSITION_TILE == 0

    mod_rows = 16
    cc = jnp.concatenate([c, c_ctx[None, :], jnp.zeros((mod_rows - b - 1, d), F32)], axis=0)
    mod = _ada(cc, ada_w, ada_b).reshape(depth, mod_rows, 1, 6 * d)

    row = lambda a: a.reshape(1, d)
    router_wt = jnp.pad(router_w, ((0, 0), (0, LANES - N_EXPERTS)))
    router_bc = router_b.reshape(N_EXPERTS, 1)

    w1 = jnp.concatenate([rw_w1[0, 0], rw_w1[0, 1]], axis=1).astype(BF16)
    a1 = jnp.concatenate([rw_a1[0, 0], rw_a1[0, 1]], axis=1).astype(BF16)

    def pad_dirs(w):
        z = jnp.zeros_like(w[0])
        return jnp.stack([jnp.concatenate([w[0], z], axis=0), jnp.concatenate([z, w[1]], axis=0)]).astype(BF16)

    assert b % N_STREAMS == 0
    nb = b // N_STREAMS
    pre_params = (row(norm_g[0, 0]), rw_mu[0], rw_w_rkv[0].astype(BF16), w1, pad_dirs(rw_w2[0]), rw_w0[0], a1,
                  pad_dirs(rw_a2[0]), rw_a0[0], rw_g1[0].astype(BF16), rw_g2[0].astype(BF16), row(rw_k_k[0]),
                  row(rw_k_a[0]), row(rw_r_k[0]))
    w_o, w_in, w_out = rw_w_o[0].astype(BF16), sc_w_in[0].astype(BF16), sc_w_out[0].astype(BF16)
    out = None
    moe_w = (moe_w_gate, moe_w_up, moe_w_down)
    streams = []
    for batch0 in range(0, b, nb):
        r, v, g, bonus, k, kk, cs0, cs1, a0, a1 = _rwkv_pre(ctx, x, mod[0], *pre_params, batch0, nb)
        yf, yr = _wkv(row(rw_k_a[0]), r, v, k, kk, cs0, a0, cs1, a1, ctx_len)
        h1, hm, eidx, gates = _readout(yf, yr, bonus, g, x, mod[0], row(rw_gn_w[0]), row(rw_gn_b[0]),
                                       w_o, row(norm_g[0, 1]), router_wt, router_bc, batch0, nb)
        streams.append((batch0, h1, gates, _moe_sort(hm, eidx)))
    for batch0, h1, gates, sorted_rows in streams:
        yg, gates = _moe_apply(sorted_rows, gates, 0, *moe_w)
        h1, hm, eidx, gates = _sconv(yg, gates, h1, mod[0], mod[1], row(norm_g[1, 0]), w_in, sc_conv[0], w_out,
                                     row(norm_g[1, 1]), router_wt, router_bc, batch0)
        yg, gates = _moe_apply(_moe_sort(hm, eidx), gates, 1, *moe_w)
        out = _final(yg, gates, h1, mod[1], row(final_g), batch0, b, out)
    return out
```

```python
import functools

import jax
import jax.numpy as jnp
from jax import lax
from jax.experimental import pallas as pl
from jax.experimental.pallas import tpu as pltpu
from jax.experimental.pallas import tpu_sc as plsc

F32 = jnp.float32
BF16 = jnp.bfloat16

D_MODEL = 1024
HEAD_DIM = 64
N_HEADS = D_MODEL // HEAD_DIM
LANES = 128
N_PAIRS = D_MODEL // LANES
CHUNK = 64
N_EXPERTS = 16
EXPERTS_PER_GROUP = 4
N_GROUPS = N_EXPERTS // EXPERTS_PER_GROUP
RMS_EPS = 1e-6
GN_EPS = 64e-5
L2_EPS = 1e-12

TOKEN_TILE = 256
WIDE_TOKEN_TILE = 512
MIX_SUB_TILES = 2
PRE_SUB_TILES = 2
EXPERT_ROW_TILE = 512
POSITION_TILE = 1024
SC_CORES, SC_SUBCORES, SC_LANES = 2, 16, 16
SC_WORKERS = SC_CORES * SC_SUBCORES
N_STREAMS = 2
SC_GATHER_ROWS = 128
PAIRS_PER_STEP = 8
STAGE_LAG = 2
CHUNKS_PER_STEP = 4
ADA_COL_TILE = 1536
VMEM_LIMIT_BYTES = 56 * 1024 * 1024


def _sigmoid(x):
    return 1.0 / (1.0 + jnp.exp(-x))


def _mm(a, b):
    return jnp.dot(a.astype(BF16), b.astype(BF16), preferred_element_type=F32)


def _mm_nt(a, b):
    return lax.dot_general(a.astype(BF16), b.astype(BF16), (((1,), (1,)), ((), ())),
                           preferred_element_type=F32)


def _split2(x):
    hi = x.astype(BF16)
    return hi, (x - hi.astype(F32)).astype(BF16)


def _head_sum(x):
    rows = x.shape[0]
    left = lax.broadcasted_iota(jnp.int32, (rows, LANES), 1) < HEAD_DIM
    outs = []
    for j in range(N_PAIRS):
        xb = x[:, LANES * j:LANES * (j + 1)]
        sa = jnp.sum(jnp.where(left, xb, 0.0), axis=-1, keepdims=True)
        sb = jnp.sum(jnp.where(left, 0.0, xb), axis=-1, keepdims=True)
        outs.append(jnp.where(left, sa, sb))
    return jnp.concatenate(outs, axis=-1)


def _rms_modulate(x, g, shift, scale):
    ms = jnp.mean(x * x, axis=-1, keepdims=True)
    xn = x * lax.rsqrt(ms + RMS_EPS) * g
    return xn * (1.0 + scale) + shift


def _row_neighbours(x, row_len):
    rows = x.shape[0]
    pos = lax.broadcasted_iota(jnp.int32, x.shape, 0) & (row_len - 1)
    prev = jnp.where(pos == 0, 0.0, pltpu.roll(x, 1, 0))
    nxt = jnp.where(pos == row_len - 1, 0.0, pltpu.roll(x, rows - 1, 0))
    return prev, nxt


def _ada_kernel(c_ref, w_ref, b_ref, o_ref):
    c = c_ref[...]
    s = c * _sigmoid(c)
    o_ref[0] = _mm(s, w_ref[0]) + b_ref[0]


def _ada(cc, ada_w, ada_b):
    depth, d, n = ada_w.shape
    rows = cc.shape[0]
    return pl.pallas_call(
        _ada_kernel,
        grid=(depth, n // ADA_COL_TILE),
        in_specs=[
            pl.BlockSpec((rows, d), lambda l, j: (0, 0)),
            pl.BlockSpec((1, d, ADA_COL_TILE), lambda l, j: (l, 0, j)),
            pl.BlockSpec((1, 1, ADA_COL_TILE), lambda l, j: (l, 0, j)),
        ],
        out_specs=pl.BlockSpec((1, rows, ADA_COL_TILE), lambda l, j: (l, 0, j)),
        out_shape=jax.ShapeDtypeStruct((depth, rows, n), F32),
        compiler_params=pltpu.CompilerParams(
            dimension_semantics=("arbitrary", "arbitrary"), vmem_limit_bytes=VMEM_LIMIT_BYTES),
        name="ada",
    )(cc, ada_w, ada_b.reshape(depth, 1, n))


def _rwkv_pre_kernel(ctx_len, ctx_ref, x_ref, sh_ref, sc_ref, ng_ref, mu_ref, wrkv_ref, w1_ref, w2_ref, w0_ref,
                     a1_ref, a2_ref, a0_ref, g1_ref, g2_ref, kk_ref, ka_ref, rk_ref,
                     r_out, v_out, g_out, bon_out, k_out, kk_out, cs0_out, cs1_out, a0_out, a1_out):
    j = pl.program_id(1)
    row_len = jnp.where(j == 0, ctx_len, CHUNK)
    xin = jnp.where(j == 0, ctx_ref[0], x_ref[0])
    xn = _rms_modulate(xin, ng_ref[...], sh_ref[0], sc_ref[0])
    prev, nxt = _row_neighbours(xn, row_len)
    xx = 0.5 * (prev + nxt) - xn

    sub = xn.shape[0] // PRE_SUB_TILES
    projected = []
    for s in range(PRE_SUB_TILES):
        rs = slice(s * sub, (s + 1) * sub)
        xn_s, xx_s = xn[rs], xx[rs]

        def mix(i, xn_s=xn_s, xx_s=xx_s):
            return xn_s + xx_s * mu_ref[i:i + 1, :]

        r = _mm(mix(0), wrkv_ref[0])
        k = _mm(mix(1), wrkv_ref[1])
        v = _mm(mix(2), wrkv_ref[2])
        g = _mm(_sigmoid(_mm(mix(5), g1_ref[...])), g2_ref[...])
        wl = jnp.tanh(_mm(mix(3), w1_ref[...]))
        al = _mm(mix(4), a1_ref[...])
        projected.append((rs, r, k, v, g, [_mm(wl, w2_ref[p]) for p in range(2)],
                          [_mm(al, a2_ref[p]) for p in range(2)]))

    ti = lax.broadcasted_iota(jnp.int32, (sub, sub), 0)
    tj = lax.broadcasted_iota(jnp.int32, (sub, sub), 1)
    same_chunk = (ti ^ tj) < CHUNK
    dir_outs = ((cs0_out, a0_out), (cs1_out, a1_out))
    for rs, r, k, v, g, zs, a_logits in projected:
        r_out[0, rs, :] = r.astype(r_out.dtype)
        v_out[0, rs, :] = v.astype(v_out.dtype)
        g_out[0, rs, :] = g.astype(g_out.dtype)
        k_out[0, rs, :] = k.astype(k_out.dtype)
        kk = k * kk_ref[...]
        kk_out[0, rs, :] = (kk * lax.rsqrt(jnp.maximum(_head_sum(kk * kk), L2_EPS * L2_EPS))).astype(kk_out.dtype)
        a_sum = None
        for p, (cs_out, a_out) in enumerate(dir_outs):
            z = w0_ref[p:p + 1, :] + zs[p]
            lw = -jnp.exp(-0.5) * _sigmoid(z)
            tri = jnp.where(same_chunk & ((tj >= ti) if p else (tj <= ti)), 1.0, 0.0).astype(BF16)
            hi, lo = _split2(lw)
            cs_out[0, rs, :] = _dot(tri, hi) + _dot(tri, lo)
            a = _sigmoid(a0_ref[p:p + 1, :] + a_logits[p])
            a_out[0, rs, :] = a.astype(a_out.dtype)
            a_sum = a if a_sum is None else a_sum + a
        k_dirs = k * (2.0 + (a_sum - 2.0) * ka_ref[...])
        bon_out[0, rs, :] = (_head_sum(r * rk_ref[...] * k_dirs) * v).astype(bon_out.dtype)


def _rwkv_pre(ctx, x, mod, norm_g, mu, w_rkv, w1, w2, w0, a1, a2, a0, g1, g2, k_k, k_a, r_k, batch0, b):
    n_batch, t, d = x.shape
    ctx_len = ctx.shape[1]
    tt = ctx_len + t
    assert ctx_len == TOKEN_TILE and t % TOKEN_TILE == 0

    def mod_spec(part):
        return pl.BlockSpec((1, 1, d), lambda i, j: (jnp.where(j == 0, n_batch, i + batch0), 0, part))

    def full(a):
        nd = a.ndim
        return pl.BlockSpec(a.shape, lambda i, j: (0,) * nd)

    tile = pl.BlockSpec((1, TOKEN_TILE, d), lambda i, j: (i, j, 0))
    latent = pl.BlockSpec((1, TOKEN_TILE, d), lambda i, j: (i, jnp.maximum(j - 1, 0), 0))
    params = (norm_g, mu, w_rkv, w1, w2, w0, a1, a2, a0, g1, g2, k_k, k_a, r_k)
    out_dtypes = (BF16,) * 6 + (F32, F32) + (BF16,) * 2
    out_specs = [tile, tile, latent, latent] + [tile] * 6
    return pl.pallas_call(
        functools.partial(_rwkv_pre_kernel, ctx_len),
        grid=(b, tt // TOKEN_TILE),
        in_specs=[pl.BlockSpec((1, TOKEN_TILE, d), lambda i, j: (i + batch0, 0, 0)),
                  pl.BlockSpec((1, TOKEN_TILE, d), lambda i, j: (i + batch0, jnp.maximum(j - 1, 0), 0)),
                  mod_spec(0), mod_spec(1)] + [full(a) for a in params],
        out_specs=out_specs,
        out_shape=[jax.ShapeDtypeStruct((b, t if spec is latent else tt, d), dt)
                   for spec, dt in zip(out_specs, out_dtypes)],
        compiler_params=pltpu.CompilerParams(
            dimension_semantics=("arbitrary", "arbitrary"), vmem_limit_bytes=VMEM_LIMIT_BYTES),
        name="rwkv_pre",
    )(ctx, x, mod, mod, *params)


def _expand(x, left):
    return jnp.concatenate([jnp.where(left, x, 0.0), jnp.where(left, 0.0, x)], axis=0)


def _dot(a, b):
    return jnp.dot(a, b, preferred_element_type=F32)


def _chunk_steps(chains, group_size):
    assert CHUNK == HEAD_DIM
    c = CHUNK
    lane = lax.broadcasted_iota(jnp.int32, (c, LANES), 1)
    left = lane < HEAD_DIM
    tt = lax.broadcasted_iota(jnp.int32, (c, LANES), 0)
    jj = lane & (c - 1)
    diag = jj == tt
    tri = {False: (jj < tt, jj <= tt), True: (jj > tt, jj >= tt)}

    def bd(x):
        return _expand(x, left).astype(BF16)

    def fold_t(x):
        xt = _expand(x, left).T
        return xt[:c] + xt[c:]

    def rows(*xs):
        return jnp.concatenate(xs, axis=0).astype(BF16)

    nt_dims = (((1,), (1,)), ((), ()))

    def s_prep(q):
        cs, r, v, k, kk, a, ka, h, reverse = q.pop("chain")
        tot = cs[0:1, :] if reverse else cs[c - 1:c, :]
        e_pos, e_neg, e_rem = jnp.exp(cs), jnp.exp(-cs), jnp.exp(tot - cs)
        first = tt == (c - 1 if reverse else 0)
        cs_prev = jnp.where(first, 0.0, pltpu.roll(cs, c - 1 if reverse else 1, 0))
        at, rt = -kk * jnp.exp(cs_prev), r * e_pos
        kd, be = k * (1.0 + (a - 1.0) * ka), kk * a
        q.update(a_e=bd(at), rt=rt, v_e=bd(v), g_tot=jnp.exp(tot), h=h, tri=tri[reverse])
        q["hat_t"] = jnp.concatenate([fold_t(be * e_rem), fold_t(kd * e_rem)], axis=1)
        q["sc"] = lax.dot_general(rows(at, rt), jnp.concatenate([bd(be * e_neg), bd(kd * e_neg)], axis=0),
                                  nt_dims, preferred_element_type=F32)

    def s_mask(q):
        (strict, incl), sc = q["tri"], q["sc"]
        q["n"] = jnp.where(strict, sc[:c, :LANES], 0.0)
        a_ak = jnp.where(strict, sc[:c, LANES:], 0.0)
        q["a_rb"] = jnp.where(incl, sc[c:, :LANES], 0.0)
        a_rk = jnp.where(incl, sc[c:, LANES:], 0.0)
        q["bh_t"] = q["hat_t"][:, :LANES]
        q["vv"] = _dot(rows(a_ak, q["hat_t"][:, LANES:], a_rk), q["v_e"])

    def s_square(q):
        q["t"] = jnp.where(diag, 1.0, 0.0) + q["n"]
        q["p"] = _dot(q["n"].astype(BF16), bd(q["n"]))

    def s_level(q):
        x = _dot(rows(q["p"], q["t"]), bd(q["p"]))
        q["p"] = x[:c]
        q["t"] = q["t"] + x[c:]

    def s_last_level(q):
        q["t"] = q["t"] + _dot(q["t"].astype(BF16), bd(q["p"]))

    def s_solve(q):
        xu = _dot(q["t"].astype(BF16), jnp.concatenate([q["a_e"], bd(q["vv"][:c])], axis=1))
        q["au_e"] = jnp.concatenate([bd(xu[:, :LANES]), bd(xu[:, LANES:])], axis=1)

    def s_affine(q):
        z = _dot(rows(q["bh_t"], q["a_rb"]), q["au_e"])
        q["m"] = z[:c, :LANES] + jnp.where(diag, q["g_tot"], 0.0)
        q["g"] = z[:c, LANES:] + q["vv"][c:2 * c]
        q["r_hat"] = q["rt"] + z[c:, :LANES]
        q["yi"] = z[c:, LANES:] + q["vv"][2 * c:]

    n_levels = (c // 4).bit_length() - 1
    stages = [s_prep, s_mask, s_square] + [s_level] * n_levels + [s_last_level, s_solve, s_affine]
    qs = [dict(chain=ch) for ch in chains]
    groups = [qs[i:i + group_size] for i in range(0, len(qs), group_size)]
    for tau in range(len(stages) + (len(groups) - 1) * STAGE_LAG):
        for gi, group in enumerate(groups):
            si = tau - gi * STAGE_LAG
            if 0 <= si < len(stages):
                for q in group:
                    stages[si](q)
    outs = []
    for q in qs:
        h = outs[q["h"]][1] if isinstance(q["h"], int) else q["h"]
        o = _dot(rows(q["r_hat"], q["m"]), bd(h))
        outs.append((o[:c] + q["yi"], o[c:] + q["g"]))
    return outs


def _wkv_kernel(ka_ref, rf_ref, vf_ref, kf_ref, kkf_ref, csf_ref, af_ref,
                rr_ref, vr_ref, kr_ref, kkr_ref, csr_ref, ar_ref,
                yf_ref, yr_ref, hf_ref, hr_ref):
    @pl.when(pl.program_id(2) == 0)
    def _():
        hf_ref[...] = jnp.zeros_like(hf_ref)
        hr_ref[...] = jnp.zeros_like(hr_ref)

    dirs = ((rf_ref, vf_ref, kf_ref, kkf_ref, csf_ref, af_ref, yf_ref, hf_ref),
            (rr_ref, vr_ref, kr_ref, kkr_ref, csr_ref, ar_ref, yr_ref, hr_ref))
    chains, dests = [], []
    per_chunk = 2 * PAIRS_PER_STEP
    for u in range(CHUNKS_PER_STEP):
        for p in range(PAIRS_PER_STEP):
            sl = slice(p * LANES, (p + 1) * LANES)
            for reverse, (r_ref, v_ref, k_ref, kk_ref, cs_ref, a_ref, y_ref, h_ref) in enumerate(dirs):
                at = CHUNKS_PER_STEP - 1 - u if reverse else u
                tm = slice(at * CHUNK, (at + 1) * CHUNK)
                h = h_ref[p] if u == 0 else len(chains) - per_chunk
                chains.append((cs_ref[0, tm, sl], r_ref[0, tm, sl].astype(F32), v_ref[0, tm, sl].astype(F32),
                               k_ref[0, tm, sl].astype(F32), kk_ref[0, tm, sl].astype(F32),
                               a_ref[0, tm, sl].astype(F32), ka_ref[:, sl], h, bool(reverse)))
                dests.append((y_ref, tm, sl, h_ref if u == CHUNKS_PER_STEP - 1 else None, p))
    for (y, h_new), (y_ref, tm, sl, h_ref, p) in zip(_chunk_steps(chains, per_chunk), dests):
        y_ref[0, tm, sl] = y
        if h_ref is not None:
            h_ref[p] = h_new


def _wkv(k_a, r, v, k, kk, cs0, a0, cs1, a1, ctx_len):
    b, tt, d = r.shape
    step_rows = CHUNKS_PER_STEP * CHUNK
    n_steps, n_ctx = tt // step_rows, ctx_len // step_rows
    assert n_steps * step_rows == tt and n_ctx * step_rows == ctx_len
    width = PAIRS_PER_STEP * LANES

    def fwd_map(i, j, s):
        return (i, s, j)

    def rev_map(i, j, s):
        return (i, jnp.where(s < n_ctx, n_ctx - 1 - s, n_steps - 1 - (s - n_ctx)), j)

    n_lat = n_steps - n_ctx

    def fwd_out(i, j, s):
        return (i, jnp.maximum(s - n_ctx, 0), j)

    def rev_out(i, j, s):
        return (i, jnp.where(s < n_ctx, n_lat - 1, n_lat - 1 - (s - n_ctx)), j)

    fwd = pl.BlockSpec((1, step_rows, width), fwd_map)
    rev = pl.BlockSpec((1, step_rows, width), rev_map)
    return pl.pallas_call(
        _wkv_kernel,
        grid=(b, d // width, n_steps),
        in_specs=[pl.BlockSpec((1, width), lambda i, j, s: (0, j))] + [fwd] * 6 + [rev] * 6,
        out_specs=[pl.BlockSpec((1, step_rows, width), fwd_out), pl.BlockSpec((1, step_rows, width), rev_out)],
        out_shape=[jax.ShapeDtypeStruct((b, tt - ctx_len, d), F32)] * 2,
        scratch_shapes=[pltpu.VMEM((PAIRS_PER_STEP, HEAD_DIM, LANES), F32)] * 2,
        compiler_params=pltpu.CompilerParams(
            dimension_semantics=("arbitrary", "arbitrary", "arbitrary"), vmem_limit_bytes=VMEM_LIMIT_BYTES),
        name="wkv",
    )(k_a, r, v, k, kk, cs0, a0, r, v, k, kk, cs1, a1)


def _pack_bf16_pairs(x):
    half = x.shape[1] // 2
    lo = pltpu.bitcast(x[:, :half].astype(BF16).astype(F32), jnp.int32)
    hi = pltpu.bitcast(x[:, half:].astype(BF16).astype(F32), jnp.int32)
    return lax.shift_right_logical(lo, jnp.int32(16)) | (hi & jnp.int32(-65536))


def _unpack_bf16_pairs(w):
    lo = pltpu.bitcast(lax.shift_left(w, jnp.int32(16)), F32)
    hi = pltpu.bitcast(w & jnp.int32(-65536), F32)
    return jnp.concatenate([lo, hi], axis=1)


def _route(h1, ng, shift, scale, rwt_ref, rb_ref, hm_out, eidx_out, gate_out, rs=slice(None)):
    hm = _rms_modulate(h1, ng, shift, scale)
    hm_out[rs, :] = _pack_bf16_pairs(hm)
    w_hi, w_mid = _split2(rwt_ref[...])
    h_hi, h_mid = _split2(hm)
    logits_t = _dot(h_hi, w_hi) + _dot(h_mid, w_hi) + _dot(h_hi, w_mid)
    logits = logits_t.T[:N_EXPERTS]
    s = _sigmoid(logits)
    sel = s + rb_ref[...]
    assert EXPERTS_PER_GROUP == 4 and N_GROUPS == 4
    cands = []
    for g in range(N_GROUPS):
        m = [sel[e:e + 1, :] for e in range(g * EXPERTS_PER_GROUP, (g + 1) * EXPERTS_PER_GROUP)]
        sg = [s[e:e + 1, :] for e in range(g * EXPERTS_PER_GROUP, (g + 1) * EXPERTS_PER_GROUP)]
        pairs = [m[i] + m[k] for i in range(4) for k in range(i + 1, 4)]
        score = jnp.maximum(jnp.maximum(jnp.maximum(pairs[0], pairs[1]), jnp.maximum(pairs[2], pairs[3])),
                            jnp.maximum(pairs[4], pairs[5]))
        chosen = []
        for i in range(4):
            ahead = [((m[k] >= m[i]) if k < i else (m[k] > m[i])).astype(jnp.int32) for k in range(4) if k != i]
            chosen.append(ahead[0] + ahead[1] + ahead[2] < 2)
        base = g * EXPERTS_PER_GROUP
        lo_idx = jnp.where(chosen[0], base, jnp.where(chosen[1], base + 1, base + 2))
        lo_gate = jnp.where(chosen[0], sg[0], jnp.where(chosen[1], sg[1], sg[2]))
        hi_idx = jnp.where(chosen[3], base + 3, jnp.where(chosen[2], base + 2, base + 1))
        hi_gate = jnp.where(chosen[3], sg[3], jnp.where(chosen[2], sg[2], sg[1]))
        cands.append((score, lo_idx, hi_idx, lo_gate, hi_gate))

    def better(x, y):
        win = y[0] > x[0]
        return tuple(jnp.where(win, yv, xv) for xv, yv in zip(x, y))

    _, e_lo, e_hi, g_lo, g_hi = better(better(cands[0], cands[1]), better(cands[2], cands[3]))
    eidx_out[:, rs] = jnp.concatenate([e_lo, e_hi], axis=0)
    gate_out[:, rs] = jnp.concatenate([g_lo, g_hi], axis=0) / (g_lo + g_hi)


def _readout_kernel(yf_ref, yr_ref, bon_ref, g_ref, h_ref, gt_ref, gnw_ref, gnb_ref, wo_ref,
                    ng_ref, sh_ref, sc_ref, rwt_ref, rb_ref, h1_out, hm_out, eidx_out, gate_out):
    sub = h_ref.shape[1] // MIX_SUB_TILES
    tiles = [slice(s * sub, (s + 1) * sub) for s in range(MIX_SUB_TILES)]
    gated, h1s = [], []
    for rs in tiles:
        y = yf_ref[0, rs, :] + yr_ref[0, rs, :] + bon_ref[0, rs, :].astype(F32)
        mean = _head_sum(y) * (1.0 / HEAD_DIM)
        yc = y - mean
        var = _head_sum(yc * yc) * (1.0 / HEAD_DIM)
        yn = yc * lax.rsqrt(var + GN_EPS) * gnw_ref[...] + gnb_ref[...]
        gated.append((yn * g_ref[0, rs, :].astype(F32)).astype(BF16))
    for rs, lhs in zip(tiles, gated):
        h1 = h_ref[0, rs, :] + gt_ref[0] * _dot(lhs, wo_ref[...])
        h1_out[0, rs, :] = h1
        h1s.append(h1)
    for rs, h1 in zip(tiles, h1s):
        _route(h1, ng_ref[...], sh_ref[0], sc_ref[0], rwt_ref, rb_ref, hm_out, eidx_out, gate_out, rs)


def _mod_spec3(part, d, batch0=0):
    return pl.BlockSpec((1, 1, d), lambda i, j: (i + batch0, 0, part))


def _route_out_specs(b, t, d, tile=TOKEN_TILE):
    nt = t // tile
    specs = [pl.BlockSpec((1, tile, d), lambda i, j: (i, j, 0)),
             pl.BlockSpec((tile, d // 2), lambda i, j: (i * nt + j, 0)),
             pl.BlockSpec((2, tile), lambda i, j: (0, i * nt + j)),
             pl.BlockSpec((2, tile), lambda i, j: (0, i * nt + j))]
    shapes = [jax.ShapeDtypeStruct((b, t, d), F32), jax.ShapeDtypeStruct((b * t, d // 2), jnp.int32),
              jax.ShapeDtypeStruct((2, b * t), jnp.int32), jax.ShapeDtypeStruct((2, b * t), F32)]
    return specs, shapes


def _full2(a):
    nd = a.ndim
    return pl.BlockSpec(a.shape, lambda i, j: (0,) * nd)


def _readout(yf, yr, bonus, g, h, mod, gn_w, gn_b, w_o, norm_g, router_wt, router_b, batch0, b):
    _, t, d = h.shape
    nt = t // WIDE_TOKEN_TILE
    local = pl.BlockSpec((1, WIDE_TOKEN_TILE, d), lambda i, j: (i, j, 0))
    tile = pl.BlockSpec((1, WIDE_TOKEN_TILE, d), lambda i, j: (i + batch0, j, 0))
    params_a = (gn_w, gn_b, w_o, norm_g)
    params_b = (router_wt, router_b)
    out_specs, out_shape = _route_out_specs(b, t, d, WIDE_TOKEN_TILE)
    return pl.pallas_call(
        _readout_kernel,
        grid=(b, nt),
        in_specs=[local] * 4 + [tile, _mod_spec3(2, d, batch0)] + [_full2(a) for a in params_a]
        + [_mod_spec3(3, d, batch0), _mod_spec3(4, d, batch0)] + [_full2(a) for a in params_b],
        out_specs=out_specs,
        out_shape=out_shape,
        compiler_params=pltpu.CompilerParams(
            dimension_semantics=("arbitrary", "arbitrary"), vmem_limit_bytes=VMEM_LIMIT_BYTES),
        name="readout",
    )(yf, yr, bonus, g, h, mod, *params_a, mod, mod, *params_b)


def _moe_residual(y0_ref, y1_ref, gate_ref, h1, gt, rs=slice(None)):
    gates = gate_ref[rs, :]
    y = gates[:, 0:1] * _unpack_bf16_pairs(y0_ref[rs, :]) + gates[:, 1:2] * _unpack_bf16_pairs(y1_ref[rs, :])
    return h1 + gt * y


def _sconv_kernel(y0_ref, y1_ref, gate_ref, hp_ref, gtp_ref, sh1_ref, sc1_ref, gt_ref, ng1_ref, win_ref, cw_ref,
                  wout_ref, ng2_ref, sh2_ref, sc2_ref, rwt_ref, rb_ref, h1_out, hm_out, eidx_out, gate_out):
    d = hp_ref.shape[2]
    sub = hp_ref.shape[1] // MIX_SUB_TILES
    tiles = [slice(s * sub, (s + 1) * sub) for s in range(MIX_SUB_TILES)]
    hs, xns, gated, h1s = [], [], [], []
    for rs in tiles:
        h = _moe_residual(y0_ref, y1_ref, gate_ref, hp_ref[0, rs, :], gtp_ref[0], rs)
        hs.append(h)
        xns.append(_rms_modulate(h, ng1_ref[...], sh1_ref[0], sc1_ref[0]).astype(BF16))
    for xn in xns:
        bg = _dot(xn, win_ref[:, 0:d])
        u = _dot(xn, win_ref[:, d:2 * d]) * _dot(xn, win_ref[:, 2 * d:3 * d])
        prev, nxt = _row_neighbours(u, CHUNK)
        conv = cw_ref[0:1, :] * prev + cw_ref[1:2, :] * u + cw_ref[2:3, :] * nxt
        gated.append((bg * conv).astype(BF16))
    for rs, h, lhs in zip(tiles, hs, gated):
        h1 = h + gt_ref[0] * _dot(lhs, wout_ref[...])
        h1_out[0, rs, :] = h1
        h1s.append(h1)
    for rs, h1 in zip(tiles, h1s):
        _route(h1, ng2_ref[...], sh2_ref[0], sc2_ref[0], rwt_ref, rb_ref, hm_out, eidx_out, gate_out, rs)


def _sconv(yg, gates, h_prev, mod_prev, mod, norm_g1, w_in, conv_w, w_out, norm_g2, router_wt, router_b, batch0):
    b, t, d = h_prev.shape
    nt = t // WIDE_TOKEN_TILE
    tile = pl.BlockSpec((1, WIDE_TOKEN_TILE, d), lambda i, j: (i, j, 0))
    params_a = (norm_g1, w_in, conv_w, w_out, norm_g2)
    params_b = (router_wt, router_b)
    out_specs, out_shape = _route_out_specs(b, t, d, WIDE_TOKEN_TILE)
    return pl.pallas_call(
        _sconv_kernel,
        grid=(b, nt),
        in_specs=_moe_out_specs(b, t, d, WIDE_TOKEN_TILE) + [tile, _mod_spec3(5, d, batch0)]
        + [_mod_spec3(0, d, batch0), _mod_spec3(1, d, batch0), _mod_spec3(2, d, batch0)]
        + [_full2(a) for a in params_a]
        + [_mod_spec3(3, d, batch0), _mod_spec3(4, d, batch0)] + [_full2(a) for a in params_b],
        out_specs=out_specs,
        out_shape=out_shape,
        compiler_params=pltpu.CompilerParams(
            dimension_semantics=("arbitrary", "arbitrary"), vmem_limit_bytes=VMEM_LIMIT_BYTES),
        name="sconv",
    )(yg, yg, gates, h_prev, mod_prev, mod, mod, mod, *params_a, mod, mod, *params_b)


def _positions_kernel(eidx_ref, pos_ref, ends_ref, run_ref):
    phase, i = pl.program_id(0), pl.program_id(1)
    tp = eidx_ref.shape[1]
    expert = lax.broadcasted_iota(jnp.int32, (N_EXPERTS, tp), 0)
    onehot = [jnp.where(eidx_ref[k:k + 1, :] == expert, 1.0, 0.0) for k in range(2)]

    @pl.when((phase == 0) & (i == 0))
    def _():
        run_ref[...] = jnp.zeros_like(run_ref)

    @pl.when(phase == 0)
    def _():
        run_ref[...] += jnp.sum(onehot[0] + onehot[1], axis=1, keepdims=True)

    @pl.when((phase == 1) & (i == 0))
    def _():
        padded = jnp.floor((run_ref[...] + (EXPERT_ROW_TILE - 1)) * (1.0 / EXPERT_ROW_TILE)) * EXPERT_ROW_TILE
        ei = lax.broadcasted_iota(jnp.int32, (N_EXPERTS, N_EXPERTS), 0)
        ej = lax.broadcasted_iota(jnp.int32, (N_EXPERTS, N_EXPERTS), 1)
        below = jnp.where(ej < ei, 1.0, 0.0)
        starts = jnp.dot(below, jnp.broadcast_to(padded, (N_EXPERTS, LANES)), precision=lax.Precision.HIGHEST,
                         preferred_element_type=F32)
        ends_ref[...] = (starts + padded).astype(jnp.int32)
        run_ref[...] = starts[:, 0:1]

    @pl.when(phase == 1)
    def _():
        ti = lax.broadcasted_iota(jnp.int32, (tp, tp), 0)
        tj = lax.broadcasted_iota(jnp.int32, (tp, tp), 1)
        before = jnp.where(ti < tj, 1.0, 0.0).astype(BF16)
        run = run_ref[...]
        for k in range(2):
            prefix = jnp.dot(onehot[k].astype(BF16), before, preferred_element_type=F32)
            pos_ref[k:k + 1, :] = jnp.sum(onehot[k] * (prefix + run), axis=0, keepdims=True).astype(jnp.int32)
            run = run + jnp.sum(onehot[k], axis=1, keepdims=True)
        run_ref[...] = run


def _positions(eidx):
    n = eidx.shape[1]
    nt = n // POSITION_TILE
    return pl.pallas_call(
        _positions_kernel,
        grid=(2, nt),
        in_specs=[pl.BlockSpec((2, POSITION_TILE), lambda p, i: (0, i))],
        out_specs=[pl.BlockSpec((2, POSITION_TILE), lambda p, i: (0, i * p)),
                   pl.BlockSpec((N_EXPERTS, LANES), lambda p, i: (0, 0))],
        out_shape=[jax.ShapeDtypeStruct((2, n), jnp.int32), jax.ShapeDtypeStruct((N_EXPERTS, LANES), jnp.int32)],
        scratch_shapes=[pltpu.VMEM((N_EXPERTS, 1), F32)],
        compiler_params=pltpu.CompilerParams(
            dimension_semantics=("arbitrary", "arbitrary"), vmem_limit_bytes=VMEM_LIMIT_BYTES),
        name="positions",
    )(eidx)


def _sc_mesh():
    return plsc.VectorSubcoreMesh(core_axis_name="c", subcore_axis_name="s")


def _sc_worker(n_workers_per_core=SC_SUBCORES):
    return lax.axis_index("c") * n_workers_per_core + lax.axis_index("s")


def _sc_sort_rows(table, pos, n_rows):
    n_slots, width = pos.shape[0], table.shape[1]
    n_tokens = n_slots // 2
    assert n_tokens & (n_tokens - 1) == 0
    per_worker = n_rows // SC_WORKERS
    assert per_worker * SC_WORKERS == n_rows and per_worker % SC_GATHER_ROWS == 0 and n_slots % SC_LANES == 0

    def body(table_hbm, pos_hbm, out_hbm, pos_v, inv_v, idx_v, rows_v, sem):
        base = _sc_worker() * per_worker
        pltpu.sync_copy(pos_hbm, pos_v)

        @pl.loop(0, per_worker, step=SC_LANES)
        def _(j):
            inv_v[pl.ds(j, SC_LANES)] = (base + j + lax.iota(jnp.int32, SC_LANES)) & (n_tokens - 1)

        @pl.loop(0, n_slots, step=SC_LANES)
        def _(s):
            local = pos_v[pl.ds(s, SC_LANES)] - base
            mine = (local >= 0) & (local < per_worker)
            token = (s + lax.iota(jnp.int32, SC_LANES)) & (n_tokens - 1)
            plsc.store_scatter(inv_v, [jnp.where(mine, local, 0)], token, mask=mine)

        @pl.loop(0, per_worker, step=SC_GATHER_ROWS)
        def _(j):
            @pl.loop(0, SC_GATHER_ROWS, step=SC_LANES)
            def _(q):
                idx_v[pl.ds(q, SC_LANES)] = inv_v[pl.ds(j + q, SC_LANES)]

            pltpu.async_copy(table_hbm.at[idx_v], rows_v, sem).wait()
            pltpu.sync_copy(rows_v, out_hbm.at[pl.ds(base + j, SC_GATHER_ROWS)])

    return pl.kernel(
        body, out_type=jax.ShapeDtypeStruct((n_rows, width), table.dtype), mesh=_sc_mesh(),
        scratch_types=[pltpu.VMEM((n_slots,), jnp.int32), pltpu.VMEM((per_worker,), jnp.int32),
                       pltpu.VMEM((SC_GATHER_ROWS,), jnp.int32), pltpu.VMEM((SC_GATHER_ROWS, width), table.dtype),
                       pltpu.SemaphoreType.DMA],
        compiler_params=pltpu.CompilerParams(needs_layout_passes=False),
        name="sc_sort_rows",
    )(table, pos)


def _sc_gather(table, idx):
    n_rows, width = idx.shape[0], table.shape[1]
    per_worker = n_rows // SC_WORKERS
    assert per_worker * SC_WORKERS == n_rows and per_worker % SC_GATHER_ROWS == 0

    def body(table_hbm, idx_hbm, out_hbm, idx_v, rows_v, sem):
        base = _sc_worker() * per_worker

        @pl.loop(0, per_worker, step=SC_GATHER_ROWS)
        def _(j):
            pltpu.sync_copy(idx_hbm.at[pl.ds(base + j, SC_GATHER_ROWS)], idx_v)
            pltpu.async_copy(table_hbm.at[idx_v], rows_v, sem).wait()
            pltpu.sync_copy(rows_v, out_hbm.at[pl.ds(base + j, SC_GATHER_ROWS)])

    return pl.kernel(
        body, out_type=jax.ShapeDtypeStruct((n_rows, width), table.dtype), mesh=_sc_mesh(),
        scratch_types=[pltpu.VMEM((SC_GATHER_ROWS,), jnp.int32), pltpu.VMEM((SC_GATHER_ROWS, width), table.dtype),
                       pltpu.SemaphoreType.DMA],
        name="sc_gather",
    )(table, idx)


def _experts_kernel(layer, te_ref, seg_ref, nxt_ref, xs_ref, wg_hbm, wu_hbm, wd_hbm, ys_ref,
                    wg_f, wu_f, wd_f, wg_bf, wu_bf, wd_bf, sems):
    i = pl.program_id(0)
    expert = te_ref[i]
    held = jnp.minimum(expert, N_EXPERTS - 1)
    slot = seg_ref[i] & 1
    new_segment = (i == 0) | (seg_ref[i] != seg_ref[jnp.maximum(i - 1, 0)])

    def weight_copies(e, s):
        return [pltpu.make_async_copy(w_hbm.at[layer, e], w_f.at[s], sems.at[k, s])
                for k, (w_hbm, w_f) in enumerate(((wg_hbm, wg_f), (wu_hbm, wu_f), (wd_hbm, wd_f)))]

    @pl.when(i == 0)
    def _():
        for cp in weight_copies(held, slot):
            cp.start()

    @pl.when(new_segment)
    def _():
        for cp in weight_copies(held, slot):
            cp.wait()
        wg_bf[...] = wg_f[slot].astype(BF16)
        wu_bf[...] = wu_f[slot].astype(BF16)
        wd_bf[...] = wd_f[slot].astype(BF16)

        @pl.when(nxt_ref[i] >= 0)
        def _():
            for cp in weight_copies(nxt_ref[i], 1 - slot):
                cp.start()

    @pl.when(expert < N_EXPERTS)
    def _():
        sub = xs_ref.shape[0] // MIX_SUB_TILES
        tiles = [slice(s * sub, (s + 1) * sub) for s in range(MIX_SUB_TILES)]
        xs = [_unpack_bf16_pairs(xs_ref[rs, :]).astype(BF16) for rs in tiles]
        hes = []
        for x in xs:
            gate, up = _dot(x, wg_bf[...]), _dot(x, wu_bf[...])
            hes.append((gate * _sigmoid(gate) * up).astype(BF16))
        for rs, he in zip(tiles, hes):
            ys_ref[rs, :] = _pack_bf16_pairs(_dot(he, wd_bf[...]))


def _experts(tile_expert, xs, layer, w_gate, w_up, w_down):
    n_rows, half = xs.shape
    _, n_e, d, de = w_gate.shape
    held = jnp.minimum(tile_expert, n_e - 1)
    segment = jnp.cumsum(jnp.concatenate([jnp.zeros((1,), jnp.int32), (held[1:] != held[:-1]).astype(jnp.int32)]))
    after = jnp.sum(held[None, :] <= held[:, None], axis=1)
    next_expert = jnp.where(after < held.shape[0], held[jnp.minimum(after, held.shape[0] - 1)], -1).astype(jnp.int32)

    rows = pl.BlockSpec((EXPERT_ROW_TILE, half), lambda i, te, seg, nxt: (i, 0))
    hbm = pl.BlockSpec(memory_space=pl.ANY)
    return pl.pallas_call(
        functools.partial(_experts_kernel, layer),
        grid_spec=pltpu.PrefetchScalarGridSpec(
            num_scalar_prefetch=3,
            grid=(n_rows // EXPERT_ROW_TILE,),
            in_specs=[rows, hbm, hbm, hbm],
            out_specs=rows,
            scratch_shapes=[pltpu.VMEM((2, d, de), F32), pltpu.VMEM((2, d, de), F32), pltpu.VMEM((2, de, d), F32),
                            pltpu.VMEM((d, de), BF16), pltpu.VMEM((d, de), BF16), pltpu.VMEM((de, d), BF16),
                            pltpu.SemaphoreType.DMA((3, 2))]),
        out_shape=jax.ShapeDtypeStruct((n_rows, half), jnp.int32),
        compiler_params=pltpu.CompilerParams(
            dimension_semantics=("arbitrary",), vmem_limit_bytes=VMEM_LIMIT_BYTES),
        name="experts",
    )(tile_expert, segment.astype(jnp.int32), next_expert, xs, w_gate, w_up, w_down)


def _final_kernel(y0_ref, y1_ref, gate_ref, h1_ref, gt_ref, fg_ref, *out_refs):
    o_ref = out_refs[-1]
    h2 = _moe_residual(y0_ref, y1_ref, gate_ref, h1_ref[0], gt_ref[0])
    ms = jnp.mean(h2 * h2, axis=-1, keepdims=True)
    o_ref[0] = h2 * lax.rsqrt(ms + RMS_EPS) * fg_ref[...]


def _moe_out_specs(b, t, d, tile=TOKEN_TILE):
    nt = t // tile
    return [pl.BlockSpec((tile, d // 2), lambda i, j: (i * nt + j, 0)),
            pl.BlockSpec((tile, d // 2), lambda i, j: (b * nt + i * nt + j, 0)),
            pl.BlockSpec((tile, 2), lambda i, j: (i * nt + j, 0))]


def _final(yg, gates, h1, mod, final_g, batch0, n_batch, out_prev):
    b, t, d = h1.shape
    tile = pl.BlockSpec((1, WIDE_TOKEN_TILE, d), lambda i, j: (i, j, 0))
    in_specs = _moe_out_specs(b, t, d, WIDE_TOKEN_TILE) + [tile, _mod_spec3(5, d, batch0),
                                                           pl.BlockSpec((1, d), lambda i, j: (0, 0))]
    args = [yg, yg, gates, h1, mod, final_g]
    aliases = {}
    if out_prev is not None:
        in_specs.append(pl.BlockSpec(memory_space=pl.ANY))
        aliases = {len(args): 0}
        args.append(out_prev)
    return pl.pallas_call(
        _final_kernel,
        grid=(b, t // WIDE_TOKEN_TILE),
        in_specs=in_specs,
        out_specs=pl.BlockSpec((1, WIDE_TOKEN_TILE, d), lambda i, j: (i + batch0, j, 0)),
        out_shape=jax.ShapeDtypeStruct((n_batch, t, d), F32),
        input_output_aliases=aliases,
        compiler_params=pltpu.CompilerParams(
            dimension_semantics=("arbitrary", "arbitrary"), vmem_limit_bytes=VMEM_LIMIT_BYTES),
        name="final",
    )(*args)


def _moe_sort(hm, eidx):
    n_slots = 2 * hm.shape[0]
    n_rows = n_slots + N_EXPERTS * EXPERT_ROW_TILE
    pos, ends = _positions(eidx)
    pos = pos.reshape(n_slots)
    tile_start = jnp.arange(n_rows // EXPERT_ROW_TILE, dtype=jnp.int32) * EXPERT_ROW_TILE
    tile_expert = jnp.sum(tile_start[:, None] >= ends[None, :, 0], axis=1).astype(jnp.int32)
    return _sc_sort_rows(hm, pos, n_rows), pos, tile_expert


def _moe_apply(sorted_rows, gates, layer, w_gate, w_up, w_down):
    xs, pos, tile_expert = sorted_rows
    ys = _experts(tile_expert, xs, layer, w_gate, w_up, w_down)
    return _sc_gather(ys, pos), gates.T


def kernel(x, c, ctx, c_ctx, ada_w, ada_b, norm_g, rw_mu, rw_w_rkv, rw_w0, rw_w1, rw_w2, rw_a0, rw_a1, rw_a2, rw_g1, rw_g2, rw_k_k, rw_k_a, rw_r_k, rw_gn_w, rw_gn_b, rw_w_o, sc_w_in, sc_conv, sc_w_out, router_w, router_b, moe_w_gate, moe_w_up, moe_w_down, final_g):
    b, t, d = x.shape
    ctx_len = ctx.shape[1]
    depth = ada_w.shape[0]
    assert d == D_MODEL and depth == 2 and t % POSITION_TILE == 0

    mod_rows = 16
    cc = jnp.concatenate([c, c_ctx[None, :], jnp.zeros((mod_rows - b - 1, d), F32)], axis=0)
    mod = _ada(cc, ada_w, ada_b).reshape(depth, mod_rows, 1, 6 * d)

    row = lambda a: a.reshape(1, d)
    router_wt = jnp.pad(router_w, ((0, 0), (0, LANES - N_EXPERTS)))
    router_bc = router_b.reshape(N_EXPERTS, 1)

    w1 = jnp.concatenate([rw_w1[0, 0], rw_w1[0, 1]], axis=1).astype(BF16)
    a1 = jnp.concatenate([rw_a1[0, 0], rw_a1[0, 1]], axis=1).astype(BF16)

    def pad_dirs(w):
        z = jnp.zeros_like(w[0])
        return jnp.stack([jnp.concatenate([w[0], z], axis=0), jnp.concatenate([z, w[1]], axis=0)]).astype(BF16)

    assert b % N_STREAMS == 0
    nb = b // N_STREAMS
    pre_params = (row(norm_g[0, 0]), rw_mu[0], rw_w_rkv[0].astype(BF16), w1, pad_dirs(rw_w2[0]), rw_w0[0], a1,
                  pad_dirs(rw_a2[0]), rw_a0[0], rw_g1[0].astype(BF16), rw_g2[0].astype(BF16), row(rw_k_k[0]),
                  row(rw_k_a[0]), row(rw_r_k[0]))
    w_o, w_in, w_out = rw_w_o[0].astype(BF16), sc_w_in[0].astype(BF16), sc_w_out[0].astype(BF16)
    out = None
    moe_w = (moe_w_gate, moe_w_up, moe_w_down)
    streams = []
    for batch0 in range(0, b, nb):
        r, v, g, bonus, k, kk, cs0, cs1, a0, a1 = _rwkv_pre(ctx, x, mod[0], *pre_params, batch0, nb)
        yf, yr = _wkv(row(rw_k_a[0]), r, v, k, kk, cs0, a0, cs1, a1, ctx_len)
        h1, hm, eidx, gates = _readout(yf, yr, bonus, g, x, mod[0], row(rw_gn_w[0]), row(rw_gn_b[0]),
                                       w_o, row(norm_g[0, 1]), router_wt, router_bc, batch0, nb)
        streams.append((batch0, h1, gates, _moe_sort(hm, eidx)))
    if len(streams) > 1:
        batch0, h1, gates, (xs, pos, tile_expert) = streams[0]
        tile_expert, _ = lax.optimization_barrier((tile_expert, yf))
        streams[0] = (batch0, h1, gates, (xs, pos, tile_expert))
    for batch0, h1, gates, sorted_rows in streams:
        yg, gates = _moe_apply(sorted_rows, gates, 0, *moe_w)
        h1, hm, eidx, gates = _sconv(yg, gates, h1, mod[0], mod[1], row(norm_g[1, 0]), w_in, sc_conv[0], w_out,
                                     row(norm_g[1, 1]), router_wt, router_bc, batch0)
        yg, gates = _moe_apply(_moe_sort(hm, eidx), gates, 1, *moe_w)
        out = _final(yg, gates, h1, mod[1], row(final_g), batch0, b, out)
    return out
```

```python
import functools

import jax
import jax.numpy as jnp
from jax import lax
from jax.experimental import pallas as pl
from jax.experimental.pallas import tpu as pltpu
from jax.experimental.pallas import tpu_sc as plsc

F32 = jnp.float32
BF16 = jnp.bfloat16

D_MODEL = 1024
HEAD_DIM = 64
LANES = 128
N_PAIRS = D_MODEL // LANES
CHUNK = 64
N_EXPERTS = 16
EXPERTS_PER_GROUP = 4
N_GROUPS = N_EXPERTS // EXPERTS_PER_GROUP
RMS_EPS = 1e-6
GN_EPS = 64e-5
L2_EPS = 1e-12

TOKEN_TILE = 256
WIDE_TOKEN_TILE = 512
PRE_SUB_TILES = 2
MIX_SUB_TILES = 2
EXPERT_ROW_TILE = 512
POSITION_TILE = 1024
N_STREAMS = 2
SC_CORES, SC_SUBCORES, SC_LANES = 2, 16, 16
SC_WORKERS = SC_CORES * SC_SUBCORES
SC_GATHER_ROWS = 128
PAIRS_PER_STEP = 8
CHUNKS_PER_STEP = 4
STAGE_LAG = 2
ADA_COL_TILE = 1536
VMEM_LIMIT_BYTES = 56 * 1024 * 1024


def _sigmoid(x):
    return 1.0 / (1.0 + jnp.exp(-x))


def _mm(a, b):
    return jnp.dot(a.astype(BF16), b.astype(BF16), preferred_element_type=F32)


def _split2(x):
    hi = x.astype(BF16)
    return hi, (x - hi.astype(F32)).astype(BF16)


def _head_sum(x):
    rows = x.shape[0]
    left = lax.broadcasted_iota(jnp.int32, (rows, LANES), 1) < HEAD_DIM
    outs = []
    for j in range(N_PAIRS):
        xb = x[:, LANES * j:LANES * (j + 1)]
        sa = jnp.sum(jnp.where(left, xb, 0.0), axis=-1, keepdims=True)
        sb = jnp.sum(jnp.where(left, 0.0, xb), axis=-1, keepdims=True)
        outs.append(jnp.where(left, sa, sb))
    return jnp.concatenate(outs, axis=-1)


def _rms_modulate(x, g, shift, scale):
    ms = jnp.mean(x * x, axis=-1, keepdims=True)
    xn = x * lax.rsqrt(ms + RMS_EPS) * g
    return xn * (1.0 + scale) + shift


def _row_neighbours(x, row_len):
    rows = x.shape[0]
    pos = lax.broadcasted_iota(jnp.int32, x.shape, 0) & (row_len - 1)
    prev = jnp.where(pos == 0, 0.0, pltpu.roll(x, 1, 0))
    nxt = jnp.where(pos == row_len - 1, 0.0, pltpu.roll(x, rows - 1, 0))
    return prev, nxt


def _ada_kernel(c_ref, w_ref, b_ref, o_ref):
    c = c_ref[...]
    s = c * _sigmoid(c)
    o_ref[0] = _mm(s, w_ref[0]) + b_ref[0]


def _ada(cc, ada_w, ada_b):
    depth, d, n = ada_w.shape
    rows = cc.shape[0]
    return pl.pallas_call(
        _ada_kernel,
        grid=(depth, n // ADA_COL_TILE),
        in_specs=[
            pl.BlockSpec((rows, d), lambda l, j: (0, 0)),
            pl.BlockSpec((1, d, ADA_COL_TILE), lambda l, j: (l, 0, j)),
            pl.BlockSpec((1, 1, ADA_COL_TILE), lambda l, j: (l, 0, j)),
        ],
        out_specs=pl.BlockSpec((1, rows, ADA_COL_TILE), lambda l, j: (l, 0, j)),
        out_shape=jax.ShapeDtypeStruct((depth, rows, n), F32),
        compiler_params=pltpu.CompilerParams(
            dimension_semantics=("arbitrary", "arbitrary"), vmem_limit_bytes=VMEM_LIMIT_BYTES),
        name="ada",
    )(cc, ada_w, ada_b.reshape(depth, 1, n))


def _rwkv_pre_kernel(ctx_len, ctx_ref, x_ref, sh_ref, sc_ref, ng_ref, mu_ref, wrkv_ref, w1_ref, w2_ref, w0_ref,
                     a1_ref, a2_ref, a0_ref, g1_ref, g2_ref, kk_ref, ka_ref, rk_ref,
                     r_out, v_out, g_out, bon_out, k_out, kk_out, cs0_out, cs1_out, a0_out, a1_out):
    j = pl.program_id(1)
    row_len = jnp.where(j == 0, ctx_len, CHUNK)
    xin = jnp.where(j == 0, ctx_ref[0], x_ref[0])
    xn = _rms_modulate(xin, ng_ref[...], sh_ref[0], sc_ref[0])
    prev, nxt = _row_neighbours(xn, row_len)
    xx = 0.5 * (prev + nxt) - xn

    sub = xn.shape[0] // PRE_SUB_TILES
    projected = []
    for s in range(PRE_SUB_TILES):
        rs = slice(s * sub, (s + 1) * sub)
        xn_s, xx_s = xn[rs], xx[rs]

        def mix(i, xn_s=xn_s, xx_s=xx_s):
            return xn_s + xx_s * mu_ref[i:i + 1, :]

        r = _mm(mix(0), wrkv_ref[0])
        k = _mm(mix(1), wrkv_ref[1])
        v = _mm(mix(2), wrkv_ref[2])
        g = _mm(_sigmoid(_mm(mix(5), g1_ref[...])), g2_ref[...])
        wl = jnp.tanh(_mm(mix(3), w1_ref[...]))
        al = _mm(mix(4), a1_ref[...])
        projected.append((rs, r, k, v, g, [_mm(wl, w2_ref[p]) for p in range(2)],
                          [_mm(al, a2_ref[p]) for p in range(2)]))

    ti = lax.broadcasted_iota(jnp.int32, (sub, sub), 0)
    tj = lax.broadcasted_iota(jnp.int32, (sub, sub), 1)
    same_chunk = (ti ^ tj) < CHUNK
    dir_outs = ((cs0_out, a0_out), (cs1_out, a1_out))
    for rs, r, k, v, g, zs, a_logits in projected:
        r_out[0, rs, :] = r.astype(r_out.dtype)
        v_out[0, rs, :] = v.astype(v_out.dtype)
        g_out[0, rs, :] = g.astype(g_out.dtype)
        k_out[0, rs, :] = k.astype(k_out.dtype)
        kk = k * kk_ref[...]
        kk_out[0, rs, :] = (kk * lax.rsqrt(jnp.maximum(_head_sum(kk * kk), L2_EPS * L2_EPS))).astype(kk_out.dtype)
        a_sum = None
        for p, (cs_out, a_out) in enumerate(dir_outs):
            z = w0_ref[p:p + 1, :] + zs[p]
            lw = -jnp.exp(-0.5) * _sigmoid(z)
            tri = jnp.where(same_chunk & ((tj >= ti) if p else (tj <= ti)), 1.0, 0.0).astype(BF16)
            hi, lo = _split2(lw)
            cs_out[0, rs, :] = _dot(tri, hi) + _dot(tri, lo)
            a = _sigmoid(a0_ref[p:p + 1, :] + a_logits[p])
            a_out[0, rs, :] = a.astype(a_out.dtype)
            a_sum = a if a_sum is None else a_sum + a
        k_dirs = k * (2.0 + (a_sum - 2.0) * ka_ref[...])
        bon_out[0, rs, :] = (_head_sum(r * rk_ref[...] * k_dirs) * v).astype(bon_out.dtype)


def _rwkv_pre(ctx, x, mod, norm_g, mu, w_rkv, w1, w2, w0, a1, a2, a0, g1, g2, k_k, k_a, r_k, batch0, b):
    n_batch, t, d = x.shape
    ctx_len = ctx.shape[1]
    tt = ctx_len + t
    assert ctx_len == TOKEN_TILE and t % TOKEN_TILE == 0

    def mod_spec(part):
        return pl.BlockSpec((1, 1, d), lambda i, j: (jnp.where(j == 0, n_batch, i + batch0), 0, part))

    def full(a):
        nd = a.ndim
        return pl.BlockSpec(a.shape, lambda i, j: (0,) * nd)

    tile = pl.BlockSpec((1, TOKEN_TILE, d), lambda i, j: (i, j, 0))
    latent = pl.BlockSpec((1, TOKEN_TILE, d), lambda i, j: (i, jnp.maximum(j - 1, 0), 0))
    params = (norm_g, mu, w_rkv, w1, w2, w0, a1, a2, a0, g1, g2, k_k, k_a, r_k)
    out_dtypes = (BF16,) * 6 + (F32, F32) + (BF16,) * 2
    out_specs = [tile, tile, latent, latent] + [tile] * 6
    return pl.pallas_call(
        functools.partial(_rwkv_pre_kernel, ctx_len),
        grid=(b, tt // TOKEN_TILE),
        in_specs=[pl.BlockSpec((1, TOKEN_TILE, d), lambda i, j: (i + batch0, 0, 0)),
                  pl.BlockSpec((1, TOKEN_TILE, d), lambda i, j: (i + batch0, jnp.maximum(j - 1, 0), 0)),
                  mod_spec(0), mod_spec(1)] + [full(a) for a in params],
        out_specs=out_specs,
        out_shape=[jax.ShapeDtypeStruct((b, t if spec is latent else tt, d), dt)
                   for spec, dt in zip(out_specs, out_dtypes)],
        compiler_params=pltpu.CompilerParams(
            dimension_semantics=("arbitrary", "arbitrary"), vmem_limit_bytes=VMEM_LIMIT_BYTES),
        name="rwkv_pre",
    )(ctx, x, mod, mod, *params)


def _expand(x, left):
    return jnp.concatenate([jnp.where(left, x, 0.0), jnp.where(left, 0.0, x)], axis=0)


def _dot(a, b):
    return jnp.dot(a, b, preferred_element_type=F32)


def _chunk_steps(chains, group_size):
    assert CHUNK == HEAD_DIM
    c = CHUNK
    lane = lax.broadcasted_iota(jnp.int32, (c, LANES), 1)
    left = lane < HEAD_DIM
    tt = lax.broadcasted_iota(jnp.int32, (c, LANES), 0)
    jj = lane & (c - 1)
    diag = jj == tt
    tri = {False: (jj < tt, jj <= tt), True: (jj > tt, jj >= tt)}

    def bd(x):
        return _expand(x, left).astype(BF16)

    def fold_t(x):
        xt = _expand(x, left).T
        return xt[:c] + xt[c:]

    def rows(*xs):
        return jnp.concatenate(xs, axis=0).astype(BF16)

    nt_dims = (((1,), (1,)), ((), ()))

    def s_prep(q):
        cs, r, v, k, kk, a, ka, h, reverse = q.pop("chain")
        tot = cs[0:1, :] if reverse else cs[c - 1:c, :]
        e_pos, e_neg, e_rem = jnp.exp(cs), jnp.exp(-cs), jnp.exp(tot - cs)
        first = tt == (c - 1 if reverse else 0)
        cs_prev = jnp.where(first, 0.0, pltpu.roll(cs, c - 1 if reverse else 1, 0))
        at, rt = -kk * jnp.exp(cs_prev), r * e_pos
        kd, be = k * (1.0 + (a - 1.0) * ka), kk * a
        q.update(a_e=bd(at), rt=rt, v_e=bd(v), g_tot=jnp.exp(tot), h=h, tri=tri[reverse])
        q["hat_t"] = jnp.concatenate([fold_t(be * e_rem), fold_t(kd * e_rem)], axis=1)
        q["sc"] = lax.dot_general(rows(at, rt), jnp.concatenate([bd(be * e_neg), bd(kd * e_neg)], axis=0),
                                  nt_dims, preferred_element_type=F32)

    def s_mask(q):
        (strict, incl), sc = q["tri"], q["sc"]
        q["n"] = jnp.where(strict, sc[:c, :LANES], 0.0)
        a_ak = jnp.where(strict, sc[:c, LANES:], 0.0)
        q["a_rb"] = jnp.where(incl, sc[c:, :LANES], 0.0)
        a_rk = jnp.where(incl, sc[c:, LANES:], 0.0)
        q["bh_t"] = q["hat_t"][:, :LANES]
        q["vv"] = _dot(rows(a_ak, q["hat_t"][:, LANES:], a_rk), q["v_e"])

    def s_square(q):
        q["t"] = jnp.where(diag, 1.0, 0.0) + q["n"]
        q["p"] = _dot(q["n"].astype(BF16), bd(q["n"]))

    def s_level(q):
        x = _dot(rows(q["p"], q["t"]), bd(q["p"]))
        q["p"] = x[:c]
        q["t"] = q["t"] + x[c:]

    def s_last_level(q):
        q["t"] = q["t"] + _dot(q["t"].astype(BF16), bd(q["p"]))

    def s_solve(q):
        xu = _dot(q["t"].astype(BF16), jnp.concatenate([q["a_e"], bd(q["vv"][:c])], axis=1))
        q["au_e"] = jnp.concatenate([bd(xu[:, :LANES]), bd(xu[:, LANES:])], axis=1)

    def s_affine(q):
        z = _dot(rows(q["bh_t"], q["a_rb"]), q["au_e"])
        q["m"] = z[:c, :LANES] + jnp.where(diag, q["g_tot"], 0.0)
        q["g"] = z[:c, LANES:] + q["vv"][c:2 * c]
        q["r_hat"] = q["rt"] + z[c:, :LANES]
        q["yi"] = z[c:, LANES:] + q["vv"][2 * c:]

    n_levels = (c // 4).bit_length() - 1
    stages = [s_prep, s_mask, s_square] + [s_level] * n_levels + [s_last_level, s_solve, s_affine]
    qs = [dict(chain=ch) for ch in chains]
    groups = [qs[i:i + group_size] for i in range(0, len(qs), group_size)]
    for tau in range(len(stages) + (len(groups) - 1) * STAGE_LAG):
        for gi, group in enumerate(groups):
            si = tau - gi * STAGE_LAG
            if 0 <= si < len(stages):
                for q in group:
                    stages[si](q)
    outs = []
    for q in qs:
        h = outs[q["h"]][1] if isinstance(q["h"], int) else q["h"]
        o = _dot(rows(q["r_hat"], q["m"]), bd(h))
        outs.append((o[:c] + q["yi"], o[c:] + q["g"]))
    return outs


def _wkv_kernel(ka_ref, rf_ref, vf_ref, kf_ref, kkf_ref, csf_ref, af_ref,
                rr_ref, vr_ref, kr_ref, kkr_ref, csr_ref, ar_ref,
                yf_ref, yr_ref, hf_ref, hr_ref):
    @pl.when(pl.program_id(2) == 0)
    def _():
        hf_ref[...] = jnp.zeros_like(hf_ref)
        hr_ref[...] = jnp.zeros_like(hr_ref)

    dirs = ((rf_ref, vf_ref, kf_ref, kkf_ref, csf_ref, af_ref, yf_ref, hf_ref),
            (rr_ref, vr_ref, kr_ref, kkr_ref, csr_ref, ar_ref, yr_ref, hr_ref))
    chains, dests = [], []
    per_chunk = 2 * PAIRS_PER_STEP
    for u in range(CHUNKS_PER_STEP):
        for p in range(PAIRS_PER_STEP):
            sl = slice(p * LANES, (p + 1) * LANES)
            for reverse, (r_ref, v_ref, k_ref, kk_ref, cs_ref, a_ref, y_ref, h_ref) in enumerate(dirs):
                at = CHUNKS_PER_STEP - 1 - u if reverse else u
                tm = slice(at * CHUNK, (at + 1) * CHUNK)
                h = h_ref[p] if u == 0 else len(chains) - per_chunk
                chains.append((cs_ref[0, tm, sl], r_ref[0, tm, sl].astype(F32), v_ref[0, tm, sl].astype(F32),
                               k_ref[0, tm, sl].astype(F32), kk_ref[0, tm, sl].astype(F32),
                               a_ref[0, tm, sl].astype(F32), ka_ref[:, sl], h, bool(reverse)))
                dests.append((y_ref, tm, sl, h_ref if u == CHUNKS_PER_STEP - 1 else None, p))
    for (y, h_new), (y_ref, tm, sl, h_ref, p) in zip(_chunk_steps(chains, per_chunk), dests):
        y_ref[0, tm, sl] = y
        if h_ref is not None:
            h_ref[p] = h_new


def _wkv(k_a, r, v, k, kk, cs0, a0, cs1, a1, ctx_len):
    b, tt, d = r.shape
    step_rows = CHUNKS_PER_STEP * CHUNK
    n_steps, n_ctx = tt // step_rows, ctx_len // step_rows
    assert n_steps * step_rows == tt and n_ctx * step_rows == ctx_len
    width = PAIRS_PER_STEP * LANES

    def fwd_map(i, j, s):
        return (i, s, j)

    def rev_map(i, j, s):
        return (i, jnp.where(s < n_ctx, n_ctx - 1 - s, n_steps - 1 - (s - n_ctx)), j)

    n_lat = n_steps - n_ctx

    def fwd_out(i, j, s):
        return (i, jnp.maximum(s - n_ctx, 0), j)

    def rev_out(i, j, s):
        return (i, jnp.where(s < n_ctx, n_lat - 1, n_lat - 1 - (s - n_ctx)), j)

    fwd = pl.BlockSpec((1, step_rows, width), fwd_map)
    rev = pl.BlockSpec((1, step_rows, width), rev_map)
    return pl.pallas_call(
        _wkv_kernel,
        grid=(b, d // width, n_steps),
        in_specs=[pl.BlockSpec((1, width), lambda i, j, s: (0, j))] + [fwd] * 6 + [rev] * 6,
        out_specs=[pl.BlockSpec((1, step_rows, width), fwd_out), pl.BlockSpec((1, step_rows, width), rev_out)],
        out_shape=[jax.ShapeDtypeStruct((b, tt - ctx_len, d), F32)] * 2,
        scratch_shapes=[pltpu.VMEM((PAIRS_PER_STEP, HEAD_DIM, LANES), F32)] * 2,
        compiler_params=pltpu.CompilerParams(
            dimension_semantics=("arbitrary", "arbitrary", "arbitrary"), vmem_limit_bytes=VMEM_LIMIT_BYTES),
        name="wkv",
    )(k_a, r, v, k, kk, cs0, a0, r, v, k, kk, cs1, a1)


def _pack_bf16_pairs(x):
    half = x.shape[1] // 2
    lo = pltpu.bitcast(x[:, :half].astype(BF16).astype(F32), jnp.int32)
    hi = pltpu.bitcast(x[:, half:].astype(BF16).astype(F32), jnp.int32)
    return lax.shift_right_logical(lo, jnp.int32(16)) | (hi & jnp.int32(-65536))


def _unpack_bf16_pairs(w):
    lo = pltpu.bitcast(lax.shift_left(w, jnp.int32(16)), F32)
    hi = pltpu.bitcast(w & jnp.int32(-65536), F32)
    return jnp.concatenate([lo, hi], axis=1)


def _route(h1, ng, shift, scale, rwt_ref, rb_ref, hm_out, eidx_out, gate_out, rs=slice(None)):
    hm = _rms_modulate(h1, ng, shift, scale)
    hm_out[rs, :] = _pack_bf16_pairs(hm)
    w_hi, w_mid = _split2(rwt_ref[...])
    h_hi, h_mid = _split2(hm)
    logits_t = _dot(h_hi, w_hi) + _dot(h_mid, w_hi) + _dot(h_hi, w_mid)
    logits = logits_t.T[:N_EXPERTS]
    s = _sigmoid(logits)
    sel = s + rb_ref[...]
    assert EXPERTS_PER_GROUP == 4 and N_GROUPS == 4
    cands = []
    for g in range(N_GROUPS):
        m = [sel[e:e + 1, :] for e in range(g * EXPERTS_PER_GROUP, (g + 1) * EXPERTS_PER_GROUP)]
        sg = [s[e:e + 1, :] for e in range(g * EXPERTS_PER_GROUP, (g + 1) * EXPERTS_PER_GROUP)]
        pairs = [m[i] + m[k] for i in range(4) for k in range(i + 1, 4)]
        score = jnp.maximum(jnp.maximum(jnp.maximum(pairs[0], pairs[1]), jnp.maximum(pairs[2], pairs[3])),
                            jnp.maximum(pairs[4], pairs[5]))
        chosen = []
        for i in range(4):
            ahead = [((m[k] >= m[i]) if k < i else (m[k] > m[i])).astype(jnp.int32) for k in range(4) if k != i]
            chosen.append(ahead[0] + ahead[1] + ahead[2] < 2)
        base = g * EXPERTS_PER_GROUP
        lo_idx = jnp.where(chosen[0], base, jnp.where(chosen[1], base + 1, base + 2))
        lo_gate = jnp.where(chosen[0], sg[0], jnp.where(chosen[1], sg[1], sg[2]))
        hi_idx = jnp.where(chosen[3], base + 3, jnp.where(chosen[2], base + 2, base + 1))
        hi_gate = jnp.where(chosen[3], sg[3], jnp.where(chosen[2], sg[2], sg[1]))
        cands.append((score, lo_idx, hi_idx, lo_gate, hi_gate))

    def better(x, y):
        win = y[0] > x[0]
        return tuple(jnp.where(win, yv, xv) for xv, yv in zip(x, y))

    _, e_lo, e_hi, g_lo, g_hi = better(better(cands[0], cands[1]), better(cands[2], cands[3]))
    eidx_out[:, rs] = jnp.concatenate([e_lo, e_hi], axis=0)
    gate_out[:, rs] = jnp.concatenate([g_lo, g_hi], axis=0) / (g_lo + g_hi)


def _readout_kernel(yf_ref, yr_ref, bon_ref, g_ref, h_ref, gt_ref, gnw_ref, gnb_ref, wo_ref,
                    ng_ref, sh_ref, sc_ref, rwt_ref, rb_ref, h1_out, hm_out, eidx_out, gate_out):
    sub = h_ref.shape[1] // MIX_SUB_TILES
    tiles = [slice(s * sub, (s + 1) * sub) for s in range(MIX_SUB_TILES)]
    gated, h1s = [], []
    for rs in tiles:
        y = yf_ref[0, rs, :] + yr_ref[0, rs, :] + bon_ref[0, rs, :].astype(F32)
        mean = _head_sum(y) * (1.0 / HEAD_DIM)
        yc = y - mean
        var = _head_sum(yc * yc) * (1.0 / HEAD_DIM)
        yn = yc * lax.rsqrt(var + GN_EPS) * gnw_ref[...] + gnb_ref[...]
        gated.append((yn * g_ref[0, rs, :].astype(F32)).astype(BF16))
    for rs, lhs in zip(tiles, gated):
        h1 = h_ref[0, rs, :] + gt_ref[0] * _dot(lhs, wo_ref[...])
        h1_out[0, rs, :] = h1
        h1s.append(h1)
    for rs, h1 in zip(tiles, h1s):
        _route(h1, ng_ref[...], sh_ref[0], sc_ref[0], rwt_ref, rb_ref, hm_out, eidx_out, gate_out, rs)


def _mod_spec3(part, d, batch0=0):
    return pl.BlockSpec((1, 1, d), lambda i, j: (i + batch0, 0, part))


def _route_out_specs(b, t, d, tile=TOKEN_TILE):
    nt = t // tile
    specs = [pl.BlockSpec((1, tile, d), lambda i, j: (i, j, 0)),
             pl.BlockSpec((tile, d // 2), lambda i, j: (i * nt + j, 0)),
             pl.BlockSpec((2, tile), lambda i, j: (0, i * nt + j)),
             pl.BlockSpec((2, tile), lambda i, j: (0, i * nt + j))]
    shapes = [jax.ShapeDtypeStruct((b, t, d), F32), jax.ShapeDtypeStruct((b * t, d // 2), jnp.int32),
              jax.ShapeDtypeStruct((2, b * t), jnp.int32), jax.ShapeDtypeStruct((2, b * t), F32)]
    return specs, shapes


def _full2(a):
    nd = a.ndim
    return pl.BlockSpec(a.shape, lambda i, j: (0,) * nd)


def _readout(yf, yr, bonus, g, h, mod, gn_w, gn_b, w_o, norm_g, router_wt, router_b, batch0, b):
    _, t, d = h.shape
    nt = t // WIDE_TOKEN_TILE
    local = pl.BlockSpec((1, WIDE_TOKEN_TILE, d), lambda i, j: (i, j, 0))
    tile = pl.BlockSpec((1, WIDE_TOKEN_TILE, d), lambda i, j: (i + batch0, j, 0))
    params_a = (gn_w, gn_b, w_o, norm_g)
    params_b = (router_wt, router_b)
    out_specs, out_shape = _route_out_specs(b, t, d, WIDE_TOKEN_TILE)
    return pl.pallas_call(
        _readout_kernel,
        grid=(b, nt),
        in_specs=[local] * 4 + [tile, _mod_spec3(2, d, batch0)] + [_full2(a) for a in params_a]
        + [_mod_spec3(3, d, batch0), _mod_spec3(4, d, batch0)] + [_full2(a) for a in params_b],
        out_specs=out_specs,
        out_shape=out_shape,
        compiler_params=pltpu.CompilerParams(
            dimension_semantics=("arbitrary", "arbitrary"), vmem_limit_bytes=VMEM_LIMIT_BYTES),
        name="readout",
    )(yf, yr, bonus, g, h, mod, *params_a, mod, mod, *params_b)


def _moe_residual(y0_ref, y1_ref, gate_ref, h1, gt, rs=slice(None)):
    gates = gate_ref[rs, :]
    y = gates[:, 0:1] * _unpack_bf16_pairs(y0_ref[rs, :]) + gates[:, 1:2] * _unpack_bf16_pairs(y1_ref[rs, :])
    return h1 + gt * y


def _sconv_kernel(y0_ref, y1_ref, gate_ref, hp_ref, gtp_ref, sh1_ref, sc1_ref, gt_ref, ng1_ref, win_ref, cw_ref,
                  wout_ref, ng2_ref, sh2_ref, sc2_ref, rwt_ref, rb_ref, h1_out, hm_out, eidx_out, gate_out):
    d = hp_ref.shape[2]
    sub = hp_ref.shape[1] // MIX_SUB_TILES
    tiles = [slice(s * sub, (s + 1) * sub) for s in range(MIX_SUB_TILES)]
    hs, xns, gated, h1s = [], [], [], []
    for rs in tiles:
        h = _moe_residual(y0_ref, y1_ref, gate_ref, hp_ref[0, rs, :], gtp_ref[0], rs)
        hs.append(h)
        xns.append(_rms_modulate(h, ng1_ref[...], sh1_ref[0], sc1_ref[0]).astype(BF16))
    for xn in xns:
        bg = _dot(xn, win_ref[:, 0:d])
        u = _dot(xn, win_ref[:, d:2 * d]) * _dot(xn, win_ref[:, 2 * d:3 * d])
        prev, nxt = _row_neighbours(u, CHUNK)
        conv = cw_ref[0:1, :] * prev + cw_ref[1:2, :] * u + cw_ref[2:3, :] * nxt
        gated.append((bg * conv).astype(BF16))
    for rs, h, lhs in zip(tiles, hs, gated):
        h1 = h + gt_ref[0] * _dot(lhs, wout_ref[...])
        h1_out[0, rs, :] = h1
        h1s.append(h1)
    for rs, h1 in zip(tiles, h1s):
        _route(h1, ng2_ref[...], sh2_ref[0], sc2_ref[0], rwt_ref, rb_ref, hm_out, eidx_out, gate_out, rs)


def _sconv(yg, gates, h_prev, mod_prev, mod, norm_g1, w_in, conv_w, w_out, norm_g2, router_wt, router_b, batch0):
    b, t, d = h_prev.shape
    nt = t // WIDE_TOKEN_TILE
    tile = pl.BlockSpec((1, WIDE_TOKEN_TILE, d), lambda i, j: (i, j, 0))
    params_a = (norm_g1, w_in, conv_w, w_out, norm_g2)
    params_b = (router_wt, router_b)
    out_specs, out_shape = _route_out_specs(b, t, d, WIDE_TOKEN_TILE)
    return pl.pallas_call(
        _sconv_kernel,
        grid=(b, nt),
        in_specs=_moe_out_specs(b, t, d, WIDE_TOKEN_TILE) + [tile, _mod_spec3(5, d, batch0)]
        + [_mod_spec3(0, d, batch0), _mod_spec3(1, d, batch0), _mod_spec3(2, d, batch0)]
        + [_full2(a) for a in params_a]
        + [_mod_spec3(3, d, batch0), _mod_spec3(4, d, batch0)] + [_full2(a) for a in params_b],
        out_specs=out_specs,
        out_shape=out_shape,
        compiler_params=pltpu.CompilerParams(
            dimension_semantics=("arbitrary", "arbitrary"), vmem_limit_bytes=VMEM_LIMIT_BYTES),
        name="sconv",
    )(yg, yg, gates, h_prev, mod_prev, mod, mod, mod, *params_a, mod, mod, *params_b)


def _positions_kernel(eidx_ref, pos_ref, ends_ref, run_ref):
    phase, i = pl.program_id(0), pl.program_id(1)
    tp = eidx_ref.shape[1]
    expert = lax.broadcasted_iota(jnp.int32, (N_EXPERTS, tp), 0)
    onehot = [jnp.where(eidx_ref[k:k + 1, :] == expert, 1.0, 0.0) for k in range(2)]

    @pl.when((phase == 0) & (i == 0))
    def _():
        run_ref[...] = jnp.zeros_like(run_ref)

    @pl.when(phase == 0)
    def _():
        run_ref[...] += jnp.sum(onehot[0] + onehot[1], axis=1, keepdims=True)

    @pl.when((phase == 1) & (i == 0))
    def _():
        padded = jnp.floor((run_ref[...] + (EXPERT_ROW_TILE - 1)) * (1.0 / EXPERT_ROW_TILE)) * EXPERT_ROW_TILE
        ei = lax.broadcasted_iota(jnp.int32, (N_EXPERTS, N_EXPERTS), 0)
        ej = lax.broadcasted_iota(jnp.int32, (N_EXPERTS, N_EXPERTS), 1)
        below = jnp.where(ej < ei, 1.0, 0.0)
        starts = jnp.dot(below, jnp.broadcast_to(padded, (N_EXPERTS, LANES)), precision=lax.Precision.HIGHEST,
                         preferred_element_type=F32)
        ends_ref[...] = (starts + padded).astype(jnp.int32)
        run_ref[...] = starts[:, 0:1]

    @pl.when(phase == 1)
    def _():
        ti = lax.broadcasted_iota(jnp.int32, (tp, tp), 0)
        tj = lax.broadcasted_iota(jnp.int32, (tp, tp), 1)
        before = jnp.where(ti < tj, 1.0, 0.0).astype(BF16)
        run = run_ref[...]
        for k in range(2):
            prefix = jnp.dot(onehot[k].astype(BF16), before, preferred_element_type=F32)
            pos_ref[k:k + 1, :] = jnp.sum(onehot[k] * (prefix + run), axis=0, keepdims=True).astype(jnp.int32)
            run = run + jnp.sum(onehot[k], axis=1, keepdims=True)
        run_ref[...] = run


def _positions(eidx):
    n = eidx.shape[1]
    nt = n // POSITION_TILE
    return pl.pallas_call(
        _positions_kernel,
        grid=(2, nt),
        in_specs=[pl.BlockSpec((2, POSITION_TILE), lambda p, i: (0, i))],
        out_specs=[pl.BlockSpec((2, POSITION_TILE), lambda p, i: (0, i * p)),
                   pl.BlockSpec((N_EXPERTS, LANES), lambda p, i: (0, 0))],
        out_shape=[jax.ShapeDtypeStruct((2, n), jnp.int32), jax.ShapeDtypeStruct((N_EXPERTS, LANES), jnp.int32)],
        scratch_shapes=[pltpu.VMEM((N_EXPERTS, 1), F32)],
        compiler_params=pltpu.CompilerParams(
            dimension_semantics=("arbitrary", "arbitrary"), vmem_limit_bytes=VMEM_LIMIT_BYTES),
        name="positions",
    )(eidx)


def _sc_mesh():
    return plsc.VectorSubcoreMesh(core_axis_name="c", subcore_axis_name="s")


def _sc_worker():
    return lax.axis_index("c") * SC_SUBCORES + lax.axis_index("s")


def _sc_sort_rows(table, pos, n_rows):
    n_slots, width = pos.shape[0], table.shape[1]
    n_tokens = n_slots // 2
    assert n_tokens & (n_tokens - 1) == 0
    per_worker = n_rows // SC_WORKERS
    assert per_worker * SC_WORKERS == n_rows and per_worker % SC_GATHER_ROWS == 0 and n_slots % SC_LANES == 0

    def body(table_hbm, pos_hbm, out_hbm, pos_v, inv_v, idx_v, rows_v, sem):
        base = _sc_worker() * per_worker
        pltpu.sync_copy(pos_hbm, pos_v)

        @pl.loop(0, per_worker, step=SC_LANES)
        def _(j):
            inv_v[pl.ds(j, SC_LANES)] = (base + j + lax.iota(jnp.int32, SC_LANES)) & (n_tokens - 1)

        @pl.loop(0, n_slots, step=SC_LANES)
        def _(s):
            local = pos_v[pl.ds(s, SC_LANES)] - base
            mine = (local >= 0) & (local < per_worker)
            token = (s + lax.iota(jnp.int32, SC_LANES)) & (n_tokens - 1)
            plsc.store_scatter(inv_v, [jnp.where(mine, local, 0)], token, mask=mine)

        @pl.loop(0, per_worker, step=SC_GATHER_ROWS)
        def _(j):
            @pl.loop(0, SC_GATHER_ROWS, step=SC_LANES)
            def _(q):
                idx_v[pl.ds(q, SC_LANES)] = inv_v[pl.ds(j + q, SC_LANES)]

            pltpu.async_copy(table_hbm.at[idx_v], rows_v, sem).wait()
            pltpu.sync_copy(rows_v, out_hbm.at[pl.ds(base + j, SC_GATHER_ROWS)])

    return pl.kernel(
        body, out_type=jax.ShapeDtypeStruct((n_rows, width), table.dtype), mesh=_sc_mesh(),
        scratch_types=[pltpu.VMEM((n_slots,), jnp.int32), pltpu.VMEM((per_worker,), jnp.int32),
                       pltpu.VMEM((SC_GATHER_ROWS,), jnp.int32), pltpu.VMEM((SC_GATHER_ROWS, width), table.dtype),
                       pltpu.SemaphoreType.DMA],
        compiler_params=pltpu.CompilerParams(needs_layout_passes=False),
        name="sc_sort_rows",
    )(table, pos)


def _sc_gather(table, idx):
    n_rows, width = idx.shape[0], table.shape[1]
    per_worker = n_rows // SC_WORKERS
    assert per_worker * SC_WORKERS == n_rows and per_worker % SC_GATHER_ROWS == 0

    def body(table_hbm, idx_hbm, out_hbm, idx_v, rows_v, sem):
        base = _sc_worker() * per_worker

        @pl.loop(0, per_worker, step=SC_GATHER_ROWS)
        def _(j):
            pltpu.sync_copy(idx_hbm.at[pl.ds(base + j, SC_GATHER_ROWS)], idx_v)
            pltpu.async_copy(table_hbm.at[idx_v], rows_v, sem).wait()
            pltpu.sync_copy(rows_v, out_hbm.at[pl.ds(base + j, SC_GATHER_ROWS)])

    return pl.kernel(
        body, out_type=jax.ShapeDtypeStruct((n_rows, width), table.dtype), mesh=_sc_mesh(),
        scratch_types=[pltpu.VMEM((SC_GATHER_ROWS,), jnp.int32), pltpu.VMEM((SC_GATHER_ROWS, width), table.dtype),
                       pltpu.SemaphoreType.DMA],
        name="sc_gather",
    )(table, idx)


def _experts_kernel(layer, te_ref, seg_ref, nxt_ref, xs_ref, wg_hbm, wu_hbm, wd_hbm, ys_ref,
                    wg_f, wu_f, wd_f, wg_bf, wu_bf, wd_bf, sems):
    i = pl.program_id(0)
    expert = te_ref[i]
    held = jnp.minimum(expert, N_EXPERTS - 1)
    slot = seg_ref[i] & 1
    new_segment = (i == 0) | (seg_ref[i] != seg_ref[jnp.maximum(i - 1, 0)])

    def weight_copies(e, s):
        return [pltpu.make_async_copy(w_hbm.at[layer, e], w_f.at[s], sems.at[k, s])
                for k, (w_hbm, w_f) in enumerate(((wg_hbm, wg_f), (wu_hbm, wu_f), (wd_hbm, wd_f)))]

    @pl.when(i == 0)
    def _():
        for cp in weight_copies(held, slot):
            cp.start()

    @pl.when(new_segment)
    def _():
        for cp in weight_copies(held, slot):
            cp.wait()
        wg_bf[...] = wg_f[slot].astype(BF16)
        wu_bf[...] = wu_f[slot].astype(BF16)
        wd_bf[...] = wd_f[slot].astype(BF16)

        @pl.when(nxt_ref[i] >= 0)
        def _():
            for cp in weight_copies(nxt_ref[i], 1 - slot):
                cp.start()

    @pl.when(expert < N_EXPERTS)
    def _():
        sub = xs_ref.shape[0] // MIX_SUB_TILES
        tiles = [slice(s * sub, (s + 1) * sub) for s in range(MIX_SUB_TILES)]
        xs = [_unpack_bf16_pairs(xs_ref[rs, :]).astype(BF16) for rs in tiles]
        hes = []
        for x in xs:
            gate, up = _dot(x, wg_bf[...]), _dot(x, wu_bf[...])
            hes.append((gate * _sigmoid(gate) * up).astype(BF16))
        for rs, he in zip(tiles, hes):
            ys_ref[rs, :] = _pack_bf16_pairs(_dot(he, wd_bf[...]))


def _experts(tile_expert, xs, layer, w_gate, w_up, w_down):
    n_rows, half = xs.shape
    _, n_e, d, de = w_gate.shape
    held = jnp.minimum(tile_expert, n_e - 1)
    segment = jnp.cumsum(jnp.concatenate([jnp.zeros((1,), jnp.int32), (held[1:] != held[:-1]).astype(jnp.int32)]))
    after = jnp.sum(held[None, :] <= held[:, None], axis=1)
    next_expert = jnp.where(after < held.shape[0], held[jnp.minimum(after, held.shape[0] - 1)], -1).astype(jnp.int32)

    rows = pl.BlockSpec((EXPERT_ROW_TILE, half), lambda i, te, seg, nxt: (i, 0))
    hbm = pl.BlockSpec(memory_space=pl.ANY)
    return pl.pallas_call(
        functools.partial(_experts_kernel, layer),
        grid_spec=pltpu.PrefetchScalarGridSpec(
            num_scalar_prefetch=3,
            grid=(n_rows // EXPERT_ROW_TILE,),
            in_specs=[rows, hbm, hbm, hbm],
            out_specs=rows,
            scratch_shapes=[pltpu.VMEM((2, d, de), F32), pltpu.VMEM((2, d, de), F32), pltpu.VMEM((2, de, d), F32),
                            pltpu.VMEM((d, de), BF16), pltpu.VMEM((d, de), BF16), pltpu.VMEM((de, d), BF16),
                            pltpu.SemaphoreType.DMA((3, 2))]),
        out_shape=jax.ShapeDtypeStruct((n_rows, half), jnp.int32),
        compiler_params=pltpu.CompilerParams(
            dimension_semantics=("arbitrary",), vmem_limit_bytes=VMEM_LIMIT_BYTES),
        name="experts",
    )(tile_expert, segment.astype(jnp.int32), next_expert, xs, w_gate, w_up, w_down)


def _final_kernel(y0_ref, y1_ref, gate_ref, h1_ref, gt_ref, fg_ref, *out_refs):
    o_ref = out_refs[-1]
    h2 = _moe_residual(y0_ref, y1_ref, gate_ref, h1_ref[0], gt_ref[0])
    ms = jnp.mean(h2 * h2, axis=-1, keepdims=True)
    o_ref[0] = h2 * lax.rsqrt(ms + RMS_EPS) * fg_ref[...]


def _moe_out_specs(b, t, d, tile=TOKEN_TILE):
    nt = t // tile
    return [pl.BlockSpec((tile, d // 2), lambda i, j: (i * nt + j, 0)),
            pl.BlockSpec((tile, d // 2), lambda i, j: (b * nt + i * nt + j, 0)),
            pl.BlockSpec((tile, 2), lambda i, j: (i * nt + j, 0))]


def _final(yg, gates, h1, mod, final_g, batch0, n_batch, out_prev):
    b, t, d = h1.shape
    tile = pl.BlockSpec((1, WIDE_TOKEN_TILE, d), lambda i, j: (i, j, 0))
    in_specs = _moe_out_specs(b, t, d, WIDE_TOKEN_TILE) + [tile, _mod_spec3(5, d, batch0),
                                                           pl.BlockSpec((1, d), lambda i, j: (0, 0))]
    args = [yg, yg, gates, h1, mod, final_g]
    aliases = {}
    if out_prev is not None:
        in_specs.append(pl.BlockSpec(memory_space=pl.ANY))
        aliases = {len(args): 0}
        args.append(out_prev)
    return pl.pallas_call(
        _final_kernel,
        grid=(b, t // WIDE_TOKEN_TILE),
        in_specs=in_specs,
        out_specs=pl.BlockSpec((1, WIDE_TOKEN_TILE, d), lambda i, j: (i + batch0, j, 0)),
        out_shape=jax.ShapeDtypeStruct((n_batch, t, d), F32),
        input_output_aliases=aliases,
        compiler_params=pltpu.CompilerParams(
            dimension_semantics=("arbitrary", "arbitrary"), vmem_limit_bytes=VMEM_LIMIT_BYTES),
        name="final",
    )(*args)


def _moe_sort(hm, eidx):
    n_slots = 2 * hm.shape[0]
    n_rows = n_slots + N_EXPERTS * EXPERT_ROW_TILE
    pos, ends = _positions(eidx)
    pos = pos.reshape(n_slots)
    tile_start = jnp.arange(n_rows // EXPERT_ROW_TILE, dtype=jnp.int32) * EXPERT_ROW_TILE
    tile_expert = jnp.sum(tile_start[:, None] >= ends[None, :, 0], axis=1).astype(jnp.int32)
    return _sc_sort_rows(hm, pos, n_rows), pos, tile_expert


def _moe_apply(sorted_rows, gates, layer, w_gate, w_up, w_down):
    xs, pos, tile_expert = sorted_rows
    ys = _experts(tile_expert, xs, layer, w_gate, w_up, w_down)
    return _sc_gather(ys, pos), gates.T


def kernel(x, c, ctx, c_ctx, ada_w, ada_b, norm_g, rw_mu, rw_w_rkv, rw_w0, rw_w1, rw_w2, rw_a0, rw_a1, rw_a2, rw_g1, rw_g2, rw_k_k, rw_k_a, rw_r_k, rw_gn_w, rw_gn_b, rw_w_o, sc_w_in, sc_conv, sc_w_out, router_w, router_b, moe_w_gate, moe_w_up, moe_w_down, final_g):
    b, t, d = x.shape
    ctx_len = ctx.shape[1]
    depth = ada_w.shape[0]
    assert d == D_MODEL and depth == 2 and t % POSITION_TILE == 0

    mod_rows = 16
    cc = jnp.concatenate([c, c_ctx[None, :], jnp.zeros((mod_rows - b - 1, d), F32)], axis=0)
    mod = _ada(cc, ada_w, ada_b).reshape(depth, mod_rows, 1, 6 * d)

    row = lambda a: a.reshape(1, d)
    router_wt = jnp.pad(router_w, ((0, 0), (0, LANES - N_EXPERTS)))
    router_bc = router_b.reshape(N_EXPERTS, 1)

    w1 = jnp.concatenate([rw_w1[0, 0], rw_w1[0, 1]], axis=1).astype(BF16)
    a1 = jnp.concatenate([rw_a1[0, 0], rw_a1[0, 1]], axis=1).astype(BF16)

    def pad_dirs(w):
        z = jnp.zeros_like(w[0])
        return jnp.stack([jnp.concatenate([w[0], z], axis=0), jnp.concatenate([z, w[1]], axis=0)]).astype(BF16)

    assert b % N_STREAMS == 0
    nb = b // N_STREAMS
    pre_params = (row(norm_g[0, 0]), rw_mu[0], rw_w_rkv[0].astype(BF16), w1, pad_dirs(rw_w2[0]), rw_w0[0], a1,
                  pad_dirs(rw_a2[0]), rw_a0[0], rw_g1[0].astype(BF16), rw_g2[0].astype(BF16), row(rw_k_k[0]),
                  row(rw_k_a[0]), row(rw_r_k[0]))
    w_o, w_in, w_out = rw_w_o[0].astype(BF16), sc_w_in[0].astype(BF16), sc_w_out[0].astype(BF16)
    out = None
    moe_w = (moe_w_gate, moe_w_up, moe_w_down)
    streams = []
    for batch0 in range(0, b, nb):
        r, v, g, bonus, k, kk, cs0, cs1, a0, a1 = _rwkv_pre(ctx, x, mod[0], *pre_params, batch0, nb)
        yf, yr = _wkv(row(rw_k_a[0]), r, v, k, kk, cs0, a0, cs1, a1, ctx_len)
        h1, hm, eidx, gates = _readout(yf, yr, bonus, g, x, mod[0], row(rw_gn_w[0]), row(rw_gn_b[0]),
                                       w_o, row(norm_g[0, 1]), router_wt, router_bc, batch0, nb)
        streams.append((batch0, h1, gates, _moe_sort(hm, eidx)))
    if len(streams) > 1:
        batch0, h1, gates, (xs, pos, tile_expert) = streams[0]
        tile_expert, _ = lax.optimization_barrier((tile_expert, yf))
        streams[0] = (batch0, h1, gates, (xs, pos, tile_expert))
    for batch0, h1, gates, sorted_rows in streams:
        yg, gates = _moe_apply(sorted_rows, gates, 0, *moe_w)
        h1, hm, eidx, gates = _sconv(yg, gates, h1, mod[0], mod[1], row(norm_g[1, 0]), w_in, sc_conv[0], w_out,
                                     row(norm_g[1, 1]), router_wt, router_bc, batch0)
        yg, gates = _moe_apply(_moe_sort(hm, eidx), gates, 1, *moe_w)
        out = _final(yg, gates, h1, mod[1], row(final_g), batch0, b, out)
    return out
```

```python
import functools

import jax
import jax.numpy as jnp
from jax import lax
from jax.experimental import pallas as pl
from jax.experimental.pallas import tpu as pltpu
from jax.experimental.pallas import tpu_sc as plsc

F32 = jnp.float32
BF16 = jnp.bfloat16

D_MODEL = 1024
HEAD_DIM = 64
LANES = 128
N_PAIRS = D_MODEL // LANES
CHUNK = 64
N_EXPERTS = 16
EXPERTS_PER_GROUP = 4
N_GROUPS = N_EXPERTS // EXPERTS_PER_GROUP
RMS_EPS = 1e-6
GN_EPS = 64e-5
L2_EPS = 1e-12

TOKEN_TILE = 256
WIDE_TOKEN_TILE = 512
PRE_SUB_TILES = 2
MIX_SUB_TILES = 2
EXPERT_ROW_TILE = 512
POSITION_TILE = 1024
N_STREAMS = 2
SC_CORES, SC_SUBCORES, SC_LANES = 2, 16, 16
SC_WORKERS = SC_CORES * SC_SUBCORES
SC_GATHER_ROWS = 128
PAIRS_PER_STEP = 8
CHUNKS_PER_STEP = 4
STAGE_LAG = 2
ADA_COL_TILE = 1536
VMEM_LIMIT_BYTES = 56 * 1024 * 1024


def _sigmoid(x):
    return 1.0 / (1.0 + jnp.exp(-x))


def _mm(a, b):
    return jnp.dot(a.astype(BF16), b.astype(BF16), preferred_element_type=F32)


def _split2(x):
    hi = x.astype(BF16)
    return hi, (x - hi.astype(F32)).astype(BF16)


def _head_sum(x):
    rows = x.shape[0]
    left = lax.broadcasted_iota(jnp.int32, (rows, LANES), 1) < HEAD_DIM
    outs = []
    for j in range(N_PAIRS):
        xb = x[:, LANES * j:LANES * (j + 1)]
        sa = jnp.sum(jnp.where(left, xb, 0.0), axis=-1, keepdims=True)
        sb = jnp.sum(jnp.where(left, 0.0, xb), axis=-1, keepdims=True)
        outs.append(jnp.where(left, sa, sb))
    return jnp.concatenate(outs, axis=-1)


def _rms_modulate(x, g, shift, scale):
    ms = jnp.mean(x * x, axis=-1, keepdims=True)
    xn = x * lax.rsqrt(ms + RMS_EPS) * g
    return xn * (1.0 + scale) + shift


def _row_neighbours(x, row_len):
    rows = x.shape[0]
    pos = lax.broadcasted_iota(jnp.int32, x.shape, 0) & (row_len - 1)
    prev = jnp.where(pos == 0, 0.0, pltpu.roll(x, 1, 0))
    nxt = jnp.where(pos == row_len - 1, 0.0, pltpu.roll(x, rows - 1, 0))
    return prev, nxt


def _ada_kernel(c_ref, w_ref, b_ref, o_ref):
    c = c_ref[...]
    s = c * _sigmoid(c)
    o_ref[0] = _mm(s, w_ref[0]) + b_ref[0]


def _ada(cc, ada_w, ada_b):
    depth, d, n = ada_w.shape
    rows = cc.shape[0]
    return pl.pallas_call(
        _ada_kernel,
        grid=(depth, n // ADA_COL_TILE),
        in_specs=[
            pl.BlockSpec((rows, d), lambda l, j: (0, 0)),
            pl.BlockSpec((1, d, ADA_COL_TILE), lambda l, j: (l, 0, j)),
            pl.BlockSpec((1, 1, ADA_COL_TILE), lambda l, j: (l, 0, j)),
        ],
        out_specs=pl.BlockSpec((1, rows, ADA_COL_TILE), lambda l, j: (l, 0, j)),
        out_shape=jax.ShapeDtypeStruct((depth, rows, n), F32),
        compiler_params=pltpu.CompilerParams(
            dimension_semantics=("arbitrary", "arbitrary"), vmem_limit_bytes=VMEM_LIMIT_BYTES),
        name="ada",
    )(cc, ada_w, ada_b.reshape(depth, 1, n))


def _rwkv_pre_kernel(ctx_len, ctx_ref, x_ref, sh_ref, sc_ref, ng_ref, mu_ref, wrkv_ref, w1_ref, w2_ref, w0_ref,
                     a1_ref, a2_ref, a0_ref, g1_ref, g2_ref, kk_ref, ka_ref, rk_ref,
                     r_out, v_out, g_out, bon_out, k_out, kk_out, cs0_out, cs1_out, a0_out, a1_out):
    j = pl.program_id(1)
    row_len = jnp.where(j == 0, ctx_len, CHUNK)
    xin = jnp.where(j == 0, ctx_ref[0], x_ref[0])
    xn = _rms_modulate(xin, ng_ref[...], sh_ref[0], sc_ref[0])
    prev, nxt = _row_neighbours(xn, row_len)
    xx = 0.5 * (prev + nxt) - xn

    sub = xn.shape[0] // PRE_SUB_TILES
    projected = []
    for s in range(PRE_SUB_TILES):
        rs = slice(s * sub, (s + 1) * sub)
        xn_s, xx_s = xn[rs], xx[rs]

        def mix(i, xn_s=xn_s, xx_s=xx_s):
            return xn_s + xx_s * mu_ref[i:i + 1, :]

        r = _mm(mix(0), wrkv_ref[0])
        k = _mm(mix(1), wrkv_ref[1])
        v = _mm(mix(2), wrkv_ref[2])
        g = _mm(_sigmoid(_mm(mix(5), g1_ref[...])), g2_ref[...])
        wl = jnp.tanh(_mm(mix(3), w1_ref[...]))
        al = _mm(mix(4), a1_ref[...])
        projected.append((rs, r, k, v, g, [_mm(wl, w2_ref[p]) for p in range(2)],
                          [_mm(al, a2_ref[p]) for p in range(2)]))

    ti = lax.broadcasted_iota(jnp.int32, (sub, sub), 0)
    tj = lax.broadcasted_iota(jnp.int32, (sub, sub), 1)
    same_chunk = (ti ^ tj) < CHUNK
    dir_outs = ((cs0_out, a0_out), (cs1_out, a1_out))
    for rs, r, k, v, g, zs, a_logits in projected:
        r_out[0, rs, :] = r.astype(r_out.dtype)
        v_out[0, rs, :] = v.astype(v_out.dtype)
        g_out[0, rs, :] = g.astype(g_out.dtype)
        k_out[0, rs, :] = k.astype(k_out.dtype)
        kk = k * kk_ref[...]
        kk_out[0, rs, :] = (kk * lax.rsqrt(jnp.maximum(_head_sum(kk * kk), L2_EPS * L2_EPS))).astype(kk_out.dtype)
        a_sum = None
        for p, (cs_out, a_out) in enumerate(dir_outs):
            z = w0_ref[p:p + 1, :] + zs[p]
            lw = -jnp.exp(-0.5) * _sigmoid(z)
            tri = jnp.where(same_chunk & ((tj >= ti) if p else (tj <= ti)), 1.0, 0.0).astype(BF16)
            hi, lo = _split2(lw)
            cs_out[0, rs, :] = _dot(tri, hi) + _dot(tri, lo)
            a = _sigmoid(a0_ref[p:p + 1, :] + a_logits[p])
            a_out[0, rs, :] = a.astype(a_out.dtype)
            a_sum = a if a_sum is None else a_sum + a
        k_dirs = k * (2.0 + (a_sum - 2.0) * ka_ref[...])
        bon_out[0, rs, :] = (_head_sum(r * rk_ref[...] * k_dirs) * v).astype(bon_out.dtype)


def _rwkv_pre(ctx, x, mod, norm_g, mu, w_rkv, w1, w2, w0, a1, a2, a0, g1, g2, k_k, k_a, r_k, batch0, b):
    n_batch, t, d = x.shape
    ctx_len = ctx.shape[1]
    tt = ctx_len + t
    assert ctx_len == TOKEN_TILE and t % TOKEN_TILE == 0

    def mod_spec(part):
        return pl.BlockSpec((1, 1, d), lambda i, j: (jnp.where(j == 0, n_batch, i + batch0), 0, part))

    def full(a):
        nd = a.ndim
        return pl.BlockSpec(a.shape, lambda i, j: (0,) * nd)

    tile = pl.BlockSpec((1, TOKEN_TILE, d), lambda i, j: (i, j, 0))
    latent = pl.BlockSpec((1, TOKEN_TILE, d), lambda i, j: (i, jnp.maximum(j - 1, 0), 0))
    params = (norm_g, mu, w_rkv, w1, w2, w0, a1, a2, a0, g1, g2, k_k, k_a, r_k)
    out_dtypes = (BF16,) * 6 + (F32, F32) + (BF16,) * 2
    out_specs = [tile, tile, latent, latent] + [tile] * 6
    return pl.pallas_call(
        functools.partial(_rwkv_pre_kernel, ctx_len),
        grid=(b, tt // TOKEN_TILE),
        in_specs=[pl.BlockSpec((1, TOKEN_TILE, d), lambda i, j: (i + batch0, 0, 0)),
                  pl.BlockSpec((1, TOKEN_TILE, d), lambda i, j: (i + batch0, jnp.maximum(j - 1, 0), 0)),
                  mod_spec(0), mod_spec(1)] + [full(a) for a in params],
        out_specs=out_specs,
        out_shape=[jax.ShapeDtypeStruct((b, t if spec is latent else tt, d), dt)
                   for spec, dt in zip(out_specs, out_dtypes)],
        compiler_params=pltpu.CompilerParams(
            dimension_semantics=("arbitrary", "arbitrary"), vmem_limit_bytes=VMEM_LIMIT_BYTES),
        name="rwkv_pre",
    )(ctx, x, mod, mod, *params)


def _expand(x, left):
    return jnp.concatenate([jnp.where(left, x, 0.0), jnp.where(left, 0.0, x)], axis=0)


def _dot(a, b):
    return jnp.dot(a, b, preferred_element_type=F32)


def _chunk_steps(chains, group_size):
    assert CHUNK == HEAD_DIM
    c = CHUNK
    lane = lax.broadcasted_iota(jnp.int32, (c, LANES), 1)
    left = lane < HEAD_DIM
    tt = lax.broadcasted_iota(jnp.int32, (c, LANES), 0)
    jj = lane & (c - 1)
    diag = jj == tt
    tri = {False: (jj < tt, jj <= tt), True: (jj > tt, jj >= tt)}

    def bd(x):
        return _expand(x, left).astype(BF16)

    def fold_t(x):
        xt = _expand(x, left).T
        return xt[:c] + xt[c:]

    def rows(*xs):
        return jnp.concatenate(xs, axis=0).astype(BF16)

    nt_dims = (((1,), (1,)), ((), ()))

    def s_prep(q):
        cs, r, v, k, kk, a, ka, h, reverse = q.pop("chain")
        tot = cs[0:1, :] if reverse else cs[c - 1:c, :]
        e_pos, e_neg, e_rem = jnp.exp(cs), jnp.exp(-cs), jnp.exp(tot - cs)
        first = tt == (c - 1 if reverse else 0)
        cs_prev = jnp.where(first, 0.0, pltpu.roll(cs, c - 1 if reverse else 1, 0))
        at, rt = -kk * jnp.exp(cs_prev), r * e_pos
        kd, be = k * (1.0 + (a - 1.0) * ka), kk * a
        q.update(a_e=bd(at), rt=rt, v_e=bd(v), g_tot=jnp.exp(tot), h=h, tri=tri[reverse])
        q["hat_t"] = jnp.concatenate([fold_t(be * e_rem), fold_t(kd * e_rem)], axis=1)
        q["sc"] = lax.dot_general(rows(at, rt), jnp.concatenate([bd(be * e_neg), bd(kd * e_neg)], axis=0),
                                  nt_dims, preferred_element_type=F32)

    def s_mask(q):
        (strict, incl), sc = q["tri"], q["sc"]
        q["n"] = jnp.where(strict, sc[:c, :LANES], 0.0)
        a_ak = jnp.where(strict, sc[:c, LANES:], 0.0)
        q["a_rb"] = jnp.where(incl, sc[c:, :LANES], 0.0)
        a_rk = jnp.where(incl, sc[c:, LANES:], 0.0)
        q["bh_t"] = q["hat_t"][:, :LANES]
        q["vv"] = _dot(rows(a_ak, q["hat_t"][:, LANES:], a_rk), q["v_e"])

    def s_square(q):
        q["t"] = jnp.where(diag, 1.0, 0.0) + q["n"]
        q["p"] = _dot(q["n"].astype(BF16), bd(q["n"]))

    def s_level(q):
        x = _dot(rows(q["p"], q["t"]), bd(q["p"]))
        q["p"] = x[:c]
        q["t"] = q["t"] + x[c:]

    def s_last_level(q):
        q["t"] = q["t"] + _dot(q["t"].astype(BF16), bd(q["p"]))

    def s_solve(q):
        xu = _dot(q["t"].astype(BF16), jnp.concatenate([q["a_e"], bd(q["vv"][:c])], axis=1))
        q["au_e"] = jnp.concatenate([bd(xu[:, :LANES]), bd(xu[:, LANES:])], axis=1)

    def s_affine(q):
        z = _dot(rows(q["bh_t"], q["a_rb"]), q["au_e"])
        q["m"] = z[:c, :LANES] + jnp.where(diag, q["g_tot"], 0.0)
        q["g"] = z[:c, LANES:] + q["vv"][c:2 * c]
        q["r_hat"] = q["rt"] + z[c:, :LANES]
        q["yi"] = z[c:, LANES:] + q["vv"][2 * c:]

    n_levels = (c // 4).bit_length() - 1
    stages = [s_prep, s_mask, s_square] + [s_level] * n_levels + [s_last_level, s_solve, s_affine]
    qs = [dict(chain=ch) for ch in chains]
    groups = [qs[i:i + group_size] for i in range(0, len(qs), group_size)]
    for tau in range(len(stages) + (len(groups) - 1) * STAGE_LAG):
        for gi, group in enumerate(groups):
            si = tau - gi * STAGE_LAG
            if 0 <= si < len(stages):
                for q in group:
                    stages[si](q)
    outs = []
    for q in qs:
        h = outs[q["h"]][1] if isinstance(q["h"], int) else q["h"]
        o = _dot(rows(q["r_hat"], q["m"]), bd(h))
        outs.append((o[:c] + q["yi"], o[c:] + q["g"]))
    return outs


def _wkv_kernel(ka_ref, rf_ref, vf_ref, kf_ref, kkf_ref, csf_ref, af_ref,
                rr_ref, vr_ref, kr_ref, kkr_ref, csr_ref, ar_ref,
                yf_ref, yr_ref, hf_ref, hr_ref):
    @pl.when(pl.program_id(2) == 0)
    def _():
        hf_ref[...] = jnp.zeros_like(hf_ref)
        hr_ref[...] = jnp.zeros_like(hr_ref)

    dirs = ((rf_ref, vf_ref, kf_ref, kkf_ref, csf_ref, af_ref, yf_ref, hf_ref),
            (rr_ref, vr_ref, kr_ref, kkr_ref, csr_ref, ar_ref, yr_ref, hr_ref))
    chains, dests = [], []
    per_chunk = 2 * PAIRS_PER_STEP
    for u in range(CHUNKS_PER_STEP):
        for p in range(PAIRS_PER_STEP):
            sl = slice(p * LANES, (p + 1) * LANES)
            for reverse, (r_ref, v_ref, k_ref, kk_ref, cs_ref, a_ref, y_ref, h_ref) in enumerate(dirs):
                at = CHUNKS_PER_STEP - 1 - u if reverse else u
                tm = slice(at * CHUNK, (at + 1) * CHUNK)
                h = h_ref[p] if u == 0 else len(chains) - per_chunk
                chains.append((cs_ref[0, tm, sl], r_ref[0, tm, sl].astype(F32), v_ref[0, tm, sl].astype(F32),
                               k_ref[0, tm, sl].astype(F32), kk_ref[0, tm, sl].astype(F32),
                               a_ref[0, tm, sl].astype(F32), ka_ref[:, sl], h, bool(reverse)))
                dests.append((y_ref, tm, sl, h_ref if u == CHUNKS_PER_STEP - 1 else None, p))
    for (y, h_new), (y_ref, tm, sl, h_ref, p) in zip(_chunk_steps(chains, per_chunk), dests):
        y_ref[0, tm, sl] = y
        if h_ref is not None:
            h_ref[p] = h_new


def _wkv(k_a, r, v, k, kk, cs0, a0, cs1, a1, ctx_len):
    b, tt, d = r.shape
    step_rows = CHUNKS_PER_STEP * CHUNK
    n_steps, n_ctx = tt // step_rows, ctx_len // step_rows
    assert n_steps * step_rows == tt and n_ctx * step_rows == ctx_len
    width = PAIRS_PER_STEP * LANES

    def fwd_map(i, j, s):
        return (i, s, j)

    def rev_map(i, j, s):
        return (i, jnp.where(s < n_ctx, n_ctx - 1 - s, n_steps - 1 - (s - n_ctx)), j)

    n_lat = n_steps - n_ctx

    def fwd_out(i, j, s):
        return (i, jnp.maximum(s - n_ctx, 0), j)

    def rev_out(i, j, s):
        return (i, jnp.where(s < n_ctx, n_lat - 1, n_lat - 1 - (s - n_ctx)), j)

    fwd = pl.BlockSpec((1, step_rows, width), fwd_map)
    rev = pl.BlockSpec((1, step_rows, width), rev_map)
    return pl.pallas_call(
        _wkv_kernel,
        grid=(b, d // width, n_steps),
        in_specs=[pl.BlockSpec((1, width), lambda i, j, s: (0, j))] + [fwd] * 6 + [rev] * 6,
        out_specs=[pl.BlockSpec((1, step_rows, width), fwd_out), pl.BlockSpec((1, step_rows, width), rev_out)],
        out_shape=[jax.ShapeDtypeStruct((b, tt - ctx_len, d), F32)] * 2,
        scratch_shapes=[pltpu.VMEM((PAIRS_PER_STEP, HEAD_DIM, LANES), F32)] * 2,
        compiler_params=pltpu.CompilerParams(
            dimension_semantics=("arbitrary", "arbitrary", "arbitrary"), vmem_limit_bytes=VMEM_LIMIT_BYTES),
        name="wkv",
    )(k_a, r, v, k, kk, cs0, a0, r, v, k, kk, cs1, a1)


def _pack_bf16_pairs(x):
    half = x.shape[1] // 2
    lo = pltpu.bitcast(x[:, :half].astype(BF16).astype(F32), jnp.int32)
    hi = pltpu.bitcast(x[:, half:].astype(BF16).astype(F32), jnp.int32)
    return lax.shift_right_logical(lo, jnp.int32(16)) | (hi & jnp.int32(-65536))


def _unpack_bf16_pairs(w):
    lo = pltpu.bitcast(lax.shift_left(w, jnp.int32(16)), F32)
    hi = pltpu.bitcast(w & jnp.int32(-65536), F32)
    return jnp.concatenate([lo, hi], axis=1)


def _route(h1, ng, shift, scale, rwt_ref, rb_ref, hm_out, eidx_out, gate_out, rs=slice(None)):
    hm = _rms_modulate(h1, ng, shift, scale)
    hm_out[rs, :] = _pack_bf16_pairs(hm)
    w_hi, w_mid = _split2(rwt_ref[...])
    h_hi, h_mid = _split2(hm)
    logits_t = _dot(h_hi, w_hi) + _dot(h_mid, w_hi) + _dot(h_hi, w_mid)
    logits = logits_t.T[:N_EXPERTS]
    s = _sigmoid(logits)
    sel = s + rb_ref[...]
    assert EXPERTS_PER_GROUP == 4 and N_GROUPS == 4
    cands = []
    for g in range(N_GROUPS):
        m = [sel[e:e + 1, :] for e in range(g * EXPERTS_PER_GROUP, (g + 1) * EXPERTS_PER_GROUP)]
        sg = [s[e:e + 1, :] for e in range(g * EXPERTS_PER_GROUP, (g + 1) * EXPERTS_PER_GROUP)]
        pairs = [m[i] + m[k] for i in range(4) for k in range(i + 1, 4)]
        score = jnp.maximum(jnp.maximum(jnp.maximum(pairs[0], pairs[1]), jnp.maximum(pairs[2], pairs[3])),
                            jnp.maximum(pairs[4], pairs[5]))
        chosen = []
        for i in range(4):
            ahead = [((m[k] >= m[i]) if k < i else (m[k] > m[i])).astype(jnp.int32) for k in range(4) if k != i]
            chosen.append(ahead[0] + ahead[1] + ahead[2] < 2)
        base = g * EXPERTS_PER_GROUP
        lo_idx = jnp.where(chosen[0], base, jnp.where(chosen[1], base + 1, base + 2))
        lo_gate = jnp.where(chosen[0], sg[0], jnp.where(chosen[1], sg[1], sg[2]))
        hi_idx = jnp.where(chosen[3], base + 3, jnp.where(chosen[2], base + 2, base + 1))
        hi_gate = jnp.where(chosen[3], sg[3], jnp.where(chosen[2], sg[2], sg[1]))
        cands.append((score, lo_idx, hi_idx, lo_gate, hi_gate))

    def better(x, y):
        win = y[0] > x[0]
        return tuple(jnp.where(win, yv, xv) for xv, yv in zip(x, y))

    _, e_lo, e_hi, g_lo, g_hi = better(better(cands[0], cands[1]), better(cands[2], cands[3]))
    eidx_out[:, rs] = jnp.concatenate([e_lo, e_hi], axis=0)
    gate_out[:, rs] = jnp.concatenate([g_lo, g_hi], axis=0) / (g_lo + g_hi)


def _readout_kernel(yf_ref, yr_ref, bon_ref, g_ref, h_ref, gt_ref, gnw_ref, gnb_ref, wo_ref,
                    ng_ref, sh_ref, sc_ref, rwt_ref, rb_ref, h1_out, hm_out, eidx_out, gate_out):
    sub = h_ref.shape[1] // MIX_SUB_TILES
    tiles = [slice(s * sub, (s + 1) * sub) for s in range(MIX_SUB_TILES)]
    gated, h1s = [], []
    for rs in tiles:
        y = yf_ref[0, rs, :] + yr_ref[0, rs, :] + bon_ref[0, rs, :].astype(F32)
        mean = _head_sum(y) * (1.0 / HEAD_DIM)
        yc = y - mean
        var = _head_sum(yc * yc) * (1.0 / HEAD_DIM)
        yn = yc * lax.rsqrt(var + GN_EPS) * gnw_ref[...] + gnb_ref[...]
        gated.append((yn * g_ref[0, rs, :].astype(F32)).astype(BF16))
    for rs, lhs in zip(tiles, gated):
        h1 = h_ref[0, rs, :] + gt_ref[0] * _dot(lhs, wo_ref[...])
        h1_out[0, rs, :] = h1
        h1s.append(h1)
    for rs, h1 in zip(tiles, h1s):
        _route(h1, ng_ref[...], sh_ref[0], sc_ref[0], rwt_ref, rb_ref, hm_out, eidx_out, gate_out, rs)


def _mod_spec3(part, d, batch0=0):
    return pl.BlockSpec((1, 1, d), lambda i, j: (i + batch0, 0, part))


def _route_out_specs(b, t, d, tile=TOKEN_TILE):
    nt = t // tile
    specs = [pl.BlockSpec((1, tile, d), lambda i, j: (i, j, 0)),
             pl.BlockSpec((tile, d // 2), lambda i, j: (i * nt + j, 0)),
             pl.BlockSpec((2, tile), lambda i, j: (0, i * nt + j)),
             pl.BlockSpec((2, tile), lambda i, j: (0, i * nt + j))]
    shapes = [jax.ShapeDtypeStruct((b, t, d), F32), jax.ShapeDtypeStruct((b * t, d // 2), jnp.int32),
              jax.ShapeDtypeStruct((2, b * t), jnp.int32), jax.ShapeDtypeStruct((2, b * t), F32)]
    return specs, shapes


def _full2(a):
    nd = a.ndim
    return pl.BlockSpec(a.shape, lambda i, j: (0,) * nd)


def _readout(yf, yr, bonus, g, h, mod, gn_w, gn_b, w_o, norm_g, router_wt, router_b, batch0, b):
    _, t, d = h.shape
    nt = t // WIDE_TOKEN_TILE
    local = pl.BlockSpec((1, WIDE_TOKEN_TILE, d), lambda i, j: (i, j, 0))
    tile = pl.BlockSpec((1, WIDE_TOKEN_TILE, d), lambda i, j: (i + batch0, j, 0))
    params_a = (gn_w, gn_b, w_o, norm_g)
    params_b = (router_wt, router_b)
    out_specs, out_shape = _route_out_specs(b, t, d, WIDE_TOKEN_TILE)
    return pl.pallas_call(
        _readout_kernel,
        grid=(b, nt),
        in_specs=[local] * 4 + [tile, _mod_spec3(2, d, batch0)] + [_full2(a) for a in params_a]
        + [_mod_spec3(3, d, batch0), _mod_spec3(4, d, batch0)] + [_full2(a) for a in params_b],
        out_specs=out_specs,
        out_shape=out_shape,
        compiler_params=pltpu.CompilerParams(
            dimension_semantics=("arbitrary", "arbitrary"), vmem_limit_bytes=VMEM_LIMIT_BYTES),
        name="readout",
    )(yf, yr, bonus, g, h, mod, *params_a, mod, mod, *params_b)


def _moe_residual(y0_ref, y1_ref, gate_ref, h1, gt, rs=slice(None)):
    gates = gate_ref[rs, :]
    y = gates[:, 0:1] * _unpack_bf16_pairs(y0_ref[rs, :]) + gates[:, 1:2] * _unpack_bf16_pairs(y1_ref[rs, :])
    return h1 + gt * y


def _sconv_kernel(y0_ref, y1_ref, gate_ref, hp_ref, gtp_ref, sh1_ref, sc1_ref, gt_ref, ng1_ref, win_ref, cw_ref,
                  wout_ref, ng2_ref, sh2_ref, sc2_ref, rwt_ref, rb_ref, h1_out, hm_out, eidx_out, gate_out):
    d = hp_ref.shape[2]
    sub = hp_ref.shape[1] // MIX_SUB_TILES
    tiles = [slice(s * sub, (s + 1) * sub) for s in range(MIX_SUB_TILES)]
    hs, xns, gated, h1s = [], [], [], []
    for rs in tiles:
        h = _moe_residual(y0_ref, y1_ref, gate_ref, hp_ref[0, rs, :], gtp_ref[0], rs)
        hs.append(h)
        xns.append(_rms_modulate(h, ng1_ref[...], sh1_ref[0], sc1_ref[0]).astype(BF16))
    for xn in xns:
        bg = _dot(xn, win_ref[:, 0:d])
        u = _dot(xn, win_ref[:, d:2 * d]) * _dot(xn, win_ref[:, 2 * d:3 * d])
        prev, nxt = _row_neighbours(u, CHUNK)
        conv = cw_ref[0:1, :] * prev + cw_ref[1:2, :] * u + cw_ref[2:3, :] * nxt
        gated.append((bg * conv).astype(BF16))
    for rs, h, lhs in zip(tiles, hs, gated):
        h1 = h + gt_ref[0] * _dot(lhs, wout_ref[...])
        h1_out[0, rs, :] = h1
        h1s.append(h1)
    for rs, h1 in zip(tiles, h1s):
        _route(h1, ng2_ref[...], sh2_ref[0], sc2_ref[0], rwt_ref, rb_ref, hm_out, eidx_out, gate_out, rs)


def _sconv(yg, gates, h_prev, mod_prev, mod, norm_g1, w_in, conv_w, w_out, norm_g2, router_wt, router_b, batch0):
    b, t, d = h_prev.shape
    nt = t // WIDE_TOKEN_TILE
    tile = pl.BlockSpec((1, WIDE_TOKEN_TILE, d), lambda i, j: (i, j, 0))
    params_a = (norm_g1, w_in, conv_w, w_out, norm_g2)
    params_b = (router_wt, router_b)
    out_specs, out_shape = _route_out_specs(b, t, d, WIDE_TOKEN_TILE)
    return pl.pallas_call(
        _sconv_kernel,
        grid=(b, nt),
        in_specs=_moe_out_specs(b, t, d, WIDE_TOKEN_TILE) + [tile, _mod_spec3(5, d, batch0)]
        + [_mod_spec3(0, d, batch0), _mod_spec3(1, d, batch0), _mod_spec3(2, d, batch0)]
        + [_full2(a) for a in params_a]
        + [_mod_spec3(3, d, batch0), _mod_spec3(4, d, batch0)] + [_full2(a) for a in params_b],
        out_specs=out_specs,
        out_shape=out_shape,
        compiler_params=pltpu.CompilerParams(
            dimension_semantics=("arbitrary", "arbitrary"), vmem_limit_bytes=VMEM_LIMIT_BYTES),
        name="sconv",
    )(yg, yg, gates, h_prev, mod_prev, mod, mod, mod, *params_a, mod, mod, *params_b)


def _positions_kernel(eidx_ref, pos_ref, ends_ref, run_ref):
    phase, i = pl.program_id(0), pl.program_id(1)
    tp = eidx_ref.shape[1]
    expert = lax.broadcasted_iota(jnp.int32, (N_EXPERTS, tp), 0)
    onehot = [jnp.where(eidx_ref[k:k + 1, :] == expert, 1.0, 0.0) for k in range(2)]

    @pl.when((phase == 0) & (i == 0))
    def _():
        run_ref[...] = jnp.zeros_like(run_ref)

    @pl.when(phase == 0)
    def _():
        run_ref[...] += jnp.sum(onehot[0] + onehot[1], axis=1, keepdims=True)

    @pl.when((phase == 1) & (i == 0))
    def _():
        padded = jnp.floor((run_ref[...] + (EXPERT_ROW_TILE - 1)) * (1.0 / EXPERT_ROW_TILE)) * EXPERT_ROW_TILE
        ei = lax.broadcasted_iota(jnp.int32, (N_EXPERTS, N_EXPERTS), 0)
        ej = lax.broadcasted_iota(jnp.int32, (N_EXPERTS, N_EXPERTS), 1)
        below = jnp.where(ej < ei, 1.0, 0.0)
        starts = jnp.dot(below, jnp.broadcast_to(padded, (N_EXPERTS, LANES)), precision=lax.Precision.HIGHEST,
                         preferred_element_type=F32)
        ends_ref[...] = (starts + padded).astype(jnp.int32)
        run_ref[...] = starts[:, 0:1]

    @pl.when(phase == 1)
    def _():
        ti = lax.broadcasted_iota(jnp.int32, (tp, tp), 0)
        tj = lax.broadcasted_iota(jnp.int32, (tp, tp), 1)
        before = jnp.where(ti < tj, 1.0, 0.0).astype(BF16)
        run = run_ref[...]
        for k in range(2):
            prefix = jnp.dot(onehot[k].astype(BF16), before, preferred_element_type=F32)
            pos_ref[k:k + 1, :] = jnp.sum(onehot[k] * (prefix + run), axis=0, keepdims=True).astype(jnp.int32)
            run = run + jnp.sum(onehot[k], axis=1, keepdims=True)
        run_ref[...] = run


def _positions(eidx):
    n = eidx.shape[1]
    nt = n // POSITION_TILE
    return pl.pallas_call(
        _positions_kernel,
        grid=(2, nt),
        in_specs=[pl.BlockSpec((2, POSITION_TILE), lambda p, i: (0, i))],
        out_specs=[pl.BlockSpec((2, POSITION_TILE), lambda p, i: (0, i * p)),
                   pl.BlockSpec((N_EXPERTS, LANES), lambda p, i: (0, 0))],
        out_shape=[jax.ShapeDtypeStruct((2, n), jnp.int32), jax.ShapeDtypeStruct((N_EXPERTS, LANES), jnp.int32)],
        scratch_shapes=[pltpu.VMEM((N_EXPERTS, 1), F32)],
        compiler_params=pltpu.CompilerParams(
            dimension_semantics=("arbitrary", "arbitrary"), vmem_limit_bytes=VMEM_LIMIT_BYTES),
        name="positions",
    )(eidx)


def _sc_mesh():
    return plsc.VectorSubcoreMesh(core_axis_name="c", subcore_axis_name="s")


def _sc_worker():
    return lax.axis_index("c") * SC_SUBCORES + lax.axis_index("s")


def _sc_sort_rows(table, pos, n_rows):
    n_slots, width = pos.shape[0], table.shape[1]
    n_tokens = n_slots // 2
    assert n_tokens & (n_tokens - 1) == 0
    per_worker = n_rows // SC_WORKERS
    assert per_worker * SC_WORKERS == n_rows and per_worker % SC_GATHER_ROWS == 0 and n_slots % SC_LANES == 0

    def body(table_hbm, pos_hbm, out_hbm, pos_v, inv_v, idx_v, rows_v, sem):
        base = _sc_worker() * per_worker
        pltpu.sync_copy(pos_hbm, pos_v)

        @pl.loop(0, per_worker, step=SC_LANES)
        def _(j):
            inv_v[pl.ds(j, SC_LANES)] = (base + j + lax.iota(jnp.int32, SC_LANES)) & (n_tokens - 1)

        @pl.loop(0, n_slots, step=SC_LANES)
        def _(s):
            local = pos_v[pl.ds(s, SC_LANES)] - base
            mine = (local >= 0) & (local < per_worker)
            token = (s + lax.iota(jnp.int32, SC_LANES)) & (n_tokens - 1)
            plsc.store_scatter(inv_v, [jnp.where(mine, local, 0)], token, mask=mine)

        @pl.loop(0, per_worker, step=SC_GATHER_ROWS)
        def _(j):
            @pl.loop(0, SC_GATHER_ROWS, step=SC_LANES)
            def _(q):
                idx_v[pl.ds(q, SC_LANES)] = inv_v[pl.ds(j + q, SC_LANES)]

            pltpu.async_copy(table_hbm.at[idx_v], rows_v, sem).wait()
            pltpu.sync_copy(rows_v, out_hbm.at[pl.ds(base + j, SC_GATHER_ROWS)])

    return pl.kernel(
        body, out_type=jax.ShapeDtypeStruct((n_rows, width), table.dtype), mesh=_sc_mesh(),
        scratch_types=[pltpu.VMEM((n_slots,), jnp.int32), pltpu.VMEM((per_worker,), jnp.int32),
                       pltpu.VMEM((SC_GATHER_ROWS,), jnp.int32), pltpu.VMEM((SC_GATHER_ROWS, width), table.dtype),
                       pltpu.SemaphoreType.DMA],
        compiler_params=pltpu.CompilerParams(needs_layout_passes=False),
        name="sc_sort_rows",
    )(table, pos)


def _sc_gather(table, idx):
    n_rows, width = idx.shape[0], table.shape[1]
    per_worker = n_rows // SC_WORKERS
    assert per_worker * SC_WORKERS == n_rows and per_worker % SC_GATHER_ROWS == 0

    def body(table_hbm, idx_hbm, out_hbm, idx_v, rows_v, sem):
        base = _sc_worker() * per_worker

        @pl.loop(0, per_worker, step=SC_GATHER_ROWS)
        def _(j):
            pltpu.sync_copy(idx_hbm.at[pl.ds(base + j, SC_GATHER_ROWS)], idx_v)
            pltpu.async_copy(table_hbm.at[idx_v], rows_v, sem).wait()
            pltpu.sync_copy(rows_v, out_hbm.at[pl.ds(base + j, SC_GATHER_ROWS)])

    return pl.kernel(
        body, out_type=jax.ShapeDtypeStruct((n_rows, width), table.dtype), mesh=_sc_mesh(),
        scratch_types=[pltpu.VMEM((SC_GATHER_ROWS,), jnp.int32), pltpu.VMEM((SC_GATHER_ROWS, width), table.dtype),
                       pltpu.SemaphoreType.DMA],
        name="sc_gather",
    )(table, idx)


def _experts_kernel(layer, te_ref, seg_ref, nxt_ref, xs_ref, wg_hbm, wu_hbm, wd_hbm, ys_ref,
                    wg_f, wu_f, wd_f, wg_bf, wu_bf, wd_bf, sems):
    i = pl.program_id(0)
    expert = te_ref[i]
    held = jnp.minimum(expert, N_EXPERTS - 1)
    slot = seg_ref[i] & 1
    new_segment = (i == 0) | (seg_ref[i] != seg_ref[jnp.maximum(i - 1, 0)])

    def weight_copies(e, s):
        return [pltpu.make_async_copy(w_hbm.at[layer, e], w_f.at[s], sems.at[k, s])
                for k, (w_hbm, w_f) in enumerate(((wg_hbm, wg_f), (wu_hbm, wu_f), (wd_hbm, wd_f)))]

    @pl.when(i == 0)
    def _():
        for cp in weight_copies(held, slot):
            cp.start()

    @pl.when(new_segment)
    def _():
        for cp in weight_copies(held, slot):
            cp.wait()
        wg_bf[...] = wg_f[slot].astype(BF16)
        wu_bf[...] = wu_f[slot].astype(BF16)
        wd_bf[...] = wd_f[slot].astype(BF16)

        @pl.when(nxt_ref[i] >= 0)
        def _():
            for cp in weight_copies(nxt_ref[i], 1 - slot):
                cp.start()

    @pl.when(expert < N_EXPERTS)
    def _():
        sub = xs_ref.shape[0] // MIX_SUB_TILES
        tiles = [slice(s * sub, (s + 1) * sub) for s in range(MIX_SUB_TILES)]
        xs = [_unpack_bf16_pairs(xs_ref[rs, :]).astype(BF16) for rs in tiles]
        hes = []
        for x in xs:
            gate, up = _dot(x, wg_bf[...]), _dot(x, wu_bf[...])
            hes.append((gate * _sigmoid(gate) * up).astype(BF16))
        for rs, he in zip(tiles, hes):
            ys_ref[rs, :] = _pack_bf16_pairs(_dot(he, wd_bf[...]))


def _experts(tile_expert, xs, layer, w_gate, w_up, w_down):
    n_rows, half = xs.shape
    _, n_e, d, de = w_gate.shape
    held = jnp.minimum(tile_expert, n_e - 1)
    segment = jnp.cumsum(jnp.concatenate([jnp.zeros((1,), jnp.int32), (held[1:] != held[:-1]).astype(jnp.int32)]))
    after = jnp.sum(held[None, :] <= held[:, None], axis=1)
    next_expert = jnp.where(after < held.shape[0], held[jnp.minimum(after, held.shape[0] - 1)], -1).astype(jnp.int32)

    rows = pl.BlockSpec((EXPERT_ROW_TILE, half), lambda i, te, seg, nxt: (i, 0))
    hbm = pl.BlockSpec(memory_space=pl.ANY)
    return pl.pallas_call(
        functools.partial(_experts_kernel, layer),
        grid_spec=pltpu.PrefetchScalarGridSpec(
            num_scalar_prefetch=3,
            grid=(n_rows // EXPERT_ROW_TILE,),
            in_specs=[rows, hbm, hbm, hbm],
            out_specs=rows,
            scratch_shapes=[pltpu.VMEM((2, d, de), F32), pltpu.VMEM((2, d, de), F32), pltpu.VMEM((2, de, d), F32),
                            pltpu.VMEM((d, de), BF16), pltpu.VMEM((d, de), BF16), pltpu.VMEM((de, d), BF16),
                            pltpu.SemaphoreType.DMA((3, 2))]),
        out_shape=jax.ShapeDtypeStruct((n_rows, half), jnp.int32),
        compiler_params=pltpu.CompilerParams(
            dimension_semantics=("arbitrary",), vmem_limit_bytes=VMEM_LIMIT_BYTES),
        name="experts",
    )(tile_expert, segment.astype(jnp.int32), next_expert, xs, w_gate, w_up, w_down)


def _final_kernel(y0_ref, y1_ref, gate_ref, h1_ref, gt_ref, fg_ref, *out_refs):
    o_ref = out_refs[-1]
    h2 = _moe_residual(y0_ref, y1_ref, gate_ref, h1_ref[0], gt_ref[0])
    ms = jnp.mean(h2 * h2, axis=-1, keepdims=True)
    o_ref[0] = h2 * lax.rsqrt(ms + RMS_EPS) * fg_ref[...]


def _moe_out_specs(b, t, d, tile=TOKEN_TILE):
    nt = t // tile
    return [pl.BlockSpec((tile, d // 2), lambda i, j: (i * nt + j, 0)),
            pl.BlockSpec((tile, d // 2), lambda i, j: (b * nt + i * nt + j, 0)),
            pl.BlockSpec((tile, 2), lambda i, j: (i * nt + j, 0))]


def _final(yg, gates, h1, mod, final_g, batch0, n_batch, out_prev):
    b, t, d = h1.shape
    tile = pl.BlockSpec((1, WIDE_TOKEN_TILE, d), lambda i, j: (i, j, 0))
    in_specs = _moe_out_specs(b, t, d, WIDE_TOKEN_TILE) + [tile, _mod_spec3(5, d, batch0),
                                                           pl.BlockSpec((1, d), lambda i, j: (0, 0))]
    args = [yg, yg, gates, h1, mod, final_g]
    aliases = {}
    if out_prev is not None:
        in_specs.append(pl.BlockSpec(memory_space=pl.ANY))
        aliases = {len(args): 0}
        args.append(out_prev)
    return pl.pallas_call(
        _final_kernel,
        grid=(b, t // WIDE_TOKEN_TILE),
        in_specs=in_specs,
        out_specs=pl.BlockSpec((1, WIDE_TOKEN_TILE, d), lambda i, j: (i + batch0, j, 0)),
        out_shape=jax.ShapeDtypeStruct((n_batch, t, d), F32),
        input_output_aliases=aliases,
        compiler_params=pltpu.CompilerParams(
            dimension_semantics=("arbitrary", "arbitrary"), vmem_limit_bytes=VMEM_LIMIT_BYTES),
        name="final",
    )(*args)


def _moe_sort(hm, eidx):
    n_slots = 2 * hm.shape[0]
    n_rows = n_slots + N_EXPERTS * EXPERT_ROW_TILE
    pos, ends = _positions(eidx)
    pos = pos.reshape(n_slots)
    tile_start = jnp.arange(n_rows // EXPERT_ROW_TILE, dtype=jnp.int32) * EXPERT_ROW_TILE
    tile_expert = jnp.sum(tile_start[:, None] >= ends[None, :, 0], axis=1).astype(jnp.int32)
    return _sc_sort_rows(hm, pos, n_rows), pos, tile_expert


def _moe_apply(sorted_rows, gates, layer, w_gate, w_up, w_down):
    xs, pos, tile_expert = sorted_rows
    ys = _experts(tile_expert, xs, layer, w_gate, w_up, w_down)
    return _sc_gather(ys, pos), gates.T, ys


def kernel(x, c, ctx, c_ctx, ada_w, ada_b, norm_g, rw_mu, rw_w_rkv, rw_w0, rw_w1, rw_w2, rw_a0, rw_a1, rw_a2, rw_g1, rw_g2, rw_k_k, rw_k_a, rw_r_k, rw_gn_w, rw_gn_b, rw_w_o, sc_w_in, sc_conv, sc_w_out, router_w, router_b, moe_w_gate, moe_w_up, moe_w_down, final_g):
    b, t, d = x.shape
    ctx_len = ctx.shape[1]
    depth = ada_w.shape[0]
    assert d == D_MODEL and depth == 2 and t % POSITION_TILE == 0

    mod_rows = 16
    cc = jnp.concatenate([c, c_ctx[None, :], jnp.zeros((mod_rows - b - 1, d), F32)], axis=0)
    mod = _ada(cc, ada_w, ada_b).reshape(depth, mod_rows, 1, 6 * d)

    row = lambda a: a.reshape(1, d)
    router_wt = jnp.pad(router_w, ((0, 0), (0, LANES - N_EXPERTS)))
    router_bc = router_b.reshape(N_EXPERTS, 1)

    w1 = jnp.concatenate([rw_w1[0, 0], rw_w1[0, 1]], axis=1).astype(BF16)
    a1 = jnp.concatenate([rw_a1[0, 0], rw_a1[0, 1]], axis=1).astype(BF16)

    def pad_dirs(w):
        z = jnp.zeros_like(w[0])
        return jnp.stack([jnp.concatenate([w[0], z], axis=0), jnp.concatenate([z, w[1]], axis=0)]).astype(BF16)

    assert b % N_STREAMS == 0
    nb = b // N_STREAMS
    pre_params = (row(norm_g[0, 0]), rw_mu[0], rw_w_rkv[0].astype(BF16), w1, pad_dirs(rw_w2[0]), rw_w0[0], a1,
                  pad_dirs(rw_a2[0]), rw_a0[0], rw_g1[0].astype(BF16), rw_g2[0].astype(BF16), row(rw_k_k[0]),
                  row(rw_k_a[0]), row(rw_r_k[0]))
    w_o, w_in, w_out = rw_w_o[0].astype(BF16), sc_w_in[0].astype(BF16), sc_w_out[0].astype(BF16)
    out = None
    moe_w = (moe_w_gate, moe_w_up, moe_w_down)
    streams = []
    for batch0 in range(0, b, nb):
        r, v, g, bonus, k, kk, cs0, cs1, a0, a1 = _rwkv_pre(ctx, x, mod[0], *pre_params, batch0, nb)
        yf, yr = _wkv(row(rw_k_a[0]), r, v, k, kk, cs0, a0, cs1, a1, ctx_len)
        h1, hm, eidx, gates = _readout(yf, yr, bonus, g, x, mod[0], row(rw_gn_w[0]), row(rw_gn_b[0]),
                                       w_o, row(norm_g[0, 1]), router_wt, router_bc, batch0, nb)
        streams.append((batch0, h1, gates, _moe_sort(hm, eidx)))
    if len(streams) > 1:
        batch0, h1, gates, (xs, pos, tile_expert) = streams[0]
        tile_expert, _ = lax.optimization_barrier((tile_expert, yf))
        streams[0] = (batch0, h1, gates, (xs, pos, tile_expert))
    prev_ys = None
    for batch0, h1, gates, sorted_rows in streams:
        yg, gates, _ = _moe_apply(sorted_rows, gates, 0, *moe_w)
        if prev_ys is not None:
            yg, _ = lax.optimization_barrier((yg, prev_ys))
        h1, hm, eidx, gates = _sconv(yg, gates, h1, mod[0], mod[1], row(norm_g[1, 0]), w_in, sc_conv[0], w_out,
                                     row(norm_g[1, 1]), router_wt, router_bc, batch0)
        yg, gates, prev_ys = _moe_apply(_moe_sort(hm, eidx), gates, 1, *moe_w)
        out = _final(yg, gates, h1, mod[1], row(final_g), batch0, b, out)
    return out
```
